```python
import math
import jax
import jax.numpy as jnp
from jax import lax
import numpy as np

D_MODEL = 1024
BATCH = 2
SEQ = 8192
DEPTH = 4

D_MIX = 1024
FOX_HEADS = 6
FOX_HEAD_DIM = 64
FOX_WIDTH = FOX_HEADS * FOX_HEAD_DIM
Q_BLOCK = 128
GLA_HEADS = 4
GLA_DV = 96
GLA_DK = 48
GLA_KWIDTH = GLA_HEADS * GLA_DK
GLA_VWIDTH = GLA_HEADS * GLA_DV
GLA_GATE_RANK = 16
GLA_GATE_TAU = 16.0
GLA_CHUNK = 64
POOL_WINDOWS = (2, 4, 8, 16)
POOL_GROUP = 64
POOL_WIDTH = len(POOL_WINDOWS) * POOL_GROUP
IN_SPLITS = (FOX_WIDTH, FOX_WIDTH, FOX_WIDTH, FOX_HEADS,
             GLA_KWIDTH, GLA_KWIDTH, GLA_VWIDTH, GLA_VWIDTH, GLA_GATE_RANK,
             POOL_WIDTH)
IN_OFFSETS = tuple(int(o) for o in np.cumsum(IN_SPLITS)[:-1])
N_IN = sum(IN_SPLITS)
FORGET_OFFSET = 3 * FOX_WIDTH
N_EXPERTS = 32
TOP_K = 4
D_EXPERT = 1024
SWIGLU_ALPHA = 1.702
SWIGLU_LIMIT = 7.0
MOE_BLOCK = 256
N_MOD = 6
LN_EPS = 1e-5
RMS_EPS = 1e-6
DEEPNORM_ALPHA = (2 * DEPTH) ** 0.25
DEEPNORM_BETA = (8 * DEPTH) ** -0.25

kernel_name = 'hybrid_fox_gla_pool_moe_deepnorm'


def layer_norm(x, g, b):
    xf = x.astype(jnp.float32)
    mu = jnp.mean(xf, axis=-1, keepdims=True)
    var = jnp.mean(jnp.square(xf - mu), axis=-1, keepdims=True)
    return ((xf - mu) * lax.rsqrt(var + LN_EPS)).astype(x.dtype) * g + b


def forgetting_attention(q, k, v, f_logit):
    B, S, H, Dh = q.shape
    nb = S // Q_BLOCK
    scale = Dh ** -0.5
    F = jnp.cumsum(jax.nn.log_sigmoid(f_logit.astype(jnp.float32)), axis=1).transpose(0, 2, 1)
    k_t = k.transpose(0, 2, 1, 3)
    v_t = v.transpose(0, 2, 1, 3)
    q_blocks = q.reshape(B, nb, Q_BLOCK, H, Dh).transpose(1, 0, 3, 2, 4)
    F_blocks = F.reshape(B, H, nb, Q_BLOCK).transpose(2, 0, 1, 3)
    k_pos = jnp.arange(S)

    def block(args):
        qb, Fq, i = args
        q_pos = i * Q_BLOCK + jnp.arange(Q_BLOCK)
        s = jnp.einsum('bhqd,bhkd->bhqk', qb, k_t, preferred_element_type=jnp.float32) * scale
        s = s + Fq[..., None] - F[:, :, None, :]
        s = jnp.where(k_pos[None, :] <= q_pos[:, None], s, -jnp.inf)
        p = jax.nn.softmax(s, axis=-1)
        return jnp.einsum('bhqk,bhkd->bhqd', p.astype(v.dtype), v_t)

    o = lax.map(block, (q_blocks, F_blocks, jnp.arange(nb)))
    return o.transpose(1, 0, 3, 2, 4).reshape(B, S, H * Dh)


def gla_chunked(q, k, v, log_a):
    B, S, H, Dk = q.shape
    Dv = v.shape[-1]
    C = GLA_CHUNK
    n = S // C

    def chunk(t):
        return t.astype(jnp.float32).reshape(B, n, C, H, -1).transpose(0, 3, 1, 2, 4)

    qc, kc, vc, ac = chunk(q) * (Dk ** -0.5), chunk(k), chunk(v), chunk(log_a)
    b = jnp.cumsum(ac, axis=3)
    b_last = b[:, :, :, -1:, :]
    q_in = qc * jnp.exp(b)
    k_in = kc * jnp.exp(-b)
    k_out = kc * jnp.exp(b_last - b)
    causal = jnp.tril(jnp.ones((C, C), dtype=bool))
    attn = jnp.where(causal, jnp.einsum('bhnid,bhnjd->bhnij', q_in, k_in), 0.0)
    o_intra = jnp.einsum('bhnij,bhnjv->bhniv', attn, vc)
    kv = jnp.einsum('bhncd,bhncv->bhndv', k_out, vc)
    decay = jnp.exp(b_last[:, :, :, 0, :])

    def step(state, inp):
        kv_n, dec_n = inp
        return state * dec_n[..., None] + kv_n, state

    init = jnp.zeros((B, H, Dk, Dv), jnp.float32)
    _, states = lax.scan(step, init, (kv.transpose(2, 0, 1, 3, 4), decay.transpose(2, 0, 1, 3)))
    o_inter = jnp.einsum('bhncd,nbhdv->bhncv', q_in, states)
    o = o_intra + o_inter
    return o.transpose(0, 2, 3, 1, 4).reshape(B, S, H, Dv)


def multiscale_pool(u, w_pool, pool_scale):
    B, S, W = u.shape
    uf = u.astype(jnp.float32)
    cs = jnp.concatenate([jnp.zeros((B, 1, W), jnp.float32), jnp.cumsum(uf, axis=1)], axis=1)
    win = jnp.repeat(jnp.array(POOL_WINDOWS, jnp.int32), POOL_GROUP)
    pos = jnp.arange(S, dtype=jnp.int32)
    lo_idx = jnp.maximum(pos[:, None] + 1 - win[None, :], 0)
    lo = jnp.take_along_axis(cs, jnp.broadcast_to(lo_idx[None], (B, S, W)), axis=1)
    count = jnp.minimum(pos[:, None] + 1, win[None, :]).astype(jnp.float32)
    pooled = (cs[:, 1:] - lo) / count[None] - uf
    pooled = pooled.reshape(B, S, len(POOL_WINDOWS), POOL_GROUP)
    mixed = jnp.einsum('bsgc,gcd->bsgd', pooled, w_pool.astype(jnp.float32)).reshape(B, S, W)
    return (mixed * pool_scale.astype(jnp.float32)).astype(u.dtype)


def hybrid_mixer(h, w_in, b_in, gla_w_a2, gla_b_a, gla_norm_g, pool_w, pool_scale, w_out):
    B, S, _ = h.shape
    z = h @ w_in + b_in
    fq, fk, fv, ff, gq, gk, gv, gr, ga1, pu = jnp.split(z, IN_OFFSETS, axis=-1)
    hd = (B, S, FOX_HEADS, FOX_HEAD_DIM)
    o_fox = forgetting_attention(fq.reshape(hd), fk.reshape(hd), fv.reshape(hd), ff)
    log_a = jax.nn.log_sigmoid((ga1 @ gla_w_a2 + gla_b_a).astype(jnp.float32)) / GLA_GATE_TAU
    kd = (B, S, GLA_HEADS, GLA_DK)
    o = gla_chunked(gq.reshape(kd), gk.reshape(kd), gv.reshape(B, S, GLA_HEADS, GLA_DV), log_a.reshape(kd))
    o = o * lax.rsqrt(jnp.mean(jnp.square(o), axis=-1, keepdims=True) + RMS_EPS)
    o_gla = (o.reshape(B, S, GLA_VWIDTH).astype(h.dtype) * gla_norm_g) * jax.nn.silu(gr)
    o_pool = multiscale_pool(pu, pool_w, pool_scale)
    return jnp.concatenate([o_fox, o_gla, o_pool], axis=-1) @ w_out


def clamped_swiglu(gu):
    glu, lin = jnp.split(gu, 2, axis=-1)
    glu = jnp.minimum(glu, SWIGLU_LIMIT)
    lin = jnp.clip(lin, -SWIGLU_LIMIT, SWIGLU_LIMIT)
    return glu * jax.nn.sigmoid(SWIGLU_ALPHA * glu) * (lin + 1.0)


def moe_ffn(h, w_router, b_router, w_gate_up, b_gate_up, w_down, b_down):
    B, S, D = h.shape
    T = B * S
    hf = h.reshape(T, D)
    logits = (hf @ w_router + b_router).astype(jnp.float32)
    top_logit, top_e = lax.top_k(logits, TOP_K)
    gate = jax.nn.softmax(top_logit, axis=-1)
    M = T * TOP_K
    flat_e = top_e.reshape(M)
    order = jnp.argsort(flat_e)
    sorted_e = flat_e[order]
    token_of = (order // TOP_K).astype(jnp.int32)
    counts = jnp.bincount(flat_e, length=N_EXPERTS)
    padded = (counts + MOE_BLOCK - 1) // MOE_BLOCK * MOE_BLOCK
    group_start = jnp.cumsum(counts) - counts
    padded_end = jnp.cumsum(padded)
    padded_start = padded_end - padded
    dest = padded_start[sorted_e] + jnp.arange(M) - group_start[sorted_e]
    n_blocks = -(-M // MOE_BLOCK) + N_EXPERTS
    rows = n_blocks * MOE_BLOCK
    row_token = jnp.full((rows,), T, jnp.int32).at[dest].set(token_of)
    h_pad = jnp.concatenate([hf, jnp.zeros((1, D), hf.dtype)], axis=0)
    x_rows = h_pad[row_token].reshape(n_blocks, MOE_BLOCK, D)
    block_expert = jnp.minimum(
        jnp.searchsorted(padded_end, jnp.arange(n_blocks) * MOE_BLOCK, side='right'), N_EXPERTS - 1)

    def expert_block(args):
        xb, e = args
        gu = xb @ w_gate_up[e] + b_gate_up[e]
        return clamped_swiglu(gu) @ w_down[e] + b_down[e]

    y_rows = lax.map(expert_block, (x_rows, block_expert)).reshape(rows, D)
    y = y_rows[dest] * gate.reshape(M)[order][:, None].astype(h.dtype)
    out = jnp.zeros((T, D), h.dtype).at[token_of].add(y)
    return out.reshape(B, S, D)


def setup_inputs(seed: int = 0) -> dict:
    key = jax.random.key(seed)
    ks = jax.random.split(key, 24)
    nrm = jax.random.normal
    f32 = jnp.float32
    L, D = DEPTH, D_MODEL
    x = nrm(ks[0], (BATCH, SEQ, D), f32)
    c = nrm(ks[1], (BATCH, D), f32)
    w_ada = nrm(ks[2], (L, D, N_MOD * D), f32) * (0.1 * D ** -0.5)
    b_ada = nrm(ks[3], (L, N_MOD * D), f32) * 0.02
    w_in = nrm(ks[4], (L, D, N_IN), f32) * D ** -0.5
    b_in = nrm(ks[5], (L, N_IN), f32) * 0.02
    b_in = b_in.at[:, FORGET_OFFSET:FORGET_OFFSET + FOX_HEADS].set(
        2.0 + 0.5 * nrm(ks[6], (L, FOX_HEADS), f32))
    gla_w_a2 = nrm(ks[7], (L, GLA_GATE_RANK, GLA_KWIDTH), f32) * GLA_GATE_RANK ** -0.5
    gla_b_a = nrm(ks[8], (L, GLA_KWIDTH), f32) * 0.1
    gla_norm_g = 1.0 + 0.02 * nrm(ks[9], (L, GLA_VWIDTH), f32)
    pool_w = nrm(ks[10], (L, len(POOL_WINDOWS), POOL_GROUP, POOL_GROUP), f32) * POOL_GROUP ** -0.5
    pool_scale = 1.0 + 0.02 * nrm(ks[11], (L, POOL_WIDTH), f32)
    w_out = nrm(ks[12], (L, D_MIX, D), f32) * (D_MIX ** -0.5 * DEEPNORM_BETA)
    ln1_g = 1.0 + 0.02 * nrm(ks[13], (L, D), f32)
    ln1_b = 0.02 * nrm(ks[14], (L, D), f32)
    w_router = nrm(ks[15], (L, D, N_EXPERTS), f32) * D ** -0.5
    b_router = 0.01 * nrm(ks[16], (L, N_EXPERTS), f32)
    w_gate_up = nrm(ks[17], (L, N_EXPERTS, D, 2 * D_EXPERT), f32) * D ** -0.5
    b_gate_up = 0.02 * nrm(ks[18], (L, N_EXPERTS, 2 * D_EXPERT), f32)
    w_down = nrm(ks[19], (L, N_EXPERTS, D_EXPERT, D), f32) * (D_EXPERT ** -0.5 * DEEPNORM_BETA)
    b_down = 0.02 * nrm(ks[20], (L, N_EXPERTS, D), f32)
    ln2_g = 1.0 + 0.02 * nrm(ks[21], (L, D), f32)
    ln2_b = 0.02 * nrm(ks[22], (L, D), f32)
    return {'x': x, 'c': c, 'w_ada': w_ada, 'b_ada': b_ada, 'w_in': w_in, 'b_in': b_in,
            'gla_w_a2': gla_w_a2, 'gla_b_a': gla_b_a, 'gla_norm_g': gla_norm_g,
            'pool_w': pool_w, 'pool_scale': pool_scale, 'w_out': w_out,
            'ln1_g': ln1_g, 'ln1_b': ln1_b, 'w_router': w_router, 'b_router': b_router,
            'w_gate_up': w_gate_up, 'b_gate_up': b_gate_up, 'w_down': w_down, 'b_down': b_down,
            'ln2_g': ln2_g, 'ln2_b': ln2_b}


def reference(x, c, w_ada, b_ada, w_in, b_in, gla_w_a2, gla_b_a, gla_norm_g, pool_w, pool_scale,
              w_out, ln1_g, ln1_b, w_router, b_router, w_gate_up, b_gate_up, w_down, b_down,
              ln2_g, ln2_b):
    cond = jax.nn.silu(c)
    for l in range(DEPTH):
        mod = cond @ w_ada[l] + b_ada[l]
        shift1, scale1, gate1, shift2, scale2, gate2 = [m[:, None, :] for m in jnp.split(mod, N_MOD, axis=-1)]
        h = x * (1.0 + scale1) + shift1
        y = hybrid_mixer(h, w_in[l], b_in[l], gla_w_a2[l], gla_b_a[l], gla_norm_g[l],
                         pool_w[l], pool_scale[l], w_out[l])
        x = layer_norm(DEEPNORM_ALPHA * x + (1.0 + gate1) * y, ln1_g[l], ln1_b[l])
        h = x * (1.0 + scale2) + shift2
        y = moe_ffn(h, w_router[l], b_router[l], w_gate_up[l], b_gate_up[l], w_down[l], b_down[l])
        x = layer_norm(DEEPNORM_ALPHA * x + (1.0 + gate2) * y, ln2_g[l], ln2_b[l])
    return x
```

```python
import functools

import numpy as np
import jax
import jax.numpy as jnp
from jax import lax
from jax.experimental import pallas as pl
from jax.experimental.pallas import tpu as pltpu

F32 = jnp.float32
BF16 = jnp.bfloat16

D_MODEL = 1024
FOX_HEADS = 6
FOX_HEAD_DIM = 64
FOX_WIDTH = FOX_HEADS * FOX_HEAD_DIM
GLA_HEADS = 4
GLA_DV = 96
GLA_DK = 48
GLA_KWIDTH = GLA_HEADS * GLA_DK
GLA_VWIDTH = GLA_HEADS * GLA_DV
GLA_GATE_RANK = 16
GLA_GATE_TAU = 16.0
GLA_CHUNK = 64
POOL_WINDOWS = (2, 4, 8, 16)
POOL_GROUP = 64
POOL_WIDTH = len(POOL_WINDOWS) * POOL_GROUP
N_EXPERTS = 32
TOP_K = 4
D_EXPERT = 1024
SWIGLU_ALPHA = 1.702
SWIGLU_LIMIT = 7.0
N_MOD = 6
LN_EPS = 1e-5
RMS_EPS = 1e-6

LANES = 128
VMEM_LIMIT_BYTES = 56 * 1024 * 1024

FOX_PAD = FOX_HEADS * LANES
GLA_PAD = GLA_HEADS * LANES
BIAS_LANE = FOX_HEAD_DIM
G_OFF_Q, G_OFF_K, G_OFF_V, G_OFF_R = 0, GLA_PAD, 2 * GLA_PAD, 3 * GLA_PAD
G_OFF_U = 4 * GLA_PAD
G_OFF_A = G_OFF_U + POOL_WIDTH
G_OFF_F = G_OFF_A + LANES
G_WIDTH = G_OFF_F + LANES

MOE_BLOCK = 256
TOPK_SHIFT = TOP_K.bit_length() - 1
assert 1 << TOPK_SHIFT == TOP_K
NEG_BIG = -1e30


def _cparams(sem, vmem=None):
    return pltpu.CompilerParams(dimension_semantics=sem, vmem_limit_bytes=vmem or VMEM_LIMIT_BYTES)


def _log_sigmoid(x):
    return jnp.minimum(x, 0.0) - jnp.log1p(jnp.exp(-jnp.abs(x)))


def _split3(x):
    hi = x.astype(BF16)
    r = x - hi.astype(F32)
    mid = r.astype(BF16)
    lo = (r - mid.astype(F32)).astype(BF16)
    return hi, mid, lo


def _dot(a, b):
    return jnp.dot(a, b, preferred_element_type=F32)


def _dot_nt(a, b):
    return lax.dot_general(a, b, (((1,), (1,)), ((), ())), preferred_element_type=F32)


def _dot_tn(a, b):
    return lax.dot_general(a, b, (((0,), (0,)), ((), ())), preferred_element_type=F32)


def _ada_kernel(c_ref, w_ref, b_ref, o_ref):
    c = c_ref[...]
    cond = c * jax.nn.sigmoid(c)
    o_ref[0] = jnp.dot(cond, w_ref[0], preferred_element_type=F32,
                       precision=lax.Precision.HIGHEST) + b_ref[0]


def _ada_mod(c, w_ada, b_ada):
    L, D, N = w_ada.shape
    B = c.shape[0]
    rows = 8
    c_pad = jnp.zeros((rows, D), F32).at[:B].set(c)
    tn = 1536
    out = pl.pallas_call(
        _ada_kernel,
        grid=(L, N // tn),
        in_specs=[pl.BlockSpec((rows, D), lambda l, j: (0, 0)),
                  pl.BlockSpec((1, D, tn), lambda l, j: (l, 0, j)),
                  pl.BlockSpec((1, 1, tn), lambda l, j: (l, 0, j))],
        out_specs=pl.BlockSpec((1, rows, tn), lambda l, j: (l, 0, j)),
        out_shape=jax.ShapeDtypeStruct((L, rows, N), F32),
        compiler_params=_cparams(("arbitrary", "arbitrary")),
        name="ada_mod",
    )(c_pad, w_ada, b_ada.reshape(L, 1, N))
    return out[:, :B].reshape(L, B, N_MOD, 1, D)


def _inproj_kernel(x_ref, sc_ref, sh_ref, wq_ref, wk_ref, wv_ref, wg_ref,
                   bq_ref, bk_ref, bv_ref, bg_ref, q_ref, k_ref, v_ref, g_ref):
    h = (x_ref[...] * (1.0 + sc_ref[0]) + sh_ref[0]).astype(BF16)
    q_ref[...] = (_dot(h, wq_ref[...]) + bq_ref[...]).astype(BF16)
    k_ref[...] = (_dot(h, wk_ref[...]) + bk_ref[...]).astype(BF16)
    v_ref[...] = (_dot(h, wv_ref[...]) + bv_ref[...]).astype(BF16)
    g_ref[...] = _dot(h, wg_ref[...]) + bg_ref[...]


def _pad_heads(w, heads, dim):
    lead = w.shape[:-1]
    w = w.reshape(lead + (heads, dim))
    w = jnp.pad(w, [(0, 0)] * len(lead) + [(0, 0), (0, LANES - dim)])
    return w.reshape(lead + (heads * LANES,))


def _prep_inproj(w_in, b_in):
    W = jnp.concatenate([w_in, b_in[None, :]], axis=0)
    o = 0
    fq = W[:, o:o + FOX_WIDTH]; o += FOX_WIDTH
    fk = W[:, o:o + FOX_WIDTH]; o += FOX_WIDTH
    fv = W[:, o:o + FOX_WIDTH]; o += FOX_WIDTH
    ff = W[:, o:o + FOX_HEADS]; o += FOX_HEADS
    gq = W[:, o:o + GLA_KWIDTH]; o += GLA_KWIDTH
    gk = W[:, o:o + GLA_KWIDTH]; o += GLA_KWIDTH
    gv = W[:, o:o + GLA_VWIDTH]; o += GLA_VWIDTH
    gr = W[:, o:o + GLA_VWIDTH]; o += GLA_VWIDTH
    ga = W[:, o:o + GLA_GATE_RANK]; o += GLA_GATE_RANK
    pu = W[:, o:o + POOL_WIDTH]
    wq = _pad_heads(fq * (FOX_HEAD_DIM ** -0.5), FOX_HEADS, FOX_HEAD_DIM)
    wk = _pad_heads(fk, FOX_HEADS, FOX_HEAD_DIM)
    wv = _pad_heads(fv, FOX_HEADS, FOX_HEAD_DIM)
    ones_col = jnp.zeros((FOX_PAD,), F32).at[jnp.arange(FOX_HEADS) * LANES + BIAS_LANE].set(1.0)
    wv = wv.at[-1].add(ones_col)
    wg = jnp.concatenate([
        _pad_heads(gq, GLA_HEADS, GLA_DK), _pad_heads(gk, GLA_HEADS, GLA_DK),
        _pad_heads(gv, GLA_HEADS, GLA_DV), _pad_heads(gr, GLA_HEADS, GLA_DV),
        pu, jnp.pad(ga, ((0, 0), (0, LANES - GLA_GATE_RANK))),
        jnp.pad(ff, ((0, 0), (0, LANES - FOX_HEADS)))], axis=1)
    outs = []
    for w in (wq, wk, wv, wg):
        outs.append((w[:-1].astype(BF16), w[-1:].astype(F32)))
    return outs


def _inproj(x2d, scale, shift, prep, B, S, tm):
    T, D = x2d.shape
    (wq, bq), (wk, bk), (wv, bv), (wg, bg) = prep
    nt = S // tm
    full = lambda a: pl.BlockSpec(a.shape, lambda i: (0,) * a.ndim)
    row = lambda w: pl.BlockSpec((tm, w), lambda i: (i, 0))
    mod = pl.BlockSpec((1, 1, D), lambda i: (i // nt, 0, 0))
    return pl.pallas_call(
        _inproj_kernel,
        grid=(T // tm,),
        in_specs=[row(D), mod, mod, full(wq), full(wk), full(wv), full(wg),
                  full(bq), full(bk), full(bv), full(bg)],
        out_specs=[row(FOX_PAD), row(FOX_PAD), row(FOX_PAD), row(G_WIDTH)],
        out_shape=[jax.ShapeDtypeStruct((T, FOX_PAD), BF16)] * 3
                  + [jax.ShapeDtypeStruct((T, G_WIDTH), F32)],
        compiler_params=_cparams(("arbitrary",)),
        name="inproj",
    )(x2d, scale, shift, wq, wk, wv, wg, bq, bk, bv, bg)


def _aug_constants():
    pq = np.zeros((3 * LANES, FOX_PAD), np.float32)
    pk = np.zeros((3 * LANES, FOX_PAD), np.float32)
    cq = np.zeros((1, FOX_PAD), np.float32)
    ck = np.zeros((1, FOX_PAD), np.float32)
    for h in range(FOX_HEADS):
        base = h * LANES + BIAS_LANE
        for p in range(3):
            pq[p * LANES + h, base + p] = 1.0
            pk[p * LANES + h, base + 3 + p] = -1.0
            cq[0, base + 3 + p] = 1.0
            ck[0, base + p] = 1.0
    return pq, pk, cq, ck


def _fgate_kernel(q_ref, k_ref, f_ref, pq_ref, pk_ref, cq_ref, ck_ref, q2_ref, k2_ref, carry):
    @pl.when(pl.program_id(1) == 0)
    def _():
        carry[...] = jnp.zeros_like(carry)

    tf = f_ref.shape[0]
    ls = _log_sigmoid(f_ref[...])
    r = lax.broadcasted_iota(jnp.int32, (tf, tf), 0)
    c = lax.broadcasted_iota(jnp.int32, (tf, tf), 1)
    tri = (c <= r).astype(BF16)
    hi, mid, lo = _split3(ls)
    cs = _dot(tri, hi) + _dot(tri, mid) + _dot(tri, lo)
    F = cs + carry[...]
    carry[...] = F[tf - 1:tf, :]
    fh, fm, fl = _split3(F)
    f3 = jnp.concatenate([fh, fm, fl], axis=1)
    q2_ref[...] = (q_ref[...].astype(F32) + _dot(f3, pq_ref[...]) + cq_ref[...]).astype(BF16)
    k2_ref[...] = (k_ref[...].astype(F32) + _dot(f3, pk_ref[...]) + ck_ref[...]).astype(BF16)


def _fgate(zq, zk, zg, B, S, tf):
    T = zq.shape[0]
    nt = S // tf
    pq, pk, cq, ck = _aug_constants()
    pq, pk = jnp.asarray(pq, BF16), jnp.asarray(pk, BF16)
    cq, ck = jnp.asarray(cq), jnp.asarray(ck)
    full = lambda a: pl.BlockSpec(a.shape, lambda b, i: (0,) * a.ndim)
    row = pl.BlockSpec((tf, FOX_PAD), lambda b, i: (b * nt + i, 0))
    return pl.pallas_call(
        _fgate_kernel,
        grid=(B, nt),
        in_specs=[row, row, pl.BlockSpec((tf, LANES), lambda b, i: (b * nt + i, G_OFF_F // LANES)),
                  full(pq), full(pk), full(cq), full(ck)],
        out_specs=[row, row],
        out_shape=[jax.ShapeDtypeStruct((T, FOX_PAD), BF16)] * 2,
        scratch_shapes=[pltpu.VMEM((1, LANES), F32)],
        compiler_params=_cparams(("arbitrary", "arbitrary")),
        name="fgate",
    )(zq, zk, zg, pq, pk, cq, ck)


def _attn_kernel(q_ref, k_ref, v_ref, o_ref, m_sc, acc_sc, *, blk):
    i = pl.program_id(2)
    q = q_ref[...]

    def step(j, diag):
        off = pl.multiple_of(j * blk, blk)
        k = k_ref[pl.ds(off, blk), :]
        v = v_ref[pl.ds(off, blk), :]
        s = _dot_nt(q, k)
        if diag:
            r = lax.broadcasted_iota(jnp.int32, (blk, blk), 0)
            c = lax.broadcasted_iota(jnp.int32, (blk, blk), 1)
            s = jnp.where(c <= r, s, NEG_BIG)
        m_prev = m_sc[...]
        m_new = jnp.maximum(m_prev, jnp.max(s, axis=-1, keepdims=True))
        alpha = jnp.exp(m_prev - m_new)
        p = jnp.exp(s - m_new)
        acc_sc[...] = alpha * acc_sc[...] + _dot(p.astype(BF16), v)
        m_sc[...] = m_new

    m_sc[...] = jnp.full_like(m_sc, NEG_BIG)
    acc_sc[...] = jnp.zeros_like(acc_sc)
    step(i, True)

    def body(t, carry):
        step(i - 1 - t, False)
        return carry

    lax.fori_loop(0, i, body, 0)
    acc = acc_sc[...]
    o_ref[...] = (acc / acc[:, BIAS_LANE:BIAS_LANE + 1]).astype(o_ref.dtype)


def _attention(q2, k2, zv, B, S, blk):
    T = q2.shape[0]
    nq = S // blk
    qspec = pl.BlockSpec((blk, LANES), lambda b, h, i: (b * nq + i, h))
    kvspec = pl.BlockSpec((S, LANES), lambda b, h, i: (b, h))
    return pl.pallas_call(
        functools.partial(_attn_kernel, blk=blk),
        grid=(B, FOX_HEADS, nq),
        in_specs=[qspec, kvspec, kvspec],
        out_specs=qspec,
        out_shape=jax.ShapeDtypeStruct((T, FOX_PAD), BF16),
        scratch_shapes=[pltpu.VMEM((blk, 1), F32), pltpu.VMEM((blk, LANES), F32)],
        compiler_params=_cparams(("arbitrary", "arbitrary", "arbitrary")),
        name="fox_attention",
    )(q2, k2, zv)


def _gla_kernel(q_ref, k_ref, v_ref, r_ref, a_ref, wa_ref, ba_ref, g_ref, o_ref, st_sc, *, tg):
    @pl.when(pl.program_id(1) == 0)
    def _():
        st_sc[...] = jnp.zeros_like(st_sc)

    C = GLA_CHUNK
    nchunk = tg // C
    la = _log_sigmoid(_dot(a_ref[...].astype(BF16), wa_ref[...]) + ba_ref[...]) * (1.0 / GLA_GATE_TAU)
    r = lax.broadcasted_iota(jnp.int32, (tg, tg), 0)
    c = lax.broadcasted_iota(jnp.int32, (tg, tg), 1)
    shift = C.bit_length() - 1
    tri = ((c <= r) & ((c >> shift) == (r >> shift))).astype(BF16)
    hi, mid, lo = _split3(la)
    b = _dot(tri, hi) + _dot(tri, mid) + _dot(tri, lo)
    eb = jnp.exp(b)
    q_in = q_ref[...] * (GLA_DK ** -0.5) * eb
    k_in = k_ref[...] * jnp.exp(-b)
    v = v_ref[...]
    rc = lax.broadcasted_iota(jnp.int32, (C, C), 0)
    cc = lax.broadcasted_iota(jnp.int32, (C, C), 1)
    causal = cc <= rc
    lane = lax.broadcasted_iota(jnp.int32, (1, LANES), 1)
    vmask = (lane < GLA_DV).astype(F32)
    outs = []
    for ci in range(nchunk):
        rows = slice(ci * C, (ci + 1) * C)
        b_last = b[ci * C + C - 1:ci * C + C, :]
        k_out = k_ref[rows, :] * jnp.exp(b_last - b[rows, :])
        dec = jnp.exp(b_last)
        heads = []
        for h in range(GLA_HEADS):
            ln = slice(h * LANES, (h + 1) * LANES)
            qh = q_in[rows, ln].astype(BF16)
            kh = k_in[rows, ln].astype(BF16)
            vh = v[rows, ln].astype(BF16)
            attn = jnp.where(causal, _dot_nt(qh, kh), 0.0)
            st = st_sc[h]
            o = _dot(attn.astype(BF16), vh) + _dot_nt(qh, st.astype(BF16))
            kv_t = _dot_tn(vh, k_out[:, ln].astype(BF16))
            st_sc[h] = st * dec[:, ln] + kv_t
            ms = jnp.sum(o * o, axis=-1, keepdims=True) * (1.0 / GLA_DV)
            heads.append(o * lax.rsqrt(ms + RMS_EPS) * vmask)
        outs.append(jnp.concatenate(heads, axis=1))
    o_all = jnp.concatenate(outs, axis=0)
    gr = r_ref[...]
    o_ref[...] = ((o_all * g_ref[...]) * (gr * jax.nn.sigmoid(gr))).astype(o_ref.dtype)


def _gla(zg, wa, ba, gn, B, S, tg):
    T = zg.shape[0]
    nt = S // tg
    col = lambda off, w: pl.BlockSpec((tg, w), lambda b, i: (b * nt + i, off // w))
    full = lambda a: pl.BlockSpec(a.shape, lambda b, i: (0,) * a.ndim)
    return pl.pallas_call(
        functools.partial(_gla_kernel, tg=tg),
        grid=(B, nt),
        in_specs=[col(G_OFF_Q, GLA_PAD), col(G_OFF_K, GLA_PAD), col(G_OFF_V, GLA_PAD),
                  col(G_OFF_R, GLA_PAD), col(G_OFF_A, LANES), full(wa), full(ba), full(gn)],
        out_specs=pl.BlockSpec((tg, GLA_PAD), lambda b, i: (b * nt + i, 0)),
        out_shape=jax.ShapeDtypeStruct((T, GLA_PAD), BF16),
        scratch_shapes=[pltpu.VMEM((GLA_HEADS, LANES, LANES), F32)],
        compiler_params=_cparams(("arbitrary", "arbitrary")),
        name="gla",
    )(zg, zg, zg, zg, zg, wa, ba, gn)


HALO = max(POOL_WINDOWS)


def _pool_kernel(u_ref, w_ref, s_ref, o_ref, xx):
    tp = u_ref.shape[0]
    i = pl.program_id(1)

    @pl.when(i == 0)
    def _():
        xx[0:HALO, :] = jnp.zeros((HALO, POOL_WIDTH), F32)

    @pl.when(i > 0)
    def _():
        xx[0:HALO, :] = xx[tp:tp + HALO, :]

    u = u_ref[...]
    xx[HALO:HALO + tp, :] = u
    lane = lax.broadcasted_iota(jnp.int32, (1, POOL_WIDTH), 1)
    grp = lane >> (POOL_GROUP.bit_length() - 1)
    pos = lax.broadcasted_iota(jnp.int32, (tp, 1), 0) + i * tp + 1
    acc = u
    pooled = jnp.zeros_like(u)
    for j in range(1, HALO):
        acc = acc + xx[HALO - j:HALO - j + tp, :]
        w = j + 1
        if w in POOL_WINDOWS:
            g = POOL_WINDOWS.index(w)
            cnt = jnp.minimum(pos, w).astype(F32)
            pooled = jnp.where(grp == g, acc / cnt - u, pooled)
    mixed = _dot(pooled.astype(BF16), w_ref[...])
    o_ref[...] = (mixed * s_ref[...]).astype(o_ref.dtype)


def _pool(zg, w_bd, scale, B, S, tp):
    T = zg.shape[0]
    nt = S // tp
    full = lambda a: pl.BlockSpec(a.shape, lambda b, i: (0,) * a.ndim)
    return pl.pallas_call(
        _pool_kernel,
        grid=(B, nt),
        in_specs=[pl.BlockSpec((tp, POOL_WIDTH), lambda b, i: (b * nt + i, G_OFF_U // POOL_WIDTH)),
                  full(w_bd), full(scale)],
        out_specs=pl.BlockSpec((tp, POOL_WIDTH), lambda b, i: (b * nt + i, 0)),
        out_shape=jax.ShapeDtypeStruct((T, POOL_WIDTH), BF16),
        scratch_shapes=[pltpu.VMEM((tp + HALO, POOL_WIDTH), F32)],
        compiler_params=_cparams(("arbitrary", "arbitrary")),
        name="pool",
    )(zg, w_bd, scale)


def _layer_norm(r, g, b):
    mu = jnp.mean(r, axis=-1, keepdims=True)
    d = r - mu
    var = jnp.mean(d * d, axis=-1, keepdims=True)
    return d * lax.rsqrt(var + LN_EPS) * g + b


def _outproj_kernel(of_ref, og_ref, op_ref, x_ref, wf_ref, wg_ref, wp_ref, gate_ref, lg_ref, lb_ref,
                    sc_ref, sh_ref, wr_ref, br_ref,
                    x1_ref, h2_ref, e_ref, gt_ref, rk_ref, cnt_ref, cnt_sc, *, alpha):
    step = pl.program_id(0)

    @pl.when(step == 0)
    def _():
        cnt_sc[...] = jnp.zeros_like(cnt_sc)

    y = _dot(of_ref[...], wf_ref[...]) + _dot(og_ref[...], wg_ref[...]) + _dot(op_ref[...], wp_ref[...])
    r = alpha * x_ref[...] + (1.0 + gate_ref[0]) * y
    x1 = _layer_norm(r, lg_ref[...], lb_ref[...])
    x1_ref[...] = x1
    h2 = x1 * (1.0 + sc_ref[0]) + sh_ref[0]
    h2_ref[...] = h2
    logits = jnp.dot(h2, wr_ref[...], preferred_element_type=F32,
                     precision=lax.Precision.HIGHEST) + br_ref[...]
    tm = logits.shape[0]
    lane_i = lax.broadcasted_iota(jnp.int32, (tm, LANES), 1)
    lane = lane_i.astype(F32)
    work = logits
    tops, idxs = [], []
    onehot = jnp.zeros((tm, LANES), F32)
    for _ in range(TOP_K):
        m = jnp.max(work, axis=-1, keepdims=True)
        idx = jnp.min(jnp.where(work == m, lane, float(LANES)), axis=-1, keepdims=True)
        sel = lane == idx
        onehot = onehot + sel.astype(F32)
        work = jnp.where(sel, -jnp.inf, work)
        tops.append(m)
        idxs.append(idx)
    ex = [jnp.exp(t - tops[0]) for t in tops]
    den = ex[0] + ex[1] + ex[2] + ex[3]
    rr = lax.broadcasted_iota(jnp.int32, (tm, tm), 0)
    cc = lax.broadcasted_iota(jnp.int32, (tm, tm), 1)
    stril = (cc < rr).astype(BF16)
    before = _dot(stril, onehot.astype(BF16)) + cnt_sc[...]
    e_out = jnp.zeros((tm, LANES), jnp.int32)
    g_out = jnp.zeros((tm, LANES), F32)
    r_out = jnp.zeros((tm, LANES), jnp.int32)
    for k in range(TOP_K):
        rank = jnp.sum(jnp.where(lane == idxs[k], before, 0.0), axis=-1, keepdims=True)
        e_out = jnp.where(lane_i == k, idxs[k].astype(jnp.int32), e_out)
        g_out = jnp.where(lane_i == k, ex[k] / den, g_out)
        r_out = jnp.where(lane_i == k, rank.astype(jnp.int32), r_out)
    e_ref[...] = e_out
    gt_ref[...] = g_out
    rk_ref[...] = r_out
    cnt_sc[...] = cnt_sc[...] + jnp.sum(onehot, axis=0, keepdims=True)
    cnt_ref[...] = cnt_sc[...].astype(jnp.int32)


def _outproj(o_fox, o_gla, o_pool, x2d, wf, wg, wp, gate1, ln_g, ln_b, scale2, shift2, wr, br,
             B, S, tm, alpha):
    T, D = x2d.shape
    nt = S // tm
    full = lambda a: pl.BlockSpec(a.shape, lambda i: (0,) * a.ndim)
    row = lambda w: pl.BlockSpec((tm, w), lambda i: (i, 0))
    mod = pl.BlockSpec((1, 1, D), lambda i: (i // nt, 0, 0))
    return pl.pallas_call(
        functools.partial(_outproj_kernel, alpha=alpha),
        grid=(T // tm,),
        in_specs=[row(FOX_PAD), row(GLA_PAD), row(POOL_WIDTH), row(D), full(wf), full(wg), full(wp),
                  mod, full(ln_g), full(ln_b), mod, mod, full(wr), full(br)],
        out_specs=[row(D), row(D), row(LANES), row(LANES), row(LANES),
                   pl.BlockSpec((1, LANES), lambda i: (0, 0))],
        out_shape=[jax.ShapeDtypeStruct((T, D), F32), jax.ShapeDtypeStruct((T, D), F32),
                   jax.ShapeDtypeStruct((T, LANES), jnp.int32), jax.ShapeDtypeStruct((T, LANES), F32),
                   jax.ShapeDtypeStruct((T, LANES), jnp.int32), jax.ShapeDtypeStruct((1, LANES), jnp.int32)],
        scratch_shapes=[pltpu.VMEM((1, LANES), F32)],
        compiler_params=_cparams(("arbitrary",)),
        name="outproj_router",
    )(o_fox, o_gla, o_pool, x2d, wf, wg, wp, gate1, ln_g, ln_b, scale2, shift2, wr, br)


def _dispatch_kernel(pstart_ref, zblk_ref, nu_ref, e_ref, rk_ref, h_ref, xr_ref, zero_sc, sem, zsem,
                     *, td, n_blocks):
    step = pl.program_id(0)

    @pl.when(step == 0)
    def _():
        zero_sc[...] = jnp.zeros_like(zero_sc)

        def zcopy(blk):
            return pltpu.make_async_copy(
                zero_sc, xr_ref.at[pl.ds(pl.multiple_of(blk * MOE_BLOCK, MOE_BLOCK), MOE_BLOCK), :], zsem)

        def zstart(e, c):
            @pl.when(zblk_ref[e] >= 0)
            def _():
                zcopy(zblk_ref[e]).start()
            return c

        def zwait(e, c):
            @pl.when(zblk_ref[e] >= 0)
            def _():
                zcopy(0).wait()
            return c

        def tstart(blk, c):
            zcopy(blk).start()
            return c

        def twait(blk, c):
            zcopy(0).wait()
            return c

        lax.fori_loop(0, N_EXPERTS, zstart, 0)
        lax.fori_loop(nu_ref[0], n_blocks, tstart, 0)
        lax.fori_loop(0, N_EXPERTS, zwait, 0)
        lax.fori_loop(nu_ref[0], n_blocks, twait, 0)

    def row_copy(t, dest):
        return pltpu.make_async_copy(h_ref.at[pl.ds(t, 1), :], xr_ref.at[pl.ds(dest, 1), :], sem)

    def start(n, c):
        row_copy(lax.shift_right_logical(n, TOPK_SHIFT), pstart_ref[e_ref[n]] + rk_ref[n]).start()
        return c

    def wait(n, c):
        row_copy(0, 0).wait()
        return c

    lax.fori_loop(0, td * TOP_K, start, 0, unroll=8)
    lax.fori_loop(0, td * TOP_K, wait, 0, unroll=8)


def _dispatch(h2, e_flat, rk_flat, pstart, zblk, n_used, rows, td):
    T, D = h2.shape
    n = td * TOP_K
    grid_spec = pltpu.PrefetchScalarGridSpec(
        num_scalar_prefetch=3,
        grid=(T // td,),
        in_specs=[pl.BlockSpec((n,), lambda i, *_: (i,), memory_space=pltpu.SMEM),
                  pl.BlockSpec((n,), lambda i, *_: (i,), memory_space=pltpu.SMEM),
                  pl.BlockSpec((td, D), lambda i, *_: (i, 0))],
        out_specs=pl.BlockSpec(memory_space=pl.ANY),
        scratch_shapes=[pltpu.VMEM((MOE_BLOCK, D), F32), pltpu.SemaphoreType.DMA, pltpu.SemaphoreType.DMA],
    )
    return pl.pallas_call(
        functools.partial(_dispatch_kernel, td=td, n_blocks=rows // MOE_BLOCK),
        grid_spec=grid_spec,
        out_shape=jax.ShapeDtypeStruct((rows, D), F32),
        compiler_params=_cparams(("arbitrary",)),
        name="moe_dispatch",
    )(pstart, zblk, n_used, e_flat, rk_flat, h2)


def _expert_kernel(be_ref, nu_ref, x_ref, wgu_ref, bgu_ref, wd_ref, bd_ref, y_ref, wgu_sc, wd_sc):
    i = pl.program_id(0)
    used = i < nu_ref[0]

    @pl.when(used)
    def _():
        prev = be_ref[jnp.maximum(i - 1, 0)]

        @pl.when((i == 0) | (be_ref[i] != prev))
        def _():
            wgu_sc[...] = wgu_ref[0].astype(BF16)
            wd_sc[...] = wd_ref[0].astype(BF16)

        x = x_ref[...].astype(BF16)
        gu = _dot(x, wgu_sc[...]) + bgu_ref[0]
        glu = jnp.minimum(gu[:, :D_EXPERT], SWIGLU_LIMIT)
        lin = jnp.clip(gu[:, D_EXPERT:], -SWIGLU_LIMIT, SWIGLU_LIMIT)
        act = glu * jax.nn.sigmoid(SWIGLU_ALPHA * glu) * (lin + 1.0)
        y_ref[...] = _dot(act.astype(BF16), wd_sc[...]) + bd_ref[0]

    @pl.when(jnp.logical_not(used))
    def _():
        y_ref[...] = jnp.zeros_like(y_ref)


def _experts(x_rows, block_expert, n_used, w_gate_up, b_gate_up, w_down, b_down, layer):
    rows, D = x_rows.shape
    nb = rows // MOE_BLOCK
    E = w_gate_up.shape[1]
    grid_spec = pltpu.PrefetchScalarGridSpec(
        num_scalar_prefetch=2,
        grid=(nb,),
        in_specs=[pl.BlockSpec((MOE_BLOCK, D), lambda i, be, nu: (jnp.minimum(i, nu[0] - 1), 0)),
                  pl.BlockSpec((1, D, 2 * D_EXPERT), lambda i, be, nu: (layer * E + be[i], 0, 0)),
                  pl.BlockSpec((1, 1, 2 * D_EXPERT), lambda i, be, nu: (layer * E + be[i], 0, 0)),
                  pl.BlockSpec((1, D_EXPERT, D), lambda i, be, nu: (layer * E + be[i], 0, 0)),
                  pl.BlockSpec((1, 1, D), lambda i, be, nu: (layer * E + be[i], 0, 0))],
        out_specs=pl.BlockSpec((MOE_BLOCK, D), lambda i, be, nu: (i, 0)),
        scratch_shapes=[pltpu.VMEM((D, 2 * D_EXPERT), BF16), pltpu.VMEM((D_EXPERT, D), BF16)],
    )
    L = w_gate_up.shape[0]
    return pl.pallas_call(
        _expert_kernel,
        grid_spec=grid_spec,
        out_shape=jax.ShapeDtypeStruct((rows, D), F32),
        compiler_params=_cparams(("arbitrary",)),
        name="moe_experts",
    )(block_expert, n_used, x_rows,
      w_gate_up.reshape(L * E, D, 2 * D_EXPERT), b_gate_up.reshape(L * E, 1, 2 * D_EXPERT),
      w_down.reshape(L * E, D_EXPERT, D), b_down.reshape(L * E, 1, D))


def _combine_kernel(pstart_ref, e_ref, rk_ref, y_ref, gt_ref, x_ref, gate_ref, lg_ref, lb_ref,
                    o_ref, ybuf, sem, *, tc, alpha):
    def row_copy(src, k, t):
        return pltpu.make_async_copy(y_ref.at[pl.ds(src, 1), :], ybuf.at[k, pl.ds(t, 1), :], sem)

    def start(n, c):
        row_copy(pstart_ref[e_ref[n]] + rk_ref[n], n & (TOP_K - 1),
                 lax.shift_right_logical(n, TOPK_SHIFT)).start()
        return c

    def wait(n, c):
        row_copy(0, 0, 0).wait()
        return c

    lax.fori_loop(0, tc * TOP_K, start, 0, unroll=8)
    lax.fori_loop(0, tc * TOP_K, wait, 0, unroll=8)
    gt = gt_ref[...]
    y = ybuf[0] * gt[:, 0:1]
    for k in range(1, TOP_K):
        y = y + ybuf[k] * gt[:, k:k + 1]
    r = alpha * x_ref[...] + (1.0 + gate_ref[0]) * y
    o_ref[...] = _layer_norm(r, lg_ref[...], lb_ref[...])


def _combine(y_rows, e_flat, rk_flat, pstart, gates, x1, gate2, ln_g, ln_b, B, S, tc, alpha):
    T, D = x1.shape
    nt = S // tc
    n = tc * TOP_K
    grid_spec = pltpu.PrefetchScalarGridSpec(
        num_scalar_prefetch=1,
        grid=(T // tc,),
        in_specs=[pl.BlockSpec((n,), lambda i, *_: (i,), memory_space=pltpu.SMEM),
                  pl.BlockSpec((n,), lambda i, *_: (i,), memory_space=pltpu.SMEM),
                  pl.BlockSpec(memory_space=pl.ANY),
                  pl.BlockSpec((tc, LANES), lambda i, *_: (i, 0)),
                  pl.BlockSpec((tc, D), lambda i, *_: (i, 0)),
                  pl.BlockSpec((1, 1, D), lambda i, *_: (i // nt, 0, 0)),
                  pl.BlockSpec((1, D), lambda i, *_: (0, 0)),
                  pl.BlockSpec((1, D), lambda i, *_: (0, 0))],
        out_specs=pl.BlockSpec((tc, D), lambda i, *_: (i, 0)),
        scratch_shapes=[pltpu.VMEM((TOP_K, tc, D), F32), pltpu.SemaphoreType.DMA],
    )
    return pl.pallas_call(
        functools.partial(_combine_kernel, tc=tc, alpha=alpha),
        grid_spec=grid_spec,
        out_shape=jax.ShapeDtypeStruct((T, D), F32),
        compiler_params=_cparams(("arbitrary",)),
        name="moe_combine",
    )(pstart, e_flat, rk_flat, y_rows, gates, x1, gate2, ln_g, ln_b)


def _tile(n, pref):
    t = min(n, pref)
    assert n % t == 0, (n, t)
    return t


def kernel(x, c, w_ada, b_ada, w_in, b_in, gla_w_a2, gla_b_a, gla_norm_g, pool_w, pool_scale, w_out,
           ln1_g, ln1_b, w_router, b_router, w_gate_up, b_gate_up, w_down, b_down, ln2_g, ln2_b):
    B, S, D = x.shape
    L = w_ada.shape[0]
    T = B * S
    assert D == D_MODEL and S % GLA_CHUNK == 0
    alpha = float((2 * L) ** 0.25)
    n_blocks = -(-(T * TOP_K) // MOE_BLOCK) + N_EXPERTS
    rows = n_blocks * MOE_BLOCK

    mod = _ada_mod(c, w_ada, b_ada)
    x2d = x.reshape(T, D)
    for l in range(L):
        shift1, scale1, gate1, shift2, scale2, gate2 = [mod[l, :, m] for m in range(N_MOD)]
        prep = _prep_inproj(w_in[l], b_in[l])
        zq, zk, zv, zg = _inproj(x2d, scale1, shift1, prep, B, S, _tile(S, 512))
        q2, k2 = _fgate(zq, zk, zg, B, S, _tile(S, 512))
        o_fox = _attention(q2, k2, zv, B, S, _tile(S, 512))
        wa = jnp.pad(_pad_heads(gla_w_a2[l], GLA_HEADS, GLA_DK),
                     ((0, LANES - GLA_GATE_RANK), (0, 0))).astype(BF16)
        ba = _pad_heads(gla_b_a[l][None, :], GLA_HEADS, GLA_DK)
        gn = _pad_heads(gla_norm_g[l][None, :], GLA_HEADS, GLA_DV)
        o_gla = _gla(zg, wa, ba, gn, B, S, _tile(S, 256))
        w_bd = jnp.zeros((POOL_WIDTH, POOL_WIDTH), F32)
        for g in range(len(POOL_WINDOWS)):
            sl = slice(g * POOL_GROUP, (g + 1) * POOL_GROUP)
            w_bd = w_bd.at[sl, sl].set(pool_w[l, g])
        o_pool = _pool(zg, w_bd.astype(BF16), pool_scale[l][None, :], B, S, _tile(S, 512))
        wo = w_out[l]
        wf = _pad_heads(wo[:FOX_WIDTH].T, FOX_HEADS, FOX_HEAD_DIM).T.astype(BF16)
        wgl = _pad_heads(wo[FOX_WIDTH:FOX_WIDTH + GLA_VWIDTH].T, GLA_HEADS, GLA_DV).T.astype(BF16)
        wp = wo[FOX_WIDTH + GLA_VWIDTH:].astype(BF16)
        wr = jnp.pad(w_router[l], ((0, 0), (0, LANES - N_EXPERTS)))
        br = jnp.pad(b_router[l][None, :], ((0, 0), (0, LANES - N_EXPERTS)), constant_values=NEG_BIG)
        x1, h2, e_pad, g_pad, r_pad, cnt = _outproj(
            o_fox, o_gla, o_pool, x2d, wf, wgl, wp, gate1, ln1_g[l][None, :], ln1_b[l][None, :],
            scale2, shift2, wr, br, B, S, _tile(S, 512), alpha)
        counts = cnt[0, :N_EXPERTS]
        padded = (counts + MOE_BLOCK - 1) // MOE_BLOCK * MOE_BLOCK
        pend = jnp.cumsum(padded)
        pstart = (pend - padded).astype(jnp.int32)
        n_used = (pend[-1] // MOE_BLOCK).astype(jnp.int32)
        blk_start = jnp.arange(n_blocks, dtype=jnp.int32) * MOE_BLOCK
        be = jnp.minimum(jnp.searchsorted(pend, blk_start, side='right'), N_EXPERTS - 1).astype(jnp.int32)
        be = jnp.where(jnp.arange(n_blocks) < n_used, be, be[jnp.maximum(n_used - 1, 0)])
        zblk = jnp.where(padded > 0, pend // MOE_BLOCK - 1, -1).astype(jnp.int32)
        e_flat = e_pad[:, :TOP_K].reshape(-1)
        rk_flat = r_pad[:, :TOP_K].reshape(-1)
        n_used = n_used.reshape(1)
        x_rows = _dispatch(h2, e_flat, rk_flat, pstart, zblk, n_used, rows, _tile(T, 256))
        y_rows = _experts(x_rows, be, n_used, w_gate_up, b_gate_up, w_down, b_down, l)
        x2d = _combine(y_rows, e_flat, rk_flat, pstart, g_pad, x1, gate2,
                       ln2_g[l][None, :], ln2_b[l][None, :], B, S, _tile(S, 256), alpha)
    return x2d.reshape(B, S, D)
```

```python
import functools

import numpy as np
import jax
import jax.numpy as jnp
from jax import lax
from jax.experimental import pallas as pl
from jax.experimental.pallas import tpu as pltpu

F32 = jnp.float32
BF16 = jnp.bfloat16

D_MODEL = 1024
FOX_HEADS = 6
FOX_HEAD_DIM = 64
FOX_WIDTH = FOX_HEADS * FOX_HEAD_DIM
GLA_HEADS = 4
GLA_DV = 96
GLA_DK = 48
GLA_KWIDTH = GLA_HEADS * GLA_DK
GLA_VWIDTH = GLA_HEADS * GLA_DV
GLA_GATE_RANK = 16
GLA_GATE_TAU = 16.0
GLA_CHUNK = 64
POOL_WINDOWS = (2, 4, 8, 16)
POOL_GROUP = 64
POOL_WIDTH = len(POOL_WINDOWS) * POOL_GROUP
N_EXPERTS = 32
TOP_K = 4
D_EXPERT = 1024
SWIGLU_ALPHA = 1.702
SWIGLU_LIMIT = 7.0
N_MOD = 6
LN_EPS = 1e-5
RMS_EPS = 1e-6

LANES = 128
VMEM_LIMIT_BYTES = 56 * 1024 * 1024

FOX_PAD = FOX_HEADS * LANES
GLA_PAD = GLA_HEADS * LANES
BIAS_LANE = FOX_HEAD_DIM
G_OFF_Q, G_OFF_K, G_OFF_V, G_OFF_R = 0, GLA_PAD, 2 * GLA_PAD, 3 * GLA_PAD
G_OFF_U = 4 * GLA_PAD
G_OFF_A = G_OFF_U + POOL_WIDTH
G_OFF_F = G_OFF_A + LANES
G_WIDTH = G_OFF_F + LANES

STAT_ROWS = 8
PRUNE_MARGIN = 105.0
NORM_SLACK = 1.01
ATTN_BLOCK = 512
MOE_BLOCK = 512
TOPK_SHIFT = TOP_K.bit_length() - 1
assert 1 << TOPK_SHIFT == TOP_K
NEG_BIG = -1e30


def _cparams(sem, vmem=None):
    return pltpu.CompilerParams(dimension_semantics=sem, vmem_limit_bytes=vmem or VMEM_LIMIT_BYTES)


def _log_sigmoid(x):
    return jnp.minimum(x, 0.0) - jnp.log1p(jnp.exp(-jnp.abs(x)))


def _split3(x):
    hi = x.astype(BF16)
    r = x - hi.astype(F32)
    mid = r.astype(BF16)
    lo = (r - mid.astype(F32)).astype(BF16)
    return hi, mid, lo


def _dot(a, b):
    return jnp.dot(a, b, preferred_element_type=F32)


def _dot_nt(a, b):
    return lax.dot_general(a, b, (((1,), (1,)), ((), ())), preferred_element_type=F32)


def _dot_tn(a, b):
    return lax.dot_general(a, b, (((0,), (0,)), ((), ())), preferred_element_type=F32)


def _ada_kernel(c_ref, w_ref, b_ref, o_ref):
    c = c_ref[...]
    cond = c * jax.nn.sigmoid(c)
    o_ref[0] = jnp.dot(cond, w_ref[0], preferred_element_type=F32,
                       precision=lax.Precision.HIGHEST) + b_ref[0]


def _ada_mod(c, w_ada, b_ada):
    L, D, N = w_ada.shape
    B = c.shape[0]
    rows = 8
    c_pad = jnp.zeros((rows, D), F32).at[:B].set(c)
    tn = 1536
    out = pl.pallas_call(
        _ada_kernel,
        grid=(L, N // tn),
        in_specs=[pl.BlockSpec((rows, D), lambda l, j: (0, 0)),
                  pl.BlockSpec((1, D, tn), lambda l, j: (l, 0, j)),
                  pl.BlockSpec((1, 1, tn), lambda l, j: (l, 0, j))],
        out_specs=pl.BlockSpec((1, rows, tn), lambda l, j: (l, 0, j)),
        out_shape=jax.ShapeDtypeStruct((L, rows, N), F32),
        compiler_params=_cparams(("arbitrary", "arbitrary")),
        name="ada_mod",
    )(c_pad, w_ada, b_ada.reshape(L, 1, N))
    return out[:, :B].reshape(L, B, N_MOD, 1, D)


def _inproj_kernel(x_ref, sc_ref, sh_ref, wq_ref, wk_ref, wv_ref, wg_ref,
                   bq_ref, bk_ref, bv_ref, bg_ref, q_ref, k_ref, v_ref, g_ref):
    h = (x_ref[...] * (1.0 + sc_ref[0]) + sh_ref[0]).astype(BF16)
    q_ref[...] = (_dot(h, wq_ref[...]) + bq_ref[...]).astype(BF16)
    k_ref[...] = (_dot(h, wk_ref[...]) + bk_ref[...]).astype(BF16)
    v_ref[...] = (_dot(h, wv_ref[...]) + bv_ref[...]).astype(BF16)
    g_ref[...] = _dot(h, wg_ref[...]) + bg_ref[...]


def _pad_heads(w, heads, dim):
    lead = w.shape[:-1]
    w = w.reshape(lead + (heads, dim))
    w = jnp.pad(w, [(0, 0)] * len(lead) + [(0, 0), (0, LANES - dim)])
    return w.reshape(lead + (heads * LANES,))


def _prep_inproj(w_in, b_in):
    W = jnp.concatenate([w_in, b_in[None, :]], axis=0)
    o = 0
    fq = W[:, o:o + FOX_WIDTH]; o += FOX_WIDTH
    fk = W[:, o:o + FOX_WIDTH]; o += FOX_WIDTH
    fv = W[:, o:o + FOX_WIDTH]; o += FOX_WIDTH
    ff = W[:, o:o + FOX_HEADS]; o += FOX_HEADS
    gq = W[:, o:o + GLA_KWIDTH]; o += GLA_KWIDTH
    gk = W[:, o:o + GLA_KWIDTH]; o += GLA_KWIDTH
    gv = W[:, o:o + GLA_VWIDTH]; o += GLA_VWIDTH
    gr = W[:, o:o + GLA_VWIDTH]; o += GLA_VWIDTH
    ga = W[:, o:o + GLA_GATE_RANK]; o += GLA_GATE_RANK
    pu = W[:, o:o + POOL_WIDTH]
    wq = _pad_heads(fq * (FOX_HEAD_DIM ** -0.5), FOX_HEADS, FOX_HEAD_DIM)
    wk = _pad_heads(fk, FOX_HEADS, FOX_HEAD_DIM)
    wv = _pad_heads(fv, FOX_HEADS, FOX_HEAD_DIM)
    ones_col = jnp.zeros((FOX_PAD,), F32).at[jnp.arange(FOX_HEADS) * LANES + BIAS_LANE].set(1.0)
    wv = wv.at[-1].add(ones_col)
    wg = jnp.concatenate([
        _pad_heads(gq, GLA_HEADS, GLA_DK), _pad_heads(gk, GLA_HEADS, GLA_DK),
        _pad_heads(gv, GLA_HEADS, GLA_DV), _pad_heads(gr, GLA_HEADS, GLA_DV),
        pu, jnp.pad(ga, ((0, 0), (0, LANES - GLA_GATE_RANK))),
        jnp.pad(ff, ((0, 0), (0, LANES - FOX_HEADS)))], axis=1)
    outs = []
    for w in (wq, wk, wv, wg):
        outs.append((w[:-1].astype(BF16), w[-1:].astype(F32)))
    return outs


def _inproj(x2d, scale, shift, prep, B, S, tm):
    T, D = x2d.shape
    (wq, bq), (wk, bk), (wv, bv), (wg, bg) = prep
    nt = S // tm
    full = lambda a: pl.BlockSpec(a.shape, lambda i: (0,) * a.ndim)
    row = lambda w: pl.BlockSpec((tm, w), lambda i: (i, 0))
    mod = pl.BlockSpec((1, 1, D), lambda i: (i // nt, 0, 0))
    return pl.pallas_call(
        _inproj_kernel,
        grid=(T // tm,),
        in_specs=[row(D), mod, mod, full(wq), full(wk), full(wv), full(wg),
                  full(bq), full(bk), full(bv), full(bg)],
        out_specs=[row(FOX_PAD), row(FOX_PAD), row(FOX_PAD), row(G_WIDTH)],
        out_shape=[jax.ShapeDtypeStruct((T, FOX_PAD), BF16)] * 3
                  + [jax.ShapeDtypeStruct((T, G_WIDTH), F32)],
        compiler_params=_cparams(("arbitrary",)),
        name="inproj",
    )(x2d, scale, shift, wq, wk, wv, wg, bq, bk, bv, bg)


def _aug_constants():
    pq = np.zeros((3 * LANES, FOX_PAD), np.float32)
    pk = np.zeros((3 * LANES, FOX_PAD), np.float32)
    cq = np.zeros((1, FOX_PAD), np.float32)
    ck = np.zeros((1, FOX_PAD), np.float32)
    for h in range(FOX_HEADS):
        base = h * LANES + BIAS_LANE
        for p in range(3):
            pq[p * LANES + h, base + p] = 1.0
            pk[p * LANES + h, base + 3 + p] = -1.0
            cq[0, base + 3 + p] = 1.0
            ck[0, base + p] = 1.0
    return pq, pk, cq, ck


def _fgate_kernel(q_ref, k_ref, v_ref, f_ref, pq_ref, pk_ref, cq_ref, ck_ref,
                  q2_ref, k2_ref, vt_ref, st_ref, carry):
    @pl.when(pl.program_id(1) == 0)
    def _():
        carry[...] = jnp.zeros_like(carry)

    vf = v_ref[...].astype(F32)
    for h in range(FOX_HEADS):
        sl = slice(h * LANES, (h + 1) * LANES)
        vt_ref[0, 0, sl, :] = vf[:, sl].T.astype(BF16)

    tf = f_ref.shape[0]
    ls = _log_sigmoid(f_ref[...])
    r = lax.broadcasted_iota(jnp.int32, (tf, tf), 0)
    c = lax.broadcasted_iota(jnp.int32, (tf, tf), 1)
    tri = (c <= r).astype(BF16)
    hi, mid, lo = _split3(ls)
    cs = _dot(tri, hi) + _dot(tri, mid) + _dot(tri, lo)
    F = cs + carry[...]
    carry[...] = F[tf - 1:tf, :]
    fh, fm, fl = _split3(F)
    f3 = jnp.concatenate([fh, fm, fl], axis=1)
    qf = q_ref[...].astype(F32)
    kf = k_ref[...].astype(F32)
    q2_ref[...] = (qf + _dot(f3, pq_ref[...]) + cq_ref[...]).astype(BF16)
    k2_ref[...] = (kf + _dot(f3, pk_ref[...]) + ck_ref[...]).astype(BF16)
    lane = lax.broadcasted_iota(jnp.int32, (1, LANES), 1)
    qstat = jnp.zeros((1, LANES), F32)
    kstat = jnp.zeros((1, LANES), F32)
    for h in range(FOX_HEADS):
        sl = slice(h * LANES, (h + 1) * LANES)
        qm = jnp.max(jnp.sum(qf[:, sl] * qf[:, sl], axis=-1, keepdims=True), axis=0, keepdims=True)
        km = jnp.max(jnp.sum(kf[:, sl] * kf[:, sl], axis=-1, keepdims=True), axis=0, keepdims=True)
        qstat = jnp.where(lane == h, qm, qstat)
        kstat = jnp.where(lane == h, km, kstat)
    row = lax.broadcasted_iota(jnp.int32, (STAT_ROWS, LANES), 0)
    st_ref[...] = jnp.where(row == 0, F[0:1, :],
                            jnp.where(row == 1, F[tf - 1:tf, :],
                                      jnp.where(row == 2, qstat, jnp.where(row == 3, kstat, 0.0))))


def _fgate(zq, zk, zv, zg, B, S, tf):
    T = zq.shape[0]
    nt = S // tf
    pq, pk, cq, ck = _aug_constants()
    pq, pk = jnp.asarray(pq, BF16), jnp.asarray(pk, BF16)
    cq, ck = jnp.asarray(cq), jnp.asarray(ck)
    full = lambda a: pl.BlockSpec(a.shape, lambda b, i: (0,) * a.ndim)
    row = pl.BlockSpec((tf, FOX_PAD), lambda b, i: (b * nt + i, 0))
    return pl.pallas_call(
        _fgate_kernel,
        grid=(B, nt),
        in_specs=[row, row, row, pl.BlockSpec((tf, LANES), lambda b, i: (b * nt + i, G_OFF_F // LANES)),
                  full(pq), full(pk), full(cq), full(ck)],
        out_specs=[row, row, pl.BlockSpec((1, 1, FOX_PAD, tf), lambda b, i: (b, i, 0, 0)),
                   pl.BlockSpec((STAT_ROWS, LANES), lambda b, i: (b * nt + i, 0))],
        out_shape=[jax.ShapeDtypeStruct((T, FOX_PAD), BF16)] * 2
                  + [jax.ShapeDtypeStruct((B, nt, FOX_PAD, tf), BF16),
                     jax.ShapeDtypeStruct((B * nt * STAT_ROWS, LANES), F32)],
        scratch_shapes=[pltpu.VMEM((1, LANES), F32)],
        compiler_params=_cparams(("arbitrary", "arbitrary")),
        name="fgate",
    )(zq, zk, zv, zg, pq, pk, cq, ck)


def _attn_kernel(ff_ref, fl_ref, qn_ref, kn_ref, q_ref, k_ref, vt_ref, o_ref, m_sc, acc_sc, s_sc,
                 *, blk, nq, nb):
    h = pl.program_id(0)
    i = pl.program_id(1)

    def scores(slot, j):
        off = pl.multiple_of(j * blk, blk)
        for b in range(nb):
            s_sc[slot, b] = _dot_nt(k_ref[b, pl.ds(off, blk), :], q_ref[b])

    def softmax_pv(slot, j, diag):
        for b in range(nb):
            s = s_sc[slot, b]
            if diag:
                r = lax.broadcasted_iota(jnp.int32, (blk, blk), 0)
                c = lax.broadcasted_iota(jnp.int32, (blk, blk), 1)
                s = jnp.where(r <= c, s, NEG_BIG)
            m_prev = m_sc[b]
            m_new = jnp.maximum(m_prev, jnp.max(s, axis=0, keepdims=True))
            alpha = jnp.exp(m_prev - m_new)
            p = jnp.exp(s - m_new)
            acc_sc[b] = alpha * acc_sc[b] + _dot(vt_ref[b, j], p.astype(BF16))
            m_sc[b] = m_new

    m_sc[...] = jnp.full_like(m_sc, NEG_BIG)
    acc_sc[...] = jnp.zeros_like(acc_sc)
    scores(0, i)
    scores(1, jnp.maximum(i - 1, 0))
    softmax_pv(0, i, True)

    n = jnp.int32(0)
    for b in range(nb):
        base = (b * FOX_HEADS + h) * nq
        slack = (qn_ref[base + i] * kn_ref[b * FOX_HEADS + h] + ff_ref[base + i]
                 - jnp.min(m_sc[b]) + PRUNE_MARGIN)

        def cond(t, base=base, slack=slack):
            return jnp.logical_and(t < i, slack - fl_ref[base + jnp.maximum(i - 1 - t, 0)] >= 0.0)

        n = jnp.maximum(n, lax.while_loop(cond, lambda t: t + 1, jnp.int32(0)))

    def pair(u, carry):
        t = 1 + 2 * u
        ja = i - t
        scores(0, jnp.maximum(ja - 1, 0))
        softmax_pv(1, ja, False)

        @pl.when(t + 1 <= n)
        def _():
            scores(1, jnp.maximum(ja - 2, 0))
            softmax_pv(0, ja - 1, False)

        return carry

    lax.fori_loop(0, lax.shift_right_logical(n + 1, 1), pair, 0)
    for b in range(nb):
        acc = acc_sc[b]
        o_ref[b] = (acc / acc[BIAS_LANE:BIAS_LANE + 1, :]).T.astype(o_ref.dtype)


def _attention(q2, k2, vt, stats, B, S, blk):
    T = q2.shape[0]
    nq = S // blk
    H = FOX_HEADS
    st = stats.reshape(B, nq, STAT_ROWS, LANES)[:, :, :, :H]
    tab = lambda r: jnp.transpose(st[:, :, r, :], (0, 2, 1)).reshape(-1)
    ffirst, flast = tab(0), tab(1)
    qn = jnp.sqrt(tab(2)) * NORM_SLACK
    kn = jnp.sqrt(jnp.max(st[:, :, 3, :], axis=1)).reshape(-1) * NORM_SLACK
    r3 = lambda a: a.reshape(B, S, FOX_PAD)
    qspec = pl.BlockSpec((B, blk, LANES), lambda h, i, *_: (0, i, h))
    kspec = pl.BlockSpec((B, S, LANES), lambda h, i, *_: (0, 0, h))
    vtspec = pl.BlockSpec((B, nq, LANES, blk), lambda h, i, *_: (0, 0, h, 0))
    grid_spec = pltpu.PrefetchScalarGridSpec(
        num_scalar_prefetch=4,
        grid=(H, nq),
        in_specs=[qspec, kspec, vtspec],
        out_specs=qspec,
        scratch_shapes=[pltpu.VMEM((B, 1, blk), F32), pltpu.VMEM((B, LANES, blk), F32),
                        pltpu.VMEM((2, B, blk, blk), F32)],
    )
    out = pl.pallas_call(
        functools.partial(_attn_kernel, blk=blk, nq=nq, nb=B),
        grid_spec=grid_spec,
        out_shape=jax.ShapeDtypeStruct((B, S, FOX_PAD), BF16),
        compiler_params=_cparams(("arbitrary", "arbitrary")),
        name="fox_attention",
    )(ffirst, flast, qn, kn, r3(q2), r3(k2), vt)
    return out.reshape(T, FOX_PAD)


def _gla_kernel(q_ref, k_ref, v_ref, r_ref, a_ref, wa_ref, ba_ref, g_ref, o_ref, st_sc, *, tg):
    @pl.when(pl.program_id(1) == 0)
    def _():
        st_sc[...] = jnp.zeros_like(st_sc)

    C = GLA_CHUNK
    nchunk = tg // C
    la = _log_sigmoid(_dot(a_ref[...].astype(BF16), wa_ref[...]) + ba_ref[...]) * (1.0 / GLA_GATE_TAU)
    r = lax.broadcasted_iota(jnp.int32, (tg, tg), 0)
    c = lax.broadcasted_iota(jnp.int32, (tg, tg), 1)
    shift = C.bit_length() - 1
    tri = ((c <= r) & ((c >> shift) == (r >> shift))).astype(BF16)
    hi, mid, lo = _split3(la)
    b = _dot(tri, hi) + _dot(tri, mid) + _dot(tri, lo)
    eb = jnp.exp(b)
    q_in = q_ref[...] * (GLA_DK ** -0.5) * eb
    k_in = k_ref[...] * jnp.exp(-b)
    v = v_ref[...]
    rc = lax.broadcasted_iota(jnp.int32, (C, C), 0)
    cc = lax.broadcasted_iota(jnp.int32, (C, C), 1)
    causal = cc <= rc
    lane = lax.broadcasted_iota(jnp.int32, (1, LANES), 1)
    vmask = (lane < GLA_DV).astype(F32)
    outs = []
    for ci in range(nchunk):
        rows = slice(ci * C, (ci + 1) * C)
        b_last = b[ci * C + C - 1:ci * C + C, :]
        k_out = k_ref[rows, :] * jnp.exp(b_last - b[rows, :])
        dec = jnp.exp(b_last)
        heads = []
        for h in range(GLA_HEADS):
            ln = slice(h * LANES, (h + 1) * LANES)
            qh = q_in[rows, ln].astype(BF16)
            kh = k_in[rows, ln].astype(BF16)
            vh = v[rows, ln].astype(BF16)
            attn = jnp.where(causal, _dot_nt(qh, kh), 0.0)
            st = st_sc[h]
            o = _dot(attn.astype(BF16), vh) + _dot_nt(qh, st.astype(BF16))
            kv_t = _dot_tn(vh, k_out[:, ln].astype(BF16))
            st_sc[h] = st * dec[:, ln] + kv_t
            ms = jnp.sum(o * o, axis=-1, keepdims=True) * (1.0 / GLA_DV)
            heads.append(o * lax.rsqrt(ms + RMS_EPS) * vmask)
        outs.append(jnp.concatenate(heads, axis=1))
    o_all = jnp.concatenate(outs, axis=0)
    gr = r_ref[...]
    o_ref[...] = ((o_all * g_ref[...]) * (gr * jax.nn.sigmoid(gr))).astype(o_ref.dtype)


def _gla(zg, wa, ba, gn, B, S, tg):
    T = zg.shape[0]
    nt = S // tg
    col = lambda off, w: pl.BlockSpec((tg, w), lambda b, i: (b * nt + i, off // w))
    full = lambda a: pl.BlockSpec(a.shape, lambda b, i: (0,) * a.ndim)
    return pl.pallas_call(
        functools.partial(_gla_kernel, tg=tg),
        grid=(B, nt),
        in_specs=[col(G_OFF_Q, GLA_PAD), col(G_OFF_K, GLA_PAD), col(G_OFF_V, GLA_PAD),
                  col(G_OFF_R, GLA_PAD), col(G_OFF_A, LANES), full(wa), full(ba), full(gn)],
        out_specs=pl.BlockSpec((tg, GLA_PAD), lambda b, i: (b * nt + i, 0)),
        out_shape=jax.ShapeDtypeStruct((T, GLA_PAD), BF16),
        scratch_shapes=[pltpu.VMEM((GLA_HEADS, LANES, LANES), F32)],
        compiler_params=_cparams(("arbitrary", "arbitrary")),
        name="gla",
    )(zg, zg, zg, zg, zg, wa, ba, gn)


HALO = max(POOL_WINDOWS)


def _pool_kernel(u_ref, w_ref, s_ref, o_ref, xx):
    tp = u_ref.shape[0]
    i = pl.program_id(1)

    @pl.when(i == 0)
    def _():
        xx[0:HALO, :] = jnp.zeros((HALO, POOL_WIDTH), F32)

    @pl.when(i > 0)
    def _():
        xx[0:HALO, :] = xx[tp:tp + HALO, :]

    u = u_ref[...]
    xx[HALO:HALO + tp, :] = u
    lane = lax.broadcasted_iota(jnp.int32, (1, POOL_WIDTH), 1)
    grp = lane >> (POOL_GROUP.bit_length() - 1)
    pos = lax.broadcasted_iota(jnp.int32, (tp, 1), 0) + i * tp + 1
    acc = u
    pooled = jnp.zeros_like(u)
    for j in range(1, HALO):
        acc = acc + xx[HALO - j:HALO - j + tp, :]
        w = j + 1
        if w in POOL_WINDOWS:
            g = POOL_WINDOWS.index(w)
            cnt = jnp.minimum(pos, w).astype(F32)
            pooled = jnp.where(grp == g, acc / cnt - u, pooled)
    mixed = _dot(pooled.astype(BF16), w_ref[...])
    o_ref[...] = (mixed * s_ref[...]).astype(o_ref.dtype)


def _pool(zg, w_bd, scale, B, S, tp):
    T = zg.shape[0]
    nt = S // tp
    full = lambda a: pl.BlockSpec(a.shape, lambda b, i: (0,) * a.ndim)
    return pl.pallas_call(
        _pool_kernel,
        grid=(B, nt),
        in_specs=[pl.BlockSpec((tp, POOL_WIDTH), lambda b, i: (b * nt + i, G_OFF_U // POOL_WIDTH)),
                  full(w_bd), full(scale)],
        out_specs=pl.BlockSpec((tp, POOL_WIDTH), lambda b, i: (b * nt + i, 0)),
        out_shape=jax.ShapeDtypeStruct((T, POOL_WIDTH), BF16),
        scratch_shapes=[pltpu.VMEM((tp + HALO, POOL_WIDTH), F32)],
        compiler_params=_cparams(("arbitrary", "arbitrary")),
        name="pool",
    )(zg, w_bd, scale)


def _layer_norm(r, g, b):
    mu = jnp.mean(r, axis=-1, keepdims=True)
    d = r - mu
    var = jnp.mean(d * d, axis=-1, keepdims=True)
    return d * lax.rsqrt(var + LN_EPS) * g + b


def _outproj_kernel(of_ref, og_ref, op_ref, x_ref, wf_ref, wg_ref, wp_ref, gate_ref, lg_ref, lb_ref,
                    sc_ref, sh_ref, wr_ref, br_ref,
                    x1_ref, h2_ref, e_ref, gt_ref, rk_ref, cnt_ref, cnt_sc, *, alpha):
    step = pl.program_id(0)

    @pl.when(step == 0)
    def _():
        cnt_sc[...] = jnp.zeros_like(cnt_sc)

    y = _dot(of_ref[...], wf_ref[...]) + _dot(og_ref[...], wg_ref[...]) + _dot(op_ref[...], wp_ref[...])
    r = alpha * x_ref[...] + (1.0 + gate_ref[0]) * y
    x1 = _layer_norm(r, lg_ref[...], lb_ref[...])
    x1_ref[...] = x1
    h2 = x1 * (1.0 + sc_ref[0]) + sh_ref[0]
    h2_ref[...] = h2
    logits = jnp.dot(h2, wr_ref[...], preferred_element_type=F32,
                     precision=lax.Precision.HIGHEST) + br_ref[...]
    tm = logits.shape[0]
    lane_i = lax.broadcasted_iota(jnp.int32, (tm, LANES), 1)
    lane = lane_i.astype(F32)
    work = logits
    tops, idxs = [], []
    onehot = jnp.zeros((tm, LANES), F32)
    for _ in range(TOP_K):
        m = jnp.max(work, axis=-1, keepdims=True)
        idx = jnp.min(jnp.where(work == m, lane, float(LANES)), axis=-1, keepdims=True)
        sel = lane == idx
        onehot = onehot + sel.astype(F32)
        work = jnp.where(sel, -jnp.inf, work)
        tops.append(m)
        idxs.append(idx)
    ex = [jnp.exp(t - tops[0]) for t in tops]
    den = ex[0] + ex[1] + ex[2] + ex[3]
    rr = lax.broadcasted_iota(jnp.int32, (tm, tm), 0)
    cc = lax.broadcasted_iota(jnp.int32, (tm, tm), 1)
    stril = (cc < rr).astype(BF16)
    before = _dot(stril, onehot.astype(BF16)) + cnt_sc[...]
    e_out = jnp.zeros((tm, LANES), jnp.int32)
    g_out = jnp.zeros((tm, LANES), F32)
    r_out = jnp.zeros((tm, LANES), jnp.int32)
    for k in range(TOP_K):
        rank = jnp.sum(jnp.where(lane == idxs[k], before, 0.0), axis=-1, keepdims=True)
        e_out = jnp.where(lane_i == k, idxs[k].astype(jnp.int32), e_out)
        g_out = jnp.where(lane_i == k, ex[k] / den, g_out)
        r_out = jnp.where(lane_i == k, rank.astype(jnp.int32), r_out)
    e_ref[...] = e_out
    gt_ref[...] = g_out
    rk_ref[...] = r_out
    cnt_sc[...] = cnt_sc[...] + jnp.sum(onehot, axis=0, keepdims=True)
    cnt_ref[...] = cnt_sc[...].astype(jnp.int32)


def _outproj(o_fox, o_gla, o_pool, x2d, wf, wg, wp, gate1, ln_g, ln_b, scale2, shift2, wr, br,
             B, S, tm, alpha):
    T, D = x2d.shape
    nt = S // tm
    full = lambda a: pl.BlockSpec(a.shape, lambda i: (0,) * a.ndim)
    row = lambda w: pl.BlockSpec((tm, w), lambda i: (i, 0))
    mod = pl.BlockSpec((1, 1, D), lambda i: (i // nt, 0, 0))
    return pl.pallas_call(
        functools.partial(_outproj_kernel, alpha=alpha),
        grid=(T // tm,),
        in_specs=[row(FOX_PAD), row(GLA_PAD), row(POOL_WIDTH), row(D), full(wf), full(wg), full(wp),
                  mod, full(ln_g), full(ln_b), mod, mod, full(wr), full(br)],
        out_specs=[row(D), row(D), row(LANES), row(LANES), row(LANES),
                   pl.BlockSpec((1, LANES), lambda i: (0, 0))],
        out_shape=[jax.ShapeDtypeStruct((T, D), F32), jax.ShapeDtypeStruct((T, D), F32),
                   jax.ShapeDtypeStruct((T, LANES), jnp.int32), jax.ShapeDtypeStruct((T, LANES), F32),
                   jax.ShapeDtypeStruct((T, LANES), jnp.int32), jax.ShapeDtypeStruct((1, LANES), jnp.int32)],
        scratch_shapes=[pltpu.VMEM((1, LANES), F32)],
        compiler_params=_cparams(("arbitrary",)),
        name="outproj_router",
    )(o_fox, o_gla, o_pool, x2d, wf, wg, wp, gate1, ln_g, ln_b, scale2, shift2, wr, br)


def _dispatch_kernel(pstart_ref, zblk_ref, nu_ref, e_ref, rk_ref, h_ref, xr_ref, zero_sc, sem, zsem,
                     *, td, n_blocks):
    step = pl.program_id(0)

    @pl.when(step == 0)
    def _():
        zero_sc[...] = jnp.zeros_like(zero_sc)

        def zcopy(blk):
            return pltpu.make_async_copy(
                zero_sc, xr_ref.at[pl.ds(pl.multiple_of(blk * MOE_BLOCK, MOE_BLOCK), MOE_BLOCK), :], zsem)

        def zstart(e, c):
            @pl.when(zblk_ref[e] >= 0)
            def _():
                zcopy(zblk_ref[e]).start()
            return c

        def zwait(e, c):
            @pl.when(zblk_ref[e] >= 0)
            def _():
                zcopy(0).wait()
            return c

        def tstart(blk, c):
            zcopy(blk).start()
            return c

        def twait(blk, c):
            zcopy(0).wait()
            return c

        lax.fori_loop(0, N_EXPERTS, zstart, 0)
        lax.fori_loop(nu_ref[0], n_blocks, tstart, 0)
        lax.fori_loop(0, N_EXPERTS, zwait, 0)
        lax.fori_loop(nu_ref[0], n_blocks, twait, 0)

    def row_copy(t, dest):
        return pltpu.make_async_copy(h_ref.at[pl.ds(t, 1), :], xr_ref.at[pl.ds(dest, 1), :], sem)

    def start(n, c):
        row_copy(lax.shift_right_logical(n, TOPK_SHIFT), pstart_ref[e_ref[n]] + rk_ref[n]).start()
        return c

    def wait(n, c):
        row_copy(0, 0).wait()
        return c

    lax.fori_loop(0, td * TOP_K, start, 0, unroll=8)
    lax.fori_loop(0, td * TOP_K, wait, 0, unroll=8)


def _dispatch(h2, e_flat, rk_flat, pstart, zblk, n_used, rows, td):
    T, D = h2.shape
    n = td * TOP_K
    grid_spec = pltpu.PrefetchScalarGridSpec(
        num_scalar_prefetch=3,
        grid=(T // td,),
        in_specs=[pl.BlockSpec((n,), lambda i, *_: (i,), memory_space=pltpu.SMEM),
                  pl.BlockSpec((n,), lambda i, *_: (i,), memory_space=pltpu.SMEM),
                  pl.BlockSpec((td, D), lambda i, *_: (i, 0))],
        out_specs=pl.BlockSpec(memory_space=pl.ANY),
        scratch_shapes=[pltpu.VMEM((MOE_BLOCK, D), F32), pltpu.SemaphoreType.DMA, pltpu.SemaphoreType.DMA],
    )
    return pl.pallas_call(
        functools.partial(_dispatch_kernel, td=td, n_blocks=rows // MOE_BLOCK),
        grid_spec=grid_spec,
        out_shape=jax.ShapeDtypeStruct((rows, D), F32),
        compiler_params=_cparams(("arbitrary",)),
        name="moe_dispatch",
    )(pstart, zblk, n_used, e_flat, rk_flat, h2)


def _expert_kernel(be_ref, nu_ref, x_ref, wgu_ref, bgu_ref, wd_ref, bd_ref, y_ref, wgu_sc, wd_sc):
    i = pl.program_id(0)
    used = i < nu_ref[0]

    @pl.when(used)
    def _():
        prev = be_ref[jnp.maximum(i - 1, 0)]

        @pl.when((i == 0) | (be_ref[i] != prev))
        def _():
            wgu_sc[...] = wgu_ref[0].astype(BF16)
            wd_sc[...] = wd_ref[0].astype(BF16)

        x = x_ref[...].astype(BF16)
        gu = _dot(x, wgu_sc[...]) + bgu_ref[0]
        glu = jnp.minimum(gu[:, :D_EXPERT], SWIGLU_LIMIT)
        lin = jnp.clip(gu[:, D_EXPERT:], -SWIGLU_LIMIT, SWIGLU_LIMIT)
        act = glu * jax.nn.sigmoid(SWIGLU_ALPHA * glu) * (lin + 1.0)
        y_ref[...] = _dot(act.astype(BF16), wd_sc[...]) + bd_ref[0]

    @pl.when(jnp.logical_not(used))
    def _():
        y_ref[...] = jnp.zeros_like(y_ref)


def _experts(x_rows, block_expert, n_used, w_gate_up, b_gate_up, w_down, b_down, layer):
    rows, D = x_rows.shape
    nb = rows // MOE_BLOCK
    E = w_gate_up.shape[1]
    grid_spec = pltpu.PrefetchScalarGridSpec(
        num_scalar_prefetch=2,
        grid=(nb,),
        in_specs=[pl.BlockSpec((MOE_BLOCK, D), lambda i, be, nu: (jnp.minimum(i, nu[0] - 1), 0)),
                  pl.BlockSpec((1, D, 2 * D_EXPERT), lambda i, be, nu: (layer * E + be[i], 0, 0)),
                  pl.BlockSpec((1, 1, 2 * D_EXPERT), lambda i, be, nu: (layer * E + be[i], 0, 0)),
                  pl.BlockSpec((1, D_EXPERT, D), lambda i, be, nu: (layer * E + be[i], 0, 0)),
                  pl.BlockSpec((1, 1, D), lambda i, be, nu: (layer * E + be[i], 0, 0))],
        out_specs=pl.BlockSpec((MOE_BLOCK, D), lambda i, be, nu: (i, 0)),
        scratch_shapes=[pltpu.VMEM((D, 2 * D_EXPERT), BF16), pltpu.VMEM((D_EXPERT, D), BF16)],
    )
    L = w_gate_up.shape[0]
    return pl.pallas_call(
        _expert_kernel,
        grid_spec=grid_spec,
        out_shape=jax.ShapeDtypeStruct((rows, D), F32),
        compiler_params=_cparams(("arbitrary",)),
        name="moe_experts",
    )(block_expert, n_used, x_rows,
      w_gate_up.reshape(L * E, D, 2 * D_EXPERT), b_gate_up.reshape(L * E, 1, 2 * D_EXPERT),
      w_down.reshape(L * E, D_EXPERT, D), b_down.reshape(L * E, 1, D))


def _combine_kernel(pstart_ref, e_ref, rk_ref, y_ref, gt_ref, x_ref, gate_ref, lg_ref, lb_ref,
                    o_ref, ybuf, sem, *, tc, alpha):
    def row_copy(src, k, t):
        return pltpu.make_async_copy(y_ref.at[pl.ds(src, 1), :], ybuf.at[k, pl.ds(t, 1), :], sem)

    def start(n, c):
        row_copy(pstart_ref[e_ref[n]] + rk_ref[n], n & (TOP_K - 1),
                 lax.shift_right_logical(n, TOPK_SHIFT)).start()
        return c

    def wait(n, c):
        row_copy(0, 0, 0).wait()
        return c

    lax.fori_loop(0, tc * TOP_K, start, 0, unroll=8)
    lax.fori_loop(0, tc * TOP_K, wait, 0, unroll=8)
    gt = gt_ref[...]
    y = ybuf[0] * gt[:, 0:1]
    for k in range(1, TOP_K):
        y = y + ybuf[k] * gt[:, k:k + 1]
    r = alpha * x_ref[...] + (1.0 + gate_ref[0]) * y
    o_ref[...] = _layer_norm(r, lg_ref[...], lb_ref[...])


def _combine(y_rows, e_flat, rk_flat, pstart, gates, x1, gate2, ln_g, ln_b, B, S, tc, alpha):
    T, D = x1.shape
    nt = S // tc
    n = tc * TOP_K
    grid_spec = pltpu.PrefetchScalarGridSpec(
        num_scalar_prefetch=1,
        grid=(T // tc,),
        in_specs=[pl.BlockSpec((n,), lambda i, *_: (i,), memory_space=pltpu.SMEM),
                  pl.BlockSpec((n,), lambda i, *_: (i,), memory_space=pltpu.SMEM),
                  pl.BlockSpec(memory_space=pl.ANY),
                  pl.BlockSpec((tc, LANES), lambda i, *_: (i, 0)),
                  pl.BlockSpec((tc, D), lambda i, *_: (i, 0)),
                  pl.BlockSpec((1, 1, D), lambda i, *_: (i // nt, 0, 0)),
                  pl.BlockSpec((1, D), lambda i, *_: (0, 0)),
                  pl.BlockSpec((1, D), lambda i, *_: (0, 0))],
        out_specs=pl.BlockSpec((tc, D), lambda i, *_: (i, 0)),
        scratch_shapes=[pltpu.VMEM((TOP_K, tc, D), F32), pltpu.SemaphoreType.DMA],
    )
    return pl.pallas_call(
        functools.partial(_combine_kernel, tc=tc, alpha=alpha),
        grid_spec=grid_spec,
        out_shape=jax.ShapeDtypeStruct((T, D), F32),
        compiler_params=_cparams(("arbitrary",)),
        name="moe_combine",
    )(pstart, e_flat, rk_flat, y_rows, gates, x1, gate2, ln_g, ln_b)


def _tile(n, pref):
    t = min(n, pref)
    assert n % t == 0, (n, t)
    return t


def kernel(x, c, w_ada, b_ada, w_in, b_in, gla_w_a2, gla_b_a, gla_norm_g, pool_w, pool_scale, w_out,
           ln1_g, ln1_b, w_router, b_router, w_gate_up, b_gate_up, w_down, b_down, ln2_g, ln2_b):
    B, S, D = x.shape
    L = w_ada.shape[0]
    T = B * S
    assert D == D_MODEL and S % GLA_CHUNK == 0
    alpha = float((2 * L) ** 0.25)
    n_blocks = -(-(T * TOP_K) // MOE_BLOCK) + N_EXPERTS
    rows = n_blocks * MOE_BLOCK

    mod = _ada_mod(c, w_ada, b_ada)
    x2d = x.reshape(T, D)
    for l in range(L):
        shift1, scale1, gate1, shift2, scale2, gate2 = [mod[l, :, m] for m in range(N_MOD)]
        prep = _prep_inproj(w_in[l], b_in[l])
        zq, zk, zv, zg = _inproj(x2d, scale1, shift1, prep, B, S, _tile(S, 512))
        q2, k2, vt, stats = _fgate(zq, zk, zv, zg, B, S, _tile(S, ATTN_BLOCK))
        o_fox = _attention(q2, k2, vt, stats, B, S, _tile(S, ATTN_BLOCK))
        wa = jnp.pad(_pad_heads(gla_w_a2[l], GLA_HEADS, GLA_DK),
                     ((0, LANES - GLA_GATE_RANK), (0, 0))).astype(BF16)
        ba = _pad_heads(gla_b_a[l][None, :], GLA_HEADS, GLA_DK)
        gn = _pad_heads(gla_norm_g[l][None, :], GLA_HEADS, GLA_DV)
        o_gla = _gla(zg, wa, ba, gn, B, S, _tile(S, 256))
        w_bd = jnp.zeros((POOL_WIDTH, POOL_WIDTH), F32)
        for g in range(len(POOL_WINDOWS)):
            sl = slice(g * POOL_GROUP, (g + 1) * POOL_GROUP)
            w_bd = w_bd.at[sl, sl].set(pool_w[l, g])
        o_pool = _pool(zg, w_bd.astype(BF16), pool_scale[l][None, :], B, S, _tile(S, 512))
        wo = w_out[l]
        wf = _pad_heads(wo[:FOX_WIDTH].T, FOX_HEADS, FOX_HEAD_DIM).T.astype(BF16)
        wgl = _pad_heads(wo[FOX_WIDTH:FOX_WIDTH + GLA_VWIDTH].T, GLA_HEADS, GLA_DV).T.astype(BF16)
        wp = wo[FOX_WIDTH + GLA_VWIDTH:].astype(BF16)
        wr = jnp.pad(w_router[l], ((0, 0), (0, LANES - N_EXPERTS)))
        br = jnp.pad(b_router[l][None, :], ((0, 0), (0, LANES - N_EXPERTS)), constant_values=NEG_BIG)
        x1, h2, e_pad, g_pad, r_pad, cnt = _outproj(
            o_fox, o_gla, o_pool, x2d, wf, wgl, wp, gate1, ln1_g[l][None, :], ln1_b[l][None, :],
            scale2, shift2, wr, br, B, S, _tile(S, 512), alpha)
        counts = cnt[0, :N_EXPERTS]
        padded = (counts + MOE_BLOCK - 1) // MOE_BLOCK * MOE_BLOCK
        pend = jnp.cumsum(padded)
        pstart = (pend - padded).astype(jnp.int32)
        n_used = (pend[-1] // MOE_BLOCK).astype(jnp.int32)
        blk_start = jnp.arange(n_blocks, dtype=jnp.int32) * MOE_BLOCK
        be = jnp.minimum(jnp.sum(blk_start[:, None] >= pend[None, :], axis=1), N_EXPERTS - 1).astype(jnp.int32)
        be = jnp.where(jnp.arange(n_blocks) < n_used, be, be[jnp.maximum(n_used - 1, 0)])
        zblk = jnp.where(padded > 0, pend // MOE_BLOCK - 1, -1).astype(jnp.int32)
        e_flat = e_pad[:, :TOP_K].reshape(-1)
        rk_flat = r_pad[:, :TOP_K].reshape(-1)
        n_used = n_used.reshape(1)
        x_rows = _dispatch(h2, e_flat, rk_flat, pstart, zblk, n_used, rows, _tile(T, 256))
        y_rows = _experts(x_rows, be, n_used, w_gate_up, b_gate_up, w_down, b_down, l)
        x2d = _combine(y_rows, e_flat, rk_flat, pstart, g_pad, x1, gate2,
                       ln2_g[l][None, :], ln2_b[l][None, :], B, S, _tile(S, 256), alpha)
    return x2d.reshape(B, S, D)
```

```python
import functools

import numpy as np
import jax
import jax.numpy as jnp
from jax import lax
from jax.experimental import pallas as pl
from jax.experimental.pallas import tpu as pltpu

F32 = jnp.float32
BF16 = jnp.bfloat16

D_MODEL = 1024
FOX_HEADS = 6
FOX_HEAD_DIM = 64
FOX_WIDTH = FOX_HEADS * FOX_HEAD_DIM
GLA_HEADS = 4
GLA_DV = 96
GLA_DK = 48
GLA_KWIDTH = GLA_HEADS * GLA_DK
GLA_VWIDTH = GLA_HEADS * GLA_DV
GLA_GATE_RANK = 16
GLA_GATE_TAU = 16.0
GLA_CHUNK = 64
POOL_WINDOWS = (2, 4, 8, 16)
POOL_GROUP = 64
POOL_WIDTH = len(POOL_WINDOWS) * POOL_GROUP
N_EXPERTS = 32
TOP_K = 4
D_EXPERT = 1024
SWIGLU_ALPHA = 1.702
SWIGLU_LIMIT = 7.0
N_MOD = 6
LN_EPS = 1e-5
RMS_EPS = 1e-6

LANES = 128
VMEM_LIMIT_BYTES = 56 * 1024 * 1024

FOX_PAD = FOX_HEADS * LANES
GLA_PAD = GLA_HEADS * LANES
BIAS_LANE = FOX_HEAD_DIM
G_OFF_Q, G_OFF_K, G_OFF_V, G_OFF_R = 0, GLA_PAD, 2 * GLA_PAD, 3 * GLA_PAD
G_OFF_U = 4 * GLA_PAD
G_OFF_A = G_OFF_U + POOL_WIDTH
G_OFF_F = G_OFF_A + LANES
G_WIDTH = G_OFF_F + LANES

STAT_ROWS = 8
PRUNE_MARGIN = 105.0
NORM_SLACK = 1.01
ATTN_BLOCK = 512
MOE_BLOCK = 512
ROUTE_TILE = 256
ROW_GROUP = 8
ROUTE_SLOTS = ROUTE_TILE * TOP_K + N_EXPERTS * ROW_GROUP
NEG_BIG = -1e30


def _cparams(sem, vmem=None):
    return pltpu.CompilerParams(dimension_semantics=sem, vmem_limit_bytes=vmem or VMEM_LIMIT_BYTES)


def _log_sigmoid(x):
    return jnp.minimum(x, 0.0) - jnp.log1p(jnp.exp(-jnp.abs(x)))


def _split3(x):
    hi = x.astype(BF16)
    r = x - hi.astype(F32)
    mid = r.astype(BF16)
    lo = (r - mid.astype(F32)).astype(BF16)
    return hi, mid, lo


def _dot(a, b):
    return jnp.dot(a, b, preferred_element_type=F32)


def _dot_nt(a, b):
    return lax.dot_general(a, b, (((1,), (1,)), ((), ())), preferred_element_type=F32)


def _dot_tn(a, b):
    return lax.dot_general(a, b, (((0,), (0,)), ((), ())), preferred_element_type=F32)


def _ada_kernel(c_ref, w_ref, b_ref, o_ref):
    c = c_ref[...]
    cond = c * jax.nn.sigmoid(c)
    o_ref[0] = jnp.dot(cond, w_ref[0], preferred_element_type=F32,
                       precision=lax.Precision.HIGHEST) + b_ref[0]


def _ada_mod(c, w_ada, b_ada):
    L, D, N = w_ada.shape
    B = c.shape[0]
    rows = 8
    c_pad = jnp.zeros((rows, D), F32).at[:B].set(c)
    tn = 1536
    out = pl.pallas_call(
        _ada_kernel,
        grid=(L, N // tn),
        in_specs=[pl.BlockSpec((rows, D), lambda l, j: (0, 0)),
                  pl.BlockSpec((1, D, tn), lambda l, j: (l, 0, j)),
                  pl.BlockSpec((1, 1, tn), lambda l, j: (l, 0, j))],
        out_specs=pl.BlockSpec((1, rows, tn), lambda l, j: (l, 0, j)),
        out_shape=jax.ShapeDtypeStruct((L, rows, N), F32),
        compiler_params=_cparams(("arbitrary", "arbitrary")),
        name="ada_mod",
    )(c_pad, w_ada, b_ada.reshape(L, 1, N))
    return out[:, :B].reshape(L, B, N_MOD, 1, D)


def _inproj_kernel(x_ref, sc_ref, sh_ref, wq_ref, wk_ref, wv_ref, wg_ref,
                   bq_ref, bk_ref, bv_ref, bg_ref, q_ref, k_ref, v_ref, g_ref):
    h = (x_ref[...] * (1.0 + sc_ref[0]) + sh_ref[0]).astype(BF16)
    q_ref[...] = (_dot(h, wq_ref[...]) + bq_ref[...]).astype(BF16)
    k_ref[...] = (_dot(h, wk_ref[...]) + bk_ref[...]).astype(BF16)
    v_ref[...] = (_dot(h, wv_ref[...]) + bv_ref[...]).astype(BF16)
    g_ref[...] = _dot(h, wg_ref[...]) + bg_ref[...]


def _pad_heads(w, heads, dim):
    lead = w.shape[:-1]
    w = w.reshape(lead + (heads, dim))
    w = jnp.pad(w, [(0, 0)] * len(lead) + [(0, 0), (0, LANES - dim)])
    return w.reshape(lead + (heads * LANES,))


def _prep_inproj(w_in, b_in):
    W = jnp.concatenate([w_in, b_in[None, :]], axis=0)
    o = 0
    fq = W[:, o:o + FOX_WIDTH]; o += FOX_WIDTH
    fk = W[:, o:o + FOX_WIDTH]; o += FOX_WIDTH
    fv = W[:, o:o + FOX_WIDTH]; o += FOX_WIDTH
    ff = W[:, o:o + FOX_HEADS]; o += FOX_HEADS
    gq = W[:, o:o + GLA_KWIDTH]; o += GLA_KWIDTH
    gk = W[:, o:o + GLA_KWIDTH]; o += GLA_KWIDTH
    gv = W[:, o:o + GLA_VWIDTH]; o += GLA_VWIDTH
    gr = W[:, o:o + GLA_VWIDTH]; o += GLA_VWIDTH
    ga = W[:, o:o + GLA_GATE_RANK]; o += GLA_GATE_RANK
    pu = W[:, o:o + POOL_WIDTH]
    wq = _pad_heads(fq * (FOX_HEAD_DIM ** -0.5), FOX_HEADS, FOX_HEAD_DIM)
    wk = _pad_heads(fk, FOX_HEADS, FOX_HEAD_DIM)
    wv = _pad_heads(fv, FOX_HEADS, FOX_HEAD_DIM)
    ones_col = jnp.zeros((FOX_PAD,), F32).at[jnp.arange(FOX_HEADS) * LANES + BIAS_LANE].set(1.0)
    wv = wv.at[-1].add(ones_col)
    wg = jnp.concatenate([
        _pad_heads(gq, GLA_HEADS, GLA_DK), _pad_heads(gk, GLA_HEADS, GLA_DK),
        _pad_heads(gv, GLA_HEADS, GLA_DV), _pad_heads(gr, GLA_HEADS, GLA_DV),
        pu, jnp.pad(ga, ((0, 0), (0, LANES - GLA_GATE_RANK))),
        jnp.pad(ff, ((0, 0), (0, LANES - FOX_HEADS)))], axis=1)
    outs = []
    for w in (wq, wk, wv, wg):
        outs.append((w[:-1].astype(BF16), w[-1:].astype(F32)))
    return outs


def _inproj(x2d, scale, shift, prep, B, S, tm):
    T, D = x2d.shape
    (wq, bq), (wk, bk), (wv, bv), (wg, bg) = prep
    nt = S // tm
    full = lambda a: pl.BlockSpec(a.shape, lambda i: (0,) * a.ndim)
    row = lambda w: pl.BlockSpec((tm, w), lambda i: (i, 0))
    mod = pl.BlockSpec((1, 1, D), lambda i: (i // nt, 0, 0))
    return pl.pallas_call(
        _inproj_kernel,
        grid=(T // tm,),
        in_specs=[row(D), mod, mod, full(wq), full(wk), full(wv), full(wg),
                  full(bq), full(bk), full(bv), full(bg)],
        out_specs=[row(FOX_PAD), row(FOX_PAD), row(FOX_PAD), row(G_WIDTH)],
        out_shape=[jax.ShapeDtypeStruct((T, FOX_PAD), BF16)] * 3
                  + [jax.ShapeDtypeStruct((T, G_WIDTH), F32)],
        compiler_params=_cparams(("arbitrary",)),
        name="inproj",
    )(x2d, scale, shift, wq, wk, wv, wg, bq, bk, bv, bg)


def _aug_constants():
    pq = np.zeros((3 * LANES, FOX_PAD), np.float32)
    pk = np.zeros((3 * LANES, FOX_PAD), np.float32)
    cq = np.zeros((1, FOX_PAD), np.float32)
    ck = np.zeros((1, FOX_PAD), np.float32)
    for h in range(FOX_HEADS):
        base = h * LANES + BIAS_LANE
        for p in range(3):
            pq[p * LANES + h, base + p] = 1.0
            pk[p * LANES + h, base + 3 + p] = -1.0
            cq[0, base + 3 + p] = 1.0
            ck[0, base + p] = 1.0
    return pq, pk, cq, ck


def _fgate_kernel(q_ref, k_ref, v_ref, f_ref, pq_ref, pk_ref, cq_ref, ck_ref,
                  q2_ref, k2_ref, vt_ref, st_ref, carry):
    @pl.when(pl.program_id(1) == 0)
    def _():
        carry[...] = jnp.zeros_like(carry)

    vf = v_ref[...].astype(F32)
    for h in range(FOX_HEADS):
        sl = slice(h * LANES, (h + 1) * LANES)
        vt_ref[0, 0, sl, :] = vf[:, sl].T.astype(BF16)

    tf = f_ref.shape[0]
    ls = _log_sigmoid(f_ref[...])
    r = lax.broadcasted_iota(jnp.int32, (tf, tf), 0)
    c = lax.broadcasted_iota(jnp.int32, (tf, tf), 1)
    tri = (c <= r).astype(BF16)
    hi, mid, lo = _split3(ls)
    cs = _dot(tri, hi) + _dot(tri, mid) + _dot(tri, lo)
    F = cs + carry[...]
    carry[...] = F[tf - 1:tf, :]
    fh, fm, fl = _split3(F)
    f3 = jnp.concatenate([fh, fm, fl], axis=1)
    qf = q_ref[...].astype(F32)
    kf = k_ref[...].astype(F32)
    q2_ref[...] = (qf + _dot(f3, pq_ref[...]) + cq_ref[...]).astype(BF16)
    k2_ref[...] = (kf + _dot(f3, pk_ref[...]) + ck_ref[...]).astype(BF16)
    lane = lax.broadcasted_iota(jnp.int32, (1, LANES), 1)
    qstat = jnp.zeros((1, LANES), F32)
    kstat = jnp.zeros((1, LANES), F32)
    for h in range(FOX_HEADS):
        sl = slice(h * LANES, (h + 1) * LANES)
        qm = jnp.max(jnp.sum(qf[:, sl] * qf[:, sl], axis=-1, keepdims=True), axis=0, keepdims=True)
        km = jnp.max(jnp.sum(kf[:, sl] * kf[:, sl], axis=-1, keepdims=True), axis=0, keepdims=True)
        qstat = jnp.where(lane == h, qm, qstat)
        kstat = jnp.where(lane == h, km, kstat)
    row = lax.broadcasted_iota(jnp.int32, (STAT_ROWS, LANES), 0)
    st_ref[...] = jnp.where(row == 0, F[0:1, :],
                            jnp.where(row == 1, F[tf - 1:tf, :],
                                      jnp.where(row == 2, qstat, jnp.where(row == 3, kstat, 0.0))))


def _fgate(zq, zk, zv, zg, B, S, tf):
    T = zq.shape[0]
    nt = S // tf
    pq, pk, cq, ck = _aug_constants()
    pq, pk = jnp.asarray(pq, BF16), jnp.asarray(pk, BF16)
    cq, ck = jnp.asarray(cq), jnp.asarray(ck)
    full = lambda a: pl.BlockSpec(a.shape, lambda b, i: (0,) * a.ndim)
    row = pl.BlockSpec((tf, FOX_PAD), lambda b, i: (b * nt + i, 0))
    return pl.pallas_call(
        _fgate_kernel,
        grid=(B, nt),
        in_specs=[row, row, row, pl.BlockSpec((tf, LANES), lambda b, i: (b * nt + i, G_OFF_F // LANES)),
                  full(pq), full(pk), full(cq), full(ck)],
        out_specs=[row, row, pl.BlockSpec((1, 1, FOX_PAD, tf), lambda b, i: (b, i, 0, 0)),
                   pl.BlockSpec((STAT_ROWS, LANES), lambda b, i: (b * nt + i, 0))],
        out_shape=[jax.ShapeDtypeStruct((T, FOX_PAD), BF16)] * 2
                  + [jax.ShapeDtypeStruct((B, nt, FOX_PAD, tf), BF16),
                     jax.ShapeDtypeStruct((B * nt * STAT_ROWS, LANES), F32)],
        scratch_shapes=[pltpu.VMEM((1, LANES), F32)],
        compiler_params=_cparams(("arbitrary", "arbitrary")),
        name="fgate",
    )(zq, zk, zv, zg, pq, pk, cq, ck)


def _attn_kernel(ff_ref, fl_ref, qn_ref, kn_ref, q_ref, k_ref, vt_ref, o_ref, m_sc, acc_sc, s_sc,
                 *, blk, nq, nb):
    h = pl.program_id(0)
    i = pl.program_id(1)

    def scores(slot, j):
        off = pl.multiple_of(j * blk, blk)
        for b in range(nb):
            s_sc[slot, b] = _dot_nt(k_ref[b, pl.ds(off, blk), :], q_ref[b])

    def softmax_pv(slot, j, diag):
        for b in range(nb):
            s = s_sc[slot, b]
            if diag:
                r = lax.broadcasted_iota(jnp.int32, (blk, blk), 0)
                c = lax.broadcasted_iota(jnp.int32, (blk, blk), 1)
                s = jnp.where(r <= c, s, NEG_BIG)
            m_prev = m_sc[b]
            m_new = jnp.maximum(m_prev, jnp.max(s, axis=0, keepdims=True))
            alpha = jnp.exp(m_prev - m_new)
            p = jnp.exp(s - m_new)
            acc_sc[b] = alpha * acc_sc[b] + _dot(vt_ref[b, j], p.astype(BF16))
            m_sc[b] = m_new

    m_sc[...] = jnp.full_like(m_sc, NEG_BIG)
    acc_sc[...] = jnp.zeros_like(acc_sc)
    scores(0, i)
    scores(1, jnp.maximum(i - 1, 0))
    softmax_pv(0, i, True)

    n = jnp.int32(0)
    for b in range(nb):
        base = (b * FOX_HEADS + h) * nq
        slack = (qn_ref[base + i] * kn_ref[b * FOX_HEADS + h] + ff_ref[base + i]
                 - jnp.min(m_sc[b]) + PRUNE_MARGIN)

        def cond(t, base=base, slack=slack):
            return jnp.logical_and(t < i, slack - fl_ref[base + jnp.maximum(i - 1 - t, 0)] >= 0.0)

        n = jnp.maximum(n, lax.while_loop(cond, lambda t: t + 1, jnp.int32(0)))

    def pair(u, carry):
        t = 1 + 2 * u
        ja = i - t
        scores(0, jnp.maximum(ja - 1, 0))
        softmax_pv(1, ja, False)

        @pl.when(t + 1 <= n)
        def _():
            scores(1, jnp.maximum(ja - 2, 0))
            softmax_pv(0, ja - 1, False)

        return carry

    lax.fori_loop(0, lax.shift_right_logical(n + 1, 1), pair, 0)
    for b in range(nb):
        acc = acc_sc[b]
        o_ref[b] = (acc / acc[BIAS_LANE:BIAS_LANE + 1, :]).T.astype(o_ref.dtype)


def _attention(q2, k2, vt, stats, B, S, blk):
    T = q2.shape[0]
    nq = S // blk
    H = FOX_HEADS
    st = stats.reshape(B, nq, STAT_ROWS, LANES)[:, :, :, :H]
    tab = lambda r: jnp.transpose(st[:, :, r, :], (0, 2, 1)).reshape(-1)
    ffirst, flast = tab(0), tab(1)
    qn = jnp.sqrt(tab(2)) * NORM_SLACK
    kn = jnp.sqrt(jnp.max(st[:, :, 3, :], axis=1)).reshape(-1) * NORM_SLACK
    r3 = lambda a: a.reshape(B, S, FOX_PAD)
    qspec = pl.BlockSpec((B, blk, LANES), lambda h, i, *_: (0, i, h))
    kspec = pl.BlockSpec((B, S, LANES), lambda h, i, *_: (0, 0, h))
    vtspec = pl.BlockSpec((B, nq, LANES, blk), lambda h, i, *_: (0, 0, h, 0))
    grid_spec = pltpu.PrefetchScalarGridSpec(
        num_scalar_prefetch=4,
        grid=(H, nq),
        in_specs=[qspec, kspec, vtspec],
        out_specs=qspec,
        scratch_shapes=[pltpu.VMEM((B, 1, blk), F32), pltpu.VMEM((B, LANES, blk), F32),
                        pltpu.VMEM((2, B, blk, blk), F32)],
    )
    out = pl.pallas_call(
        functools.partial(_attn_kernel, blk=blk, nq=nq, nb=B),
        grid_spec=grid_spec,
        out_shape=jax.ShapeDtypeStruct((B, S, FOX_PAD), BF16),
        compiler_params=_cparams(("arbitrary", "arbitrary")),
        name="fox_attention",
    )(ffirst, flast, qn, kn, r3(q2), r3(k2), vt)
    return out.reshape(T, FOX_PAD)


def _gla_kernel(q_ref, k_ref, v_ref, r_ref, a_ref, wa_ref, ba_ref, g_ref, o_ref, st_sc, *, tg):
    @pl.when(pl.program_id(1) == 0)
    def _():
        st_sc[...] = jnp.zeros_like(st_sc)

    C = GLA_CHUNK
    nchunk = tg // C
    la = _log_sigmoid(_dot(a_ref[...].astype(BF16), wa_ref[...]) + ba_ref[...]) * (1.0 / GLA_GATE_TAU)
    r = lax.broadcasted_iota(jnp.int32, (tg, tg), 0)
    c = lax.broadcasted_iota(jnp.int32, (tg, tg), 1)
    shift = C.bit_length() - 1
    tri = ((c <= r) & ((c >> shift) == (r >> shift))).astype(BF16)
    hi, mid, lo = _split3(la)
    b = _dot(tri, hi) + _dot(tri, mid) + _dot(tri, lo)
    eb = jnp.exp(b)
    q_in = q_ref[...] * (GLA_DK ** -0.5) * eb
    k_in = k_ref[...] * jnp.exp(-b)
    v = v_ref[...]
    rc = lax.broadcasted_iota(jnp.int32, (C, C), 0)
    cc = lax.broadcasted_iota(jnp.int32, (C, C), 1)
    causal = cc <= rc
    lane = lax.broadcasted_iota(jnp.int32, (1, LANES), 1)
    vmask = (lane < GLA_DV).astype(F32)
    outs = []
    for ci in range(nchunk):
        rows = slice(ci * C, (ci + 1) * C)
        b_last = b[ci * C + C - 1:ci * C + C, :]
        k_out = k_ref[rows, :] * jnp.exp(b_last - b[rows, :])
        dec = jnp.exp(b_last)
        heads = []
        for h in range(GLA_HEADS):
            ln = slice(h * LANES, (h + 1) * LANES)
            qh = q_in[rows, ln].astype(BF16)
            kh = k_in[rows, ln].astype(BF16)
            vh = v[rows, ln].astype(BF16)
            attn = jnp.where(causal, _dot_nt(qh, kh), 0.0)
            st = st_sc[h]
            o = _dot(attn.astype(BF16), vh) + _dot_nt(qh, st.astype(BF16))
            kv_t = _dot_tn(vh, k_out[:, ln].astype(BF16))
            st_sc[h] = st * dec[:, ln] + kv_t
            ms = jnp.sum(o * o, axis=-1, keepdims=True) * (1.0 / GLA_DV)
            heads.append(o * lax.rsqrt(ms + RMS_EPS) * vmask)
        outs.append(jnp.concatenate(heads, axis=1))
    o_all = jnp.concatenate(outs, axis=0)
    gr = r_ref[...]
    o_ref[...] = ((o_all * g_ref[...]) * (gr * jax.nn.sigmoid(gr))).astype(o_ref.dtype)


def _gla(zg, wa, ba, gn, B, S, tg):
    T = zg.shape[0]
    nt = S // tg
    col = lambda off, w: pl.BlockSpec((tg, w), lambda b, i: (b * nt + i, off // w))
    full = lambda a: pl.BlockSpec(a.shape, lambda b, i: (0,) * a.ndim)
    return pl.pallas_call(
        functools.partial(_gla_kernel, tg=tg),
        grid=(B, nt),
        in_specs=[col(G_OFF_Q, GLA_PAD), col(G_OFF_K, GLA_PAD), col(G_OFF_V, GLA_PAD),
                  col(G_OFF_R, GLA_PAD), col(G_OFF_A, LANES), full(wa), full(ba), full(gn)],
        out_specs=pl.BlockSpec((tg, GLA_PAD), lambda b, i: (b * nt + i, 0)),
        out_shape=jax.ShapeDtypeStruct((T, GLA_PAD), BF16),
        scratch_shapes=[pltpu.VMEM((GLA_HEADS, LANES, LANES), F32)],
        compiler_params=_cparams(("arbitrary", "arbitrary")),
        name="gla",
    )(zg, zg, zg, zg, zg, wa, ba, gn)


HALO = max(POOL_WINDOWS)


def _pool_kernel(u_ref, w_ref, s_ref, o_ref, xx):
    tp = u_ref.shape[0]
    i = pl.program_id(1)

    @pl.when(i == 0)
    def _():
        xx[0:HALO, :] = jnp.zeros((HALO, POOL_WIDTH), F32)

    @pl.when(i > 0)
    def _():
        xx[0:HALO, :] = xx[tp:tp + HALO, :]

    u = u_ref[...]
    xx[HALO:HALO + tp, :] = u
    lane = lax.broadcasted_iota(jnp.int32, (1, POOL_WIDTH), 1)
    grp = lane >> (POOL_GROUP.bit_length() - 1)
    pos = lax.broadcasted_iota(jnp.int32, (tp, 1), 0) + i * tp + 1
    acc = u
    pooled = jnp.zeros_like(u)
    for j in range(1, HALO):
        acc = acc + xx[HALO - j:HALO - j + tp, :]
        w = j + 1
        if w in POOL_WINDOWS:
            g = POOL_WINDOWS.index(w)
            cnt = jnp.minimum(pos, w).astype(F32)
            pooled = jnp.where(grp == g, acc / cnt - u, pooled)
    mixed = _dot(pooled.astype(BF16), w_ref[...])
    o_ref[...] = (mixed * s_ref[...]).astype(o_ref.dtype)


def _pool(zg, w_bd, scale, B, S, tp):
    T = zg.shape[0]
    nt = S // tp
    full = lambda a: pl.BlockSpec(a.shape, lambda b, i: (0,) * a.ndim)
    return pl.pallas_call(
        _pool_kernel,
        grid=(B, nt),
        in_specs=[pl.BlockSpec((tp, POOL_WIDTH), lambda b, i: (b * nt + i, G_OFF_U // POOL_WIDTH)),
                  full(w_bd), full(scale)],
        out_specs=pl.BlockSpec((tp, POOL_WIDTH), lambda b, i: (b * nt + i, 0)),
        out_shape=jax.ShapeDtypeStruct((T, POOL_WIDTH), BF16),
        scratch_shapes=[pltpu.VMEM((tp + HALO, POOL_WIDTH), F32)],
        compiler_params=_cparams(("arbitrary", "arbitrary")),
        name="pool",
    )(zg, w_bd, scale)


def _layer_norm(r, g, b):
    mu = jnp.mean(r, axis=-1, keepdims=True)
    d = r - mu
    var = jnp.mean(d * d, axis=-1, keepdims=True)
    return d * lax.rsqrt(var + LN_EPS) * g + b


def _outproj_kernel(of_ref, og_ref, op_ref, x_ref, wf_ref, wg_ref, wp_ref, gate_ref, lg_ref, lb_ref,
                    sc_ref, sh_ref, wr_ref, br_ref,
                    x1_ref, h2_ref, e_ref, gt_ref, cnt_ref, *, alpha):
    y = _dot(of_ref[...], wf_ref[...]) + _dot(og_ref[...], wg_ref[...]) + _dot(op_ref[...], wp_ref[...])
    r = alpha * x_ref[...] + (1.0 + gate_ref[0]) * y
    x1 = _layer_norm(r, lg_ref[...], lb_ref[...])
    x1_ref[...] = x1
    h2 = x1 * (1.0 + sc_ref[0]) + sh_ref[0]
    h2_ref[...] = h2
    logits = jnp.dot(h2, wr_ref[...], preferred_element_type=F32,
                     precision=lax.Precision.HIGHEST) + br_ref[...]
    tm = logits.shape[0]
    lane_i = lax.broadcasted_iota(jnp.int32, (tm, LANES), 1)
    lane = lane_i.astype(F32)
    work = logits
    tops, idxs = [], []
    onehot = jnp.zeros((tm, LANES), F32)
    for _ in range(TOP_K):
        m = jnp.max(work, axis=-1, keepdims=True)
        idx = jnp.min(jnp.where(work == m, lane, float(LANES)), axis=-1, keepdims=True)
        sel = lane == idx
        onehot = onehot + sel.astype(F32)
        work = jnp.where(sel, -jnp.inf, work)
        tops.append(m)
        idxs.append(idx)
    ex = [jnp.exp(t - tops[0]) for t in tops]
    den = ex[0] + ex[1] + ex[2] + ex[3]
    e_out = jnp.zeros((tm, LANES), jnp.int32)
    g_out = jnp.zeros((tm, LANES), F32)
    for k in range(TOP_K):
        e_out = jnp.where(lane_i == k, idxs[k].astype(jnp.int32), e_out)
        g_out = jnp.where(lane_i == k, ex[k] / den, g_out)
    e_ref[...] = e_out
    gt_ref[...] = g_out
    for u in range(tm // ROUTE_TILE):
        rows = slice(u * ROUTE_TILE, (u + 1) * ROUTE_TILE)
        cnt_ref[u] = jnp.sum(onehot[rows], axis=0, keepdims=True).astype(jnp.int32)


def _outproj(o_fox, o_gla, o_pool, x2d, wf, wg, wp, gate1, ln_g, ln_b, scale2, shift2, wr, br,
             B, S, tm, alpha):
    T, D = x2d.shape
    nt = S // tm
    full = lambda a: pl.BlockSpec(a.shape, lambda i: (0,) * a.ndim)
    row = lambda w: pl.BlockSpec((tm, w), lambda i: (i, 0))
    mod = pl.BlockSpec((1, 1, D), lambda i: (i // nt, 0, 0))
    return pl.pallas_call(
        functools.partial(_outproj_kernel, alpha=alpha),
        grid=(T // tm,),
        in_specs=[row(FOX_PAD), row(GLA_PAD), row(POOL_WIDTH), row(D), full(wf), full(wg), full(wp),
                  mod, full(ln_g), full(ln_b), mod, mod, full(wr), full(br)],
        out_specs=[row(D), row(D), row(LANES), row(LANES),
                   pl.BlockSpec((tm // ROUTE_TILE, 1, LANES), lambda i: (i, 0, 0))],
        out_shape=[jax.ShapeDtypeStruct((T, D), F32), jax.ShapeDtypeStruct((T, D), F32),
                   jax.ShapeDtypeStruct((T, LANES), jnp.int32), jax.ShapeDtypeStruct((T, LANES), F32),
                   jax.ShapeDtypeStruct((T // ROUTE_TILE, 1, LANES), jnp.int32)],
        compiler_params=_cparams(("arbitrary",)),
        name="outproj_router",
    )(o_fox, o_gla, o_pool, x2d, wf, wg, wp, gate1, ln_g, ln_b, scale2, shift2, wr, br)


def _tile_slots(e_i32, off_row):
    td = e_i32.shape[0]
    lane = lax.broadcasted_iota(jnp.int32, (td, LANES), 1).astype(F32)
    ef = e_i32.astype(F32)
    sel = [lane == ef[:, k:k + 1] for k in range(TOP_K)]
    onehot = sel[0].astype(F32)
    for k in range(1, TOP_K):
        onehot = onehot + sel[k].astype(F32)
    rr = lax.broadcasted_iota(jnp.int32, (td, td), 0)
    cc = lax.broadcasted_iota(jnp.int32, (td, td), 1)
    stril = (cc < rr).astype(BF16)
    tab = _dot(stril, onehot.astype(BF16)) + off_row
    return [jnp.sum(jnp.where(sel[k], tab, 0.0), axis=-1, keepdims=True) for k in range(TOP_K)]


def _group_bits(max_groups):
    return [1 << s for s in range(max_groups.bit_length() - 1, -1, -1)]


def _for_each_chunk(tile, m_ref, off_ref, dst_ref, fn):
    def per_expert(e, carry):
        idx = tile * N_EXPERTS + e
        m = m_ref[idx]
        so = off_ref[idx]
        do = dst_ref[idx]
        pos = jnp.int32(0)
        for bit in _group_bits(ROUTE_TILE // ROW_GROUP):
            take = (m & bit) != 0

            @pl.when(take)
            def _(pos=pos, bit=bit):
                fn(pl.multiple_of(so + pos * ROW_GROUP, ROW_GROUP),
                   pl.multiple_of(do + pos * ROW_GROUP, ROW_GROUP), bit * ROW_GROUP)

            pos = pos + jnp.where(take, bit, 0)
        return carry

    lax.fori_loop(0, N_EXPERTS, per_expert, 0)


def _for_each_total_piece(total_groups, fn):
    for bit in _group_bits(ROUTE_SLOTS // ROW_GROUP):
        @pl.when((total_groups & bit) != 0)
        def _(bit=bit):
            fn(bit * ROW_GROUP)


def _dispatch_kernel(m_ref, off_ref, dst_ref, tot_ref, zblk_ref, nu_ref, e_ref, offrow_ref, h_ref,
                     xr_ref, sort_sc, zero_sc, sems, zsem, *, n_blocks, nt):
    step = pl.program_id(0)

    @pl.when(step == 0)
    def _():
        zero_sc[...] = jnp.zeros_like(zero_sc)

        def zcopy(blk):
            return pltpu.make_async_copy(
                zero_sc, xr_ref.at[pl.ds(pl.multiple_of(blk * MOE_BLOCK, MOE_BLOCK), MOE_BLOCK), :], zsem)

        def zstart(e, c):
            @pl.when(zblk_ref[e] >= 0)
            def _():
                zcopy(zblk_ref[e]).start()
            return c

        def zwait(e, c):
            @pl.when(zblk_ref[e] >= 0)
            def _():
                zcopy(0).wait()
            return c

        def tstart(blk, c):
            zcopy(blk).start()
            return c

        def twait(blk, c):
            zcopy(0).wait()
            return c

        lax.fori_loop(0, N_EXPERTS, zstart, 0)
        lax.fori_loop(nu_ref[0], n_blocks, tstart, 0)
        lax.fori_loop(0, N_EXPERTS, zwait, 0)
        lax.fori_loop(nu_ref[0], n_blocks, twait, 0)

    td = h_ref.shape[0]
    slots = _tile_slots(e_ref[...], offrow_ref[0])
    lane = lax.broadcasted_iota(jnp.int32, (td, LANES), 1)
    cols = jnp.full((td, LANES), -1.0, F32)
    for k in range(TOP_K):
        cols = jnp.where(lane == k, slots[k], cols)
    rows_t = cols.T
    sub = lax.broadcasted_iota(jnp.int32, (ROUTE_SLOTS, td), 0).astype(F32)
    pick = sub == rows_t[0:1, :]
    for k in range(1, TOP_K):
        pick = pick | (sub == rows_t[k:k + 1, :])
    buf = step & 1
    sort_sc[buf] = _dot(pick.astype(BF16), h_ref[...].astype(BF16))

    def start_chunk(slot0, row0, n):
        pltpu.make_async_copy(sort_sc.at[buf, pl.ds(slot0, n), :], xr_ref.at[pl.ds(row0, n), :],
                              sems.at[buf]).start()

    _for_each_chunk(step, m_ref, off_ref, dst_ref, start_chunk)

    def wait_tile(tile, b):
        def wait_piece(n):
            pltpu.make_async_copy(sort_sc.at[b, pl.ds(0, n), :], xr_ref.at[pl.ds(0, n), :],
                                  sems.at[b]).wait()
        _for_each_total_piece(tot_ref[tile], wait_piece)

    @pl.when(step > 0)
    def _():
        wait_tile(step - 1, 1 - buf)

    @pl.when(step == nt - 1)
    def _():
        wait_tile(step, buf)


def _dispatch(h2, e_pad, tables, rows):
    T, D = h2.shape
    td = ROUTE_TILE
    nt = T // td
    m_tab, off_tab, dst_tab, tot_tab, zblk, n_used, off_rows = tables
    grid_spec = pltpu.PrefetchScalarGridSpec(
        num_scalar_prefetch=6,
        grid=(nt,),
        in_specs=[pl.BlockSpec((td, LANES), lambda i, *_: (i, 0)),
                  pl.BlockSpec((1, 1, LANES), lambda i, *_: (i, 0, 0)),
                  pl.BlockSpec((td, D), lambda i, *_: (i, 0))],
        out_specs=pl.BlockSpec(memory_space=pl.ANY),
        scratch_shapes=[pltpu.VMEM((2, ROUTE_SLOTS, D), F32), pltpu.VMEM((MOE_BLOCK, D), F32),
                        pltpu.SemaphoreType.DMA((2,)), pltpu.SemaphoreType.DMA],
    )
    return pl.pallas_call(
        functools.partial(_dispatch_kernel, n_blocks=rows // MOE_BLOCK, nt=nt),
        grid_spec=grid_spec,
        out_shape=jax.ShapeDtypeStruct((rows, D), F32),
        compiler_params=_cparams(("arbitrary",)),
        name="moe_dispatch",
    )(m_tab, off_tab, dst_tab, tot_tab, zblk, n_used, e_pad, off_rows, h2)


def _expert_kernel(be_ref, nu_ref, x_ref, wgu_ref, bgu_ref, wd_ref, bd_ref, y_ref, wgu_sc, wd_sc):
    i = pl.program_id(0)
    used = i < nu_ref[0]

    @pl.when(used)
    def _():
        prev = be_ref[jnp.maximum(i - 1, 0)]

        @pl.when((i == 0) | (be_ref[i] != prev))
        def _():
            wgu_sc[...] = wgu_ref[0].astype(BF16)
            wd_sc[...] = wd_ref[0].astype(BF16)

        x = x_ref[...].astype(BF16)
        gu = _dot(x, wgu_sc[...]) + bgu_ref[0]
        glu = jnp.minimum(gu[:, :D_EXPERT], SWIGLU_LIMIT)
        lin = jnp.clip(gu[:, D_EXPERT:], -SWIGLU_LIMIT, SWIGLU_LIMIT)
        act = glu * jax.nn.sigmoid(SWIGLU_ALPHA * glu) * (lin + 1.0)
        y_ref[...] = _dot(act.astype(BF16), wd_sc[...]) + bd_ref[0]

    @pl.when(jnp.logical_not(used))
    def _():
        y_ref[...] = jnp.zeros_like(y_ref)


def _experts(x_rows, block_expert, n_used, w_gate_up, b_gate_up, w_down, b_down, layer):
    rows, D = x_rows.shape
    nb = rows // MOE_BLOCK
    E = w_gate_up.shape[1]
    grid_spec = pltpu.PrefetchScalarGridSpec(
        num_scalar_prefetch=2,
        grid=(nb,),
        in_specs=[pl.BlockSpec((MOE_BLOCK, D), lambda i, be, nu: (jnp.minimum(i, nu[0] - 1), 0)),
                  pl.BlockSpec((1, D, 2 * D_EXPERT), lambda i, be, nu: (layer * E + be[i], 0, 0)),
                  pl.BlockSpec((1, 1, 2 * D_EXPERT), lambda i, be, nu: (layer * E + be[i], 0, 0)),
                  pl.BlockSpec((1, D_EXPERT, D), lambda i, be, nu: (layer * E + be[i], 0, 0)),
                  pl.BlockSpec((1, 1, D), lambda i, be, nu: (layer * E + be[i], 0, 0))],
        out_specs=pl.BlockSpec((MOE_BLOCK, D), lambda i, be, nu: (i, 0)),
        scratch_shapes=[pltpu.VMEM((D, 2 * D_EXPERT), BF16), pltpu.VMEM((D_EXPERT, D), BF16)],
    )
    L = w_gate_up.shape[0]
    return pl.pallas_call(
        _expert_kernel,
        grid_spec=grid_spec,
        out_shape=jax.ShapeDtypeStruct((rows, D), F32),
        compiler_params=_cparams(("arbitrary",)),
        name="moe_experts",
    )(block_expert, n_used, x_rows,
      w_gate_up.reshape(L * E, D, 2 * D_EXPERT), b_gate_up.reshape(L * E, 1, 2 * D_EXPERT),
      w_down.reshape(L * E, D_EXPERT, D), b_down.reshape(L * E, 1, D))


def _combine_kernel(m_ref, off_ref, dst_ref, tot_ref, e_ref, offrow_ref, gt_ref, y_ref, x_ref, gate_ref,
                    lg_ref, lb_ref, o_ref, ybuf, sems, *, nt, alpha):
    step = pl.program_id(0)
    buf = step & 1

    def fetch(tile, b):
        def start_chunk(slot0, row0, n):
            pltpu.make_async_copy(y_ref.at[pl.ds(row0, n), :], ybuf.at[b, pl.ds(slot0, n), :],
                                  sems.at[b]).start()
        _for_each_chunk(tile, m_ref, off_ref, dst_ref, start_chunk)

    @pl.when(step == 0)
    def _():
        ybuf[...] = jnp.zeros_like(ybuf)
        fetch(step, buf)

    @pl.when(step + 1 < nt)
    def _():
        fetch(step + 1, 1 - buf)

    def wait_piece(n):
        pltpu.make_async_copy(y_ref.at[pl.ds(0, n), :], ybuf.at[buf, pl.ds(0, n), :], sems.at[buf]).wait()

    _for_each_total_piece(tot_ref[step], wait_piece)

    td = x_ref.shape[0]
    slots = _tile_slots(e_ref[...], offrow_ref[0])
    gt = gt_ref[...]
    lane = lax.broadcasted_iota(jnp.int32, (td, ROUTE_SLOTS), 1).astype(F32)
    w = jnp.where(lane == slots[0], gt[:, 0:1], 0.0)
    for k in range(1, TOP_K):
        w = w + jnp.where(lane == slots[k], gt[:, k:k + 1], 0.0)
    y = _dot(w.astype(BF16), ybuf[buf].astype(BF16))
    r = alpha * x_ref[...] + (1.0 + gate_ref[0]) * y
    o_ref[...] = _layer_norm(r, lg_ref[...], lb_ref[...])


def _combine(y_rows, e_pad, tables, gates, x1, gate2, ln_g, ln_b, B, S, alpha):
    T, D = x1.shape
    td = ROUTE_TILE
    nt = T // td
    per_batch = S // td
    m_tab, off_tab, dst_tab, tot_tab, _, _, off_rows = tables
    grid_spec = pltpu.PrefetchScalarGridSpec(
        num_scalar_prefetch=4,
        grid=(nt,),
        in_specs=[pl.BlockSpec((td, LANES), lambda i, *_: (i, 0)),
                  pl.BlockSpec((1, 1, LANES), lambda i, *_: (i, 0, 0)),
                  pl.BlockSpec((td, LANES), lambda i, *_: (i, 0)),
                  pl.BlockSpec(memory_space=pl.ANY),
                  pl.BlockSpec((td, D), lambda i, *_: (i, 0)),
                  pl.BlockSpec((1, 1, D), lambda i, *_: (i // per_batch, 0, 0)),
                  pl.BlockSpec((1, D), lambda i, *_: (0, 0)),
                  pl.BlockSpec((1, D), lambda i, *_: (0, 0))],
        out_specs=pl.BlockSpec((td, D), lambda i, *_: (i, 0)),
        scratch_shapes=[pltpu.VMEM((2, ROUTE_SLOTS, D), F32), pltpu.SemaphoreType.DMA((2,))],
    )
    return pl.pallas_call(
        functools.partial(_combine_kernel, nt=nt, alpha=alpha),
        grid_spec=grid_spec,
        out_shape=jax.ShapeDtypeStruct((T, D), F32),
        compiler_params=_cparams(("arbitrary",)),
        name="moe_combine",
    )(m_tab, off_tab, dst_tab, tot_tab, e_pad, off_rows, gates, y_rows, x1, gate2, ln_g, ln_b)


def _tile(n, pref):
    t = min(n, pref)
    assert n % t == 0, (n, t)
    return t


def kernel(x, c, w_ada, b_ada, w_in, b_in, gla_w_a2, gla_b_a, gla_norm_g, pool_w, pool_scale, w_out,
           ln1_g, ln1_b, w_router, b_router, w_gate_up, b_gate_up, w_down, b_down, ln2_g, ln2_b):
    B, S, D = x.shape
    L = w_ada.shape[0]
    T = B * S
    assert D == D_MODEL and S % GLA_CHUNK == 0
    alpha = float((2 * L) ** 0.25)
    assert T % ROUTE_TILE == 0
    n_tiles = T // ROUTE_TILE
    max_rows = T * TOP_K + n_tiles * N_EXPERTS * (ROW_GROUP - 1) + N_EXPERTS * (MOE_BLOCK - 1)
    n_blocks = -(-max_rows // MOE_BLOCK)
    rows = n_blocks * MOE_BLOCK

    mod = _ada_mod(c, w_ada, b_ada)
    x2d = x.reshape(T, D)
    for l in range(L):
        shift1, scale1, gate1, shift2, scale2, gate2 = [mod[l, :, m] for m in range(N_MOD)]
        prep = _prep_inproj(w_in[l], b_in[l])
        zq, zk, zv, zg = _inproj(x2d, scale1, shift1, prep, B, S, _tile(S, 512))
        q2, k2, vt, stats = _fgate(zq, zk, zv, zg, B, S, _tile(S, ATTN_BLOCK))
        o_fox = _attention(q2, k2, vt, stats, B, S, _tile(S, ATTN_BLOCK))
        wa = jnp.pad(_pad_heads(gla_w_a2[l], GLA_HEADS, GLA_DK),
                     ((0, LANES - GLA_GATE_RANK), (0, 0))).astype(BF16)
        ba = _pad_heads(gla_b_a[l][None, :], GLA_HEADS, GLA_DK)
        gn = _pad_heads(gla_norm_g[l][None, :], GLA_HEADS, GLA_DV)
        o_gla = _gla(zg, wa, ba, gn, B, S, _tile(S, 256))
        w_bd = jnp.zeros((POOL_WIDTH, POOL_WIDTH), F32)
        for g in range(len(POOL_WINDOWS)):
            sl = slice(g * POOL_GROUP, (g + 1) * POOL_GROUP)
            w_bd = w_bd.at[sl, sl].set(pool_w[l, g])
        o_pool = _pool(zg, w_bd.astype(BF16), pool_scale[l][None, :], B, S, _tile(S, 512))
        wo = w_out[l]
        wf = _pad_heads(wo[:FOX_WIDTH].T, FOX_HEADS, FOX_HEAD_DIM).T.astype(BF16)
        wgl = _pad_heads(wo[FOX_WIDTH:FOX_WIDTH + GLA_VWIDTH].T, GLA_HEADS, GLA_DV).T.astype(BF16)
        wp = wo[FOX_WIDTH + GLA_VWIDTH:].astype(BF16)
        wr = jnp.pad(w_router[l], ((0, 0), (0, LANES - N_EXPERTS)))
        br = jnp.pad(b_router[l][None, :], ((0, 0), (0, LANES - N_EXPERTS)), constant_values=NEG_BIG)
        x1, h2, e_pad, g_pad, cnt = _outproj(
            o_fox, o_gla, o_pool, x2d, wf, wgl, wp, gate1, ln1_g[l][None, :], ln1_b[l][None, :],
            scale2, shift2, wr, br, B, S, _tile(S, 512), alpha)
        cte = cnt[:, 0, :N_EXPERTS]
        c8 = (cte + ROW_GROUP - 1) // ROW_GROUP * ROW_GROUP
        off = jnp.cumsum(c8, axis=1) - c8
        per_expert = jnp.sum(c8, axis=0)
        padded = (per_expert + MOE_BLOCK - 1) // MOE_BLOCK * MOE_BLOCK
        pend = jnp.cumsum(padded)
        pstart = pend - padded
        dst = pstart[None, :] + jnp.cumsum(c8, axis=0) - c8
        n_used = (pend[-1] // MOE_BLOCK).astype(jnp.int32)
        blk_start = jnp.arange(n_blocks, dtype=jnp.int32) * MOE_BLOCK
        be = jnp.minimum(jnp.sum(blk_start[:, None] >= pend[None, :], axis=1), N_EXPERTS - 1).astype(jnp.int32)
        be = jnp.where(jnp.arange(n_blocks) < n_used, be, be[jnp.maximum(n_used - 1, 0)])
        zblk = jnp.where(padded > 0, pend // MOE_BLOCK - 1, -1).astype(jnp.int32)
        flat = lambda a: a.astype(jnp.int32).reshape(-1)
        off_rows = jnp.zeros((n_tiles, 1, LANES), F32).at[:, 0, :N_EXPERTS].set(off.astype(F32))
        n_used = n_used.reshape(1)
        tables = (flat(c8 // ROW_GROUP), flat(off), flat(dst), flat(jnp.sum(c8, axis=1) // ROW_GROUP),
                  zblk, n_used, off_rows)
        x_rows = _dispatch(h2, e_pad, tables, rows)
        y_rows = _experts(x_rows, be, n_used, w_gate_up, b_gate_up, w_down, b_down, l)
        x2d = _combine(y_rows, e_pad, tables, g_pad, x1, gate2,
                       ln2_g[l][None, :], ln2_b[l][None, :], B, S, alpha)
    return x2d.reshape(B, S, D)
```

```python
import functools

import numpy as np
import jax
import jax.numpy as jnp
from jax import lax
from jax.experimental import pallas as pl
from jax.experimental.pallas import tpu as pltpu

F32 = jnp.float32
BF16 = jnp.bfloat16

D_MODEL = 1024
FOX_HEADS = 6
FOX_HEAD_DIM = 64
FOX_WIDTH = FOX_HEADS * FOX_HEAD_DIM
GLA_HEADS = 4
GLA_DV = 96
GLA_DK = 48
GLA_KWIDTH = GLA_HEADS * GLA_DK
GLA_VWIDTH = GLA_HEADS * GLA_DV
GLA_GATE_RANK = 16
GLA_GATE_TAU = 16.0
GLA_CHUNK = 64
POOL_WINDOWS = (2, 4, 8, 16)
POOL_GROUP = 64
POOL_WIDTH = len(POOL_WINDOWS) * POOL_GROUP
N_EXPERTS = 32
TOP_K = 4
D_EXPERT = 1024
SWIGLU_ALPHA = 1.702
SWIGLU_LIMIT = 7.0
N_MOD = 6
LN_EPS = 1e-5
RMS_EPS = 1e-6

LANES = 128
VMEM_LIMIT_BYTES = 56 * 1024 * 1024

FOX_PAD = FOX_HEADS * LANES
GLA_PAD = GLA_HEADS * LANES
BIAS_LANE = FOX_HEAD_DIM
G_OFF_Q, G_OFF_K, G_OFF_V, G_OFF_R = 0, GLA_PAD, 2 * GLA_PAD, 3 * GLA_PAD
G_OFF_U = 4 * GLA_PAD
G_OFF_A = G_OFF_U + POOL_WIDTH
G_OFF_F = G_OFF_A + LANES
G_WIDTH = G_OFF_F + LANES

STAT_ROWS = 8
PRUNE_MARGIN = 105.0
NORM_SLACK = 1.01
ATTN_BLOCK = 512
MOE_BLOCK = 512
ROUTE_TILE = 256
ROW_GROUP = 8
ROUTE_SLOTS = ROUTE_TILE * TOP_K + N_EXPERTS * ROW_GROUP
COMMON_GROUPS = 8
NEG_BIG = -1e30


def _cparams(sem, vmem=None):
    return pltpu.CompilerParams(dimension_semantics=sem, vmem_limit_bytes=vmem or VMEM_LIMIT_BYTES)


def _log_sigmoid(x):
    return jnp.minimum(x, 0.0) - jnp.log1p(jnp.exp(-jnp.abs(x)))


def _split3(x):
    hi = x.astype(BF16)
    r = x - hi.astype(F32)
    mid = r.astype(BF16)
    lo = (r - mid.astype(F32)).astype(BF16)
    return hi, mid, lo


def _dot(a, b):
    return jnp.dot(a, b, preferred_element_type=F32)


def _dot_nt(a, b):
    return lax.dot_general(a, b, (((1,), (1,)), ((), ())), preferred_element_type=F32)


def _dot_tn(a, b):
    return lax.dot_general(a, b, (((0,), (0,)), ((), ())), preferred_element_type=F32)


def _ada_kernel(c_ref, w_ref, b_ref, o_ref):
    c = c_ref[...]
    cond = c * jax.nn.sigmoid(c)
    o_ref[0] = jnp.dot(cond, w_ref[0], preferred_element_type=F32,
                       precision=lax.Precision.HIGHEST) + b_ref[0]


def _ada_mod(c, w_ada, b_ada):
    L, D, N = w_ada.shape
    B = c.shape[0]
    rows = 8
    c_pad = jnp.zeros((rows, D), F32).at[:B].set(c)
    tn = 1536
    out = pl.pallas_call(
        _ada_kernel,
        grid=(L, N // tn),
        in_specs=[pl.BlockSpec((rows, D), lambda l, j: (0, 0)),
                  pl.BlockSpec((1, D, tn), lambda l, j: (l, 0, j)),
                  pl.BlockSpec((1, 1, tn), lambda l, j: (l, 0, j))],
        out_specs=pl.BlockSpec((1, rows, tn), lambda l, j: (l, 0, j)),
        out_shape=jax.ShapeDtypeStruct((L, rows, N), F32),
        compiler_params=_cparams(("arbitrary", "arbitrary")),
        name="ada_mod",
    )(c_pad, w_ada, b_ada.reshape(L, 1, N))
    return out[:, :B].reshape(L, B, N_MOD, 1, D)


def _inproj_kernel(x_ref, sc_ref, sh_ref, wq_ref, wk_ref, wv_ref, wg_ref,
                   bq_ref, bk_ref, bv_ref, bg_ref, q_ref, k_ref, v_ref, g_ref):
    h = (x_ref[...] * (1.0 + sc_ref[0]) + sh_ref[0]).astype(BF16)
    q_ref[...] = (_dot(h, wq_ref[...]) + bq_ref[...]).astype(BF16)
    k_ref[...] = (_dot(h, wk_ref[...]) + bk_ref[...]).astype(BF16)
    v_ref[...] = (_dot(h, wv_ref[...]) + bv_ref[...]).astype(BF16)
    g_ref[...] = _dot(h, wg_ref[...]) + bg_ref[...]


def _pad_heads(w, heads, dim):
    lead = w.shape[:-1]
    w = w.reshape(lead + (heads, dim))
    w = jnp.pad(w, [(0, 0)] * len(lead) + [(0, 0), (0, LANES - dim)])
    return w.reshape(lead + (heads * LANES,))


def _pad_last(a, width):
    return jnp.pad(a, [(0, 0)] * (a.ndim - 1) + [(0, width - a.shape[-1])])


def _prep_inproj(w_in, b_in):
    W = jnp.concatenate([w_in, b_in[:, None, :]], axis=1)
    o = 0
    fq = W[..., o:o + FOX_WIDTH]; o += FOX_WIDTH
    fk = W[..., o:o + FOX_WIDTH]; o += FOX_WIDTH
    fv = W[..., o:o + FOX_WIDTH]; o += FOX_WIDTH
    ff = W[..., o:o + FOX_HEADS]; o += FOX_HEADS
    gq = W[..., o:o + GLA_KWIDTH]; o += GLA_KWIDTH
    gk = W[..., o:o + GLA_KWIDTH]; o += GLA_KWIDTH
    gv = W[..., o:o + GLA_VWIDTH]; o += GLA_VWIDTH
    gr = W[..., o:o + GLA_VWIDTH]; o += GLA_VWIDTH
    ga = W[..., o:o + GLA_GATE_RANK]; o += GLA_GATE_RANK
    pu = W[..., o:o + POOL_WIDTH]
    wq = fq * (FOX_HEAD_DIM ** -0.5)
    wk = fk
    wv = fv
    wg = jnp.concatenate([
        _pad_heads(gq, GLA_HEADS, GLA_DK), _pad_heads(gk, GLA_HEADS, GLA_DK),
        _pad_heads(gv, GLA_HEADS, GLA_DV), _pad_heads(gr, GLA_HEADS, GLA_DV),
        pu, _pad_last(ga, LANES), _pad_last(ff, LANES)], axis=-1)
    outs = []
    for w in (wq, wk, wv, wg):
        outs.append((w[:, :-1].astype(BF16), w[:, -1:].astype(F32)))
    return outs


def _inproj(x2d, scale, shift, prep, B, S, tm):
    T, D = x2d.shape
    (wq, bq), (wk, bk), (wv, bv), (wg, bg) = prep
    nt = S // tm
    full = lambda a: pl.BlockSpec(a.shape, lambda i: (0,) * a.ndim)
    row = lambda w: pl.BlockSpec((tm, w), lambda i: (i, 0))
    mod = pl.BlockSpec((1, 1, D), lambda i: (i // nt, 0, 0))
    return pl.pallas_call(
        _inproj_kernel,
        grid=(T // tm,),
        in_specs=[row(D), mod, mod, full(wq), full(wk), full(wv), full(wg),
                  full(bq), full(bk), full(bv), full(bg)],
        out_specs=[row(FOX_WIDTH), row(FOX_WIDTH), row(FOX_WIDTH), row(G_WIDTH)],
        out_shape=[jax.ShapeDtypeStruct((T, FOX_WIDTH), BF16)] * 3
                  + [jax.ShapeDtypeStruct((T, G_WIDTH), F32)],
        compiler_params=_cparams(("arbitrary",)),
        name="inproj",
    )(x2d, scale, shift, wq, wk, wv, wg, bq, bk, bv, bg)


def _aug_constants():
    pq = np.zeros((3 * LANES, FOX_PAD), np.float32)
    pk = np.zeros((3 * LANES, FOX_PAD), np.float32)
    cq = np.zeros((1, FOX_PAD), np.float32)
    ck = np.zeros((1, FOX_PAD), np.float32)
    cv = np.zeros((1, FOX_PAD), np.float32)
    spread = np.zeros((FOX_WIDTH, FOX_PAD), np.float32)
    for h in range(FOX_HEADS):
        base = h * LANES + BIAS_LANE
        cv[0, base] = 1.0
        for d in range(FOX_HEAD_DIM):
            spread[h * FOX_HEAD_DIM + d, h * LANES + d] = 1.0
        for p in range(3):
            pq[p * LANES + h, base + p] = 1.0
            pk[p * LANES + h, base + 3 + p] = -1.0
            cq[0, base + 3 + p] = 1.0
            ck[0, base + p] = 1.0
    return pq, pk, cq, ck, cv, spread


def _fgate_kernel(q_ref, k_ref, v_ref, f_ref, pq_ref, pk_ref, cq_ref, ck_ref, cv_ref, sp_ref,
                  q2_ref, k2_ref, vt_ref, st_ref, carry):
    @pl.when(pl.program_id(1) == 0)
    def _():
        carry[...] = jnp.zeros_like(carry)

    vf = _dot(v_ref[...], sp_ref[...]) + cv_ref[...]
    for h in range(FOX_HEADS):
        sl = slice(h * LANES, (h + 1) * LANES)
        vt_ref[0, 0, sl, :] = vf[:, sl].T.astype(BF16)

    tf = f_ref.shape[0]
    ls = _log_sigmoid(f_ref[...])
    r = lax.broadcasted_iota(jnp.int32, (tf, tf), 0)
    c = lax.broadcasted_iota(jnp.int32, (tf, tf), 1)
    tri = (c <= r).astype(BF16)
    hi, mid, lo = _split3(ls)
    cs = _dot(tri, hi) + _dot(tri, mid) + _dot(tri, lo)
    F = cs + carry[...]
    carry[...] = F[tf - 1:tf, :]
    fh, fm, fl = _split3(F)
    f3 = jnp.concatenate([fh, fm, fl], axis=1)
    qf = _dot(q_ref[...], sp_ref[...])
    kf = _dot(k_ref[...], sp_ref[...])
    q2_ref[...] = (qf + _dot(f3, pq_ref[...]) + cq_ref[...]).astype(BF16)
    k2_ref[...] = (kf + _dot(f3, pk_ref[...]) + ck_ref[...]).astype(BF16)
    lane = lax.broadcasted_iota(jnp.int32, (1, LANES), 1)
    qstat = jnp.zeros((1, LANES), F32)
    kstat = jnp.zeros((1, LANES), F32)
    for h in range(FOX_HEADS):
        sl = slice(h * LANES, (h + 1) * LANES)
        qm = jnp.max(jnp.sum(qf[:, sl] * qf[:, sl], axis=-1, keepdims=True), axis=0, keepdims=True)
        km = jnp.max(jnp.sum(kf[:, sl] * kf[:, sl], axis=-1, keepdims=True), axis=0, keepdims=True)
        qstat = jnp.where(lane == h, qm, qstat)
        kstat = jnp.where(lane == h, km, kstat)
    row = lax.broadcasted_iota(jnp.int32, (STAT_ROWS, LANES), 0)
    st_ref[...] = jnp.where(row == 0, F[0:1, :],
                            jnp.where(row == 1, F[tf - 1:tf, :],
                                      jnp.where(row == 2, qstat, jnp.where(row == 3, kstat, 0.0))))


def _fgate(zq, zk, zv, zg, B, S, tf):
    T = zq.shape[0]
    nt = S // tf
    pq, pk, cq, ck, cv, sp = _aug_constants()
    pq, pk, sp = jnp.asarray(pq, BF16), jnp.asarray(pk, BF16), jnp.asarray(sp, BF16)
    cq, ck, cv = jnp.asarray(cq), jnp.asarray(ck), jnp.asarray(cv)
    full = lambda a: pl.BlockSpec(a.shape, lambda b, i: (0,) * a.ndim)
    row = pl.BlockSpec((tf, FOX_PAD), lambda b, i: (b * nt + i, 0))
    packed = pl.BlockSpec((tf, FOX_WIDTH), lambda b, i: (b * nt + i, 0))
    return pl.pallas_call(
        _fgate_kernel,
        grid=(B, nt),
        in_specs=[packed, packed, packed,
                  pl.BlockSpec((tf, LANES), lambda b, i: (b * nt + i, G_OFF_F // LANES)),
                  full(pq), full(pk), full(cq), full(ck), full(cv), full(sp)],
        out_specs=[row, row, pl.BlockSpec((1, 1, FOX_PAD, tf), lambda b, i: (b, i, 0, 0)),
                   pl.BlockSpec((STAT_ROWS, LANES), lambda b, i: (b * nt + i, 0))],
        out_shape=[jax.ShapeDtypeStruct((T, FOX_PAD), BF16)] * 2
                  + [jax.ShapeDtypeStruct((B, nt, FOX_PAD, tf), BF16),
                     jax.ShapeDtypeStruct((B * nt * STAT_ROWS, LANES), F32)],
        scratch_shapes=[pltpu.VMEM((1, LANES), F32)],
        compiler_params=_cparams(("arbitrary", "arbitrary")),
        name="fgate",
    )(zq, zk, zv, zg, pq, pk, cq, ck, cv, sp)


def _attn_kernel(ff_ref, fl_ref, qn_ref, kn_ref, q_ref, k_ref, vt_ref, o_ref, m_sc, acc_sc, s_sc,
                 *, blk, nq, nb):
    h = pl.program_id(0)
    i = pl.program_id(1)

    def scores(slot, j):
        off = pl.multiple_of(j * blk, blk)
        for b in range(nb):
            s_sc[slot, b] = _dot_nt(k_ref[b, pl.ds(off, blk), :], q_ref[b])

    def softmax_pv(slot, j, diag):
        for b in range(nb):
            s = s_sc[slot, b]
            if diag:
                r = lax.broadcasted_iota(jnp.int32, (blk, blk), 0)
                c = lax.broadcasted_iota(jnp.int32, (blk, blk), 1)
                s = jnp.where(r <= c, s, NEG_BIG)
            m_prev = m_sc[b]
            m_new = jnp.maximum(m_prev, jnp.max(s, axis=0, keepdims=True))
            alpha = jnp.exp(m_prev - m_new)
            p = jnp.exp(s - m_new)
            acc_sc[b] = alpha * acc_sc[b] + _dot(vt_ref[b, j], p.astype(BF16))
            m_sc[b] = m_new

    m_sc[...] = jnp.full_like(m_sc, NEG_BIG)
    acc_sc[...] = jnp.zeros_like(acc_sc)
    scores(0, i)
    scores(1, jnp.maximum(i - 1, 0))
    softmax_pv(0, i, True)

    n = jnp.int32(0)
    for b in range(nb):
        base = (b * FOX_HEADS + h) * nq
        slack = (qn_ref[base + i] * kn_ref[b * FOX_HEADS + h] + ff_ref[base + i]
                 - jnp.min(m_sc[b]) + PRUNE_MARGIN)

        def cond(t, base=base, slack=slack):
            return jnp.logical_and(t < i, slack - fl_ref[base + jnp.maximum(i - 1 - t, 0)] >= 0.0)

        n = jnp.maximum(n, lax.while_loop(cond, lambda t: t + 1, jnp.int32(0)))

    def pair(u, carry):
        t = 1 + 2 * u
        ja = i - t
        scores(0, jnp.maximum(ja - 1, 0))
        softmax_pv(1, ja, False)

        @pl.when(t + 1 <= n)
        def _():
            scores(1, jnp.maximum(ja - 2, 0))
            softmax_pv(0, ja - 1, False)

        return carry

    lax.fori_loop(0, lax.shift_right_logical(n + 1, 1), pair, 0)
    for b in range(nb):
        acc = acc_sc[b]
        o_ref[b] = (acc / acc[BIAS_LANE:BIAS_LANE + 1, :]).T.astype(o_ref.dtype)


def _attention(q2, k2, vt, stats, B, S, blk):
    T = q2.shape[0]
    nq = S // blk
    H = FOX_HEADS
    st = stats.reshape(B, nq, STAT_ROWS, LANES)[:, :, :, :H]
    tab = lambda r: jnp.transpose(st[:, :, r, :], (0, 2, 1)).reshape(-1)
    ffirst, flast = tab(0), tab(1)
    qn = jnp.sqrt(tab(2)) * NORM_SLACK
    kn = jnp.sqrt(jnp.max(st[:, :, 3, :], axis=1)).reshape(-1) * NORM_SLACK
    r3 = lambda a: a.reshape(B, S, FOX_PAD)
    qspec = pl.BlockSpec((B, blk, LANES), lambda h, i, *_: (0, i, h))
    kspec = pl.BlockSpec((B, S, LANES), lambda h, i, *_: (0, 0, h))
    vtspec = pl.BlockSpec((B, nq, LANES, blk), lambda h, i, *_: (0, 0, h, 0))
    grid_spec = pltpu.PrefetchScalarGridSpec(
        num_scalar_prefetch=4,
        grid=(H, nq),
        in_specs=[qspec, kspec, vtspec],
        out_specs=qspec,
        scratch_shapes=[pltpu.VMEM((B, 1, blk), F32), pltpu.VMEM((B, LANES, blk), F32),
                        pltpu.VMEM((2, B, blk, blk), F32)],
    )
    out = pl.pallas_call(
        functools.partial(_attn_kernel, blk=blk, nq=nq, nb=B),
        grid_spec=grid_spec,
        out_shape=jax.ShapeDtypeStruct((B, S, FOX_PAD), BF16),
        compiler_params=_cparams(("arbitrary", "arbitrary")),
        name="fox_attention",
    )(ffirst, flast, qn, kn, r3(q2), r3(k2), vt)
    return out.reshape(T, FOX_PAD)


def _gla_kernel(q_ref, k_ref, v_ref, r_ref, a_ref, wa_ref, ba_ref, g_ref, o_ref, st_sc, *, tg):
    @pl.when(pl.program_id(1) == 0)
    def _():
        st_sc[...] = jnp.zeros_like(st_sc)

    C = GLA_CHUNK
    nchunk = tg // C
    la = _log_sigmoid(_dot(a_ref[...].astype(BF16), wa_ref[...]) + ba_ref[...]) * (1.0 / GLA_GATE_TAU)
    r = lax.broadcasted_iota(jnp.int32, (tg, tg), 0)
    c = lax.broadcasted_iota(jnp.int32, (tg, tg), 1)
    shift = C.bit_length() - 1
    tri = ((c <= r) & ((c >> shift) == (r >> shift))).astype(BF16)
    hi, mid, lo = _split3(la)
    b = _dot(tri, hi) + _dot(tri, mid) + _dot(tri, lo)
    eb = jnp.exp(b)
    q_in = q_ref[...] * (GLA_DK ** -0.5) * eb
    k_in = k_ref[...] * jnp.exp(-b)
    v = v_ref[...]
    rc = lax.broadcasted_iota(jnp.int32, (C, C), 0)
    cc = lax.broadcasted_iota(jnp.int32, (C, C), 1)
    causal = cc <= rc
    lane = lax.broadcasted_iota(jnp.int32, (1, LANES), 1)
    vmask = (lane < GLA_DV).astype(F32)
    outs = []
    for ci in range(nchunk):
        rows = slice(ci * C, (ci + 1) * C)
        b_last = b[ci * C + C - 1:ci * C + C, :]
        k_out = k_ref[rows, :] * jnp.exp(b_last - b[rows, :])
        dec = jnp.exp(b_last)
        heads = []
        for h in range(GLA_HEADS):
            ln = slice(h * LANES, (h + 1) * LANES)
            qh = q_in[rows, ln].astype(BF16)
            kh = k_in[rows, ln].astype(BF16)
            vh = v[rows, ln].astype(BF16)
            attn = jnp.where(causal, _dot_nt(qh, kh), 0.0)
            st = st_sc[h]
            o = _dot(attn.astype(BF16), vh) + _dot_nt(qh, st.astype(BF16))
            kv_t = _dot_tn(vh, k_out[:, ln].astype(BF16))
            st_sc[h] = st * dec[:, ln] + kv_t
            ms = jnp.sum(o * o, axis=-1, keepdims=True) * (1.0 / GLA_DV)
            heads.append(o * lax.rsqrt(ms + RMS_EPS) * vmask)
        outs.append(jnp.concatenate(heads, axis=1))
    o_all = jnp.concatenate(outs, axis=0)
    gr = r_ref[...]
    o_ref[...] = ((o_all * g_ref[...]) * (gr * jax.nn.sigmoid(gr))).astype(o_ref.dtype)


def _gla(zg, wa, ba, gn, B, S, tg):
    T = zg.shape[0]
    nt = S // tg
    col = lambda off, w: pl.BlockSpec((tg, w), lambda b, i: (b * nt + i, off // w))
    full = lambda a: pl.BlockSpec(a.shape, lambda b, i: (0,) * a.ndim)
    return pl.pallas_call(
        functools.partial(_gla_kernel, tg=tg),
        grid=(B, nt),
        in_specs=[col(G_OFF_Q, GLA_PAD), col(G_OFF_K, GLA_PAD), col(G_OFF_V, GLA_PAD),
                  col(G_OFF_R, GLA_PAD), col(G_OFF_A, LANES), full(wa), full(ba), full(gn)],
        out_specs=pl.BlockSpec((tg, GLA_PAD), lambda b, i: (b * nt + i, 0)),
        out_shape=jax.ShapeDtypeStruct((T, GLA_PAD), BF16),
        scratch_shapes=[pltpu.VMEM((GLA_HEADS, LANES, LANES), F32)],
        compiler_params=_cparams(("arbitrary", "arbitrary")),
        name="gla",
    )(zg, zg, zg, zg, zg, wa, ba, gn)


HALO = max(POOL_WINDOWS)


def _pool_kernel(u_ref, w_ref, s_ref, o_ref, xx):
    tp = u_ref.shape[0]
    i = pl.program_id(1)

    @pl.when(i == 0)
    def _():
        xx[0:HALO, :] = jnp.zeros((HALO, POOL_WIDTH), F32)

    @pl.when(i > 0)
    def _():
        xx[0:HALO, :] = xx[tp:tp + HALO, :]

    u = u_ref[...]
    xx[HALO:HALO + tp, :] = u
    lane = lax.broadcasted_iota(jnp.int32, (1, POOL_WIDTH), 1)
    grp = lane >> (POOL_GROUP.bit_length() - 1)
    pos = lax.broadcasted_iota(jnp.int32, (tp, 1), 0) + i * tp + 1
    acc = u
    pooled = jnp.zeros_like(u)
    for j in range(1, HALO):
        acc = acc + xx[HALO - j:HALO - j + tp, :]
        w = j + 1
        if w in POOL_WINDOWS:
            g = POOL_WINDOWS.index(w)
            cnt = jnp.minimum(pos, w).astype(F32)
            pooled = jnp.where(grp == g, acc / cnt - u, pooled)
    mixed = _dot(pooled.astype(BF16), w_ref[...])
    o_ref[...] = (mixed * s_ref[...]).astype(o_ref.dtype)


def _pool(zg, w_bd, scale, B, S, tp):
    T = zg.shape[0]
    nt = S // tp
    full = lambda a: pl.BlockSpec(a.shape, lambda b, i: (0,) * a.ndim)
    return pl.pallas_call(
        _pool_kernel,
        grid=(B, nt),
        in_specs=[pl.BlockSpec((tp, POOL_WIDTH), lambda b, i: (b * nt + i, G_OFF_U // POOL_WIDTH)),
                  full(w_bd), full(scale)],
        out_specs=pl.BlockSpec((tp, POOL_WIDTH), lambda b, i: (b * nt + i, 0)),
        out_shape=jax.ShapeDtypeStruct((T, POOL_WIDTH), BF16),
        scratch_shapes=[pltpu.VMEM((tp + HALO, POOL_WIDTH), F32)],
        compiler_params=_cparams(("arbitrary", "arbitrary")),
        name="pool",
    )(zg, w_bd, scale)


def _layer_norm(r, g, b):
    mu = jnp.mean(r, axis=-1, keepdims=True)
    d = r - mu
    var = jnp.mean(d * d, axis=-1, keepdims=True)
    return d * lax.rsqrt(var + LN_EPS) * g + b


def _outproj_kernel(of_ref, og_ref, op_ref, x_ref, wf_ref, wg_ref, wp_ref, gate_ref, lg_ref, lb_ref,
                    sc_ref, sh_ref, wr_ref, br_ref,
                    x1_ref, h2_ref, e_ref, gt_ref, cnt_ref, *, alpha):
    y = _dot(of_ref[...], wf_ref[...]) + _dot(og_ref[...], wg_ref[...]) + _dot(op_ref[...], wp_ref[...])
    r = alpha * x_ref[...] + (1.0 + gate_ref[0]) * y
    x1 = _layer_norm(r, lg_ref[...], lb_ref[...])
    x1_ref[...] = x1
    h2 = x1 * (1.0 + sc_ref[0]) + sh_ref[0]
    h2_ref[...] = h2
    h_hi = h2.astype(BF16)
    h_lo = (h2 - h_hi.astype(F32)).astype(BF16)
    logits = (_dot(h_hi, wr_ref[0]) + (_dot(h_hi, wr_ref[1]) + _dot(h_lo, wr_ref[0]))
              + br_ref[...])
    tm = logits.shape[0]
    lane_i = lax.broadcasted_iota(jnp.int32, (tm, LANES), 1)
    lane = lane_i.astype(F32)
    work = logits
    tops, idxs = [], []
    onehot = jnp.zeros((tm, LANES), F32)
    for _ in range(TOP_K):
        m = jnp.max(work, axis=-1, keepdims=True)
        idx = jnp.min(jnp.where(work == m, lane, float(LANES)), axis=-1, keepdims=True)
        sel = lane == idx
        onehot = onehot + sel.astype(F32)
        work = jnp.where(sel, -jnp.inf, work)
        tops.append(m)
        idxs.append(idx)
    ex = [jnp.exp(t - tops[0]) for t in tops]
    den = ex[0] + ex[1] + ex[2] + ex[3]
    e_out = jnp.zeros((tm, LANES), jnp.int32)
    g_out = jnp.zeros((tm, LANES), F32)
    for k in range(TOP_K):
        e_out = jnp.where(lane_i == k, idxs[k].astype(jnp.int32), e_out)
        g_out = jnp.where(lane_i == k, ex[k] / den, g_out)
    e_ref[...] = e_out
    gt_ref[...] = g_out
    for u in range(tm // ROUTE_TILE):
        rows = slice(u * ROUTE_TILE, (u + 1) * ROUTE_TILE)
        cnt_ref[u] = jnp.sum(onehot[rows], axis=0, keepdims=True).astype(jnp.int32)


def _outproj(o_fox, o_gla, o_pool, x2d, wf, wg, wp, gate1, ln_g, ln_b, scale2, shift2, wr, br,
             B, S, tm, alpha):
    T, D = x2d.shape
    nt = S // tm
    full = lambda a: pl.BlockSpec(a.shape, lambda i: (0,) * a.ndim)
    row = lambda w: pl.BlockSpec((tm, w), lambda i: (i, 0))
    mod = pl.BlockSpec((1, 1, D), lambda i: (i // nt, 0, 0))
    return pl.pallas_call(
        functools.partial(_outproj_kernel, alpha=alpha),
        grid=(T // tm,),
        in_specs=[row(FOX_PAD), row(GLA_PAD), row(POOL_WIDTH), row(D), full(wf), full(wg), full(wp),
                  mod, full(ln_g), full(ln_b), mod, mod, full(wr), full(br)],
        out_specs=[row(D), row(D), row(LANES), row(LANES),
                   pl.BlockSpec((tm // ROUTE_TILE, 1, LANES), lambda i: (i, 0, 0))],
        out_shape=[jax.ShapeDtypeStruct((T, D), F32), jax.ShapeDtypeStruct((T, D), F32),
                   jax.ShapeDtypeStruct((T, LANES), jnp.int32), jax.ShapeDtypeStruct((T, LANES), F32),
                   jax.ShapeDtypeStruct((T // ROUTE_TILE, 1, LANES), jnp.int32)],
        compiler_params=_cparams(("arbitrary",)),
        name="outproj_router",
    )(o_fox, o_gla, o_pool, x2d, wf, wg, wp, gate1, ln_g, ln_b, scale2, shift2, wr, br)


def _tile_slots(e_i32, off_row):
    td = e_i32.shape[0]
    lane = lax.broadcasted_iota(jnp.int32, (td, LANES), 1).astype(F32)
    ef = e_i32.astype(F32)
    sel = [lane == ef[:, k:k + 1] for k in range(TOP_K)]
    onehot = sel[0].astype(F32)
    for k in range(1, TOP_K):
        onehot = onehot + sel[k].astype(F32)
    rr = lax.broadcasted_iota(jnp.int32, (td, td), 0)
    cc = lax.broadcasted_iota(jnp.int32, (td, td), 1)
    stril = (cc < rr).astype(BF16)
    tab = _dot(stril, onehot.astype(BF16)) + off_row
    return [jnp.sum(jnp.where(sel[k], tab, 0.0), axis=-1, keepdims=True) for k in range(TOP_K)]


def _group_bits(max_groups):
    return [1 << s for s in range(max_groups.bit_length() - 1, -1, -1)]


def _for_each_chunk(tile, m_ref, off_ref, dst_ref, fn):
    def per_expert(e, carry):
        idx = tile * N_EXPERTS + e
        m = m_ref[idx]
        so = off_ref[idx]
        do = dst_ref[idx]

        def piece(bit):
            @pl.when((m & bit) != 0)
            def _():
                pos = (m & ~(2 * bit - 1)) * ROW_GROUP
                fn(pl.multiple_of(so + pos, ROW_GROUP), pl.multiple_of(do + pos, ROW_GROUP),
                   bit * ROW_GROUP)

        bits = _group_bits(ROUTE_TILE // ROW_GROUP)

        @pl.when(m >= COMMON_GROUPS)
        def _():
            for bit in bits:
                if bit >= COMMON_GROUPS:
                    piece(bit)

        for bit in bits:
            if bit < COMMON_GROUPS:
                piece(bit)
        return carry

    lax.fori_loop(0, N_EXPERTS, per_expert, 0)


def _for_each_total_piece(total_groups, fn):
    for bit in _group_bits(ROUTE_SLOTS // ROW_GROUP):
        @pl.when((total_groups & bit) != 0)
        def _(bit=bit):
            fn(bit * ROW_GROUP)


def _dispatch_kernel(m_ref, off_ref, dst_ref, tot_ref, zblk_ref, nu_ref, e_ref, offrow_ref, h_ref,
                     xr_ref, sort_sc, zero_sc, sems, zsem, *, n_blocks, nt):
    step = pl.program_id(0)

    @pl.when(step == 0)
    def _():
        zero_sc[...] = jnp.zeros_like(zero_sc)

        def zcopy(blk):
            return pltpu.make_async_copy(
                zero_sc, xr_ref.at[pl.ds(pl.multiple_of(blk * MOE_BLOCK, MOE_BLOCK), MOE_BLOCK), :], zsem)

        def zstart(e, c):
            @pl.when(zblk_ref[e] >= 0)
            def _():
                zcopy(zblk_ref[e]).start()
            return c

        def zwait(e, c):
            @pl.when(zblk_ref[e] >= 0)
            def _():
                zcopy(0).wait()
            return c

        def tstart(blk, c):
            zcopy(blk).start()
            return c

        def twait(blk, c):
            zcopy(0).wait()
            return c

        lax.fori_loop(0, N_EXPERTS, zstart, 0)
        lax.fori_loop(nu_ref[0], n_blocks, tstart, 0)
        lax.fori_loop(0, N_EXPERTS, zwait, 0)
        lax.fori_loop(nu_ref[0], n_blocks, twait, 0)

    td = h_ref.shape[0]
    slots = _tile_slots(e_ref[...], offrow_ref[0])
    lane = lax.broadcasted_iota(jnp.int32, (td, LANES), 1)
    cols = jnp.full((td, LANES), -1.0, F32)
    for k in range(TOP_K):
        cols = jnp.where(lane == k, slots[k], cols)
    rows_t = cols.T
    sub = lax.broadcasted_iota(jnp.int32, (ROUTE_SLOTS, td), 0).astype(F32)
    pick = sub == rows_t[0:1, :]
    for k in range(1, TOP_K):
        pick = pick | (sub == rows_t[k:k + 1, :])
    buf = step & 1
    sort_sc[buf] = _dot(pick.astype(BF16), h_ref[...].astype(BF16))

    def start_chunk(slot0, row0, n):
        pltpu.make_async_copy(sort_sc.at[buf, pl.ds(slot0, n), :], xr_ref.at[pl.ds(row0, n), :],
                              sems.at[buf]).start()

    _for_each_chunk(step, m_ref, off_ref, dst_ref, start_chunk)

    def wait_tile(tile, b):
        def wait_piece(n):
            pltpu.make_async_copy(sort_sc.at[b, pl.ds(0, n), :], xr_ref.at[pl.ds(0, n), :],
                                  sems.at[b]).wait()
        _for_each_total_piece(tot_ref[tile], wait_piece)

    @pl.when(step > 0)
    def _():
        wait_tile(step - 1, 1 - buf)

    @pl.when(step == nt - 1)
    def _():
        wait_tile(step, buf)


def _dispatch(h2, e_pad, tables, rows):
    T, D = h2.shape
    td = ROUTE_TILE
    nt = T // td
    m_tab, off_tab, dst_tab, tot_tab, zblk, n_used, off_rows = tables
    grid_spec = pltpu.PrefetchScalarGridSpec(
        num_scalar_prefetch=6,
        grid=(nt,),
        in_specs=[pl.BlockSpec((td, LANES), lambda i, *_: (i, 0)),
                  pl.BlockSpec((1, 1, LANES), lambda i, *_: (i, 0, 0)),
                  pl.BlockSpec((td, D), lambda i, *_: (i, 0))],
        out_specs=pl.BlockSpec(memory_space=pl.ANY),
        scratch_shapes=[pltpu.VMEM((2, ROUTE_SLOTS, D), F32), pltpu.VMEM((MOE_BLOCK, D), F32),
                        pltpu.SemaphoreType.DMA((2,)), pltpu.SemaphoreType.DMA],
    )
    return pl.pallas_call(
        functools.partial(_dispatch_kernel, n_blocks=rows // MOE_BLOCK, nt=nt),
        grid_spec=grid_spec,
        out_shape=jax.ShapeDtypeStruct((rows, D), F32),
        compiler_params=_cparams(("arbitrary",)),
        name="moe_dispatch",
    )(m_tab, off_tab, dst_tab, tot_tab, zblk, n_used, e_pad, off_rows, h2)


def _expert_kernel(be_ref, nu_ref, x_ref, wgu_ref, bgu_ref, wd_ref, bd_ref, y_ref, wgu_sc, wd_sc):
    i = pl.program_id(0)
    used = i < nu_ref[0]

    @pl.when(used)
    def _():
        prev = be_ref[jnp.maximum(i - 1, 0)]

        @pl.when((i == 0) | (be_ref[i] != prev))
        def _():
            wgu_sc[...] = wgu_ref[0].astype(BF16)
            wd_sc[...] = wd_ref[0].astype(BF16)

        x = x_ref[...].astype(BF16)
        gu = _dot(x, wgu_sc[...]) + bgu_ref[0]
        glu = jnp.minimum(gu[:, :D_EXPERT], SWIGLU_LIMIT)
        lin = jnp.clip(gu[:, D_EXPERT:], -SWIGLU_LIMIT, SWIGLU_LIMIT)
        act = glu * jax.nn.sigmoid(SWIGLU_ALPHA * glu) * (lin + 1.0)
        y_ref[...] = _dot(act.astype(BF16), wd_sc[...]) + bd_ref[0]

    @pl.when(jnp.logical_not(used))
    def _():
        y_ref[...] = jnp.zeros_like(y_ref)


def _experts(x_rows, block_expert, n_used, w_gate_up, b_gate_up, w_down, b_down, layer):
    rows, D = x_rows.shape
    nb = rows // MOE_BLOCK
    E = w_gate_up.shape[1]
    grid_spec = pltpu.PrefetchScalarGridSpec(
        num_scalar_prefetch=2,
        grid=(nb,),
        in_specs=[pl.BlockSpec((MOE_BLOCK, D), lambda i, be, nu: (jnp.minimum(i, nu[0] - 1), 0)),
                  pl.BlockSpec((1, D, 2 * D_EXPERT), lambda i, be, nu: (layer * E + be[i], 0, 0)),
                  pl.BlockSpec((1, 1, 2 * D_EXPERT), lambda i, be, nu: (layer * E + be[i], 0, 0)),
                  pl.BlockSpec((1, D_EXPERT, D), lambda i, be, nu: (layer * E + be[i], 0, 0)),
                  pl.BlockSpec((1, 1, D), lambda i, be, nu: (layer * E + be[i], 0, 0))],
        out_specs=pl.BlockSpec((MOE_BLOCK, D), lambda i, be, nu: (i, 0)),
        scratch_shapes=[pltpu.VMEM((D, 2 * D_EXPERT), BF16), pltpu.VMEM((D_EXPERT, D), BF16)],
    )
    L = w_gate_up.shape[0]
    return pl.pallas_call(
        _expert_kernel,
        grid_spec=grid_spec,
        out_shape=jax.ShapeDtypeStruct((rows, D), F32),
        compiler_params=_cparams(("arbitrary",)),
        name="moe_experts",
    )(block_expert, n_used, x_rows,
      w_gate_up.reshape(L * E, D, 2 * D_EXPERT), b_gate_up.reshape(L * E, 1, 2 * D_EXPERT),
      w_down.reshape(L * E, D_EXPERT, D), b_down.reshape(L * E, 1, D))


def _combine_kernel(m_ref, off_ref, dst_ref, tot_ref, e_ref, offrow_ref, gt_ref, y_ref, x_ref, gate_ref,
                    lg_ref, lb_ref, o_ref, ybuf, sems, *, nt, alpha):
    step = pl.program_id(0)
    buf = step & 1

    def fetch(tile, b):
        def start_chunk(slot0, row0, n):
            pltpu.make_async_copy(y_ref.at[pl.ds(row0, n), :], ybuf.at[b, pl.ds(slot0, n), :],
                                  sems.at[b]).start()
        _for_each_chunk(tile, m_ref, off_ref, dst_ref, start_chunk)

    @pl.when(step == 0)
    def _():
        ybuf[...] = jnp.zeros_like(ybuf)
        fetch(step, buf)

    @pl.when(step + 1 < nt)
    def _():
        fetch(step + 1, 1 - buf)

    def wait_piece(n):
        pltpu.make_async_copy(y_ref.at[pl.ds(0, n), :], ybuf.at[buf, pl.ds(0, n), :], sems.at[buf]).wait()

    _for_each_total_piece(tot_ref[step], wait_piece)

    td = x_ref.shape[0]
    slots = _tile_slots(e_ref[...], offrow_ref[0])
    gt = gt_ref[...]
    lane = lax.broadcasted_iota(jnp.int32, (td, ROUTE_SLOTS), 1).astype(F32)
    w = jnp.where(lane == slots[0], gt[:, 0:1], 0.0)
    for k in range(1, TOP_K):
        w = w + jnp.where(lane == slots[k], gt[:, k:k + 1], 0.0)
    y = _dot(w.astype(BF16), ybuf[buf].astype(BF16))
    r = alpha * x_ref[...] + (1.0 + gate_ref[0]) * y
    o_ref[...] = _layer_norm(r, lg_ref[...], lb_ref[...])


def _combine(y_rows, e_pad, tables, gates, x1, gate2, ln_g, ln_b, B, S, alpha):
    T, D = x1.shape
    td = ROUTE_TILE
    nt = T // td
    per_batch = S // td
    m_tab, off_tab, dst_tab, tot_tab, _, _, off_rows = tables
    grid_spec = pltpu.PrefetchScalarGridSpec(
        num_scalar_prefetch=4,
        grid=(nt,),
        in_specs=[pl.BlockSpec((td, LANES), lambda i, *_: (i, 0)),
                  pl.BlockSpec((1, 1, LANES), lambda i, *_: (i, 0, 0)),
                  pl.BlockSpec((td, LANES), lambda i, *_: (i, 0)),
                  pl.BlockSpec(memory_space=pl.ANY),
                  pl.BlockSpec((td, D), lambda i, *_: (i, 0)),
                  pl.BlockSpec((1, 1, D), lambda i, *_: (i // per_batch, 0, 0)),
                  pl.BlockSpec((1, D), lambda i, *_: (0, 0)),
                  pl.BlockSpec((1, D), lambda i, *_: (0, 0))],
        out_specs=pl.BlockSpec((td, D), lambda i, *_: (i, 0)),
        scratch_shapes=[pltpu.VMEM((2, ROUTE_SLOTS, D), F32), pltpu.SemaphoreType.DMA((2,))],
    )
    return pl.pallas_call(
        functools.partial(_combine_kernel, nt=nt, alpha=alpha),
        grid_spec=grid_spec,
        out_shape=jax.ShapeDtypeStruct((T, D), F32),
        compiler_params=_cparams(("arbitrary",)),
        name="moe_combine",
    )(m_tab, off_tab, dst_tab, tot_tab, e_pad, off_rows, gates, y_rows, x1, gate2, ln_g, ln_b)


def _tile(n, pref):
    t = min(n, pref)
    assert n % t == 0, (n, t)
    return t


def kernel(x, c, w_ada, b_ada, w_in, b_in, gla_w_a2, gla_b_a, gla_norm_g, pool_w, pool_scale, w_out,
           ln1_g, ln1_b, w_router, b_router, w_gate_up, b_gate_up, w_down, b_down, ln2_g, ln2_b):
    B, S, D = x.shape
    L = w_ada.shape[0]
    T = B * S
    assert D == D_MODEL and S % GLA_CHUNK == 0
    alpha = float((2 * L) ** 0.25)
    assert T % ROUTE_TILE == 0
    n_tiles = T // ROUTE_TILE
    max_rows = T * TOP_K + n_tiles * N_EXPERTS * (ROW_GROUP - 1) + N_EXPERTS * (MOE_BLOCK - 1)
    n_blocks = -(-max_rows // MOE_BLOCK)
    rows = n_blocks * MOE_BLOCK

    mod = _ada_mod(c, w_ada, b_ada)
    prep_all = _prep_inproj(w_in, b_in)
    wa_all = jnp.pad(_pad_heads(gla_w_a2, GLA_HEADS, GLA_DK),
                     ((0, 0), (0, LANES - GLA_GATE_RANK), (0, 0))).astype(BF16)
    ba_all = _pad_heads(gla_b_a[:, None, :], GLA_HEADS, GLA_DK)
    gn_all = _pad_heads(gla_norm_g[:, None, :], GLA_HEADS, GLA_DV)
    w_bd_all = jnp.zeros((L, POOL_WIDTH, POOL_WIDTH), F32)
    for g in range(len(POOL_WINDOWS)):
        sl = slice(g * POOL_GROUP, (g + 1) * POOL_GROUP)
        w_bd_all = w_bd_all.at[:, sl, sl].set(pool_w[:, g])
    w_bd_all = w_bd_all.astype(BF16)
    pad_rows = lambda w, heads, dim: jnp.swapaxes(_pad_heads(jnp.swapaxes(w, 1, 2), heads, dim), 1, 2)
    wf_all = pad_rows(w_out[:, :FOX_WIDTH], FOX_HEADS, FOX_HEAD_DIM).astype(BF16)
    wgl_all = pad_rows(w_out[:, FOX_WIDTH:FOX_WIDTH + GLA_VWIDTH], GLA_HEADS, GLA_DV).astype(BF16)
    wp_all = w_out[:, FOX_WIDTH + GLA_VWIDTH:].astype(BF16)
    wr_f32 = _pad_last(w_router, LANES)
    wr_hi = wr_f32.astype(BF16)
    wr_all = jnp.stack([wr_hi, (wr_f32 - wr_hi.astype(F32)).astype(BF16)], axis=1)
    br_all = jnp.pad(b_router[:, None, :], ((0, 0), (0, 0), (0, LANES - N_EXPERTS)), constant_values=NEG_BIG)

    x2d = x.reshape(T, D)
    for l in range(L):
        shift1, scale1, gate1, shift2, scale2, gate2 = [mod[l, :, m] for m in range(N_MOD)]
        prep = [(w[l], b[l]) for w, b in prep_all]
        zq, zk, zv, zg = _inproj(x2d, scale1, shift1, prep, B, S, _tile(S, 512))
        q2, k2, vt, stats = _fgate(zq, zk, zv, zg, B, S, _tile(S, ATTN_BLOCK))
        o_fox = _attention(q2, k2, vt, stats, B, S, _tile(S, ATTN_BLOCK))
        o_gla = _gla(zg, wa_all[l], ba_all[l], gn_all[l], B, S, _tile(S, 256))
        o_pool = _pool(zg, w_bd_all[l], pool_scale[l][None, :], B, S, _tile(S, 512))
        x1, h2, e_pad, g_pad, cnt = _outproj(
            o_fox, o_gla, o_pool, x2d, wf_all[l], wgl_all[l], wp_all[l], gate1,
            ln1_g[l][None, :], ln1_b[l][None, :], scale2, shift2, wr_all[l], br_all[l],
            B, S, _tile(S, 512), alpha)
        cte = cnt[:, 0, :N_EXPERTS]
        c8 = (cte + ROW_GROUP - 1) // ROW_GROUP * ROW_GROUP
        off = jnp.cumsum(c8, axis=1) - c8
        per_expert = jnp.sum(c8, axis=0)
        padded = (per_expert + MOE_BLOCK - 1) // MOE_BLOCK * MOE_BLOCK
        pend = jnp.cumsum(padded)
        pstart = pend - padded
        dst = pstart[None, :] + jnp.cumsum(c8, axis=0) - c8
        n_used = (pend[-1] // MOE_BLOCK).astype(jnp.int32)
        blk_start = jnp.arange(n_blocks, dtype=jnp.int32) * MOE_BLOCK
        be = jnp.minimum(jnp.sum(blk_start[:, None] >= pend[None, :], axis=1), N_EXPERTS - 1).astype(jnp.int32)
        be = jnp.where(jnp.arange(n_blocks) < n_used, be, be[jnp.maximum(n_used - 1, 0)])
        zblk = jnp.where(padded > 0, pend // MOE_BLOCK - 1, -1).astype(jnp.int32)
        flat = lambda a: a.astype(jnp.int32).reshape(-1)
        off_rows = jnp.zeros((n_tiles, 1, LANES), F32).at[:, 0, :N_EXPERTS].set(off.astype(F32))
        n_used = n_used.reshape(1)
        tables = (flat(c8 // ROW_GROUP), flat(off), flat(dst), flat(jnp.sum(c8, axis=1) // ROW_GROUP),
                  zblk, n_used, off_rows)
        x_rows = _dispatch(h2, e_pad, tables, rows)
        y_rows = _experts(x_rows, be, n_used, w_gate_up, b_gate_up, w_down, b_down, l)
        x2d = _combine(y_rows, e_pad, tables, g_pad, x1, gate2,
                       ln2_g[l][None, :], ln2_b[l][None, :], B, S, alpha)
    return x2d.reshape(B, S, D)
```

```python
import functools

import numpy as np
import jax
import jax.numpy as jnp
from jax import lax
from jax.experimental import pallas as pl
from jax.experimental.pallas import tpu as pltpu

F32 = jnp.float32
BF16 = jnp.bfloat16

D_MODEL = 1024
FOX_HEADS = 6
FOX_HEAD_DIM = 64
FOX_WIDTH = FOX_HEADS * FOX_HEAD_DIM
GLA_HEADS = 4
GLA_DV = 96
GLA_DK = 48
GLA_KWIDTH = GLA_HEADS * GLA_DK
GLA_VWIDTH = GLA_HEADS * GLA_DV
GLA_GATE_RANK = 16
GLA_GATE_TAU = 16.0
GLA_CHUNK = 64
POOL_WINDOWS = (2, 4, 8, 16)
POOL_GROUP = 64
POOL_WIDTH = len(POOL_WINDOWS) * POOL_GROUP
N_EXPERTS = 32
TOP_K = 4
D_EXPERT = 1024
SWIGLU_ALPHA = 1.702
SWIGLU_LIMIT = 7.0
N_MOD = 6
LN_EPS = 1e-5
RMS_EPS = 1e-6

LANES = 128
VMEM_LIMIT_BYTES = 56 * 1024 * 1024

FOX_PAD = FOX_HEADS * LANES
GLA_PAD = GLA_HEADS * LANES
BIAS_LANE = FOX_HEAD_DIM
G_OFF_Q, G_OFF_K, G_OFF_V, G_OFF_R = 0, GLA_PAD, 2 * GLA_PAD, 3 * GLA_PAD
G_OFF_U = 4 * GLA_PAD
G_OFF_A = G_OFF_U + POOL_WIDTH
G_OFF_F = G_OFF_A + LANES
G_WIDTH = G_OFF_F + LANES

STAT_ROWS = 8
PRUNE_MARGIN = 105.0
NORM_SLACK = 1.01
ATTN_BLOCK = 512
MOE_BLOCK = 512
ROUTE_TILE = 256
ROW_GROUP = 8
ROUTE_SLOTS = ROUTE_TILE * TOP_K + N_EXPERTS * ROW_GROUP
COMMON_GROUPS = 8
NEG_BIG = -1e30


def _cparams(sem, vmem=None):
    return pltpu.CompilerParams(dimension_semantics=sem, vmem_limit_bytes=vmem or VMEM_LIMIT_BYTES)


def _log_sigmoid(x):
    return jnp.minimum(x, 0.0) - jnp.log(1.0 + jnp.exp(-jnp.abs(x)))


def _split3(x):
    hi = x.astype(BF16)
    r = x - hi.astype(F32)
    mid = r.astype(BF16)
    lo = (r - mid.astype(F32)).astype(BF16)
    return hi, mid, lo


def _dot(a, b):
    return jnp.dot(a, b, preferred_element_type=F32)


def _dot_nt(a, b):
    return lax.dot_general(a, b, (((1,), (1,)), ((), ())), preferred_element_type=F32)


def _dot_tn(a, b):
    return lax.dot_general(a, b, (((0,), (0,)), ((), ())), preferred_element_type=F32)


def _ada_kernel(c_ref, w_ref, b_ref, o_ref):
    c = c_ref[...]
    cond = c * jax.nn.sigmoid(c)
    o_ref[0] = jnp.dot(cond, w_ref[0], preferred_element_type=F32,
                       precision=lax.Precision.HIGHEST) + b_ref[0]


def _ada_mod(c, w_ada, b_ada):
    L, D, N = w_ada.shape
    B = c.shape[0]
    rows = 8
    c_pad = jnp.zeros((rows, D), F32).at[:B].set(c)
    tn = 1536
    out = pl.pallas_call(
        _ada_kernel,
        grid=(L, N // tn),
        in_specs=[pl.BlockSpec((rows, D), lambda l, j: (0, 0)),
                  pl.BlockSpec((1, D, tn), lambda l, j: (l, 0, j)),
                  pl.BlockSpec((1, 1, tn), lambda l, j: (l, 0, j))],
        out_specs=pl.BlockSpec((1, rows, tn), lambda l, j: (l, 0, j)),
        out_shape=jax.ShapeDtypeStruct((L, rows, N), F32),
        compiler_params=_cparams(("arbitrary", "arbitrary")),
        name="ada_mod",
    )(c_pad, w_ada, b_ada.reshape(L, 1, N))
    return out[:, :B].reshape(L, B, N_MOD, 1, D)


def _inproj_kernel(x_ref, sc_ref, sh_ref, wq_ref, wk_ref, wv_ref, wg_ref,
                   bq_ref, bk_ref, bv_ref, bg_ref, q_ref, k_ref, v_ref, g_ref):
    h = (x_ref[...] * (1.0 + sc_ref[0]) + sh_ref[0]).astype(BF16)
    q_ref[...] = (_dot(h, wq_ref[...]) + bq_ref[...]).astype(BF16)
    k_ref[...] = (_dot(h, wk_ref[...]) + bk_ref[...]).astype(BF16)
    v_ref[...] = (_dot(h, wv_ref[...]) + bv_ref[...]).astype(BF16)
    g_ref[...] = _dot(h, wg_ref[...]) + bg_ref[...]


def _pad_heads(w, heads, dim):
    lead = w.shape[:-1]
    w = w.reshape(lead + (heads, dim))
    w = jnp.pad(w, [(0, 0)] * len(lead) + [(0, 0), (0, LANES - dim)])
    return w.reshape(lead + (heads * LANES,))


def _pad_last(a, width):
    return jnp.pad(a, [(0, 0)] * (a.ndim - 1) + [(0, width - a.shape[-1])])


def _prep_inproj(w_in, b_in):
    W = jnp.concatenate([w_in, b_in[:, None, :]], axis=1)
    o = 0
    fq = W[..., o:o + FOX_WIDTH]; o += FOX_WIDTH
    fk = W[..., o:o + FOX_WIDTH]; o += FOX_WIDTH
    fv = W[..., o:o + FOX_WIDTH]; o += FOX_WIDTH
    ff = W[..., o:o + FOX_HEADS]; o += FOX_HEADS
    gq = W[..., o:o + GLA_KWIDTH]; o += GLA_KWIDTH
    gk = W[..., o:o + GLA_KWIDTH]; o += GLA_KWIDTH
    gv = W[..., o:o + GLA_VWIDTH]; o += GLA_VWIDTH
    gr = W[..., o:o + GLA_VWIDTH]; o += GLA_VWIDTH
    ga = W[..., o:o + GLA_GATE_RANK]; o += GLA_GATE_RANK
    pu = W[..., o:o + POOL_WIDTH]
    wq = fq * (FOX_HEAD_DIM ** -0.5)
    wk = fk
    wv = fv
    wg = jnp.concatenate([
        _pad_heads(gq, GLA_HEADS, GLA_DK), _pad_heads(gk, GLA_HEADS, GLA_DK),
        _pad_heads(gv, GLA_HEADS, GLA_DV), _pad_heads(gr, GLA_HEADS, GLA_DV),
        pu, _pad_last(ga, LANES), _pad_last(ff, LANES)], axis=-1)
    outs = []
    for w in (wq, wk, wv, wg):
        outs.append((w[:, :-1].astype(BF16), w[:, -1:].astype(F32)))
    return outs


def _inproj(x2d, scale, shift, prep, B, S, tm):
    T, D = x2d.shape
    (wq, bq), (wk, bk), (wv, bv), (wg, bg) = prep
    nt = S // tm
    full = lambda a: pl.BlockSpec(a.shape, lambda i: (0,) * a.ndim)
    row = lambda w: pl.BlockSpec((tm, w), lambda i: (i, 0))
    mod = pl.BlockSpec((1, 1, D), lambda i: (i // nt, 0, 0))
    return pl.pallas_call(
        _inproj_kernel,
        grid=(T // tm,),
        in_specs=[row(D), mod, mod, full(wq), full(wk), full(wv), full(wg),
                  full(bq), full(bk), full(bv), full(bg)],
        out_specs=[row(FOX_WIDTH), row(FOX_WIDTH), row(FOX_WIDTH), row(G_WIDTH)],
        out_shape=[jax.ShapeDtypeStruct((T, FOX_WIDTH), BF16)] * 3
                  + [jax.ShapeDtypeStruct((T, G_WIDTH), F32)],
        compiler_params=_cparams(("arbitrary",)),
        name="inproj",
    )(x2d, scale, shift, wq, wk, wv, wg, bq, bk, bv, bg)


def _aug_constants():
    pq = np.zeros((3 * LANES, FOX_PAD), np.float32)
    pk = np.zeros((3 * LANES, FOX_PAD), np.float32)
    cq = np.zeros((1, FOX_PAD), np.float32)
    ck = np.zeros((1, FOX_PAD), np.float32)
    cv = np.zeros((1, FOX_PAD), np.float32)
    spread = np.zeros((FOX_WIDTH, FOX_PAD), np.float32)
    for h in range(FOX_HEADS):
        base = h * LANES + BIAS_LANE
        cv[0, base] = 1.0
        for d in range(FOX_HEAD_DIM):
            spread[h * FOX_HEAD_DIM + d, h * LANES + d] = 1.0
        for p in range(3):
            pq[p * LANES + h, base + p] = 1.0
            pk[p * LANES + h, base + 3 + p] = -1.0
            cq[0, base + 3 + p] = 1.0
            ck[0, base + p] = 1.0
    return pq, pk, cq, ck, cv, spread


def _fgate_kernel(q_ref, k_ref, v_ref, f_ref, pq_ref, pk_ref, cq_ref, ck_ref, cv_ref, sp_ref,
                  q2_ref, k2_ref, vt_ref, st_ref, carry):
    @pl.when(pl.program_id(1) == 0)
    def _():
        carry[...] = jnp.zeros_like(carry)

    vf = _dot(v_ref[...], sp_ref[...]) + cv_ref[...]
    for h in range(FOX_HEADS):
        sl = slice(h * LANES, (h + 1) * LANES)
        vt_ref[0, 0, sl, :] = vf[:, sl].T.astype(BF16)

    tf = f_ref.shape[0]
    ls = _log_sigmoid(f_ref[...])
    r = lax.broadcasted_iota(jnp.int32, (tf, tf), 0)
    c = lax.broadcasted_iota(jnp.int32, (tf, tf), 1)
    tri = (c <= r).astype(BF16)
    hi, mid, lo = _split3(ls)
    cs = _dot(tri, hi) + _dot(tri, mid) + _dot(tri, lo)
    F = cs + carry[...]
    carry[...] = F[tf - 1:tf, :]
    fh, fm, fl = _split3(F)
    f3 = jnp.concatenate([fh, fm, fl], axis=1)
    qf = _dot(q_ref[...], sp_ref[...])
    kf = _dot(k_ref[...], sp_ref[...])
    q2_ref[...] = (qf + _dot(f3, pq_ref[...]) + cq_ref[...]).astype(BF16)
    k2_ref[...] = (kf + _dot(f3, pk_ref[...]) + ck_ref[...]).astype(BF16)
    lane = lax.broadcasted_iota(jnp.int32, (1, LANES), 1)
    qstat = jnp.zeros((1, LANES), F32)
    kstat = jnp.zeros((1, LANES), F32)
    for h in range(FOX_HEADS):
        sl = slice(h * LANES, (h + 1) * LANES)
        qm = jnp.max(jnp.sum(qf[:, sl] * qf[:, sl], axis=-1, keepdims=True), axis=0, keepdims=True)
        km = jnp.max(jnp.sum(kf[:, sl] * kf[:, sl], axis=-1, keepdims=True), axis=0, keepdims=True)
        qstat = jnp.where(lane == h, qm, qstat)
        kstat = jnp.where(lane == h, km, kstat)
    row = lax.broadcasted_iota(jnp.int32, (STAT_ROWS, LANES), 0)
    st_ref[...] = jnp.where(row == 0, F[0:1, :],
                            jnp.where(row == 1, F[tf - 1:tf, :],
                                      jnp.where(row == 2, qstat, jnp.where(row == 3, kstat, 0.0))))


def _fgate(zq, zk, zv, zg, B, S, tf):
    T = zq.shape[0]
    nt = S // tf
    pq, pk, cq, ck, cv, sp = _aug_constants()
    pq, pk, sp = jnp.asarray(pq, BF16), jnp.asarray(pk, BF16), jnp.asarray(sp, BF16)
    cq, ck, cv = jnp.asarray(cq), jnp.asarray(ck), jnp.asarray(cv)
    full = lambda a: pl.BlockSpec(a.shape, lambda b, i: (0,) * a.ndim)
    row = pl.BlockSpec((tf, FOX_PAD), lambda b, i: (b * nt + i, 0))
    packed = pl.BlockSpec((tf, FOX_WIDTH), lambda b, i: (b * nt + i, 0))
    return pl.pallas_call(
        _fgate_kernel,
        grid=(B, nt),
        in_specs=[packed, packed, packed,
                  pl.BlockSpec((tf, LANES), lambda b, i: (b * nt + i, G_OFF_F // LANES)),
                  full(pq), full(pk), full(cq), full(ck), full(cv), full(sp)],
        out_specs=[row, row, pl.BlockSpec((1, 1, FOX_PAD, tf), lambda b, i: (b, i, 0, 0)),
                   pl.BlockSpec((STAT_ROWS, LANES), lambda b, i: (b * nt + i, 0))],
        out_shape=[jax.ShapeDtypeStruct((T, FOX_PAD), BF16)] * 2
                  + [jax.ShapeDtypeStruct((B, nt, FOX_PAD, tf), BF16),
                     jax.ShapeDtypeStruct((B * nt * STAT_ROWS, LANES), F32)],
        scratch_shapes=[pltpu.VMEM((1, LANES), F32)],
        compiler_params=_cparams(("arbitrary", "arbitrary")),
        name="fgate",
    )(zq, zk, zv, zg, pq, pk, cq, ck, cv, sp)


def _attn_kernel(ff_ref, fl_ref, qn_ref, kn_ref, q_ref, k_ref, vt_ref, o_ref, m_sc, acc_sc, s_sc,
                 *, blk, nq, nb):
    h = pl.program_id(0)
    i = pl.program_id(1)

    def scores(slot, j):
        off = pl.multiple_of(j * blk, blk)
        for b in range(nb):
            s_sc[slot, b] = _dot_nt(k_ref[b, pl.ds(off, blk), :], q_ref[b])

    def softmax_pv(slot, j, diag):
        for b in range(nb):
            s = s_sc[slot, b]
            if diag:
                r = lax.broadcasted_iota(jnp.int32, (blk, blk), 0)
                c = lax.broadcasted_iota(jnp.int32, (blk, blk), 1)
                s = jnp.where(r <= c, s, NEG_BIG)
            m_prev = m_sc[b]
            m_new = jnp.maximum(m_prev, jnp.max(s, axis=0, keepdims=True))
            alpha = jnp.exp(m_prev - m_new)
            p = jnp.exp(s - m_new)
            acc_sc[b] = alpha * acc_sc[b] + _dot(vt_ref[b, j], p.astype(BF16))
            m_sc[b] = m_new

    m_sc[...] = jnp.full_like(m_sc, NEG_BIG)
    acc_sc[...] = jnp.zeros_like(acc_sc)
    scores(0, i)
    scores(1, jnp.maximum(i - 1, 0))
    softmax_pv(0, i, True)

    n = jnp.int32(0)
    for b in range(nb):
        base = (b * FOX_HEADS + h) * nq
        slack = (qn_ref[base + i] * kn_ref[b * FOX_HEADS + h] + ff_ref[base + i]
                 - jnp.min(m_sc[b]) + PRUNE_MARGIN)

        def cond(t, base=base, slack=slack):
            return jnp.logical_and(t < i, slack - fl_ref[base + jnp.maximum(i - 1 - t, 0)] >= 0.0)

        n = jnp.maximum(n, lax.while_loop(cond, lambda t: t + 1, jnp.int32(0)))

    def pair(u, carry):
        t = 1 + 2 * u
        ja = i - t
        scores(0, jnp.maximum(ja - 1, 0))
        softmax_pv(1, ja, False)

        @pl.when(t + 1 <= n)
        def _():
            scores(1, jnp.maximum(ja - 2, 0))
            softmax_pv(0, ja - 1, False)

        return carry

    lax.fori_loop(0, lax.shift_right_logical(n + 1, 1), pair, 0)
    for b in range(nb):
        acc = acc_sc[b]
        o_ref[b] = (acc / acc[BIAS_LANE:BIAS_LANE + 1, :]).T.astype(o_ref.dtype)


def _attention(q2, k2, vt, stats, B, S, blk):
    T = q2.shape[0]
    nq = S // blk
    H = FOX_HEADS
    st = stats.reshape(B, nq, STAT_ROWS, LANES)[:, :, :, :H]
    tab = lambda r: jnp.transpose(st[:, :, r, :], (0, 2, 1)).reshape(-1)
    ffirst, flast = tab(0), tab(1)
    qn = jnp.sqrt(tab(2)) * NORM_SLACK
    kn = jnp.sqrt(jnp.max(st[:, :, 3, :], axis=1)).reshape(-1) * NORM_SLACK
    r3 = lambda a: a.reshape(B, S, FOX_PAD)
    qspec = pl.BlockSpec((B, blk, LANES), lambda h, i, *_: (0, i, h))
    kspec = pl.BlockSpec((B, S, LANES), lambda h, i, *_: (0, 0, h))
    vtspec = pl.BlockSpec((B, nq, LANES, blk), lambda h, i, *_: (0, 0, h, 0))
    grid_spec = pltpu.PrefetchScalarGridSpec(
        num_scalar_prefetch=4,
        grid=(H, nq),
        in_specs=[qspec, kspec, vtspec],
        out_specs=qspec,
        scratch_shapes=[pltpu.VMEM((B, 1, blk), F32), pltpu.VMEM((B, LANES, blk), F32),
                        pltpu.VMEM((2, B, blk, blk), F32)],
    )
    out = pl.pallas_call(
        functools.partial(_attn_kernel, blk=blk, nq=nq, nb=B),
        grid_spec=grid_spec,
        out_shape=jax.ShapeDtypeStruct((B, S, FOX_PAD), BF16),
        compiler_params=_cparams(("arbitrary", "arbitrary")),
        name="fox_attention",
    )(ffirst, flast, qn, kn, r3(q2), r3(k2), vt)
    return out.reshape(T, FOX_PAD)


def _gla_kernel(q_ref, k_ref, v_ref, r_ref, a_ref, wa_ref, ba_ref, g_ref, o_ref, st_sc, *, tg):
    @pl.when(pl.program_id(1) == 0)
    def _():
        st_sc[...] = jnp.zeros_like(st_sc)

    C = GLA_CHUNK
    nchunk = tg // C
    la = _log_sigmoid(_dot(a_ref[...].astype(BF16), wa_ref[...]) + ba_ref[...]) * (1.0 / GLA_GATE_TAU)
    r = lax.broadcasted_iota(jnp.int32, (tg, tg), 0)
    c = lax.broadcasted_iota(jnp.int32, (tg, tg), 1)
    shift = C.bit_length() - 1
    tri = ((c <= r) & ((c >> shift) == (r >> shift))).astype(BF16)
    hi, mid, lo = _split3(la)
    b = _dot(tri, hi) + _dot(tri, mid) + _dot(tri, lo)
    eb = jnp.exp(b)
    q_in = q_ref[...] * (GLA_DK ** -0.5) * eb
    k_in = k_ref[...] * jnp.exp(-b)
    v = v_ref[...]
    rc = lax.broadcasted_iota(jnp.int32, (C, C), 0)
    cc = lax.broadcasted_iota(jnp.int32, (C, C), 1)
    causal = cc <= rc
    lane = lax.broadcasted_iota(jnp.int32, (1, LANES), 1)
    vmask = (lane < GLA_DV).astype(F32)
    outs = []
    for ci in range(nchunk):
        rows = slice(ci * C, (ci + 1) * C)
        b_last = b[ci * C + C - 1:ci * C + C, :]
        k_out = k_ref[rows, :] * jnp.exp(b_last - b[rows, :])
        dec = jnp.exp(b_last)
        heads = []
        for h in range(GLA_HEADS):
            ln = slice(h * LANES, (h + 1) * LANES)
            qh = q_in[rows, ln].astype(BF16)
            kh = k_in[rows, ln].astype(BF16)
            vh = v[rows, ln].astype(BF16)
            attn = jnp.where(causal, _dot_nt(qh, kh), 0.0)
            st = st_sc[h]
            o = _dot(attn.astype(BF16), vh) + _dot_nt(qh, st.astype(BF16))
            kv_t = _dot_tn(vh, k_out[:, ln].astype(BF16))
            st_sc[h] = st * dec[:, ln] + kv_t
            ms = jnp.sum(o * o, axis=-1, keepdims=True) * (1.0 / GLA_DV)
            heads.append(o * lax.rsqrt(ms + RMS_EPS) * vmask)
        outs.append(jnp.concatenate(heads, axis=1))
    o_all = jnp.concatenate(outs, axis=0)
    gr = r_ref[...]
    o_ref[...] = ((o_all * g_ref[...]) * (gr * jax.nn.sigmoid(gr))).astype(o_ref.dtype)


def _gla(zg, wa, ba, gn, B, S, tg):
    T = zg.shape[0]
    nt = S // tg
    col = lambda off, w: pl.BlockSpec((tg, w), lambda b, i: (b * nt + i, off // w))
    full = lambda a: pl.BlockSpec(a.shape, lambda b, i: (0,) * a.ndim)
    return pl.pallas_call(
        functools.partial(_gla_kernel, tg=tg),
        grid=(B, nt),
        in_specs=[col(G_OFF_Q, GLA_PAD), col(G_OFF_K, GLA_PAD), col(G_OFF_V, GLA_PAD),
                  col(G_OFF_R, GLA_PAD), col(G_OFF_A, LANES), full(wa), full(ba), full(gn)],
        out_specs=pl.BlockSpec((tg, GLA_PAD), lambda b, i: (b * nt + i, 0)),
        out_shape=jax.ShapeDtypeStruct((T, GLA_PAD), BF16),
        scratch_shapes=[pltpu.VMEM((GLA_HEADS, LANES, LANES), F32)],
        compiler_params=_cparams(("arbitrary", "arbitrary")),
        name="gla",
    )(zg, zg, zg, zg, zg, wa, ba, gn)


HALO = max(POOL_WINDOWS)


def _pool_kernel(u_ref, w_ref, s_ref, o_ref, xx):
    tp = u_ref.shape[0]
    i = pl.program_id(1)

    @pl.when(i == 0)
    def _():
        xx[0:HALO, :] = jnp.zeros((HALO, POOL_WIDTH), F32)

    @pl.when(i > 0)
    def _():
        xx[0:HALO, :] = xx[tp:tp + HALO, :]

    u = u_ref[...]
    xx[HALO:HALO + tp, :] = u
    lane = lax.broadcasted_iota(jnp.int32, (1, POOL_WIDTH), 1)
    grp = lane >> (POOL_GROUP.bit_length() - 1)
    pos = lax.broadcasted_iota(jnp.int32, (tp, 1), 0) + i * tp + 1
    acc = u
    pooled = jnp.zeros_like(u)
    for j in range(1, HALO):
        acc = acc + xx[HALO - j:HALO - j + tp, :]
        w = j + 1
        if w in POOL_WINDOWS:
            g = POOL_WINDOWS.index(w)
            cnt = jnp.minimum(pos, w).astype(F32)
            pooled = jnp.where(grp == g, acc / cnt - u, pooled)
    mixed = _dot(pooled.astype(BF16), w_ref[...])
    o_ref[...] = (mixed * s_ref[...]).astype(o_ref.dtype)


def _pool(zg, w_bd, scale, B, S, tp):
    T = zg.shape[0]
    nt = S // tp
    full = lambda a: pl.BlockSpec(a.shape, lambda b, i: (0,) * a.ndim)
    return pl.pallas_call(
        _pool_kernel,
        grid=(B, nt),
        in_specs=[pl.BlockSpec((tp, POOL_WIDTH), lambda b, i: (b * nt + i, G_OFF_U // POOL_WIDTH)),
                  full(w_bd), full(scale)],
        out_specs=pl.BlockSpec((tp, POOL_WIDTH), lambda b, i: (b * nt + i, 0)),
        out_shape=jax.ShapeDtypeStruct((T, POOL_WIDTH), BF16),
        scratch_shapes=[pltpu.VMEM((tp + HALO, POOL_WIDTH), F32)],
        compiler_params=_cparams(("arbitrary", "arbitrary")),
        name="pool",
    )(zg, w_bd, scale)


def _layer_norm(r, g, b):
    mu = jnp.mean(r, axis=-1, keepdims=True)
    d = r - mu
    var = jnp.mean(d * d, axis=-1, keepdims=True)
    return d * lax.rsqrt(var + LN_EPS) * g + b


def _outproj_kernel(of_ref, og_ref, op_ref, x_ref, wf_ref, wg_ref, wp_ref, gate_ref, lg_ref, lb_ref,
                    sc_ref, sh_ref, wr_ref, br_ref,
                    x1_ref, h2_ref, e_ref, gt_ref, cnt_ref, *, alpha):
    y = _dot(of_ref[...], wf_ref[...]) + _dot(og_ref[...], wg_ref[...]) + _dot(op_ref[...], wp_ref[...])
    r = alpha * x_ref[...] + (1.0 + gate_ref[0]) * y
    x1 = _layer_norm(r, lg_ref[...], lb_ref[...])
    x1_ref[...] = x1
    h2 = x1 * (1.0 + sc_ref[0]) + sh_ref[0]
    h2_ref[...] = h2
    h_hi = h2.astype(BF16)
    h_lo = (h2 - h_hi.astype(F32)).astype(BF16)
    logits = (_dot(h_hi, wr_ref[0]) + (_dot(h_hi, wr_ref[1]) + _dot(h_lo, wr_ref[0]))
              + br_ref[...])
    tm = logits.shape[0]
    lane_i = lax.broadcasted_iota(jnp.int32, (tm, LANES), 1)
    lane = lane_i.astype(F32)
    work = logits
    tops, idxs = [], []
    onehot = jnp.zeros((tm, LANES), F32)
    for _ in range(TOP_K):
        m = jnp.max(work, axis=-1, keepdims=True)
        idx = jnp.min(jnp.where(work == m, lane, float(LANES)), axis=-1, keepdims=True)
        sel = lane == idx
        onehot = onehot + sel.astype(F32)
        work = jnp.where(sel, -jnp.inf, work)
        tops.append(m)
        idxs.append(idx)
    ex = [jnp.exp(t - tops[0]) for t in tops]
    den = ex[0] + ex[1] + ex[2] + ex[3]
    e_out = jnp.zeros((tm, LANES), jnp.int32)
    g_out = jnp.zeros((tm, LANES), F32)
    for k in range(TOP_K):
        e_out = jnp.where(lane_i == k, idxs[k].astype(jnp.int32), e_out)
        g_out = jnp.where(lane_i == k, ex[k] / den, g_out)
    e_ref[...] = e_out
    gt_ref[...] = g_out
    for u in range(tm // ROUTE_TILE):
        rows = slice(u * ROUTE_TILE, (u + 1) * ROUTE_TILE)
        cnt_ref[u] = jnp.sum(onehot[rows], axis=0, keepdims=True).astype(jnp.int32)


def _outproj(o_fox, o_gla, o_pool, x2d, wf, wg, wp, gate1, ln_g, ln_b, scale2, shift2, wr, br,
             B, S, tm, alpha):
    T, D = x2d.shape
    nt = S // tm
    full = lambda a: pl.BlockSpec(a.shape, lambda i: (0,) * a.ndim)
    row = lambda w: pl.BlockSpec((tm, w), lambda i: (i, 0))
    mod = pl.BlockSpec((1, 1, D), lambda i: (i // nt, 0, 0))
    return pl.pallas_call(
        functools.partial(_outproj_kernel, alpha=alpha),
        grid=(T // tm,),
        in_specs=[row(FOX_PAD), row(GLA_PAD), row(POOL_WIDTH), row(D), full(wf), full(wg), full(wp),
                  mod, full(ln_g), full(ln_b), mod, mod, full(wr), full(br)],
        out_specs=[row(D), row(D), row(LANES), row(LANES),
                   pl.BlockSpec((tm // ROUTE_TILE, 1, LANES), lambda i: (i, 0, 0))],
        out_shape=[jax.ShapeDtypeStruct((T, D), F32), jax.ShapeDtypeStruct((T, D), F32),
                   jax.ShapeDtypeStruct((T, LANES), jnp.int32), jax.ShapeDtypeStruct((T, LANES), F32),
                   jax.ShapeDtypeStruct((T // ROUTE_TILE, 1, LANES), jnp.int32)],
        compiler_params=_cparams(("arbitrary",)),
        name="outproj_router",
    )(o_fox, o_gla, o_pool, x2d, wf, wg, wp, gate1, ln_g, ln_b, scale2, shift2, wr, br)


def _tile_slots(e_i32, off_row):
    td = e_i32.shape[0]
    lane = lax.broadcasted_iota(jnp.int32, (td, LANES), 1).astype(F32)
    ef = e_i32.astype(F32)
    sel = [lane == ef[:, k:k + 1] for k in range(TOP_K)]
    onehot = sel[0].astype(F32)
    for k in range(1, TOP_K):
        onehot = onehot + sel[k].astype(F32)
    rr = lax.broadcasted_iota(jnp.int32, (td, td), 0)
    cc = lax.broadcasted_iota(jnp.int32, (td, td), 1)
    stril = (cc < rr).astype(BF16)
    tab = _dot(stril, onehot.astype(BF16)) + off_row
    return [jnp.sum(jnp.where(sel[k], tab, 0.0), axis=-1, keepdims=True) for k in range(TOP_K)]


def _group_bits(max_groups):
    return [1 << s for s in range(max_groups.bit_length() - 1, -1, -1)]


PIECE_CLASSES = (ROUTE_TILE // ROW_GROUP).bit_length()
PIECE_LIST = 1024
SLOT_BITS = 8
assert ROUTE_SLOTS // ROW_GROUP < (1 << SLOT_BITS) and PIECE_CLASSES * N_EXPERTS <= PIECE_LIST


def _piece_lists(m, off, dst):
    nt = m.shape[0]
    j = jnp.arange(N_EXPERTS, dtype=jnp.int32)
    lists, counts = [], []
    for b in range(PIECE_CLASSES):
        has = (m >> b) & 1
        before = (m & ~((2 << b) - 1)) * ROW_GROUP
        packed = (((dst + before) // ROW_GROUP) << SLOT_BITS) | ((off + before) // ROW_GROUP)
        rank = jnp.cumsum(has, axis=1) - has
        hit = (has[:, :, None] == 1) & (rank[:, :, None] == j[None, None, :])
        lists.append(jnp.sum(jnp.where(hit, packed[:, :, None], 0), axis=1))
        counts.append(jnp.sum(has, axis=1))
    lst = jnp.concatenate(lists, axis=1).astype(jnp.int32)
    lst = jnp.pad(lst, ((0, 0), (0, PIECE_LIST - lst.shape[1])))
    return lst.reshape(-1), jnp.stack(counts, axis=1).astype(jnp.int32).reshape(-1)


def _for_each_piece(tile, cnt_ref, list_ref, fn):
    for b in range(PIECE_CLASSES):
        def body(i, carry, b=b):
            v = list_ref[b * N_EXPERTS + i]
            fn(pl.multiple_of((v & ((1 << SLOT_BITS) - 1)) * ROW_GROUP, ROW_GROUP),
               pl.multiple_of(lax.shift_right_logical(v, SLOT_BITS) * ROW_GROUP, ROW_GROUP),
               (1 << b) * ROW_GROUP)
            return carry

        lax.fori_loop(0, cnt_ref[tile * PIECE_CLASSES + b], body, 0)


def _for_each_total_piece(total_groups, fn):
    for bit in _group_bits(ROUTE_SLOTS // ROW_GROUP):
        @pl.when((total_groups & bit) != 0)
        def _(bit=bit):
            fn(bit * ROW_GROUP)


def _dispatch_kernel(cnt_ref, tot_ref, zblk_ref, nu_ref, list_ref, e_ref, offrow_ref, h_ref,
                     xr_ref, slot_ref, sort_sc, zero_sc, sems, zsem, *, n_blocks, nt):
    step = pl.program_id(0)

    @pl.when(step == 0)
    def _():
        zero_sc[...] = jnp.zeros_like(zero_sc)

        def zcopy(blk):
            return pltpu.make_async_copy(
                zero_sc, xr_ref.at[pl.ds(pl.multiple_of(blk * MOE_BLOCK, MOE_BLOCK), MOE_BLOCK), :], zsem)

        def zstart(e, c):
            @pl.when(zblk_ref[e] >= 0)
            def _():
                zcopy(zblk_ref[e]).start()
            return c

        def zwait(e, c):
            @pl.when(zblk_ref[e] >= 0)
            def _():
                zcopy(0).wait()
            return c

        def tstart(blk, c):
            zcopy(blk).start()
            return c

        def twait(blk, c):
            zcopy(0).wait()
            return c

        lax.fori_loop(0, N_EXPERTS, zstart, 0)
        lax.fori_loop(nu_ref[0], n_blocks, tstart, 0)
        lax.fori_loop(0, N_EXPERTS, zwait, 0)
        lax.fori_loop(nu_ref[0], n_blocks, twait, 0)

    td = h_ref.shape[0]
    slots = _tile_slots(e_ref[...], offrow_ref[0])
    lane = lax.broadcasted_iota(jnp.int32, (td, LANES), 1)
    cols = jnp.full((td, LANES), -1.0, F32)
    for k in range(TOP_K):
        cols = jnp.where(lane == k, slots[k], cols)
    slot_ref[...] = cols
    rows_t = cols.T
    sub = lax.broadcasted_iota(jnp.int32, (ROUTE_SLOTS, td), 0).astype(F32)
    pick = sub == rows_t[0:1, :]
    for k in range(1, TOP_K):
        pick = pick | (sub == rows_t[k:k + 1, :])
    buf = step & 1
    sort_sc[buf] = _dot(pick.astype(BF16), h_ref[...].astype(BF16))

    def start_chunk(slot0, row0, n):
        pltpu.make_async_copy(sort_sc.at[buf, pl.ds(slot0, n), :], xr_ref.at[pl.ds(row0, n), :],
                              sems.at[buf]).start()

    _for_each_piece(step, cnt_ref, list_ref, start_chunk)

    def wait_tile(tile, b):
        def wait_piece(n):
            pltpu.make_async_copy(sort_sc.at[b, pl.ds(0, n), :], xr_ref.at[pl.ds(0, n), :],
                                  sems.at[b]).wait()
        _for_each_total_piece(tot_ref[tile], wait_piece)

    @pl.when(step > 0)
    def _():
        wait_tile(step - 1, 1 - buf)

    @pl.when(step == nt - 1)
    def _():
        wait_tile(step, buf)


def _dispatch(h2, e_pad, tables, rows):
    T, D = h2.shape
    td = ROUTE_TILE
    nt = T // td
    piece_list, piece_cnt, tot_tab, zblk, n_used, off_rows = tables
    grid_spec = pltpu.PrefetchScalarGridSpec(
        num_scalar_prefetch=4,
        grid=(nt,),
        in_specs=[pl.BlockSpec((PIECE_LIST,), lambda i, *_: (i,), memory_space=pltpu.SMEM),
                  pl.BlockSpec((td, LANES), lambda i, *_: (i, 0)),
                  pl.BlockSpec((1, 1, LANES), lambda i, *_: (i, 0, 0)),
                  pl.BlockSpec((td, D), lambda i, *_: (i, 0))],
        out_specs=[pl.BlockSpec(memory_space=pl.ANY),
                   pl.BlockSpec((td, LANES), lambda i, *_: (i, 0))],
        scratch_shapes=[pltpu.VMEM((2, ROUTE_SLOTS, D), F32), pltpu.VMEM((MOE_BLOCK, D), F32),
                        pltpu.SemaphoreType.DMA((2,)), pltpu.SemaphoreType.DMA],
    )
    return pl.pallas_call(
        functools.partial(_dispatch_kernel, n_blocks=rows // MOE_BLOCK, nt=nt),
        grid_spec=grid_spec,
        out_shape=[jax.ShapeDtypeStruct((rows, D), F32), jax.ShapeDtypeStruct((T, LANES), F32)],
        compiler_params=_cparams(("arbitrary",)),
        name="moe_dispatch",
    )(piece_cnt, tot_tab, zblk, n_used, piece_list, e_pad, off_rows, h2)


def _expert_kernel(be_ref, nu_ref, vb_ref, x_ref, wgu_ref, bgu_ref, wd_ref, bd_ref, y_ref, wgu_sc, wd_sc):
    i = pl.program_id(0)
    used = i < nu_ref[0]

    @pl.when(used)
    def _():
        prev = be_ref[jnp.maximum(i - 1, 0)]

        @pl.when((i == 0) | (be_ref[i] != prev))
        def _():
            wgu_sc[...] = wgu_ref[0].astype(BF16)
            wd_sc[...] = wd_ref[0].astype(BF16)

        def mlp(rows):
            x = x_ref[rows, :].astype(BF16)
            gu = _dot(x, wgu_sc[...]) + bgu_ref[0]
            glu = jnp.minimum(gu[:, :D_EXPERT], SWIGLU_LIMIT)
            lin = jnp.clip(gu[:, D_EXPERT:], -SWIGLU_LIMIT, SWIGLU_LIMIT)
            act = glu * jax.nn.sigmoid(SWIGLU_ALPHA * glu) * (lin + 1.0)
            y_ref[rows, :] = _dot(act.astype(BF16), wd_sc[...]) + bd_ref[0]

        half = MOE_BLOCK // 2
        full_block = vb_ref[i] > half

        @pl.when(full_block)
        def _():
            mlp(slice(0, MOE_BLOCK))

        @pl.when(jnp.logical_not(full_block))
        def _():
            mlp(slice(0, half))
            y_ref[half:, :] = jnp.zeros((MOE_BLOCK - half, y_ref.shape[1]), y_ref.dtype)

    @pl.when(jnp.logical_not(used))
    def _():
        y_ref[...] = jnp.zeros_like(y_ref)


def _experts(x_rows, block_expert, n_used, valid_rows, w_gate_up, b_gate_up, w_down, b_down, layer):
    rows, D = x_rows.shape
    nb = rows // MOE_BLOCK
    E = w_gate_up.shape[1]
    grid_spec = pltpu.PrefetchScalarGridSpec(
        num_scalar_prefetch=3,
        grid=(nb,),
        in_specs=[pl.BlockSpec((MOE_BLOCK, D), lambda i, be, nu, vb: (jnp.minimum(i, nu[0] - 1), 0)),
                  pl.BlockSpec((1, D, 2 * D_EXPERT), lambda i, be, nu, vb: (layer * E + be[i], 0, 0)),
                  pl.BlockSpec((1, 1, 2 * D_EXPERT), lambda i, be, nu, vb: (layer * E + be[i], 0, 0)),
                  pl.BlockSpec((1, D_EXPERT, D), lambda i, be, nu, vb: (layer * E + be[i], 0, 0)),
                  pl.BlockSpec((1, 1, D), lambda i, be, nu, vb: (layer * E + be[i], 0, 0))],
        out_specs=pl.BlockSpec((MOE_BLOCK, D), lambda i, be, nu, vb: (i, 0)),
        scratch_shapes=[pltpu.VMEM((D, 2 * D_EXPERT), BF16), pltpu.VMEM((D_EXPERT, D), BF16)],
    )
    L = w_gate_up.shape[0]
    return pl.pallas_call(
        _expert_kernel,
        grid_spec=grid_spec,
        out_shape=jax.ShapeDtypeStruct((rows, D), F32),
        compiler_params=_cparams(("arbitrary",)),
        name="moe_experts",
    )(block_expert, n_used, valid_rows, x_rows,
      w_gate_up.reshape(L * E, D, 2 * D_EXPERT), b_gate_up.reshape(L * E, 1, 2 * D_EXPERT),
      w_down.reshape(L * E, D_EXPERT, D), b_down.reshape(L * E, 1, D))


def _combine_kernel(cnt_ref, tot_ref, list_ref, next_list_ref, slot_ref, gt_ref, y_ref, x_ref, gate_ref,
                    lg_ref, lb_ref, o_ref, ybuf, sems, *, nt, alpha):
    step = pl.program_id(0)
    buf = step & 1

    def fetch(tile, b, lst):
        def start_chunk(slot0, row0, n):
            pltpu.make_async_copy(y_ref.at[pl.ds(row0, n), :], ybuf.at[b, pl.ds(slot0, n), :],
                                  sems.at[b]).start()
        _for_each_piece(tile, cnt_ref, lst, start_chunk)

    @pl.when(step == 0)
    def _():
        ybuf[...] = jnp.zeros_like(ybuf)
        fetch(step, buf, list_ref)

    @pl.when(step + 1 < nt)
    def _():
        fetch(step + 1, 1 - buf, next_list_ref)

    def wait_piece(n):
        pltpu.make_async_copy(y_ref.at[pl.ds(0, n), :], ybuf.at[buf, pl.ds(0, n), :], sems.at[buf]).wait()

    _for_each_total_piece(tot_ref[step], wait_piece)

    td = x_ref.shape[0]
    slots = slot_ref[...]
    gt = gt_ref[...]
    lane = lax.broadcasted_iota(jnp.int32, (td, ROUTE_SLOTS), 1).astype(F32)
    w = jnp.where(lane == slots[:, 0:1], gt[:, 0:1], 0.0)
    for k in range(1, TOP_K):
        w = w + jnp.where(lane == slots[:, k:k + 1], gt[:, k:k + 1], 0.0)
    y = _dot(w.astype(BF16), ybuf[buf].astype(BF16))
    r = alpha * x_ref[...] + (1.0 + gate_ref[0]) * y
    o_ref[...] = _layer_norm(r, lg_ref[...], lb_ref[...])


def _combine(y_rows, slots, tables, gates, x1, gate2, ln_g, ln_b, B, S, alpha):
    T, D = x1.shape
    td = ROUTE_TILE
    nt = T // td
    per_batch = S // td
    piece_list, piece_cnt, tot_tab, _, _, _ = tables
    grid_spec = pltpu.PrefetchScalarGridSpec(
        num_scalar_prefetch=2,
        grid=(nt,),
        in_specs=[pl.BlockSpec((PIECE_LIST,), lambda i, *_: (i,), memory_space=pltpu.SMEM),
                  pl.BlockSpec((PIECE_LIST,), lambda i, *_: (jnp.minimum(i + 1, nt - 1),),
                               memory_space=pltpu.SMEM),
                  pl.BlockSpec((td, LANES), lambda i, *_: (i, 0)),
                  pl.BlockSpec((td, LANES), lambda i, *_: (i, 0)),
                  pl.BlockSpec(memory_space=pl.ANY),
                  pl.BlockSpec((td, D), lambda i, *_: (i, 0)),
                  pl.BlockSpec((1, 1, D), lambda i, *_: (i // per_batch, 0, 0)),
                  pl.BlockSpec((1, D), lambda i, *_: (0, 0)),
                  pl.BlockSpec((1, D), lambda i, *_: (0, 0))],
        out_specs=pl.BlockSpec((td, D), lambda i, *_: (i, 0)),
        scratch_shapes=[pltpu.VMEM((2, ROUTE_SLOTS, D), F32), pltpu.SemaphoreType.DMA((2,))],
    )
    return pl.pallas_call(
        functools.partial(_combine_kernel, nt=nt, alpha=alpha),
        grid_spec=grid_spec,
        out_shape=jax.ShapeDtypeStruct((T, D), F32),
        compiler_params=_cparams(("arbitrary",)),
        name="moe_combine",
    )(piece_cnt, tot_tab, piece_list, piece_list, slots, gates, y_rows, x1, gate2, ln_g, ln_b)


def _tile(n, pref):
    t = min(n, pref)
    assert n % t == 0, (n, t)
    return t


def kernel(x, c, w_ada, b_ada, w_in, b_in, gla_w_a2, gla_b_a, gla_norm_g, pool_w, pool_scale, w_out,
           ln1_g, ln1_b, w_router, b_router, w_gate_up, b_gate_up, w_down, b_down, ln2_g, ln2_b):
    B, S, D = x.shape
    L = w_ada.shape[0]
    T = B * S
    assert D == D_MODEL and S % GLA_CHUNK == 0
    alpha = float((2 * L) ** 0.25)
    assert T % ROUTE_TILE == 0
    n_tiles = T // ROUTE_TILE
    max_rows = T * TOP_K + n_tiles * N_EXPERTS * (ROW_GROUP - 1) + N_EXPERTS * (MOE_BLOCK - 1)
    n_blocks = -(-max_rows // MOE_BLOCK)
    rows = n_blocks * MOE_BLOCK

    mod = _ada_mod(c, w_ada, b_ada)
    prep_all = _prep_inproj(w_in, b_in)
    wa_all = jnp.pad(_pad_heads(gla_w_a2, GLA_HEADS, GLA_DK),
                     ((0, 0), (0, LANES - GLA_GATE_RANK), (0, 0))).astype(BF16)
    ba_all = _pad_heads(gla_b_a[:, None, :], GLA_HEADS, GLA_DK)
    gn_all = _pad_heads(gla_norm_g[:, None, :], GLA_HEADS, GLA_DV)
    w_bd_all = jnp.zeros((L, POOL_WIDTH, POOL_WIDTH), F32)
    for g in range(len(POOL_WINDOWS)):
        sl = slice(g * POOL_GROUP, (g + 1) * POOL_GROUP)
        w_bd_all = w_bd_all.at[:, sl, sl].set(pool_w[:, g])
    w_bd_all = w_bd_all.astype(BF16)
    pad_rows = lambda w, heads, dim: jnp.swapaxes(_pad_heads(jnp.swapaxes(w, 1, 2), heads, dim), 1, 2)
    wf_all = pad_rows(w_out[:, :FOX_WIDTH], FOX_HEADS, FOX_HEAD_DIM).astype(BF16)
    wgl_all = pad_rows(w_out[:, FOX_WIDTH:FOX_WIDTH + GLA_VWIDTH], GLA_HEADS, GLA_DV).astype(BF16)
    wp_all = w_out[:, FOX_WIDTH + GLA_VWIDTH:].astype(BF16)
    wr_f32 = _pad_last(w_router, LANES)
    wr_hi = wr_f32.astype(BF16)
    wr_all = jnp.stack([wr_hi, (wr_f32 - wr_hi.astype(F32)).astype(BF16)], axis=1)
    br_all = jnp.pad(b_router[:, None, :], ((0, 0), (0, 0), (0, LANES - N_EXPERTS)), constant_values=NEG_BIG)

    x2d = x.reshape(T, D)
    for l in range(L):
        shift1, scale1, gate1, shift2, scale2, gate2 = [mod[l, :, m] for m in range(N_MOD)]
        prep = [(w[l], b[l]) for w, b in prep_all]
        zq, zk, zv, zg = _inproj(x2d, scale1, shift1, prep, B, S, _tile(S, 512))
        q2, k2, vt, stats = _fgate(zq, zk, zv, zg, B, S, _tile(S, ATTN_BLOCK))
        o_fox = _attention(q2, k2, vt, stats, B, S, _tile(S, ATTN_BLOCK))
        o_gla = _gla(zg, wa_all[l], ba_all[l], gn_all[l], B, S, _tile(S, 256))
        o_pool = _pool(zg, w_bd_all[l], pool_scale[l][None, :], B, S, _tile(S, 512))
        x1, h2, e_pad, g_pad, cnt = _outproj(
            o_fox, o_gla, o_pool, x2d, wf_all[l], wgl_all[l], wp_all[l], gate1,
            ln1_g[l][None, :], ln1_b[l][None, :], scale2, shift2, wr_all[l], br_all[l],
            B, S, _tile(S, 512), alpha)
        cte = cnt[:, 0, :N_EXPERTS]
        c8 = (cte + ROW_GROUP - 1) // ROW_GROUP * ROW_GROUP
        off = jnp.cumsum(c8, axis=1) - c8
        per_expert = jnp.sum(c8, axis=0)
        padded = (per_expert + MOE_BLOCK - 1) // MOE_BLOCK * MOE_BLOCK
        pend = jnp.cumsum(padded)
        pstart = pend - padded
        dst = pstart[None, :] + jnp.cumsum(c8, axis=0) - c8
        n_used = (pend[-1] // MOE_BLOCK).astype(jnp.int32)
        blk_start = jnp.arange(n_blocks, dtype=jnp.int32) * MOE_BLOCK
        be = jnp.minimum(jnp.sum(blk_start[:, None] >= pend[None, :], axis=1), N_EXPERTS - 1).astype(jnp.int32)
        be = jnp.where(jnp.arange(n_blocks) < n_used, be, be[jnp.maximum(n_used - 1, 0)])
        zblk = jnp.where(padded > 0, pend // MOE_BLOCK - 1, -1).astype(jnp.int32)
        off_rows = _pad_last(off.astype(F32), LANES)[:, None, :]
        n_used = n_used.reshape(1)
        piece_list, piece_cnt = _piece_lists(c8 // ROW_GROUP, off, dst)
        tables = (piece_list, piece_cnt, (jnp.sum(c8, axis=1) // ROW_GROUP).astype(jnp.int32),
                  zblk, n_used, off_rows)
        x_rows, slots = _dispatch(h2, e_pad, tables, rows)
        valid_rows = jnp.clip(per_expert[be] - (blk_start - pstart[be]), 0, MOE_BLOCK).astype(jnp.int32)
        y_rows = _experts(x_rows, be, n_used, valid_rows, w_gate_up, b_gate_up, w_down, b_down, l)
        x2d = _combine(y_rows, slots, tables, g_pad, x1, gate2,
                       ln2_g[l][None, :], ln2_b[l][None, :], B, S, alpha)
    return x2d.reshape(B, S, D)
```

```python
import functools

import numpy as np
import jax
import jax.numpy as jnp
from jax import lax
from jax.experimental import pallas as pl
from jax.experimental.pallas import tpu as pltpu

F32 = jnp.float32
BF16 = jnp.bfloat16

D_MODEL = 1024
FOX_HEADS = 6
FOX_HEAD_DIM = 64
FOX_WIDTH = FOX_HEADS * FOX_HEAD_DIM
GLA_HEADS = 4
GLA_DV = 96
GLA_DK = 48
GLA_KWIDTH = GLA_HEADS * GLA_DK
GLA_VWIDTH = GLA_HEADS * GLA_DV
GLA_GATE_RANK = 16
GLA_GATE_TAU = 16.0
GLA_CHUNK = 64
POOL_WINDOWS = (2, 4, 8, 16)
POOL_GROUP = 64
POOL_WIDTH = len(POOL_WINDOWS) * POOL_GROUP
N_EXPERTS = 32
TOP_K = 4
D_EXPERT = 1024
SWIGLU_ALPHA = 1.702
SWIGLU_LIMIT = 7.0
N_MOD = 6
LN_EPS = 1e-5
RMS_EPS = 1e-6

LANES = 128
VMEM_LIMIT_BYTES = 56 * 1024 * 1024

FOX_PAD = FOX_HEADS * LANES
GLA_PAD = GLA_HEADS * LANES
BIAS_LANE = FOX_HEAD_DIM
G_OFF_Q, G_OFF_K, G_OFF_V, G_OFF_R = 0, GLA_PAD, 2 * GLA_PAD, 3 * GLA_PAD
G_OFF_U = 4 * GLA_PAD
G_OFF_A = G_OFF_U + POOL_WIDTH
G_OFF_F = G_OFF_A + LANES
G_WIDTH = G_OFF_F + LANES

STAT_ROWS = 8
PRUNE_MARGIN = 105.0
NORM_SLACK = 1.01
ATTN_BLOCK = 512
MOE_BLOCK = 512
ROUTE_TILE = 256
ROW_GROUP = 8
ROUTE_SLOTS = ROUTE_TILE * TOP_K + N_EXPERTS * ROW_GROUP
COMMON_GROUPS = 8
NEG_BIG = -1e30


def _cparams(sem, vmem=None):
    return pltpu.CompilerParams(dimension_semantics=sem, vmem_limit_bytes=vmem or VMEM_LIMIT_BYTES)


def _log_sigmoid(x):
    return jnp.minimum(x, 0.0) - jnp.log(1.0 + jnp.exp(-jnp.abs(x)))


def _split3(x):
    hi = x.astype(BF16)
    r = x - hi.astype(F32)
    mid = r.astype(BF16)
    lo = (r - mid.astype(F32)).astype(BF16)
    return hi, mid, lo


def _dot(a, b):
    return jnp.dot(a, b, preferred_element_type=F32)


def _dot_nt(a, b):
    return lax.dot_general(a, b, (((1,), (1,)), ((), ())), preferred_element_type=F32)


def _dot_tn(a, b):
    return lax.dot_general(a, b, (((0,), (0,)), ((), ())), preferred_element_type=F32)


def _ada_kernel(c_ref, w_ref, b_ref, o_ref):
    c = c_ref[...]
    cond = c * jax.nn.sigmoid(c)
    o_ref[0] = jnp.dot(cond, w_ref[0], preferred_element_type=F32,
                       precision=lax.Precision.HIGHEST) + b_ref[0]


def _ada_mod(c, w_ada, b_ada):
    L, D, N = w_ada.shape
    B = c.shape[0]
    rows = 8
    c_pad = jnp.zeros((rows, D), F32).at[:B].set(c)
    tn = 1536
    out = pl.pallas_call(
        _ada_kernel,
        grid=(L, N // tn),
        in_specs=[pl.BlockSpec((rows, D), lambda l, j: (0, 0)),
                  pl.BlockSpec((1, D, tn), lambda l, j: (l, 0, j)),
                  pl.BlockSpec((1, 1, tn), lambda l, j: (l, 0, j))],
        out_specs=pl.BlockSpec((1, rows, tn), lambda l, j: (l, 0, j)),
        out_shape=jax.ShapeDtypeStruct((L, rows, N), F32),
        compiler_params=_cparams(("arbitrary", "arbitrary")),
        name="ada_mod",
    )(c_pad, w_ada, b_ada.reshape(L, 1, N))
    return out[:, :B].reshape(L, B, N_MOD, 1, D)


def _inproj_kernel(x_ref, sc_ref, sh_ref, wq_ref, wk_ref, wv_ref, wg_ref,
                   bq_ref, bk_ref, bv_ref, bg_ref, q_ref, k_ref, v_ref, g_ref):
    h = (x_ref[...] * (1.0 + sc_ref[0]) + sh_ref[0]).astype(BF16)
    q_ref[...] = (_dot(h, wq_ref[...]) + bq_ref[...]).astype(BF16)
    k_ref[...] = (_dot(h, wk_ref[...]) + bk_ref[...]).astype(BF16)
    v_ref[...] = (_dot(h, wv_ref[...]) + bv_ref[...]).astype(BF16)
    g_ref[...] = _dot(h, wg_ref[...]) + bg_ref[...]


def _pad_heads(w, heads, dim):
    lead = w.shape[:-1]
    w = w.reshape(lead + (heads, dim))
    w = jnp.pad(w, [(0, 0)] * len(lead) + [(0, 0), (0, LANES - dim)])
    return w.reshape(lead + (heads * LANES,))


def _pad_last(a, width):
    return jnp.pad(a, [(0, 0)] * (a.ndim - 1) + [(0, width - a.shape[-1])])


def _prep_inproj(w_in, b_in):
    W = jnp.concatenate([w_in, b_in[:, None, :]], axis=1)
    o = 0
    fq = W[..., o:o + FOX_WIDTH]; o += FOX_WIDTH
    fk = W[..., o:o + FOX_WIDTH]; o += FOX_WIDTH
    fv = W[..., o:o + FOX_WIDTH]; o += FOX_WIDTH
    ff = W[..., o:o + FOX_HEADS]; o += FOX_HEADS
    gq = W[..., o:o + GLA_KWIDTH]; o += GLA_KWIDTH
    gk = W[..., o:o + GLA_KWIDTH]; o += GLA_KWIDTH
    gv = W[..., o:o + GLA_VWIDTH]; o += GLA_VWIDTH
    gr = W[..., o:o + GLA_VWIDTH]; o += GLA_VWIDTH
    ga = W[..., o:o + GLA_GATE_RANK]; o += GLA_GATE_RANK
    pu = W[..., o:o + POOL_WIDTH]
    wq = fq * (FOX_HEAD_DIM ** -0.5)
    wk = fk
    wv = fv
    wg = jnp.concatenate([
        _pad_heads(gq, GLA_HEADS, GLA_DK), _pad_heads(gk, GLA_HEADS, GLA_DK),
        _pad_heads(gv, GLA_HEADS, GLA_DV), _pad_heads(gr, GLA_HEADS, GLA_DV),
        pu, _pad_last(ga, LANES), _pad_last(ff, LANES)], axis=-1)
    outs = []
    for w in (wq, wk, wv, wg):
        outs.append((w[:, :-1].astype(BF16), w[:, -1:].astype(F32)))
    return outs


def _inproj(x2d, scale, shift, prep, B, S, tm):
    T, D = x2d.shape
    (wq, bq), (wk, bk), (wv, bv), (wg, bg) = prep
    nt = S // tm
    full = lambda a: pl.BlockSpec(a.shape, lambda i: (0,) * a.ndim)
    row = lambda w: pl.BlockSpec((tm, w), lambda i: (i, 0))
    mod = pl.BlockSpec((1, 1, D), lambda i: (i // nt, 0, 0))
    return pl.pallas_call(
        _inproj_kernel,
        grid=(T // tm,),
        in_specs=[row(D), mod, mod, full(wq), full(wk), full(wv), full(wg),
                  full(bq), full(bk), full(bv), full(bg)],
        out_specs=[row(FOX_WIDTH), row(FOX_WIDTH), row(FOX_WIDTH), row(G_WIDTH)],
        out_shape=[jax.ShapeDtypeStruct((T, FOX_WIDTH), BF16)] * 3
                  + [jax.ShapeDtypeStruct((T, G_WIDTH), F32)],
        compiler_params=_cparams(("arbitrary",)),
        name="inproj",
    )(x2d, scale, shift, wq, wk, wv, wg, bq, bk, bv, bg)


def _aug_constants():
    pq = np.zeros((3 * LANES, FOX_PAD), np.float32)
    pk = np.zeros((3 * LANES, FOX_PAD), np.float32)
    cq = np.zeros((1, FOX_PAD), np.float32)
    ck = np.zeros((1, FOX_PAD), np.float32)
    cv = np.zeros((1, FOX_PAD), np.float32)
    spread = np.zeros((FOX_WIDTH, FOX_PAD), np.float32)
    for h in range(FOX_HEADS):
        base = h * LANES + BIAS_LANE
        cv[0, base] = 1.0
        for d in range(FOX_HEAD_DIM):
            spread[h * FOX_HEAD_DIM + d, h * LANES + d] = 1.0
        for p in range(3):
            pq[p * LANES + h, base + p] = 1.0
            pk[p * LANES + h, base + 3 + p] = -1.0
            cq[0, base + 3 + p] = 1.0
            ck[0, base + p] = 1.0
    return pq, pk, cq, ck, cv, spread


def _fgate_kernel(q_ref, k_ref, v_ref, f_ref, pq_ref, pk_ref, cq_ref, ck_ref, cv_ref, sp_ref,
                  q2_ref, k2_ref, vt_ref, st_ref, carry):
    @pl.when(pl.program_id(1) == 0)
    def _():
        carry[...] = jnp.zeros_like(carry)

    vf = _dot(v_ref[...], sp_ref[...]) + cv_ref[...]
    for h in range(FOX_HEADS):
        sl = slice(h * LANES, (h + 1) * LANES)
        vt_ref[0, 0, sl, :] = vf[:, sl].T.astype(BF16)

    tf = f_ref.shape[0]
    ls = _log_sigmoid(f_ref[...])
    r = lax.broadcasted_iota(jnp.int32, (tf, tf), 0)
    c = lax.broadcasted_iota(jnp.int32, (tf, tf), 1)
    tri = (c <= r).astype(BF16)
    hi, mid, lo = _split3(ls)
    cs = _dot(tri, hi) + _dot(tri, mid) + _dot(tri, lo)
    F = cs + carry[...]
    carry[...] = F[tf - 1:tf, :]
    fh, fm, fl = _split3(F)
    f3 = jnp.concatenate([fh, fm, fl], axis=1)
    qf = _dot(q_ref[...], sp_ref[...])
    kf = _dot(k_ref[...], sp_ref[...])
    q2_ref[...] = (qf + _dot(f3, pq_ref[...]) + cq_ref[...]).astype(BF16)
    k2_ref[...] = (kf + _dot(f3, pk_ref[...]) + ck_ref[...]).astype(BF16)
    lane = lax.broadcasted_iota(jnp.int32, (1, LANES), 1)
    qstat = jnp.zeros((1, LANES), F32)
    kstat = jnp.zeros((1, LANES), F32)
    for h in range(FOX_HEADS):
        sl = slice(h * LANES, (h + 1) * LANES)
        qm = jnp.max(jnp.sum(qf[:, sl] * qf[:, sl], axis=-1, keepdims=True), axis=0, keepdims=True)
        km = jnp.max(jnp.sum(kf[:, sl] * kf[:, sl], axis=-1, keepdims=True), axis=0, keepdims=True)
        qstat = jnp.where(lane == h, qm, qstat)
        kstat = jnp.where(lane == h, km, kstat)
    row = lax.broadcasted_iota(jnp.int32, (STAT_ROWS, LANES), 0)
    st_ref[...] = jnp.where(row == 0, F[0:1, :],
                            jnp.where(row == 1, F[tf - 1:tf, :],
                                      jnp.where(row == 2, qstat, jnp.where(row == 3, kstat, 0.0))))


def _fgate(zq, zk, zv, zg, B, S, tf):
    T = zq.shape[0]
    nt = S // tf
    pq, pk, cq, ck, cv, sp = _aug_constants()
    pq, pk, sp = jnp.asarray(pq, BF16), jnp.asarray(pk, BF16), jnp.asarray(sp, BF16)
    cq, ck, cv = jnp.asarray(cq), jnp.asarray(ck), jnp.asarray(cv)
    full = lambda a: pl.BlockSpec(a.shape, lambda b, i: (0,) * a.ndim)
    row = pl.BlockSpec((tf, FOX_PAD), lambda b, i: (b * nt + i, 0))
    packed = pl.BlockSpec((tf, FOX_WIDTH), lambda b, i: (b * nt + i, 0))
    return pl.pallas_call(
        _fgate_kernel,
        grid=(B, nt),
        in_specs=[packed, packed, packed,
                  pl.BlockSpec((tf, LANES), lambda b, i: (b * nt + i, G_OFF_F // LANES)),
                  full(pq), full(pk), full(cq), full(ck), full(cv), full(sp)],
        out_specs=[row, row, pl.BlockSpec((1, 1, FOX_PAD, tf), lambda b, i: (b, i, 0, 0)),
                   pl.BlockSpec((STAT_ROWS, LANES), lambda b, i: (b * nt + i, 0))],
        out_shape=[jax.ShapeDtypeStruct((T, FOX_PAD), BF16)] * 2
                  + [jax.ShapeDtypeStruct((B, nt, FOX_PAD, tf), BF16),
                     jax.ShapeDtypeStruct((B * nt * STAT_ROWS, LANES), F32)],
        scratch_shapes=[pltpu.VMEM((1, LANES), F32)],
        compiler_params=_cparams(("arbitrary", "arbitrary")),
        name="fgate",
    )(zq, zk, zv, zg, pq, pk, cq, ck, cv, sp)


def _attn_kernel(ff_ref, fl_ref, qn_ref, kn_ref, q_ref, k_ref, vt_ref, o_ref, m_sc, acc_sc, s_sc,
                 *, blk, nq, nb):
    h = pl.program_id(0)
    i = pl.program_id(1)

    def scores(slot, j):
        off = pl.multiple_of(j * blk, blk)
        for b in range(nb):
            s_sc[slot, b] = _dot_nt(k_ref[b, pl.ds(off, blk), :], q_ref[b])

    def softmax_pv(slot, j, diag):
        for b in range(nb):
            s = s_sc[slot, b]
            if diag:
                r = lax.broadcasted_iota(jnp.int32, (blk, blk), 0)
                c = lax.broadcasted_iota(jnp.int32, (blk, blk), 1)
                s = jnp.where(r <= c, s, NEG_BIG)
            m_prev = m_sc[b]
            m_new = jnp.maximum(m_prev, jnp.max(s, axis=0, keepdims=True))
            alpha = jnp.exp(m_prev - m_new)
            p = jnp.exp(s - m_new)
            acc_sc[b] = alpha * acc_sc[b] + _dot(vt_ref[b, j], p.astype(BF16))
            m_sc[b] = m_new

    m_sc[...] = jnp.full_like(m_sc, NEG_BIG)
    acc_sc[...] = jnp.zeros_like(acc_sc)
    scores(0, i)
    scores(1, jnp.maximum(i - 1, 0))
    softmax_pv(0, i, True)

    n = jnp.int32(0)
    for b in range(nb):
        base = (b * FOX_HEADS + h) * nq
        slack = (qn_ref[base + i] * kn_ref[b * FOX_HEADS + h] + ff_ref[base + i]
                 - jnp.min(m_sc[b]) + PRUNE_MARGIN)

        def cond(t, base=base, slack=slack):
            return jnp.logical_and(t < i, slack - fl_ref[base + jnp.maximum(i - 1 - t, 0)] >= 0.0)

        n = jnp.maximum(n, lax.while_loop(cond, lambda t: t + 1, jnp.int32(0)))

    def pair(u, carry):
        t = 1 + 2 * u
        ja = i - t
        scores(0, jnp.maximum(ja - 1, 0))
        softmax_pv(1, ja, False)

        @pl.when(t + 1 <= n)
        def _():
            scores(1, jnp.maximum(ja - 2, 0))
            softmax_pv(0, ja - 1, False)

        return carry

    lax.fori_loop(0, lax.shift_right_logical(n + 1, 1), pair, 0)
    for b in range(nb):
        acc = acc_sc[b]
        o_ref[b] = (acc / acc[BIAS_LANE:BIAS_LANE + 1, :]).T.astype(o_ref.dtype)


def _attention(q2, k2, vt, stats, B, S, blk):
    T = q2.shape[0]
    nq = S // blk
    H = FOX_HEADS
    st = stats.reshape(B, nq, STAT_ROWS, LANES)[:, :, :, :H]
    tab = lambda r: jnp.transpose(st[:, :, r, :], (0, 2, 1)).reshape(-1)
    ffirst, flast = tab(0), tab(1)
    qn = jnp.sqrt(tab(2)) * NORM_SLACK
    kn = jnp.sqrt(jnp.max(st[:, :, 3, :], axis=1)).reshape(-1) * NORM_SLACK
    r3 = lambda a: a.reshape(B, S, FOX_PAD)
    qspec = pl.BlockSpec((B, blk, LANES), lambda h, i, *_: (0, i, h))
    kspec = pl.BlockSpec((B, S, LANES), lambda h, i, *_: (0, 0, h))
    vtspec = pl.BlockSpec((B, nq, LANES, blk), lambda h, i, *_: (0, 0, h, 0))
    grid_spec = pltpu.PrefetchScalarGridSpec(
        num_scalar_prefetch=4,
        grid=(H, nq),
        in_specs=[qspec, kspec, vtspec],
        out_specs=qspec,
        scratch_shapes=[pltpu.VMEM((B, 1, blk), F32), pltpu.VMEM((B, LANES, blk), F32),
                        pltpu.VMEM((2, B, blk, blk), F32)],
    )
    out = pl.pallas_call(
        functools.partial(_attn_kernel, blk=blk, nq=nq, nb=B),
        grid_spec=grid_spec,
        out_shape=jax.ShapeDtypeStruct((B, S, FOX_PAD), BF16),
        compiler_params=_cparams(("arbitrary", "arbitrary")),
        name="fox_attention",
    )(ffirst, flast, qn, kn, r3(q2), r3(k2), vt)
    return out.reshape(T, FOX_PAD)


def _gla_kernel(q_ref, k_ref, v_ref, r_ref, a_ref, wa_ref, ba_ref, g_ref, o_ref, st_sc, *, tg):
    @pl.when(pl.program_id(1) == 0)
    def _():
        st_sc[...] = jnp.zeros_like(st_sc)

    C = GLA_CHUNK
    nchunk = tg // C
    la = _log_sigmoid(_dot(a_ref[...].astype(BF16), wa_ref[...]) + ba_ref[...]) * (1.0 / GLA_GATE_TAU)
    r = lax.broadcasted_iota(jnp.int32, (tg, tg), 0)
    c = lax.broadcasted_iota(jnp.int32, (tg, tg), 1)
    shift = C.bit_length() - 1
    tri = ((c <= r) & ((c >> shift) == (r >> shift))).astype(BF16)
    hi, mid, lo = _split3(la)
    b = _dot(tri, hi) + _dot(tri, mid) + _dot(tri, lo)
    eb = jnp.exp(b)
    q_in = q_ref[...] * (GLA_DK ** -0.5) * eb
    k_in = k_ref[...] * jnp.exp(-b)
    v = v_ref[...]
    rc = lax.broadcasted_iota(jnp.int32, (C, C), 0)
    cc = lax.broadcasted_iota(jnp.int32, (C, C), 1)
    causal = cc <= rc
    lane = lax.broadcasted_iota(jnp.int32, (1, LANES), 1)
    vmask = (lane < GLA_DV).astype(F32)
    outs = []
    for ci in range(nchunk):
        rows = slice(ci * C, (ci + 1) * C)
        b_last = b[ci * C + C - 1:ci * C + C, :]
        k_out = k_ref[rows, :] * jnp.exp(b_last - b[rows, :])
        dec = jnp.exp(b_last)
        heads = []
        for h in range(GLA_HEADS):
            ln = slice(h * LANES, (h + 1) * LANES)
            qh = q_in[rows, ln].astype(BF16)
            kh = k_in[rows, ln].astype(BF16)
            vh = v[rows, ln].astype(BF16)
            attn = jnp.where(causal, _dot_nt(qh, kh), 0.0)
            st = st_sc[h]
            o = _dot(attn.astype(BF16), vh) + _dot_nt(qh, st.astype(BF16))
            kv_t = _dot_tn(vh, k_out[:, ln].astype(BF16))
            st_sc[h] = st * dec[:, ln] + kv_t
            ms = jnp.sum(o * o, axis=-1, keepdims=True) * (1.0 / GLA_DV)
            heads.append(o * lax.rsqrt(ms + RMS_EPS) * vmask)
        outs.append(jnp.concatenate(heads, axis=1))
    o_all = jnp.concatenate(outs, axis=0)
    gr = r_ref[...]
    o_ref[...] = ((o_all * g_ref[...]) * (gr * jax.nn.sigmoid(gr))).astype(o_ref.dtype)


def _gla(zg, wa, ba, gn, B, S, tg):
    T = zg.shape[0]
    nt = S // tg
    col = lambda off, w: pl.BlockSpec((tg, w), lambda b, i: (b * nt + i, off // w))
    full = lambda a: pl.BlockSpec(a.shape, lambda b, i: (0,) * a.ndim)
    return pl.pallas_call(
        functools.partial(_gla_kernel, tg=tg),
        grid=(B, nt),
        in_specs=[col(G_OFF_Q, GLA_PAD), col(G_OFF_K, GLA_PAD), col(G_OFF_V, GLA_PAD),
                  col(G_OFF_R, GLA_PAD), col(G_OFF_A, LANES), full(wa), full(ba), full(gn)],
        out_specs=pl.BlockSpec((tg, GLA_PAD), lambda b, i: (b * nt + i, 0)),
        out_shape=jax.ShapeDtypeStruct((T, GLA_PAD), BF16),
        scratch_shapes=[pltpu.VMEM((GLA_HEADS, LANES, LANES), F32)],
        compiler_params=_cparams(("arbitrary", "arbitrary")),
        name="gla",
    )(zg, zg, zg, zg, zg, wa, ba, gn)


HALO = max(POOL_WINDOWS)


def _pool_kernel(u_ref, w_ref, s_ref, o_ref, xx):
    tp = u_ref.shape[0]
    i = pl.program_id(1)

    @pl.when(i == 0)
    def _():
        xx[0:HALO, :] = jnp.zeros((HALO, POOL_WIDTH), F32)

    @pl.when(i > 0)
    def _():
        xx[0:HALO, :] = xx[tp:tp + HALO, :]

    u = u_ref[...]
    xx[HALO:HALO + tp, :] = u
    lane = lax.broadcasted_iota(jnp.int32, (1, POOL_WIDTH), 1)
    grp = lane >> (POOL_GROUP.bit_length() - 1)
    pos = lax.broadcasted_iota(jnp.int32, (tp, 1), 0) + i * tp + 1
    acc = u
    pooled = jnp.zeros_like(u)
    for j in range(1, HALO):
        acc = acc + xx[HALO - j:HALO - j + tp, :]
        w = j + 1
        if w in POOL_WINDOWS:
            g = POOL_WINDOWS.index(w)
            cnt = jnp.minimum(pos, w).astype(F32)
            pooled = jnp.where(grp == g, acc / cnt - u, pooled)
    mixed = _dot(pooled.astype(BF16), w_ref[...])
    o_ref[...] = (mixed * s_ref[...]).astype(o_ref.dtype)


def _pool(zg, w_bd, scale, B, S, tp):
    T = zg.shape[0]
    nt = S // tp
    full = lambda a: pl.BlockSpec(a.shape, lambda b, i: (0,) * a.ndim)
    return pl.pallas_call(
        _pool_kernel,
        grid=(B, nt),
        in_specs=[pl.BlockSpec((tp, POOL_WIDTH), lambda b, i: (b * nt + i, G_OFF_U // POOL_WIDTH)),
                  full(w_bd), full(scale)],
        out_specs=pl.BlockSpec((tp, POOL_WIDTH), lambda b, i: (b * nt + i, 0)),
        out_shape=jax.ShapeDtypeStruct((T, POOL_WIDTH), BF16),
        scratch_shapes=[pltpu.VMEM((tp + HALO, POOL_WIDTH), F32)],
        compiler_params=_cparams(("arbitrary", "arbitrary")),
        name="pool",
    )(zg, w_bd, scale)


def _layer_norm(r, g, b):
    mu = jnp.mean(r, axis=-1, keepdims=True)
    d = r - mu
    var = jnp.mean(d * d, axis=-1, keepdims=True)
    return d * lax.rsqrt(var + LN_EPS) * g + b


def _outproj_kernel(of_ref, og_ref, op_ref, x_ref, wf_ref, wg_ref, wp_ref, gate_ref, lg_ref, lb_ref,
                    sc_ref, sh_ref, wr_ref, br_ref,
                    x1_ref, h2_ref, e_ref, gt_ref, cnt_ref, *, alpha):
    y = _dot(of_ref[...], wf_ref[...]) + _dot(og_ref[...], wg_ref[...]) + _dot(op_ref[...], wp_ref[...])
    r = alpha * x_ref[...] + (1.0 + gate_ref[0]) * y
    x1 = _layer_norm(r, lg_ref[...], lb_ref[...])
    x1_ref[...] = x1
    h2 = x1 * (1.0 + sc_ref[0]) + sh_ref[0]
    h2_ref[...] = h2
    h_hi = h2.astype(BF16)
    h_lo = (h2 - h_hi.astype(F32)).astype(BF16)
    logits = (_dot(h_hi, wr_ref[0]) + (_dot(h_hi, wr_ref[1]) + _dot(h_lo, wr_ref[0]))
              + br_ref[...])
    tm = logits.shape[0]
    lane_i = lax.broadcasted_iota(jnp.int32, (tm, LANES), 1)
    lane = lane_i.astype(F32)
    work = logits
    tops, idxs = [], []
    onehot = jnp.zeros((tm, LANES), F32)
    for _ in range(TOP_K):
        m = jnp.max(work, axis=-1, keepdims=True)
        idx = jnp.min(jnp.where(work == m, lane, float(LANES)), axis=-1, keepdims=True)
        sel = lane == idx
        onehot = onehot + sel.astype(F32)
        work = jnp.where(sel, -jnp.inf, work)
        tops.append(m)
        idxs.append(idx)
    ex = [jnp.exp(t - tops[0]) for t in tops]
    den = ex[0] + ex[1] + ex[2] + ex[3]
    e_out = jnp.zeros((tm, LANES), jnp.int32)
    g_out = jnp.zeros((tm, LANES), F32)
    for k in range(TOP_K):
        e_out = jnp.where(lane_i == k, idxs[k].astype(jnp.int32), e_out)
        g_out = jnp.where(lane_i == k, ex[k] / den, g_out)
    e_ref[...] = e_out
    gt_ref[...] = g_out
    for u in range(tm // ROUTE_TILE):
        rows = slice(u * ROUTE_TILE, (u + 1) * ROUTE_TILE)
        cnt_ref[u] = jnp.sum(onehot[rows], axis=0, keepdims=True).astype(jnp.int32)


def _outproj(o_fox, o_gla, o_pool, x2d, wf, wg, wp, gate1, ln_g, ln_b, scale2, shift2, wr, br,
             B, S, tm, alpha):
    T, D = x2d.shape
    nt = S // tm
    full = lambda a: pl.BlockSpec(a.shape, lambda i: (0,) * a.ndim)
    row = lambda w: pl.BlockSpec((tm, w), lambda i: (i, 0))
    mod = pl.BlockSpec((1, 1, D), lambda i: (i // nt, 0, 0))
    return pl.pallas_call(
        functools.partial(_outproj_kernel, alpha=alpha),
        grid=(T // tm,),
        in_specs=[row(FOX_PAD), row(GLA_PAD), row(POOL_WIDTH), row(D), full(wf), full(wg), full(wp),
                  mod, full(ln_g), full(ln_b), mod, mod, full(wr), full(br)],
        out_specs=[row(D), row(D), row(LANES), row(LANES),
                   pl.BlockSpec((tm // ROUTE_TILE, 1, LANES), lambda i: (i, 0, 0))],
        out_shape=[jax.ShapeDtypeStruct((T, D), F32), jax.ShapeDtypeStruct((T, D), F32),
                   jax.ShapeDtypeStruct((T, LANES), jnp.int32), jax.ShapeDtypeStruct((T, LANES), F32),
                   jax.ShapeDtypeStruct((T // ROUTE_TILE, 1, LANES), jnp.int32)],
        compiler_params=_cparams(("arbitrary",)),
        name="outproj_router",
    )(o_fox, o_gla, o_pool, x2d, wf, wg, wp, gate1, ln_g, ln_b, scale2, shift2, wr, br)


def _tile_slots(e_i32, off_row):
    td = e_i32.shape[0]
    lane = lax.broadcasted_iota(jnp.int32, (td, LANES), 1).astype(F32)
    ef = e_i32.astype(F32)
    sel = [lane == ef[:, k:k + 1] for k in range(TOP_K)]
    onehot = sel[0].astype(F32)
    for k in range(1, TOP_K):
        onehot = onehot + sel[k].astype(F32)
    rr = lax.broadcasted_iota(jnp.int32, (td, td), 0)
    cc = lax.broadcasted_iota(jnp.int32, (td, td), 1)
    stril = (cc < rr).astype(BF16)
    tab = _dot(stril, onehot.astype(BF16)) + off_row
    return [jnp.sum(jnp.where(sel[k], tab, 0.0), axis=-1, keepdims=True) for k in range(TOP_K)]


def _group_bits(max_groups):
    return [1 << s for s in range(max_groups.bit_length() - 1, -1, -1)]


PIECE_CLASSES = (ROUTE_TILE // ROW_GROUP).bit_length()
PIECE_LIST = 1024
SLOT_BITS = 8
assert ROUTE_SLOTS // ROW_GROUP < (1 << SLOT_BITS) and PIECE_CLASSES * N_EXPERTS <= PIECE_LIST


def _piece_lists(m, off, dst):
    nt = m.shape[0]
    j = jnp.arange(N_EXPERTS, dtype=jnp.int32)
    lists, counts = [], []
    for b in range(PIECE_CLASSES):
        has = (m >> b) & 1
        before = (m & ~((2 << b) - 1)) * ROW_GROUP
        packed = (((dst + before) // ROW_GROUP) << SLOT_BITS) | ((off + before) // ROW_GROUP)
        rank = jnp.cumsum(has, axis=1) - has
        hit = (has[:, :, None] == 1) & (rank[:, :, None] == j[None, None, :])
        lists.append(jnp.sum(jnp.where(hit, packed[:, :, None], 0), axis=1))
        counts.append(jnp.sum(has, axis=1))
    lst = jnp.concatenate(lists, axis=1).astype(jnp.int32)
    lst = jnp.pad(lst, ((0, 0), (0, PIECE_LIST - lst.shape[1])))
    return lst.reshape(-1), jnp.stack(counts, axis=1).astype(jnp.int32).reshape(-1)


def _for_each_piece(tile, cnt_ref, list_ref, fn):
    for b in range(PIECE_CLASSES):
        def body(i, carry, b=b):
            v = list_ref[b * N_EXPERTS + i]
            fn(pl.multiple_of((v & ((1 << SLOT_BITS) - 1)) * ROW_GROUP, ROW_GROUP),
               pl.multiple_of(lax.shift_right_logical(v, SLOT_BITS) * ROW_GROUP, ROW_GROUP),
               (1 << b) * ROW_GROUP)
            return carry

        lax.fori_loop(0, cnt_ref[tile * PIECE_CLASSES + b], body, 0)


def _for_each_total_piece(total_groups, fn):
    for bit in _group_bits(ROUTE_SLOTS // ROW_GROUP):
        @pl.when((total_groups & bit) != 0)
        def _(bit=bit):
            fn(bit * ROW_GROUP)


def _dispatch_kernel(cnt_ref, tot_ref, zblk_ref, nu_ref, list_ref, e_ref, offrow_ref, h_ref,
                     xr_ref, slot_ref, sort_sc, zero_sc, sems, zsem, *, n_blocks, nt):
    step = pl.program_id(0)

    @pl.when(step == 0)
    def _():
        zero_sc[...] = jnp.zeros_like(zero_sc)

        def zcopy(blk):
            return pltpu.make_async_copy(
                zero_sc, xr_ref.at[pl.ds(pl.multiple_of(blk * MOE_BLOCK, MOE_BLOCK), MOE_BLOCK), :], zsem)

        def zstart(e, c):
            @pl.when(zblk_ref[e] >= 0)
            def _():
                zcopy(zblk_ref[e]).start()
            return c

        def zwait(e, c):
            @pl.when(zblk_ref[e] >= 0)
            def _():
                zcopy(0).wait()
            return c

        def tstart(blk, c):
            zcopy(blk).start()
            return c

        def twait(blk, c):
            zcopy(0).wait()
            return c

        lax.fori_loop(0, N_EXPERTS, zstart, 0)
        lax.fori_loop(nu_ref[0], n_blocks, tstart, 0)
        lax.fori_loop(0, N_EXPERTS, zwait, 0)
        lax.fori_loop(nu_ref[0], n_blocks, twait, 0)

    td = h_ref.shape[0]
    slots = _tile_slots(e_ref[...], offrow_ref[0])
    lane = lax.broadcasted_iota(jnp.int32, (td, LANES), 1)
    cols = jnp.full((td, LANES), -1.0, F32)
    for k in range(TOP_K):
        cols = jnp.where(lane == k, slots[k], cols)
    slot_ref[...] = cols
    rows_t = cols.T
    sub = lax.broadcasted_iota(jnp.int32, (ROUTE_SLOTS, td), 0).astype(F32)
    pick = sub == rows_t[0:1, :]
    for k in range(1, TOP_K):
        pick = pick | (sub == rows_t[k:k + 1, :])
    buf = step & 1
    sort_sc[buf] = _dot(pick.astype(BF16), h_ref[...].astype(BF16))

    def start_chunk(slot0, row0, n):
        pltpu.make_async_copy(sort_sc.at[buf, pl.ds(slot0, n), :], xr_ref.at[pl.ds(row0, n), :],
                              sems.at[buf]).start()

    _for_each_piece(step, cnt_ref, list_ref, start_chunk)

    def wait_tile(tile, b):
        def wait_piece(n):
            pltpu.make_async_copy(sort_sc.at[b, pl.ds(0, n), :], xr_ref.at[pl.ds(0, n), :],
                                  sems.at[b]).wait()
        _for_each_total_piece(tot_ref[tile], wait_piece)

    @pl.when(step > 0)
    def _():
        wait_tile(step - 1, 1 - buf)

    @pl.when(step == nt - 1)
    def _():
        wait_tile(step, buf)


def _dispatch(h2, e_pad, tables, rows):
    T, D = h2.shape
    td = ROUTE_TILE
    nt = T // td
    piece_list, piece_cnt, tot_tab, zblk, n_used, off_rows = tables
    grid_spec = pltpu.PrefetchScalarGridSpec(
        num_scalar_prefetch=4,
        grid=(nt,),
        in_specs=[pl.BlockSpec((PIECE_LIST,), lambda i, *_: (i,), memory_space=pltpu.SMEM),
                  pl.BlockSpec((td, LANES), lambda i, *_: (i, 0)),
                  pl.BlockSpec((1, 1, LANES), lambda i, *_: (i, 0, 0)),
                  pl.BlockSpec((td, D), lambda i, *_: (i, 0))],
        out_specs=[pl.BlockSpec(memory_space=pl.ANY),
                   pl.BlockSpec((td, LANES), lambda i, *_: (i, 0))],
        scratch_shapes=[pltpu.VMEM((2, ROUTE_SLOTS, D), F32), pltpu.VMEM((MOE_BLOCK, D), F32),
                        pltpu.SemaphoreType.DMA((2,)), pltpu.SemaphoreType.DMA],
    )
    return pl.pallas_call(
        functools.partial(_dispatch_kernel, n_blocks=rows // MOE_BLOCK, nt=nt),
        grid_spec=grid_spec,
        out_shape=[jax.ShapeDtypeStruct((rows, D), F32), jax.ShapeDtypeStruct((T, LANES), F32)],
        compiler_params=_cparams(("arbitrary",)),
        name="moe_dispatch",
    )(piece_cnt, tot_tab, zblk, n_used, piece_list, e_pad, off_rows, h2)


def _expert_kernel(be_ref, nu_ref, vb_ref, par_ref, nxt_ref, x_ref, wgu_hbm, bgu_ref, wd_hbm, bd_ref, y_ref,
                   wgu_sc, wd_sc, gu_buf, d_buf, sems, *, first_expert):
    i = pl.program_id(0)
    used = i < nu_ref[0]

    def fetch(e, slot):
        return (pltpu.make_async_copy(wgu_hbm.at[first_expert + e], gu_buf.at[slot], sems.at[slot, 0]),
                pltpu.make_async_copy(wd_hbm.at[first_expert + e], d_buf.at[slot], sems.at[slot, 1]))

    @pl.when(used)
    def _():
        prev = be_ref[jnp.maximum(i - 1, 0)]

        @pl.when((i == 0) | (be_ref[i] != prev))
        def _():
            slot = par_ref[i]

            @pl.when(i == 0)
            def _():
                for cp in fetch(be_ref[i], slot):
                    cp.start()

            for cp in fetch(be_ref[i], slot):
                cp.wait()
            wgu_sc[...] = gu_buf[slot].astype(BF16)
            wd_sc[...] = d_buf[slot].astype(BF16)

            @pl.when(nxt_ref[i] >= 0)
            def _():
                for cp in fetch(nxt_ref[i], 1 - slot):
                    cp.start()

        def mlp(rows):
            x = x_ref[rows, :].astype(BF16)
            gu = _dot(x, wgu_sc[...]) + bgu_ref[0]
            glu = jnp.minimum(gu[:, :D_EXPERT], SWIGLU_LIMIT)
            lin = jnp.clip(gu[:, D_EXPERT:], -SWIGLU_LIMIT, SWIGLU_LIMIT)
            act = glu * jax.nn.sigmoid(SWIGLU_ALPHA * glu) * (lin + 1.0)
            y_ref[rows, :] = _dot(act.astype(BF16), wd_sc[...]) + bd_ref[0]

        half = MOE_BLOCK // 2
        full_block = vb_ref[i] > half

        @pl.when(full_block)
        def _():
            mlp(slice(0, MOE_BLOCK))

        @pl.when(jnp.logical_not(full_block))
        def _():
            mlp(slice(0, half))
            y_ref[half:, :] = jnp.zeros((MOE_BLOCK - half, y_ref.shape[1]), y_ref.dtype)

    @pl.when(jnp.logical_not(used))
    def _():
        y_ref[...] = jnp.zeros_like(y_ref)


def _experts(x_rows, block_expert, n_used, valid_rows, slot_parity, next_expert,
             w_gate_up, b_gate_up, w_down, b_down, layer):
    rows, D = x_rows.shape
    nb = rows // MOE_BLOCK
    E = w_gate_up.shape[1]
    grid_spec = pltpu.PrefetchScalarGridSpec(
        num_scalar_prefetch=5,
        grid=(nb,),
        in_specs=[pl.BlockSpec((MOE_BLOCK, D), lambda i, be, nu, *_: (jnp.minimum(i, nu[0] - 1), 0)),
                  pl.BlockSpec(memory_space=pl.ANY),
                  pl.BlockSpec((1, 1, 2 * D_EXPERT), lambda i, be, *_: (layer * E + be[i], 0, 0)),
                  pl.BlockSpec(memory_space=pl.ANY),
                  pl.BlockSpec((1, 1, D), lambda i, be, *_: (layer * E + be[i], 0, 0))],
        out_specs=pl.BlockSpec((MOE_BLOCK, D), lambda i, *_: (i, 0)),
        scratch_shapes=[pltpu.VMEM((D, 2 * D_EXPERT), BF16), pltpu.VMEM((D_EXPERT, D), BF16),
                        pltpu.VMEM((2, D, 2 * D_EXPERT), F32), pltpu.VMEM((2, D_EXPERT, D), F32),
                        pltpu.SemaphoreType.DMA((2, 2))],
    )
    L = w_gate_up.shape[0]
    return pl.pallas_call(
        functools.partial(_expert_kernel, first_expert=layer * E),
        grid_spec=grid_spec,
        out_shape=jax.ShapeDtypeStruct((rows, D), F32),
        compiler_params=_cparams(("arbitrary",)),
        name="moe_experts",
    )(block_expert, n_used, valid_rows, slot_parity, next_expert, x_rows,
      w_gate_up.reshape(L * E, D, 2 * D_EXPERT), b_gate_up.reshape(L * E, 1, 2 * D_EXPERT),
      w_down.reshape(L * E, D_EXPERT, D), b_down.reshape(L * E, 1, D))


def _combine_kernel(cnt_ref, tot_ref, list_ref, next_list_ref, slot_ref, gt_ref, y_ref, x_ref, gate_ref,
                    lg_ref, lb_ref, o_ref, ybuf, sems, *, nt, alpha):
    step = pl.program_id(0)
    buf = step & 1

    def fetch(tile, b, lst):
        def start_chunk(slot0, row0, n):
            pltpu.make_async_copy(y_ref.at[pl.ds(row0, n), :], ybuf.at[b, pl.ds(slot0, n), :],
                                  sems.at[b]).start()
        _for_each_piece(tile, cnt_ref, lst, start_chunk)

    @pl.when(step == 0)
    def _():
        ybuf[...] = jnp.zeros_like(ybuf)
        fetch(step, buf, list_ref)

    @pl.when(step + 1 < nt)
    def _():
        fetch(step + 1, 1 - buf, next_list_ref)

    def wait_piece(n):
        pltpu.make_async_copy(y_ref.at[pl.ds(0, n), :], ybuf.at[buf, pl.ds(0, n), :], sems.at[buf]).wait()

    _for_each_total_piece(tot_ref[step], wait_piece)

    td = x_ref.shape[0]
    slots = slot_ref[...]
    gt = gt_ref[...]
    lane = lax.broadcasted_iota(jnp.int32, (td, ROUTE_SLOTS), 1).astype(F32)
    w = jnp.where(lane == slots[:, 0:1], gt[:, 0:1], 0.0)
    for k in range(1, TOP_K):
        w = w + jnp.where(lane == slots[:, k:k + 1], gt[:, k:k + 1], 0.0)
    y = _dot(w.astype(BF16), ybuf[buf].astype(BF16))
    r = alpha * x_ref[...] + (1.0 + gate_ref[0]) * y
    o_ref[...] = _layer_norm(r, lg_ref[...], lb_ref[...])


def _combine(y_rows, slots, tables, gates, x1, gate2, ln_g, ln_b, B, S, alpha):
    T, D = x1.shape
    td = ROUTE_TILE
    nt = T // td
    per_batch = S // td
    piece_list, piece_cnt, tot_tab, _, _, _ = tables
    grid_spec = pltpu.PrefetchScalarGridSpec(
        num_scalar_prefetch=2,
        grid=(nt,),
        in_specs=[pl.BlockSpec((PIECE_LIST,), lambda i, *_: (i,), memory_space=pltpu.SMEM),
                  pl.BlockSpec((PIECE_LIST,), lambda i, *_: (jnp.minimum(i + 1, nt - 1),),
                               memory_space=pltpu.SMEM),
                  pl.BlockSpec((td, LANES), lambda i, *_: (i, 0)),
                  pl.BlockSpec((td, LANES), lambda i, *_: (i, 0)),
                  pl.BlockSpec(memory_space=pl.ANY),
                  pl.BlockSpec((td, D), lambda i, *_: (i, 0)),
                  pl.BlockSpec((1, 1, D), lambda i, *_: (i // per_batch, 0, 0)),
                  pl.BlockSpec((1, D), lambda i, *_: (0, 0)),
                  pl.BlockSpec((1, D), lambda i, *_: (0, 0))],
        out_specs=pl.BlockSpec((td, D), lambda i, *_: (i, 0)),
        scratch_shapes=[pltpu.VMEM((2, ROUTE_SLOTS, D), F32), pltpu.SemaphoreType.DMA((2,))],
    )
    return pl.pallas_call(
        functools.partial(_combine_kernel, nt=nt, alpha=alpha),
        grid_spec=grid_spec,
        out_shape=jax.ShapeDtypeStruct((T, D), F32),
        compiler_params=_cparams(("arbitrary",)),
        name="moe_combine",
    )(piece_cnt, tot_tab, piece_list, piece_list, slots, gates, y_rows, x1, gate2, ln_g, ln_b)


def _tile(n, pref):
    t = min(n, pref)
    assert n % t == 0, (n, t)
    return t


def kernel(x, c, w_ada, b_ada, w_in, b_in, gla_w_a2, gla_b_a, gla_norm_g, pool_w, pool_scale, w_out,
           ln1_g, ln1_b, w_router, b_router, w_gate_up, b_gate_up, w_down, b_down, ln2_g, ln2_b):
    B, S, D = x.shape
    L = w_ada.shape[0]
    T = B * S
    assert D == D_MODEL and S % GLA_CHUNK == 0
    alpha = float((2 * L) ** 0.25)
    assert T % ROUTE_TILE == 0
    n_tiles = T // ROUTE_TILE
    max_rows = T * TOP_K + n_tiles * N_EXPERTS * (ROW_GROUP - 1) + N_EXPERTS * (MOE_BLOCK - 1)
    n_blocks = -(-max_rows // MOE_BLOCK)
    rows = n_blocks * MOE_BLOCK

    mod = _ada_mod(c, w_ada, b_ada)
    prep_all = _prep_inproj(w_in, b_in)
    wa_all = jnp.pad(_pad_heads(gla_w_a2, GLA_HEADS, GLA_DK),
                     ((0, 0), (0, LANES - GLA_GATE_RANK), (0, 0))).astype(BF16)
    ba_all = _pad_heads(gla_b_a[:, None, :], GLA_HEADS, GLA_DK)
    gn_all = _pad_heads(gla_norm_g[:, None, :], GLA_HEADS, GLA_DV)
    w_bd_all = jnp.zeros((L, POOL_WIDTH, POOL_WIDTH), F32)
    for g in range(len(POOL_WINDOWS)):
        sl = slice(g * POOL_GROUP, (g + 1) * POOL_GROUP)
        w_bd_all = w_bd_all.at[:, sl, sl].set(pool_w[:, g])
    w_bd_all = w_bd_all.astype(BF16)
    pad_rows = lambda w, heads, dim: jnp.swapaxes(_pad_heads(jnp.swapaxes(w, 1, 2), heads, dim), 1, 2)
    wf_all = pad_rows(w_out[:, :FOX_WIDTH], FOX_HEADS, FOX_HEAD_DIM).astype(BF16)
    wgl_all = pad_rows(w_out[:, FOX_WIDTH:FOX_WIDTH + GLA_VWIDTH], GLA_HEADS, GLA_DV).astype(BF16)
    wp_all = w_out[:, FOX_WIDTH + GLA_VWIDTH:].astype(BF16)
    wr_f32 = _pad_last(w_router, LANES)
    wr_hi = wr_f32.astype(BF16)
    wr_all = jnp.stack([wr_hi, (wr_f32 - wr_hi.astype(F32)).astype(BF16)], axis=1)
    br_all = jnp.pad(b_router[:, None, :], ((0, 0), (0, 0), (0, LANES - N_EXPERTS)), constant_values=NEG_BIG)

    x2d = x.reshape(T, D)
    for l in range(L):
        shift1, scale1, gate1, shift2, scale2, gate2 = [mod[l, :, m] for m in range(N_MOD)]
        prep = [(w[l], b[l]) for w, b in prep_all]
        zq, zk, zv, zg = _inproj(x2d, scale1, shift1, prep, B, S, _tile(S, 512))
        q2, k2, vt, stats = _fgate(zq, zk, zv, zg, B, S, _tile(S, ATTN_BLOCK))
        o_fox = _attention(q2, k2, vt, stats, B, S, _tile(S, ATTN_BLOCK))
        o_gla = _gla(zg, wa_all[l], ba_all[l], gn_all[l], B, S, _tile(S, 256))
        o_pool = _pool(zg, w_bd_all[l], pool_scale[l][None, :], B, S, _tile(S, 512))
        x1, h2, e_pad, g_pad, cnt = _outproj(
            o_fox, o_gla, o_pool, x2d, wf_all[l], wgl_all[l], wp_all[l], gate1,
            ln1_g[l][None, :], ln1_b[l][None, :], scale2, shift2, wr_all[l], br_all[l],
            B, S, _tile(S, 512), alpha)
        cte = cnt[:, 0, :N_EXPERTS]
        c8 = (cte + ROW_GROUP - 1) // ROW_GROUP * ROW_GROUP
        off = jnp.cumsum(c8, axis=1) - c8
        per_expert = jnp.sum(c8, axis=0)
        padded = (per_expert + MOE_BLOCK - 1) // MOE_BLOCK * MOE_BLOCK
        pend = jnp.cumsum(padded)
        pstart = pend - padded
        dst = pstart[None, :] + jnp.cumsum(c8, axis=0) - c8
        n_used = (pend[-1] // MOE_BLOCK).astype(jnp.int32)
        blk_start = jnp.arange(n_blocks, dtype=jnp.int32) * MOE_BLOCK
        be = jnp.minimum(jnp.sum(blk_start[:, None] >= pend[None, :], axis=1), N_EXPERTS - 1).astype(jnp.int32)
        be = jnp.where(jnp.arange(n_blocks) < n_used, be, be[jnp.maximum(n_used - 1, 0)])
        zblk = jnp.where(padded > 0, pend // MOE_BLOCK - 1, -1).astype(jnp.int32)
        off_rows = _pad_last(off.astype(F32), LANES)[:, None, :]
        n_used = n_used.reshape(1)
        piece_list, piece_cnt = _piece_lists(c8 // ROW_GROUP, off, dst)
        tables = (piece_list, piece_cnt, (jnp.sum(c8, axis=1) // ROW_GROUP).astype(jnp.int32),
                  zblk, n_used, off_rows)
        x_rows, slots = _dispatch(h2, e_pad, tables, rows)
        valid_rows = jnp.clip(per_expert[be] - (blk_start - pstart[be]), 0, MOE_BLOCK).astype(jnp.int32)
        owns = padded > 0
        ids = jnp.arange(N_EXPERTS, dtype=jnp.int32)
        later = jnp.where(owns[None, :] & (ids[None, :] > ids[:, None]), ids[None, :], N_EXPERTS)
        nxt_e = jnp.min(later, axis=1)
        nxt_e = jnp.where(nxt_e < N_EXPERTS, nxt_e, -1).astype(jnp.int32)
        par_e = ((jnp.cumsum(owns) - 1) & 1).astype(jnp.int32)
        y_rows = _experts(x_rows, be, n_used, valid_rows, par_e[be], nxt_e[be],
                          w_gate_up, b_gate_up, w_down, b_down, l)
        x2d = _combine(y_rows, slots, tables, g_pad, x1, gate2,
                       ln2_g[l][None, :], ln2_b[l][None, :], B, S, alpha)
    return x2d.reshape(B, S, D)
```

```python
import functools

import numpy as np
import jax
import jax.numpy as jnp
from jax import lax
from jax.experimental import pallas as pl
from jax.experimental.pallas import tpu as pltpu

F32 = jnp.float32
BF16 = jnp.bfloat16

D_MODEL = 1024
FOX_HEADS = 6
FOX_HEAD_DIM = 64
FOX_WIDTH = FOX_HEADS * FOX_HEAD_DIM
GLA_HEADS = 4
GLA_DV = 96
GLA_DK = 48
GLA_KWIDTH = GLA_HEADS * GLA_DK
GLA_VWIDTH = GLA_HEADS * GLA_DV
GLA_GATE_RANK = 16
GLA_GATE_TAU = 16.0
GLA_CHUNK = 64
POOL_WINDOWS = (2, 4, 8, 16)
POOL_GROUP = 64
POOL_WIDTH = len(POOL_WINDOWS) * POOL_GROUP
N_EXPERTS = 32
TOP_K = 4
D_EXPERT = 1024
SWIGLU_ALPHA = 1.702
SWIGLU_LIMIT = 7.0
N_MOD = 6
LN_EPS = 1e-5
RMS_EPS = 1e-6

LANES = 128
VMEM_LIMIT_BYTES = 56 * 1024 * 1024

FOX_PAD = FOX_HEADS * LANES
GLA_PAD = GLA_HEADS * LANES
BIAS_LANE = FOX_HEAD_DIM
G_OFF_Q, G_OFF_K, G_OFF_V, G_OFF_R = 0, GLA_PAD, 2 * GLA_PAD, 3 * GLA_PAD
G_OFF_U = 4 * GLA_PAD
G_OFF_A = G_OFF_U + POOL_WIDTH
G_OFF_F = G_OFF_A + LANES
G_WIDTH = G_OFF_F + LANES

STAT_ROWS = 8
PRUNE_MARGIN = 105.0
NORM_SLACK = 1.01
ATTN_BLOCK = 512
MOE_BLOCK = 512
ROUTE_TILE = 256
ROW_GROUP = 8
ROUTE_SLOTS = ROUTE_TILE * TOP_K + N_EXPERTS * ROW_GROUP
NEG_BIG = -1e30


def _cparams(sem, vmem=None):
    return pltpu.CompilerParams(dimension_semantics=sem, vmem_limit_bytes=vmem or VMEM_LIMIT_BYTES)


def _log_sigmoid(x):
    return jnp.minimum(x, 0.0) - jnp.log(1.0 + jnp.exp(-jnp.abs(x)))


def _split3(x):
    hi = x.astype(BF16)
    r = x - hi.astype(F32)
    mid = r.astype(BF16)
    lo = (r - mid.astype(F32)).astype(BF16)
    return hi, mid, lo


def _dot(a, b):
    return jnp.dot(a, b, preferred_element_type=F32)


def _dot_nt(a, b):
    return lax.dot_general(a, b, (((1,), (1,)), ((), ())), preferred_element_type=F32)


def _dot_tn(a, b):
    return lax.dot_general(a, b, (((0,), (0,)), ((), ())), preferred_element_type=F32)


def _ada_kernel(c_ref, w_ref, b_ref, o_ref):
    c = c_ref[...]
    cond = c * jax.nn.sigmoid(c)
    o_ref[0] = jnp.dot(cond, w_ref[0], preferred_element_type=F32,
                       precision=lax.Precision.HIGHEST) + b_ref[0]


def _ada_mod(c, w_ada, b_ada):
    L, D, N = w_ada.shape
    B = c.shape[0]
    rows = 8
    c_pad = jnp.zeros((rows, D), F32).at[:B].set(c)
    tn = 1536
    out = pl.pallas_call(
        _ada_kernel,
        grid=(L, N // tn),
        in_specs=[pl.BlockSpec((rows, D), lambda l, j: (0, 0)),
                  pl.BlockSpec((1, D, tn), lambda l, j: (l, 0, j)),
                  pl.BlockSpec((1, 1, tn), lambda l, j: (l, 0, j))],
        out_specs=pl.BlockSpec((1, rows, tn), lambda l, j: (l, 0, j)),
        out_shape=jax.ShapeDtypeStruct((L, rows, N), F32),
        compiler_params=_cparams(("arbitrary", "arbitrary")),
        name="ada_mod",
    )(c_pad, w_ada, b_ada.reshape(L, 1, N))
    return out[:, :B].reshape(L, B, N_MOD, 1, D)


def _inproj_kernel(x_ref, sc_ref, sh_ref, wq_ref, wk_ref, wv_ref, wg_ref,
                   bq_ref, bk_ref, bv_ref, bg_ref, q_ref, k_ref, v_ref, g_ref):
    h = (x_ref[...] * (1.0 + sc_ref[0]) + sh_ref[0]).astype(BF16)
    q_ref[...] = (_dot(h, wq_ref[...]) + bq_ref[...]).astype(BF16)
    k_ref[...] = (_dot(h, wk_ref[...]) + bk_ref[...]).astype(BF16)
    v_ref[...] = (_dot(h, wv_ref[...]) + bv_ref[...]).astype(BF16)
    g_ref[...] = _dot(h, wg_ref[...]) + bg_ref[...]


def _pad_heads(w, heads, dim):
    lead = w.shape[:-1]
    w = w.reshape(lead + (heads, dim))
    w = jnp.pad(w, [(0, 0)] * len(lead) + [(0, 0), (0, LANES - dim)])
    return w.reshape(lead + (heads * LANES,))


def _pad_last(a, width):
    return jnp.pad(a, [(0, 0)] * (a.ndim - 1) + [(0, width - a.shape[-1])])


def _excl_prefix(a, axis):
    n = a.shape[axis]
    idx = jnp.arange(n)
    earlier = idx[None, :] < idx[:, None]
    am = jnp.moveaxis(a, axis, -1)
    out = jnp.sum(jnp.where(earlier, am[..., None, :], 0), axis=-1)
    return jnp.moveaxis(out, -1, axis)


def _prep_inproj(w_in, b_in):
    W = jnp.concatenate([w_in, b_in[:, None, :]], axis=1)
    o = 0
    fq = W[..., o:o + FOX_WIDTH]; o += FOX_WIDTH
    fk = W[..., o:o + FOX_WIDTH]; o += FOX_WIDTH
    fv = W[..., o:o + FOX_WIDTH]; o += FOX_WIDTH
    ff = W[..., o:o + FOX_HEADS]; o += FOX_HEADS
    gq = W[..., o:o + GLA_KWIDTH]; o += GLA_KWIDTH
    gk = W[..., o:o + GLA_KWIDTH]; o += GLA_KWIDTH
    gv = W[..., o:o + GLA_VWIDTH]; o += GLA_VWIDTH
    gr = W[..., o:o + GLA_VWIDTH]; o += GLA_VWIDTH
    ga = W[..., o:o + GLA_GATE_RANK]; o += GLA_GATE_RANK
    pu = W[..., o:o + POOL_WIDTH]
    wq = fq * (FOX_HEAD_DIM ** -0.5)
    wk = fk
    wv = fv
    wg = jnp.concatenate([
        _pad_heads(gq, GLA_HEADS, GLA_DK), _pad_heads(gk, GLA_HEADS, GLA_DK),
        _pad_heads(gv, GLA_HEADS, GLA_DV), _pad_heads(gr, GLA_HEADS, GLA_DV),
        pu, _pad_last(ga, LANES), _pad_last(ff, LANES)], axis=-1)
    outs = []
    for w in (wq, wk, wv, wg):
        outs.append((w[:, :-1].astype(BF16), w[:, -1:].astype(F32)))
    return outs


def _inproj(x2d, scale, shift, prep, B, S, tm):
    T, D = x2d.shape
    (wq, bq), (wk, bk), (wv, bv), (wg, bg) = prep
    nt = S // tm
    full = lambda a: pl.BlockSpec(a.shape, lambda i: (0,) * a.ndim)
    row = lambda w: pl.BlockSpec((tm, w), lambda i: (i, 0))
    mod = pl.BlockSpec((1, 1, D), lambda i: (i // nt, 0, 0))
    return pl.pallas_call(
        _inproj_kernel,
        grid=(T // tm,),
        in_specs=[row(D), mod, mod, full(wq), full(wk), full(wv), full(wg),
                  full(bq), full(bk), full(bv), full(bg)],
        out_specs=[row(FOX_WIDTH), row(FOX_WIDTH), row(FOX_WIDTH), row(G_WIDTH)],
        out_shape=[jax.ShapeDtypeStruct((T, FOX_WIDTH), BF16)] * 3
                  + [jax.ShapeDtypeStruct((T, G_WIDTH), F32)],
        compiler_params=_cparams(("arbitrary",)),
        name="inproj",
    )(x2d, scale, shift, wq, wk, wv, wg, bq, bk, bv, bg)


def _aug_constants():
    pq = np.zeros((3 * LANES, FOX_PAD), np.float32)
    pk = np.zeros((3 * LANES, FOX_PAD), np.float32)
    cq = np.zeros((1, FOX_PAD), np.float32)
    ck = np.zeros((1, FOX_PAD), np.float32)
    cv = np.zeros((1, FOX_PAD), np.float32)
    spread = np.zeros((FOX_WIDTH, FOX_PAD), np.float32)
    for h in range(FOX_HEADS):
        base = h * LANES + BIAS_LANE
        cv[0, base] = 1.0
        for d in range(FOX_HEAD_DIM):
            spread[h * FOX_HEAD_DIM + d, h * LANES + d] = 1.0
        for p in range(3):
            pq[p * LANES + h, base + p] = 1.0
            pk[p * LANES + h, base + 3 + p] = -1.0
            cq[0, base + 3 + p] = 1.0
            ck[0, base + p] = 1.0
    return pq, pk, cq, ck, cv, spread


def _fgate_kernel(q_ref, k_ref, v_ref, f_ref, pq_ref, pk_ref, cq_ref, ck_ref, cv_ref, sp_ref,
                  q2_ref, k2_ref, vt_ref, st_ref, carry):
    @pl.when(pl.program_id(1) == 0)
    def _():
        carry[...] = jnp.zeros_like(carry)

    vf = _dot(v_ref[...], sp_ref[...]) + cv_ref[...]
    for h in range(FOX_HEADS):
        sl = slice(h * LANES, (h + 1) * LANES)
        vt_ref[0, 0, sl, :] = vf[:, sl].T.astype(BF16)

    tf = f_ref.shape[0]
    ls = _log_sigmoid(f_ref[...])
    r = lax.broadcasted_iota(jnp.int32, (tf, tf), 0)
    c = lax.broadcasted_iota(jnp.int32, (tf, tf), 1)
    tri = (c <= r).astype(BF16)
    hi, mid, lo = _split3(ls)
    cs = _dot(tri, hi) + _dot(tri, mid) + _dot(tri, lo)
    F = cs + carry[...]
    carry[...] = F[tf - 1:tf, :]
    fh, fm, fl = _split3(F)
    f3 = jnp.concatenate([fh, fm, fl], axis=1)
    qf = _dot(q_ref[...], sp_ref[...])
    kf = _dot(k_ref[...], sp_ref[...])
    q2_ref[...] = (qf + _dot(f3, pq_ref[...]) + cq_ref[...]).astype(BF16)
    k2_ref[...] = (kf + _dot(f3, pk_ref[...]) + ck_ref[...]).astype(BF16)
    lane = lax.broadcasted_iota(jnp.int32, (1, LANES), 1)
    qstat = jnp.zeros((1, LANES), F32)
    kstat = jnp.zeros((1, LANES), F32)
    for h in range(FOX_HEADS):
        sl = slice(h * LANES, (h + 1) * LANES)
        qm = jnp.max(jnp.sum(qf[:, sl] * qf[:, sl], axis=-1, keepdims=True), axis=0, keepdims=True)
        km = jnp.max(jnp.sum(kf[:, sl] * kf[:, sl], axis=-1, keepdims=True), axis=0, keepdims=True)
        qstat = jnp.where(lane == h, qm, qstat)
        kstat = jnp.where(lane == h, km, kstat)
    row = lax.broadcasted_iota(jnp.int32, (STAT_ROWS, LANES), 0)
    st_ref[...] = jnp.where(row == 0, F[0:1, :],
                            jnp.where(row == 1, F[tf - 1:tf, :],
                                      jnp.where(row == 2, qstat, jnp.where(row == 3, kstat, 0.0))))


def _fgate(zq, zk, zv, zg, B, S, tf):
    T = zq.shape[0]
    nt = S // tf
    pq, pk, cq, ck, cv, sp = _aug_constants()
    pq, pk, sp = jnp.asarray(pq, BF16), jnp.asarray(pk, BF16), jnp.asarray(sp, BF16)
    cq, ck, cv = jnp.asarray(cq), jnp.asarray(ck), jnp.asarray(cv)
    full = lambda a: pl.BlockSpec(a.shape, lambda b, i: (0,) * a.ndim)
    row = pl.BlockSpec((tf, FOX_PAD), lambda b, i: (b * nt + i, 0))
    packed = pl.BlockSpec((tf, FOX_WIDTH), lambda b, i: (b * nt + i, 0))
    return pl.pallas_call(
        _fgate_kernel,
        grid=(B, nt),
        in_specs=[packed, packed, packed,
                  pl.BlockSpec((tf, LANES), lambda b, i: (b * nt + i, G_OFF_F // LANES)),
                  full(pq), full(pk), full(cq), full(ck), full(cv), full(sp)],
        out_specs=[row, row, pl.BlockSpec((1, 1, FOX_PAD, tf), lambda b, i: (b, i, 0, 0)),
                   pl.BlockSpec((STAT_ROWS, LANES), lambda b, i: (b * nt + i, 0))],
        out_shape=[jax.ShapeDtypeStruct((T, FOX_PAD), BF16)] * 2
                  + [jax.ShapeDtypeStruct((B, nt, FOX_PAD, tf), BF16),
                     jax.ShapeDtypeStruct((B * nt * STAT_ROWS, LANES), F32)],
        scratch_shapes=[pltpu.VMEM((1, LANES), F32)],
        compiler_params=_cparams(("arbitrary", "arbitrary")),
        name="fgate",
    )(zq, zk, zv, zg, pq, pk, cq, ck, cv, sp)


def _attn_kernel(ff_ref, fl_ref, qn_ref, kn_ref, q_ref, k_ref, vt_ref, o_ref, m_sc, acc_sc, s_sc,
                 *, blk, nq, nb):
    h = pl.program_id(0)
    i = pl.program_id(1)

    def scores(slot, j):
        off = pl.multiple_of(j * blk, blk)
        for b in range(nb):
            s_sc[slot, b] = _dot_nt(k_ref[b, pl.ds(off, blk), :], q_ref[b])

    def softmax_pv(slot, j, diag):
        for b in range(nb):
            s = s_sc[slot, b]
            if diag:
                r = lax.broadcasted_iota(jnp.int32, (blk, blk), 0)
                c = lax.broadcasted_iota(jnp.int32, (blk, blk), 1)
                s = jnp.where(r <= c, s, NEG_BIG)
            m_prev = m_sc[b]
            m_new = jnp.maximum(m_prev, jnp.max(s, axis=0, keepdims=True))
            alpha = jnp.exp(m_prev - m_new)
            p = jnp.exp(s - m_new)
            acc_sc[b] = alpha * acc_sc[b] + _dot(vt_ref[b, j], p.astype(BF16))
            m_sc[b] = m_new

    m_sc[...] = jnp.full_like(m_sc, NEG_BIG)
    acc_sc[...] = jnp.zeros_like(acc_sc)
    scores(0, i)
    scores(1, jnp.maximum(i - 1, 0))
    softmax_pv(0, i, True)

    n = jnp.int32(0)
    for b in range(nb):
        base = (b * FOX_HEADS + h) * nq
        slack = (qn_ref[base + i] * kn_ref[b * FOX_HEADS + h] + ff_ref[base + i]
                 - jnp.min(m_sc[b]) + PRUNE_MARGIN)

        def cond(t, base=base, slack=slack):
            return jnp.logical_and(t < i, slack - fl_ref[base + jnp.maximum(i - 1 - t, 0)] >= 0.0)

        n = jnp.maximum(n, lax.while_loop(cond, lambda t: t + 1, jnp.int32(0)))

    def pair(u, carry):
        t = 1 + 2 * u
        ja = i - t
        scores(0, jnp.maximum(ja - 1, 0))
        softmax_pv(1, ja, False)

        @pl.when(t + 1 <= n)
        def _():
            scores(1, jnp.maximum(ja - 2, 0))
            softmax_pv(0, ja - 1, False)

        return carry

    lax.fori_loop(0, lax.shift_right_logical(n + 1, 1), pair, 0)
    for b in range(nb):
        acc = acc_sc[b]
        o_ref[b] = (acc / acc[BIAS_LANE:BIAS_LANE + 1, :]).T.astype(o_ref.dtype)


def _attention(q2, k2, vt, stats, B, S, blk):
    T = q2.shape[0]
    nq = S // blk
    H = FOX_HEADS
    st = stats.reshape(B, nq, STAT_ROWS, LANES)[:, :, :, :H]
    tab = lambda r: jnp.transpose(st[:, :, r, :], (0, 2, 1)).reshape(-1)
    ffirst, flast = tab(0), tab(1)
    qn = jnp.sqrt(tab(2)) * NORM_SLACK
    kn = jnp.sqrt(jnp.max(st[:, :, 3, :], axis=1)).reshape(-1) * NORM_SLACK
    r3 = lambda a: a.reshape(B, S, FOX_PAD)
    qspec = pl.BlockSpec((B, blk, LANES), lambda h, i, *_: (0, i, h))
    kspec = pl.BlockSpec((B, S, LANES), lambda h, i, *_: (0, 0, h))
    vtspec = pl.BlockSpec((B, nq, LANES, blk), lambda h, i, *_: (0, 0, h, 0))
    grid_spec = pltpu.PrefetchScalarGridSpec(
        num_scalar_prefetch=4,
        grid=(H, nq),
        in_specs=[qspec, kspec, vtspec],
        out_specs=qspec,
        scratch_shapes=[pltpu.VMEM((B, 1, blk), F32), pltpu.VMEM((B, LANES, blk), F32),
                        pltpu.VMEM((2, B, blk, blk), F32)],
    )
    out = pl.pallas_call(
        functools.partial(_attn_kernel, blk=blk, nq=nq, nb=B),
        grid_spec=grid_spec,
        out_shape=jax.ShapeDtypeStruct((B, S, FOX_PAD), BF16),
        compiler_params=_cparams(("arbitrary", "arbitrary")),
        name="fox_attention",
    )(ffirst, flast, qn, kn, r3(q2), r3(k2), vt)
    return out.reshape(T, FOX_PAD)


def _gla_kernel(q_ref, k_ref, v_ref, r_ref, a_ref, wa_ref, ba_ref, g_ref, o_ref, st_sc, *, tg):
    @pl.when(pl.program_id(1) == 0)
    def _():
        st_sc[...] = jnp.zeros_like(st_sc)

    C = GLA_CHUNK
    nchunk = tg // C
    la = _log_sigmoid(_dot(a_ref[...].astype(BF16), wa_ref[...]) + ba_ref[...]) * (1.0 / GLA_GATE_TAU)
    r = lax.broadcasted_iota(jnp.int32, (tg, tg), 0)
    c = lax.broadcasted_iota(jnp.int32, (tg, tg), 1)
    shift = C.bit_length() - 1
    tri = ((c <= r) & ((c >> shift) == (r >> shift))).astype(BF16)
    hi, mid, lo = _split3(la)
    b = _dot(tri, hi) + _dot(tri, mid) + _dot(tri, lo)
    eb = jnp.exp(b)
    q_in = q_ref[...] * (GLA_DK ** -0.5) * eb
    k_in = k_ref[...] * jnp.exp(-b)
    v = v_ref[...]
    rc = lax.broadcasted_iota(jnp.int32, (C, C), 0)
    cc = lax.broadcasted_iota(jnp.int32, (C, C), 1)
    causal = cc <= rc
    lane = lax.broadcasted_iota(jnp.int32, (1, LANES), 1)
    vmask = (lane < GLA_DV).astype(F32)
    outs = []
    for ci in range(nchunk):
        rows = slice(ci * C, (ci + 1) * C)
        b_last = b[ci * C + C - 1:ci * C + C, :]
        k_out = k_ref[rows, :] * jnp.exp(b_last - b[rows, :])
        dec = jnp.exp(b_last)
        heads = []
        for h in range(GLA_HEADS):
            ln = slice(h * LANES, (h + 1) * LANES)
            qh = q_in[rows, ln].astype(BF16)
            kh = k_in[rows, ln].astype(BF16)
            vh = v[rows, ln].astype(BF16)
            attn = jnp.where(causal, _dot_nt(qh, kh), 0.0)
            st = st_sc[h]
            o = _dot(attn.astype(BF16), vh) + _dot_nt(qh, st.astype(BF16))
            kv_t = _dot_tn(vh, k_out[:, ln].astype(BF16))
            st_sc[h] = st * dec[:, ln] + kv_t
            ms = jnp.sum(o * o, axis=-1, keepdims=True) * (1.0 / GLA_DV)
            heads.append(o * lax.rsqrt(ms + RMS_EPS) * vmask)
        outs.append(jnp.concatenate(heads, axis=1))
    o_all = jnp.concatenate(outs, axis=0)
    gr = r_ref[...]
    o_ref[...] = ((o_all * g_ref[...]) * (gr * jax.nn.sigmoid(gr))).astype(o_ref.dtype)


def _gla(zg, wa, ba, gn, B, S, tg):
    T = zg.shape[0]
    nt = S // tg
    col = lambda off, w: pl.BlockSpec((tg, w), lambda b, i: (b * nt + i, off // w))
    full = lambda a: pl.BlockSpec(a.shape, lambda b, i: (0,) * a.ndim)
    return pl.pallas_call(
        functools.partial(_gla_kernel, tg=tg),
        grid=(B, nt),
        in_specs=[col(G_OFF_Q, GLA_PAD), col(G_OFF_K, GLA_PAD), col(G_OFF_V, GLA_PAD),
                  col(G_OFF_R, GLA_PAD), col(G_OFF_A, LANES), full(wa), full(ba), full(gn)],
        out_specs=pl.BlockSpec((tg, GLA_PAD), lambda b, i: (b * nt + i, 0)),
        out_shape=jax.ShapeDtypeStruct((T, GLA_PAD), BF16),
        scratch_shapes=[pltpu.VMEM((GLA_HEADS, LANES, LANES), F32)],
        compiler_params=_cparams(("arbitrary", "arbitrary")),
        name="gla",
    )(zg, zg, zg, zg, zg, wa, ba, gn)


HALO = max(POOL_WINDOWS)


def _pool_kernel(u_ref, w_ref, s_ref, o_ref, xx):
    tp = u_ref.shape[0]
    i = pl.program_id(1)

    @pl.when(i == 0)
    def _():
        xx[0:HALO, :] = jnp.zeros((HALO, POOL_WIDTH), F32)

    @pl.when(i > 0)
    def _():
        xx[0:HALO, :] = xx[tp:tp + HALO, :]

    u = u_ref[...]
    xx[HALO:HALO + tp, :] = u
    lane = lax.broadcasted_iota(jnp.int32, (1, POOL_WIDTH), 1)
    grp = lane >> (POOL_GROUP.bit_length() - 1)
    pos = lax.broadcasted_iota(jnp.int32, (tp, 1), 0) + i * tp + 1
    acc = u
    pooled = jnp.zeros_like(u)
    for j in range(1, HALO):
        acc = acc + xx[HALO - j:HALO - j + tp, :]
        w = j + 1
        if w in POOL_WINDOWS:
            g = POOL_WINDOWS.index(w)
            cnt = jnp.minimum(pos, w).astype(F32)
            pooled = jnp.where(grp == g, acc / cnt - u, pooled)
    mixed = _dot(pooled.astype(BF16), w_ref[...])
    o_ref[...] = (mixed * s_ref[...]).astype(o_ref.dtype)


def _pool(zg, w_bd, scale, B, S, tp):
    T = zg.shape[0]
    nt = S // tp
    full = lambda a: pl.BlockSpec(a.shape, lambda b, i: (0,) * a.ndim)
    return pl.pallas_call(
        _pool_kernel,
        grid=(B, nt),
        in_specs=[pl.BlockSpec((tp, POOL_WIDTH), lambda b, i: (b * nt + i, G_OFF_U // POOL_WIDTH)),
                  full(w_bd), full(scale)],
        out_specs=pl.BlockSpec((tp, POOL_WIDTH), lambda b, i: (b * nt + i, 0)),
        out_shape=jax.ShapeDtypeStruct((T, POOL_WIDTH), BF16),
        scratch_shapes=[pltpu.VMEM((tp + HALO, POOL_WIDTH), F32)],
        compiler_params=_cparams(("arbitrary", "arbitrary")),
        name="pool",
    )(zg, w_bd, scale)


def _layer_norm(r, g, b):
    mu = jnp.mean(r, axis=-1, keepdims=True)
    d = r - mu
    var = jnp.mean(d * d, axis=-1, keepdims=True)
    return d * lax.rsqrt(var + LN_EPS) * g + b


def _outproj_kernel(of_ref, og_ref, op_ref, x_ref, wf_ref, wg_ref, wp_ref, gate_ref, lg_ref, lb_ref,
                    sc_ref, sh_ref, wr_ref, br_ref,
                    x1_ref, h2_ref, e_ref, gt_ref, cnt_ref, *, alpha):
    y = _dot(of_ref[...], wf_ref[...]) + _dot(og_ref[...], wg_ref[...]) + _dot(op_ref[...], wp_ref[...])
    r = alpha * x_ref[...] + (1.0 + gate_ref[0]) * y
    x1 = _layer_norm(r, lg_ref[...], lb_ref[...])
    x1_ref[...] = x1
    h2 = x1 * (1.0 + sc_ref[0]) + sh_ref[0]
    h2_ref[...] = h2
    h_hi = h2.astype(BF16)
    h_lo = (h2 - h_hi.astype(F32)).astype(BF16)
    logits = (_dot(h_hi, wr_ref[0]) + (_dot(h_hi, wr_ref[1]) + _dot(h_lo, wr_ref[0]))
              + br_ref[...])
    tm = logits.shape[0]
    lane_i = lax.broadcasted_iota(jnp.int32, (tm, LANES), 1)
    lane = lane_i.astype(F32)
    work = logits
    tops, idxs = [], []
    onehot = jnp.zeros((tm, LANES), F32)
    for _ in range(TOP_K):
        m = jnp.max(work, axis=-1, keepdims=True)
        idx = jnp.min(jnp.where(work == m, lane, float(LANES)), axis=-1, keepdims=True)
        sel = lane == idx
        onehot = onehot + sel.astype(F32)
        work = jnp.where(sel, -jnp.inf, work)
        tops.append(m)
        idxs.append(idx)
    ex = [jnp.exp(t - tops[0]) for t in tops]
    den = ex[0] + ex[1] + ex[2] + ex[3]
    e_out = jnp.zeros((tm, LANES), jnp.int32)
    g_out = jnp.zeros((tm, LANES), F32)
    for k in range(TOP_K):
        e_out = jnp.where(lane_i == k, idxs[k].astype(jnp.int32), e_out)
        g_out = jnp.where(lane_i == k, ex[k] / den, g_out)
    e_ref[...] = e_out
    gt_ref[...] = g_out
    for u in range(tm // ROUTE_TILE):
        rows = slice(u * ROUTE_TILE, (u + 1) * ROUTE_TILE)
        cnt_ref[u] = jnp.sum(onehot[rows], axis=0, keepdims=True).astype(jnp.int32)


def _outproj(o_fox, o_gla, o_pool, x2d, wf, wg, wp, gate1, ln_g, ln_b, scale2, shift2, wr, br,
             B, S, tm, alpha):
    T, D = x2d.shape
    nt = S // tm
    full = lambda a: pl.BlockSpec(a.shape, lambda i: (0,) * a.ndim)
    row = lambda w: pl.BlockSpec((tm, w), lambda i: (i, 0))
    mod = pl.BlockSpec((1, 1, D), lambda i: (i // nt, 0, 0))
    return pl.pallas_call(
        functools.partial(_outproj_kernel, alpha=alpha),
        grid=(T // tm,),
        in_specs=[row(FOX_PAD), row(GLA_PAD), row(POOL_WIDTH), row(D), full(wf), full(wg), full(wp),
                  mod, full(ln_g), full(ln_b), mod, mod, full(wr), full(br)],
        out_specs=[row(D), row(D), row(LANES), row(LANES),
                   pl.BlockSpec((tm // ROUTE_TILE, 1, LANES), lambda i: (i, 0, 0))],
        out_shape=[jax.ShapeDtypeStruct((T, D), F32), jax.ShapeDtypeStruct((T, D), F32),
                   jax.ShapeDtypeStruct((T, LANES), jnp.int32), jax.ShapeDtypeStruct((T, LANES), F32),
                   jax.ShapeDtypeStruct((T // ROUTE_TILE, 1, LANES), jnp.int32)],
        compiler_params=_cparams(("arbitrary",)),
        name="outproj_router",
    )(o_fox, o_gla, o_pool, x2d, wf, wg, wp, gate1, ln_g, ln_b, scale2, shift2, wr, br)


def _tile_slots(e_i32, off_row):
    td = e_i32.shape[0]
    lane = lax.broadcasted_iota(jnp.int32, (td, LANES), 1).astype(F32)
    ef = e_i32.astype(F32)
    sel = [lane == ef[:, k:k + 1] for k in range(TOP_K)]
    onehot = sel[0].astype(F32)
    for k in range(1, TOP_K):
        onehot = onehot + sel[k].astype(F32)
    rr = lax.broadcasted_iota(jnp.int32, (td, td), 0)
    cc = lax.broadcasted_iota(jnp.int32, (td, td), 1)
    stril = (cc < rr).astype(BF16)
    tab = _dot(stril, onehot.astype(BF16)) + off_row
    return [jnp.sum(jnp.where(sel[k], tab, 0.0), axis=-1, keepdims=True) for k in range(TOP_K)]


def _group_bits(max_groups):
    return [1 << s for s in range(max_groups.bit_length() - 1, -1, -1)]


PIECE_CLASSES = (ROUTE_TILE // ROW_GROUP).bit_length()
PIECE_LIST = 1024
SLOT_BITS = 8
assert ROUTE_SLOTS // ROW_GROUP < (1 << SLOT_BITS) and PIECE_CLASSES * N_EXPERTS <= PIECE_LIST


def _piece_lists(m, off, dst):
    nt = m.shape[0]
    j = jnp.arange(N_EXPERTS, dtype=jnp.int32)
    b = jnp.arange(PIECE_CLASSES, dtype=jnp.int32)[None, :, None]
    mm, oo, dd = m[:, None, :], off[:, None, :], dst[:, None, :]
    has = (mm >> b) & 1
    before = (mm & ~((2 << b) - 1)) * ROW_GROUP
    packed = (((dd + before) // ROW_GROUP) << SLOT_BITS) | ((oo + before) // ROW_GROUP)
    rank = _excl_prefix(has, 2)
    hit = (has[..., None] == 1) & (rank[..., None] == j)
    lst = jnp.sum(jnp.where(hit, packed[..., None], 0), axis=2)
    lst = _pad_last(lst.reshape(nt, PIECE_CLASSES * N_EXPERTS).astype(jnp.int32), PIECE_LIST)
    return lst.reshape(-1), jnp.sum(has, axis=2).astype(jnp.int32).reshape(-1)


def _for_each_piece(tile, cnt_ref, list_ref, fn):
    for b in range(PIECE_CLASSES):
        def body(i, carry, b=b):
            v = list_ref[b * N_EXPERTS + i]
            fn(pl.multiple_of((v & ((1 << SLOT_BITS) - 1)) * ROW_GROUP, ROW_GROUP),
               pl.multiple_of(lax.shift_right_logical(v, SLOT_BITS) * ROW_GROUP, ROW_GROUP),
               (1 << b) * ROW_GROUP)
            return carry

        lax.fori_loop(0, cnt_ref[tile * PIECE_CLASSES + b], body, 0)


def _for_each_total_piece(total_groups, fn):
    for bit in _group_bits(ROUTE_SLOTS // ROW_GROUP):
        @pl.when((total_groups & bit) != 0)
        def _(bit=bit):
            fn(bit * ROW_GROUP)


def _dispatch_kernel(cnt_ref, tot_ref, zblk_ref, nu_ref, list_ref, e_ref, offrow_ref, h_ref,
                     xr_ref, slot_ref, sort_sc, zero_sc, sems, zsem, *, n_blocks, nt):
    step = pl.program_id(0)

    @pl.when(step == 0)
    def _():
        zero_sc[...] = jnp.zeros_like(zero_sc)

        def zcopy(blk):
            return pltpu.make_async_copy(
                zero_sc, xr_ref.at[pl.ds(pl.multiple_of(blk * MOE_BLOCK, MOE_BLOCK), MOE_BLOCK), :], zsem)

        def zstart(e, c):
            @pl.when(zblk_ref[e] >= 0)
            def _():
                zcopy(zblk_ref[e]).start()
            return c

        def zwait(e, c):
            @pl.when(zblk_ref[e] >= 0)
            def _():
                zcopy(0).wait()
            return c

        def tstart(blk, c):
            zcopy(blk).start()
            return c

        def twait(blk, c):
            zcopy(0).wait()
            return c

        lax.fori_loop(0, N_EXPERTS, zstart, 0)
        lax.fori_loop(nu_ref[0], n_blocks, tstart, 0)
        lax.fori_loop(0, N_EXPERTS, zwait, 0)
        lax.fori_loop(nu_ref[0], n_blocks, twait, 0)

    td = h_ref.shape[0]
    slots = _tile_slots(e_ref[...], offrow_ref[0])
    lane = lax.broadcasted_iota(jnp.int32, (td, LANES), 1)
    cols = jnp.full((td, LANES), -1.0, F32)
    for k in range(TOP_K):
        cols = jnp.where(lane == k, slots[k], cols)
    slot_ref[...] = cols
    rows_t = cols.T
    sub = lax.broadcasted_iota(jnp.int32, (ROUTE_SLOTS, td), 0).astype(F32)
    pick = sub == rows_t[0:1, :]
    for k in range(1, TOP_K):
        pick = pick | (sub == rows_t[k:k + 1, :])
    buf = step & 1
    sort_sc[buf] = _dot(pick.astype(BF16), h_ref[...].astype(BF16))

    def start_chunk(slot0, row0, n):
        pltpu.make_async_copy(sort_sc.at[buf, pl.ds(slot0, n), :], xr_ref.at[pl.ds(row0, n), :],
                              sems.at[buf]).start()

    _for_each_piece(step, cnt_ref, list_ref, start_chunk)

    def wait_tile(tile, b):
        def wait_piece(n):
            pltpu.make_async_copy(sort_sc.at[b, pl.ds(0, n), :], xr_ref.at[pl.ds(0, n), :],
                                  sems.at[b]).wait()
        _for_each_total_piece(tot_ref[tile], wait_piece)

    @pl.when(step > 0)
    def _():
        wait_tile(step - 1, 1 - buf)

    @pl.when(step == nt - 1)
    def _():
        wait_tile(step, buf)


def _dispatch(h2, e_pad, tables, rows):
    T, D = h2.shape
    td = ROUTE_TILE
    nt = T // td
    piece_list, piece_cnt, tot_tab, zblk, n_used, off_rows = tables
    grid_spec = pltpu.PrefetchScalarGridSpec(
        num_scalar_prefetch=4,
        grid=(nt,),
        in_specs=[pl.BlockSpec((PIECE_LIST,), lambda i, *_: (i,), memory_space=pltpu.SMEM),
                  pl.BlockSpec((td, LANES), lambda i, *_: (i, 0)),
                  pl.BlockSpec((1, 1, LANES), lambda i, *_: (i, 0, 0)),
                  pl.BlockSpec((td, D), lambda i, *_: (i, 0))],
        out_specs=[pl.BlockSpec(memory_space=pl.ANY),
                   pl.BlockSpec((td, LANES), lambda i, *_: (i, 0))],
        scratch_shapes=[pltpu.VMEM((2, ROUTE_SLOTS, D), F32), pltpu.VMEM((MOE_BLOCK, D), F32),
                        pltpu.SemaphoreType.DMA((2,)), pltpu.SemaphoreType.DMA],
    )
    return pl.pallas_call(
        functools.partial(_dispatch_kernel, n_blocks=rows // MOE_BLOCK, nt=nt),
        grid_spec=grid_spec,
        out_shape=[jax.ShapeDtypeStruct((rows, D), F32), jax.ShapeDtypeStruct((T, LANES), F32)],
        compiler_params=_cparams(("arbitrary",)),
        name="moe_dispatch",
    )(piece_cnt, tot_tab, zblk, n_used, piece_list, e_pad, off_rows, h2)


def _expert_kernel(be_ref, nu_ref, vb_ref, par_ref, nxt_ref, x_ref, wgu_hbm, bgu_ref, wd_hbm, bd_ref, y_ref,
                   wgu_sc, wd_sc, gu_buf, d_buf, sems, *, first_expert):
    i = pl.program_id(0)
    used = i < nu_ref[0]

    def fetch(e, slot):
        return (pltpu.make_async_copy(wgu_hbm.at[first_expert + e], gu_buf.at[slot], sems.at[slot, 0]),
                pltpu.make_async_copy(wd_hbm.at[first_expert + e], d_buf.at[slot], sems.at[slot, 1]))

    @pl.when(used)
    def _():
        prev = be_ref[jnp.maximum(i - 1, 0)]

        @pl.when((i == 0) | (be_ref[i] != prev))
        def _():
            slot = par_ref[i]

            @pl.when(i == 0)
            def _():
                for cp in fetch(be_ref[i], slot):
                    cp.start()

            for cp in fetch(be_ref[i], slot):
                cp.wait()
            wgu_sc[...] = gu_buf[slot].astype(BF16)
            wd_sc[...] = d_buf[slot].astype(BF16)

            @pl.when(nxt_ref[i] >= 0)
            def _():
                for cp in fetch(nxt_ref[i], 1 - slot):
                    cp.start()

        def mlp(rows):
            x = x_ref[rows, :].astype(BF16)
            gu = _dot(x, wgu_sc[...]) + bgu_ref[0]
            glu = jnp.minimum(gu[:, :D_EXPERT], SWIGLU_LIMIT)
            lin = jnp.clip(gu[:, D_EXPERT:], -SWIGLU_LIMIT, SWIGLU_LIMIT)
            act = glu * jax.nn.sigmoid(SWIGLU_ALPHA * glu) * (lin + 1.0)
            y_ref[rows, :] = _dot(act.astype(BF16), wd_sc[...]) + bd_ref[0]

        half = MOE_BLOCK // 2
        full_block = vb_ref[i] > half

        @pl.when(full_block)
        def _():
            mlp(slice(0, MOE_BLOCK))

        @pl.when(jnp.logical_not(full_block))
        def _():
            mlp(slice(0, half))
            y_ref[half:, :] = jnp.zeros((MOE_BLOCK - half, y_ref.shape[1]), y_ref.dtype)

    @pl.when(jnp.logical_not(used))
    def _():
        y_ref[...] = jnp.zeros_like(y_ref)


def _experts(x_rows, block_expert, n_used, valid_rows, slot_parity, next_expert,
             w_gate_up, b_gate_up, w_down, b_down, layer):
    rows, D = x_rows.shape
    nb = rows // MOE_BLOCK
    E = w_gate_up.shape[1]
    grid_spec = pltpu.PrefetchScalarGridSpec(
        num_scalar_prefetch=5,
        grid=(nb,),
        in_specs=[pl.BlockSpec((MOE_BLOCK, D), lambda i, be, nu, *_: (jnp.minimum(i, nu[0] - 1), 0)),
                  pl.BlockSpec(memory_space=pl.ANY),
                  pl.BlockSpec((1, 1, 2 * D_EXPERT), lambda i, be, *_: (layer * E + be[i], 0, 0)),
                  pl.BlockSpec(memory_space=pl.ANY),
                  pl.BlockSpec((1, 1, D), lambda i, be, *_: (layer * E + be[i], 0, 0))],
        out_specs=pl.BlockSpec((MOE_BLOCK, D), lambda i, *_: (i, 0)),
        scratch_shapes=[pltpu.VMEM((D, 2 * D_EXPERT), BF16), pltpu.VMEM((D_EXPERT, D), BF16),
                        pltpu.VMEM((2, D, 2 * D_EXPERT), F32), pltpu.VMEM((2, D_EXPERT, D), F32),
                        pltpu.SemaphoreType.DMA((2, 2))],
    )
    L = w_gate_up.shape[0]
    return pl.pallas_call(
        functools.partial(_expert_kernel, first_expert=layer * E),
        grid_spec=grid_spec,
        out_shape=jax.ShapeDtypeStruct((rows, D), F32),
        compiler_params=_cparams(("arbitrary",)),
        name="moe_experts",
    )(block_expert, n_used, valid_rows, slot_parity, next_expert, x_rows,
      w_gate_up.reshape(L * E, D, 2 * D_EXPERT), b_gate_up.reshape(L * E, 1, 2 * D_EXPERT),
      w_down.reshape(L * E, D_EXPERT, D), b_down.reshape(L * E, 1, D))


def _combine_kernel(cnt_ref, tot_ref, list_ref, next_list_ref, slot_ref, gt_ref, y_ref, x_ref, gate_ref,
                    lg_ref, lb_ref, o_ref, ybuf, sems, *, nt, alpha):
    step = pl.program_id(0)
    buf = step & 1

    def fetch(tile, b, lst):
        def start_chunk(slot0, row0, n):
            pltpu.make_async_copy(y_ref.at[pl.ds(row0, n), :], ybuf.at[b, pl.ds(slot0, n), :],
                                  sems.at[b]).start()
        _for_each_piece(tile, cnt_ref, lst, start_chunk)

    @pl.when(step == 0)
    def _():
        ybuf[...] = jnp.zeros_like(ybuf)
        fetch(step, buf, list_ref)

    @pl.when(step + 1 < nt)
    def _():
        fetch(step + 1, 1 - buf, next_list_ref)

    def wait_piece(n):
        pltpu.make_async_copy(y_ref.at[pl.ds(0, n), :], ybuf.at[buf, pl.ds(0, n), :], sems.at[buf]).wait()

    _for_each_total_piece(tot_ref[step], wait_piece)

    td = x_ref.shape[0]
    slots = slot_ref[...]
    gt = gt_ref[...]
    lane = lax.broadcasted_iota(jnp.int32, (td, ROUTE_SLOTS), 1).astype(F32)
    w = jnp.where(lane == slots[:, 0:1], gt[:, 0:1], 0.0)
    for k in range(1, TOP_K):
        w = w + jnp.where(lane == slots[:, k:k + 1], gt[:, k:k + 1], 0.0)
    y = _dot(w.astype(BF16), ybuf[buf].astype(BF16))
    r = alpha * x_ref[...] + (1.0 + gate_ref[0]) * y
    o_ref[...] = _layer_norm(r, lg_ref[...], lb_ref[...])


def _combine(y_rows, slots, tables, gates, x1, gate2, ln_g, ln_b, B, S, alpha):
    T, D = x1.shape
    td = ROUTE_TILE
    nt = T // td
    per_batch = S // td
    piece_list, piece_cnt, tot_tab, _, _, _ = tables
    grid_spec = pltpu.PrefetchScalarGridSpec(
        num_scalar_prefetch=2,
        grid=(nt,),
        in_specs=[pl.BlockSpec((PIECE_LIST,), lambda i, *_: (i,), memory_space=pltpu.SMEM),
                  pl.BlockSpec((PIECE_LIST,), lambda i, *_: (jnp.minimum(i + 1, nt - 1),),
                               memory_space=pltpu.SMEM),
                  pl.BlockSpec((td, LANES), lambda i, *_: (i, 0)),
                  pl.BlockSpec((td, LANES), lambda i, *_: (i, 0)),
                  pl.BlockSpec(memory_space=pl.ANY),
                  pl.BlockSpec((td, D), lambda i, *_: (i, 0)),
                  pl.BlockSpec((1, 1, D), lambda i, *_: (i // per_batch, 0, 0)),
                  pl.BlockSpec((1, D), lambda i, *_: (0, 0)),
                  pl.BlockSpec((1, D), lambda i, *_: (0, 0))],
        out_specs=pl.BlockSpec((td, D), lambda i, *_: (i, 0)),
        scratch_shapes=[pltpu.VMEM((2, ROUTE_SLOTS, D), F32), pltpu.SemaphoreType.DMA((2,))],
    )
    return pl.pallas_call(
        functools.partial(_combine_kernel, nt=nt, alpha=alpha),
        grid_spec=grid_spec,
        out_shape=jax.ShapeDtypeStruct((T, D), F32),
        compiler_params=_cparams(("arbitrary",)),
        name="moe_combine",
    )(piece_cnt, tot_tab, piece_list, piece_list, slots, gates, y_rows, x1, gate2, ln_g, ln_b)


def _tile(n, pref):
    t = min(n, pref)
    assert n % t == 0, (n, t)
    return t


def kernel(x, c, w_ada, b_ada, w_in, b_in, gla_w_a2, gla_b_a, gla_norm_g, pool_w, pool_scale, w_out,
           ln1_g, ln1_b, w_router, b_router, w_gate_up, b_gate_up, w_down, b_down, ln2_g, ln2_b):
    B, S, D = x.shape
    L = w_ada.shape[0]
    T = B * S
    assert D == D_MODEL and S % GLA_CHUNK == 0
    alpha = float((2 * L) ** 0.25)
    assert T % ROUTE_TILE == 0
    n_tiles = T // ROUTE_TILE
    max_rows = T * TOP_K + n_tiles * N_EXPERTS * (ROW_GROUP - 1) + N_EXPERTS * (MOE_BLOCK - 1)
    n_blocks = -(-max_rows // MOE_BLOCK)
    rows = n_blocks * MOE_BLOCK

    mod = _ada_mod(c, w_ada, b_ada)
    prep_all = _prep_inproj(w_in, b_in)
    wa_all = jnp.pad(_pad_heads(gla_w_a2, GLA_HEADS, GLA_DK),
                     ((0, 0), (0, LANES - GLA_GATE_RANK), (0, 0))).astype(BF16)
    ba_all = _pad_heads(gla_b_a[:, None, :], GLA_HEADS, GLA_DK)
    gn_all = _pad_heads(gla_norm_g[:, None, :], GLA_HEADS, GLA_DV)
    w_bd_all = jnp.zeros((L, POOL_WIDTH, POOL_WIDTH), F32)
    for g in range(len(POOL_WINDOWS)):
        sl = slice(g * POOL_GROUP, (g + 1) * POOL_GROUP)
        w_bd_all = w_bd_all.at[:, sl, sl].set(pool_w[:, g])
    w_bd_all = w_bd_all.astype(BF16)
    pad_rows = lambda w, heads, dim: jnp.swapaxes(_pad_heads(jnp.swapaxes(w, 1, 2), heads, dim), 1, 2)
    wf_all = pad_rows(w_out[:, :FOX_WIDTH], FOX_HEADS, FOX_HEAD_DIM).astype(BF16)
    wgl_all = pad_rows(w_out[:, FOX_WIDTH:FOX_WIDTH + GLA_VWIDTH], GLA_HEADS, GLA_DV).astype(BF16)
    wp_all = w_out[:, FOX_WIDTH + GLA_VWIDTH:].astype(BF16)
    wr_f32 = _pad_last(w_router, LANES)
    wr_hi = wr_f32.astype(BF16)
    wr_all = jnp.stack([wr_hi, (wr_f32 - wr_hi.astype(F32)).astype(BF16)], axis=1)
    br_all = jnp.pad(b_router[:, None, :], ((0, 0), (0, 0), (0, LANES - N_EXPERTS)), constant_values=NEG_BIG)

    x2d = x.reshape(T, D)
    for l in range(L):
        shift1, scale1, gate1, shift2, scale2, gate2 = [mod[l, :, m] for m in range(N_MOD)]
        prep = [(w[l], b[l]) for w, b in prep_all]
        zq, zk, zv, zg = _inproj(x2d, scale1, shift1, prep, B, S, _tile(S, 512))
        q2, k2, vt, stats = _fgate(zq, zk, zv, zg, B, S, _tile(S, ATTN_BLOCK))
        o_fox = _attention(q2, k2, vt, stats, B, S, _tile(S, ATTN_BLOCK))
        o_gla = _gla(zg, wa_all[l], ba_all[l], gn_all[l], B, S, _tile(S, 256))
        o_pool = _pool(zg, w_bd_all[l], pool_scale[l][None, :], B, S, _tile(S, 512))
        x1, h2, e_pad, g_pad, cnt = _outproj(
            o_fox, o_gla, o_pool, x2d, wf_all[l], wgl_all[l], wp_all[l], gate1,
            ln1_g[l][None, :], ln1_b[l][None, :], scale2, shift2, wr_all[l], br_all[l],
            B, S, _tile(S, 512), alpha)
        cte = cnt[:, 0, :N_EXPERTS]
        c8 = (cte + ROW_GROUP - 1) // ROW_GROUP * ROW_GROUP
        off = _excl_prefix(c8, 1)
        per_expert = jnp.sum(c8, axis=0)
        padded = (per_expert + MOE_BLOCK - 1) // MOE_BLOCK * MOE_BLOCK
        pstart = _excl_prefix(padded, 0)
        pend = pstart + padded
        dst = pstart[None, :] + _excl_prefix(c8, 0)
        n_used = (pend[-1] // MOE_BLOCK).astype(jnp.int32)
        blk_start = jnp.arange(n_blocks, dtype=jnp.int32) * MOE_BLOCK
        be = jnp.minimum(jnp.sum(blk_start[:, None] >= pend[None, :], axis=1), N_EXPERTS - 1).astype(jnp.int32)
        last_used = jnp.minimum(jnp.sum(MOE_BLOCK * (n_used - 1) >= pend), N_EXPERTS - 1).astype(jnp.int32)
        be = jnp.where(jnp.arange(n_blocks) < n_used, be, last_used)
        zblk = jnp.where(padded > 0, pend // MOE_BLOCK - 1, -1).astype(jnp.int32)
        off_rows = _pad_last(off.astype(F32), LANES)[:, None, :]
        n_used = n_used.reshape(1)
        piece_list, piece_cnt = _piece_lists(c8 // ROW_GROUP, off, dst)
        tables = (piece_list, piece_cnt, (jnp.sum(c8, axis=1) // ROW_GROUP).astype(jnp.int32),
                  zblk, n_used, off_rows)
        x_rows, slots = _dispatch(h2, e_pad, tables, rows)
        owns = padded > 0
        ids = jnp.arange(N_EXPERTS, dtype=jnp.int32)
        later = jnp.where(owns[None, :] & (ids[None, :] > ids[:, None]), ids[None, :], N_EXPERTS)
        nxt_e = jnp.min(later, axis=1)
        nxt_e = jnp.where(nxt_e < N_EXPERTS, nxt_e, -1).astype(jnp.int32)
        par_e = (_excl_prefix(owns.astype(jnp.int32), 0) & 1).astype(jnp.int32)
        pick = be[:, None] == ids[None, :]
        lookup = lambda tab: jnp.sum(jnp.where(pick, tab[None, :], 0), axis=1).astype(jnp.int32)
        valid_rows = jnp.clip(lookup(per_expert) - (blk_start - lookup(pstart)), 0, MOE_BLOCK).astype(jnp.int32)
        y_rows = _experts(x_rows, be, n_used, valid_rows, lookup(par_e), lookup(nxt_e),
                          w_gate_up, b_gate_up, w_down, b_down, l)
        x2d = _combine(y_rows, slots, tables, g_pad, x1, gate2,
                       ln2_g[l][None, :], ln2_b[l][None, :], B, S, alpha)
    return x2d.reshape(B, S, D)
```

```python
import functools

import numpy as np
import jax
import jax.numpy as jnp
from jax import lax
from jax.experimental import pallas as pl
from jax.experimental.pallas import tpu as pltpu

F32 = jnp.float32
BF16 = jnp.bfloat16

D_MODEL = 1024
FOX_HEADS = 6
FOX_HEAD_DIM = 64
FOX_WIDTH = FOX_HEADS * FOX_HEAD_DIM
GLA_HEADS = 4
GLA_DV = 96
GLA_DK = 48
GLA_KWIDTH = GLA_HEADS * GLA_DK
GLA_VWIDTH = GLA_HEADS * GLA_DV
GLA_GATE_RANK = 16
GLA_GATE_TAU = 16.0
GLA_CHUNK = 64
POOL_WINDOWS = (2, 4, 8, 16)
POOL_GROUP = 64
POOL_WIDTH = len(POOL_WINDOWS) * POOL_GROUP
N_EXPERTS = 32
TOP_K = 4
D_EXPERT = 1024
SWIGLU_ALPHA = 1.702
SWIGLU_LIMIT = 7.0
N_MOD = 6
LN_EPS = 1e-5
RMS_EPS = 1e-6

LANES = 128
VMEM_LIMIT_BYTES = 56 * 1024 * 1024

FOX_PAD = FOX_HEADS * LANES
GLA_PAD = GLA_HEADS * LANES
BIAS_LANE = FOX_HEAD_DIM
G_OFF_Q, G_OFF_K, G_OFF_V, G_OFF_R = 0, GLA_PAD, 2 * GLA_PAD, 3 * GLA_PAD
G_OFF_U = 4 * GLA_PAD
G_OFF_A = G_OFF_U + POOL_WIDTH
G_OFF_F = G_OFF_A + LANES
G_WIDTH = G_OFF_F + LANES

STAT_ROWS = 8
PRUNE_MARGIN = 105.0
NORM_SLACK = 1.01
ATTN_BLOCK = 512
MOE_BLOCK = 512
ROUTE_TILE = 256
ROW_GROUP = 8
ROUTE_SLOTS = ROUTE_TILE * TOP_K + N_EXPERTS * ROW_GROUP
NEG_BIG = -1e30


def _cparams(sem, vmem=None):
    return pltpu.CompilerParams(dimension_semantics=sem, vmem_limit_bytes=vmem or VMEM_LIMIT_BYTES)


def _log_sigmoid(x):
    return jnp.minimum(x, 0.0) - jnp.log(1.0 + jnp.exp(-jnp.abs(x)))


def _split3(x):
    hi = x.astype(BF16)
    r = x - hi.astype(F32)
    mid = r.astype(BF16)
    lo = (r - mid.astype(F32)).astype(BF16)
    return hi, mid, lo


def _dot(a, b):
    return jnp.dot(a, b, preferred_element_type=F32)


def _dot_nt(a, b):
    return lax.dot_general(a, b, (((1,), (1,)), ((), ())), preferred_element_type=F32)


def _dot_tn(a, b):
    return lax.dot_general(a, b, (((0,), (0,)), ((), ())), preferred_element_type=F32)


def _ada_kernel(c_ref, w_ref, b_ref, o_ref):
    c = c_ref[...]
    cond = c * jax.nn.sigmoid(c)
    o_ref[0] = jnp.dot(cond, w_ref[0], preferred_element_type=F32,
                       precision=lax.Precision.HIGHEST) + b_ref[0]


def _ada_mod(c, w_ada, b_ada):
    L, D, N = w_ada.shape
    B = c.shape[0]
    rows = 8
    c_pad = jnp.zeros((rows, D), F32).at[:B].set(c)
    tn = 1536
    out = pl.pallas_call(
        _ada_kernel,
        grid=(L, N // tn),
        in_specs=[pl.BlockSpec((rows, D), lambda l, j: (0, 0)),
                  pl.BlockSpec((1, D, tn), lambda l, j: (l, 0, j)),
                  pl.BlockSpec((1, 1, tn), lambda l, j: (l, 0, j))],
        out_specs=pl.BlockSpec((1, rows, tn), lambda l, j: (l, 0, j)),
        out_shape=jax.ShapeDtypeStruct((L, rows, N), F32),
        compiler_params=_cparams(("arbitrary", "arbitrary")),
        name="ada_mod",
    )(c_pad, w_ada, b_ada.reshape(L, 1, N))
    return out[:, :B].reshape(L, B, N_MOD, 1, D)


def _inproj_kernel(x_ref, sc_ref, sh_ref, wq_ref, wk_ref, wv_ref, wg_ref,
                   bq_ref, bk_ref, bv_ref, bg_ref, q_ref, k_ref, v_ref, g_ref):
    h = (x_ref[...] * (1.0 + sc_ref[0]) + sh_ref[0]).astype(BF16)
    q_ref[...] = (_dot(h, wq_ref[...]) + bq_ref[...]).astype(BF16)
    k_ref[...] = (_dot(h, wk_ref[...]) + bk_ref[...]).astype(BF16)
    v_ref[...] = (_dot(h, wv_ref[...]) + bv_ref[...]).astype(BF16)
    g_ref[...] = _dot(h, wg_ref[...]) + bg_ref[...]


def _pad_heads(w, heads, dim):
    lead = w.shape[:-1]
    w = w.reshape(lead + (heads, dim))
    w = jnp.pad(w, [(0, 0)] * len(lead) + [(0, 0), (0, LANES - dim)])
    return w.reshape(lead + (heads * LANES,))


def _pad_last(a, width):
    return jnp.pad(a, [(0, 0)] * (a.ndim - 1) + [(0, width - a.shape[-1])])


def _excl_prefix(a, axis):
    n = a.shape[axis]
    idx = jnp.arange(n)
    earlier = idx[None, :] < idx[:, None]
    am = jnp.moveaxis(a, axis, -1)
    out = jnp.sum(jnp.where(earlier, am[..., None, :], 0), axis=-1)
    return jnp.moveaxis(out, -1, axis)


def _prep_inproj(w_in, b_in):
    def relayout(W):
        o = 0
        fq = W[..., o:o + FOX_WIDTH]; o += FOX_WIDTH
        fk = W[..., o:o + FOX_WIDTH]; o += FOX_WIDTH
        fv = W[..., o:o + FOX_WIDTH]; o += FOX_WIDTH
        ff = W[..., o:o + FOX_HEADS]; o += FOX_HEADS
        gq = W[..., o:o + GLA_KWIDTH]; o += GLA_KWIDTH
        gk = W[..., o:o + GLA_KWIDTH]; o += GLA_KWIDTH
        gv = W[..., o:o + GLA_VWIDTH]; o += GLA_VWIDTH
        gr = W[..., o:o + GLA_VWIDTH]; o += GLA_VWIDTH
        ga = W[..., o:o + GLA_GATE_RANK]; o += GLA_GATE_RANK
        pu = W[..., o:o + POOL_WIDTH]
        wq = fq * (FOX_HEAD_DIM ** -0.5)
        wg = jnp.concatenate([
            _pad_heads(gq, GLA_HEADS, GLA_DK), _pad_heads(gk, GLA_HEADS, GLA_DK),
            _pad_heads(gv, GLA_HEADS, GLA_DV), _pad_heads(gr, GLA_HEADS, GLA_DV),
            pu, _pad_last(ga, LANES), _pad_last(ff, LANES)], axis=-1)
        return wq, fk, fv, wg

    ws = relayout(w_in.astype(BF16))
    bs = relayout(b_in[:, None, :])
    return list(zip(ws, bs))


def _inproj(x2d, scale, shift, prep, B, S, tm):
    T, D = x2d.shape
    (wq, bq), (wk, bk), (wv, bv), (wg, bg) = prep
    nt = S // tm
    full = lambda a: pl.BlockSpec(a.shape, lambda i: (0,) * a.ndim)
    row = lambda w: pl.BlockSpec((tm, w), lambda i: (i, 0))
    mod = pl.BlockSpec((1, 1, D), lambda i: (i // nt, 0, 0))
    return pl.pallas_call(
        _inproj_kernel,
        grid=(T // tm,),
        in_specs=[row(D), mod, mod, full(wq), full(wk), full(wv), full(wg),
                  full(bq), full(bk), full(bv), full(bg)],
        out_specs=[row(FOX_WIDTH), row(FOX_WIDTH), row(FOX_WIDTH), row(G_WIDTH)],
        out_shape=[jax.ShapeDtypeStruct((T, FOX_WIDTH), BF16)] * 3
                  + [jax.ShapeDtypeStruct((T, G_WIDTH), F32)],
        compiler_params=_cparams(("arbitrary",)),
        name="inproj",
    )(x2d, scale, shift, wq, wk, wv, wg, bq, bk, bv, bg)


def _aug_constants():
    pq = np.zeros((3 * LANES, FOX_PAD), np.float32)
    pk = np.zeros((3 * LANES, FOX_PAD), np.float32)
    cq = np.zeros((1, FOX_PAD), np.float32)
    ck = np.zeros((1, FOX_PAD), np.float32)
    cv = np.zeros((1, FOX_PAD), np.float32)
    spread = np.zeros((FOX_WIDTH, FOX_PAD), np.float32)
    for h in range(FOX_HEADS):
        base = h * LANES + BIAS_LANE
        cv[0, base] = 1.0
        for d in range(FOX_HEAD_DIM):
            spread[h * FOX_HEAD_DIM + d, h * LANES + d] = 1.0
        for p in range(3):
            pq[p * LANES + h, base + p] = 1.0
            pk[p * LANES + h, base + 3 + p] = -1.0
            cq[0, base + 3 + p] = 1.0
            ck[0, base + p] = 1.0
    return pq, pk, cq, ck, cv, spread


def _fgate_kernel(q_ref, k_ref, v_ref, f_ref, pq_ref, pk_ref, cq_ref, ck_ref, cv_ref, sp_ref,
                  u_ref, wpool_ref, spool_ref,
                  q2_ref, k2_ref, vt_ref, st_ref, opool_ref, carry, xx):
    @pl.when(pl.program_id(1) == 0)
    def _():
        carry[...] = jnp.zeros_like(carry)

    _pool_kernel(u_ref, wpool_ref, spool_ref, opool_ref, xx)

    vf = _dot(v_ref[...], sp_ref[...]) + cv_ref[...]
    for h in range(FOX_HEADS):
        sl = slice(h * LANES, (h + 1) * LANES)
        vt_ref[0, 0, sl, :] = vf[:, sl].T.astype(BF16)

    tf = f_ref.shape[0]
    ls = _log_sigmoid(f_ref[...])
    r = lax.broadcasted_iota(jnp.int32, (tf, tf), 0)
    c = lax.broadcasted_iota(jnp.int32, (tf, tf), 1)
    tri = (c <= r).astype(BF16)
    hi, mid, lo = _split3(ls)
    cs = _dot(tri, hi) + _dot(tri, mid) + _dot(tri, lo)
    F = cs + carry[...]
    carry[...] = F[tf - 1:tf, :]
    fh, fm, fl = _split3(F)
    f3 = jnp.concatenate([fh, fm, fl], axis=1)
    qf = _dot(q_ref[...], sp_ref[...])
    kf = _dot(k_ref[...], sp_ref[...])
    q2_ref[...] = (qf + _dot(f3, pq_ref[...]) + cq_ref[...]).astype(BF16)
    k2_ref[...] = (kf + _dot(f3, pk_ref[...]) + ck_ref[...]).astype(BF16)
    lane = lax.broadcasted_iota(jnp.int32, (1, LANES), 1)
    qstat = jnp.zeros((1, LANES), F32)
    kstat = jnp.zeros((1, LANES), F32)
    for h in range(FOX_HEADS):
        sl = slice(h * LANES, (h + 1) * LANES)
        qm = jnp.max(jnp.sum(qf[:, sl] * qf[:, sl], axis=-1, keepdims=True), axis=0, keepdims=True)
        km = jnp.max(jnp.sum(kf[:, sl] * kf[:, sl], axis=-1, keepdims=True), axis=0, keepdims=True)
        qstat = jnp.where(lane == h, qm, qstat)
        kstat = jnp.where(lane == h, km, kstat)
    row = lax.broadcasted_iota(jnp.int32, (STAT_ROWS, LANES), 0)
    st_ref[...] = jnp.where(row == 0, F[0:1, :],
                            jnp.where(row == 1, F[tf - 1:tf, :],
                                      jnp.where(row == 2, qstat, jnp.where(row == 3, kstat, 0.0))))


def _fgate(zq, zk, zv, zg, w_pool, s_pool, B, S, tf):
    T = zq.shape[0]
    nt = S // tf
    pq, pk, cq, ck, cv, sp = _aug_constants()
    pq, pk, sp = jnp.asarray(pq, BF16), jnp.asarray(pk, BF16), jnp.asarray(sp, BF16)
    cq, ck, cv = jnp.asarray(cq), jnp.asarray(ck), jnp.asarray(cv)
    full = lambda a: pl.BlockSpec(a.shape, lambda b, i: (0,) * a.ndim)
    row = pl.BlockSpec((tf, FOX_PAD), lambda b, i: (b * nt + i, 0))
    packed = pl.BlockSpec((tf, FOX_WIDTH), lambda b, i: (b * nt + i, 0))
    return pl.pallas_call(
        _fgate_kernel,
        grid=(B, nt),
        in_specs=[packed, packed, packed,
                  pl.BlockSpec((tf, LANES), lambda b, i: (b * nt + i, G_OFF_F // LANES)),
                  full(pq), full(pk), full(cq), full(ck), full(cv), full(sp),
                  pl.BlockSpec((tf, POOL_WIDTH), lambda b, i: (b * nt + i, G_OFF_U // POOL_WIDTH)),
                  full(w_pool), full(s_pool)],
        out_specs=[row, row, pl.BlockSpec((1, 1, FOX_PAD, tf), lambda b, i: (b, i, 0, 0)),
                   pl.BlockSpec((STAT_ROWS, LANES), lambda b, i: (b * nt + i, 0)),
                   pl.BlockSpec((tf, POOL_WIDTH), lambda b, i: (b * nt + i, 0))],
        out_shape=[jax.ShapeDtypeStruct((T, FOX_PAD), BF16)] * 2
                  + [jax.ShapeDtypeStruct((B, nt, FOX_PAD, tf), BF16),
                     jax.ShapeDtypeStruct((B * nt * STAT_ROWS, LANES), F32),
                     jax.ShapeDtypeStruct((T, POOL_WIDTH), BF16)],
        scratch_shapes=[pltpu.VMEM((1, LANES), F32), pltpu.VMEM((tf + HALO, POOL_WIDTH), F32)],
        compiler_params=_cparams(("arbitrary", "arbitrary")),
        name="fgate_pool",
    )(zq, zk, zv, zg, pq, pk, cq, ck, cv, sp, zg, w_pool, s_pool)


def _attn_kernel(ff_ref, fl_ref, qn_ref, kn_ref, q_ref, k_ref, vt_ref, o_ref, m_sc, acc_sc, s_sc,
                 *, blk, nq, nb):
    h = pl.program_id(0)
    i = pl.program_id(1)

    def scores(slot, j):
        off = pl.multiple_of(j * blk, blk)
        for b in range(nb):
            s_sc[slot, b] = _dot_nt(k_ref[b, pl.ds(off, blk), :], q_ref[b])

    def softmax_pv(slot, j, diag):
        for b in range(nb):
            s = s_sc[slot, b]
            if diag:
                r = lax.broadcasted_iota(jnp.int32, (blk, blk), 0)
                c = lax.broadcasted_iota(jnp.int32, (blk, blk), 1)
                s = jnp.where(r <= c, s, NEG_BIG)
            m_prev = m_sc[b]
            m_new = jnp.maximum(m_prev, jnp.max(s, axis=0, keepdims=True))
            alpha = jnp.exp(m_prev - m_new)
            p = jnp.exp(s - m_new)
            acc_sc[b] = alpha * acc_sc[b] + _dot(vt_ref[b, j], p.astype(BF16))
            m_sc[b] = m_new

    m_sc[...] = jnp.full_like(m_sc, NEG_BIG)
    acc_sc[...] = jnp.zeros_like(acc_sc)
    scores(0, i)
    scores(1, jnp.maximum(i - 1, 0))
    softmax_pv(0, i, True)

    n = jnp.int32(0)
    for b in range(nb):
        base = (b * FOX_HEADS + h) * nq
        slack = (qn_ref[base + i] * kn_ref[b * FOX_HEADS + h] + ff_ref[base + i]
                 - jnp.min(m_sc[b]) + PRUNE_MARGIN)

        def cond(t, base=base, slack=slack):
            return jnp.logical_and(t < i, slack - fl_ref[base + jnp.maximum(i - 1 - t, 0)] >= 0.0)

        n = jnp.maximum(n, lax.while_loop(cond, lambda t: t + 1, jnp.int32(0)))

    def pair(u, carry):
        t = 1 + 2 * u
        ja = i - t
        scores(0, jnp.maximum(ja - 1, 0))
        softmax_pv(1, ja, False)

        @pl.when(t + 1 <= n)
        def _():
            scores(1, jnp.maximum(ja - 2, 0))
            softmax_pv(0, ja - 1, False)

        return carry

    lax.fori_loop(0, lax.shift_right_logical(n + 1, 1), pair, 0)
    for b in range(nb):
        acc = acc_sc[b]
        o_ref[b] = (acc / acc[BIAS_LANE:BIAS_LANE + 1, :]).T.astype(o_ref.dtype)


def _attention(q2, k2, vt, stats, B, S, blk):
    T = q2.shape[0]
    nq = S // blk
    H = FOX_HEADS
    st = stats.reshape(B, nq, STAT_ROWS, LANES)[:, :, :, :H]
    tab = lambda r: jnp.transpose(st[:, :, r, :], (0, 2, 1)).reshape(-1)
    ffirst, flast = tab(0), tab(1)
    qn = jnp.sqrt(tab(2)) * NORM_SLACK
    kn = jnp.sqrt(jnp.max(st[:, :, 3, :], axis=1)).reshape(-1) * NORM_SLACK
    r3 = lambda a: a.reshape(B, S, FOX_PAD)
    qspec = pl.BlockSpec((B, blk, LANES), lambda h, i, *_: (0, i, h))
    kspec = pl.BlockSpec((B, S, LANES), lambda h, i, *_: (0, 0, h))
    vtspec = pl.BlockSpec((B, nq, LANES, blk), lambda h, i, *_: (0, 0, h, 0))
    grid_spec = pltpu.PrefetchScalarGridSpec(
        num_scalar_prefetch=4,
        grid=(H, nq),
        in_specs=[qspec, kspec, vtspec],
        out_specs=qspec,
        scratch_shapes=[pltpu.VMEM((B, 1, blk), F32), pltpu.VMEM((B, LANES, blk), F32),
                        pltpu.VMEM((2, B, blk, blk), F32)],
    )
    out = pl.pallas_call(
        functools.partial(_attn_kernel, blk=blk, nq=nq, nb=B),
        grid_spec=grid_spec,
        out_shape=jax.ShapeDtypeStruct((B, S, FOX_PAD), BF16),
        compiler_params=_cparams(("arbitrary", "arbitrary")),
        name="fox_attention",
    )(ffirst, flast, qn, kn, r3(q2), r3(k2), vt)
    return out.reshape(T, FOX_PAD)


def _gla_kernel(q_ref, k_ref, v_ref, r_ref, a_ref, wa_ref, ba_ref, g_ref, o_ref, st_sc, *, tg, nb):
    @pl.when(pl.program_id(0) == 0)
    def _():
        st_sc[...] = jnp.zeros_like(st_sc)

    C = GLA_CHUNK
    nchunk = tg // C
    r = lax.broadcasted_iota(jnp.int32, (tg, tg), 0)
    c = lax.broadcasted_iota(jnp.int32, (tg, tg), 1)
    shift = C.bit_length() - 1
    tri = ((c <= r) & ((c >> shift) == (r >> shift))).astype(BF16)
    rc = lax.broadcasted_iota(jnp.int32, (C, C), 0)
    cc = lax.broadcasted_iota(jnp.int32, (C, C), 1)
    causal = cc <= rc
    lane = lax.broadcasted_iota(jnp.int32, (1, LANES), 1)
    vmask = (lane < GLA_DV).astype(F32)
    for bi in range(nb):
        la = (_log_sigmoid(_dot(a_ref[bi].astype(BF16), wa_ref[...]) + ba_ref[...])
              * (1.0 / GLA_GATE_TAU))
        hi, mid, lo = _split3(la)
        b = _dot(tri, hi) + _dot(tri, mid) + _dot(tri, lo)
        eb = jnp.exp(b)
        q_in = q_ref[bi] * (GLA_DK ** -0.5) * eb
        k_in = k_ref[bi] * jnp.exp(-b)
        v = v_ref[bi]
        outs = []
        for ci in range(nchunk):
            rows = slice(ci * C, (ci + 1) * C)
            b_last = b[ci * C + C - 1:ci * C + C, :]
            k_out = k_ref[bi, rows, :] * jnp.exp(b_last - b[rows, :])
            dec = jnp.exp(b_last)
            heads = []
            for h in range(GLA_HEADS):
                ln = slice(h * LANES, (h + 1) * LANES)
                qh = q_in[rows, ln].astype(BF16)
                kh = k_in[rows, ln].astype(BF16)
                vh = v[rows, ln].astype(BF16)
                attn = jnp.where(causal, _dot_nt(qh, kh), 0.0)
                st = st_sc[bi * GLA_HEADS + h]
                o = _dot(attn.astype(BF16), vh) + _dot_nt(qh, st.astype(BF16))
                kv_t = _dot_tn(vh, k_out[:, ln].astype(BF16))
                st_sc[bi * GLA_HEADS + h] = st * dec[:, ln] + kv_t
                ms = jnp.sum(o * o, axis=-1, keepdims=True) * (1.0 / GLA_DV)
                heads.append(o * lax.rsqrt(ms + RMS_EPS) * vmask)
            outs.append(jnp.concatenate(heads, axis=1))
        o_all = jnp.concatenate(outs, axis=0)
        gr = r_ref[bi]
        o_ref[bi] = ((o_all * g_ref[...]) * (gr * jax.nn.sigmoid(gr))).astype(o_ref.dtype)


def _gla(zg, wa, ba, gn, B, S, tg):
    T = zg.shape[0]
    nt = S // tg
    zg3 = zg.reshape(B, S, G_WIDTH)
    col = lambda off, w: pl.BlockSpec((B, tg, w), lambda i: (0, i, off // w))
    full = lambda a: pl.BlockSpec(a.shape, lambda i: (0,) * a.ndim)
    out = pl.pallas_call(
        functools.partial(_gla_kernel, tg=tg, nb=B),
        grid=(nt,),
        in_specs=[col(G_OFF_Q, GLA_PAD), col(G_OFF_K, GLA_PAD), col(G_OFF_V, GLA_PAD),
                  col(G_OFF_R, GLA_PAD), col(G_OFF_A, LANES), full(wa), full(ba), full(gn)],
        out_specs=pl.BlockSpec((B, tg, GLA_PAD), lambda i: (0, i, 0)),
        out_shape=jax.ShapeDtypeStruct((B, S, GLA_PAD), BF16),
        scratch_shapes=[pltpu.VMEM((B * GLA_HEADS, LANES, LANES), F32)],
        compiler_params=_cparams(("arbitrary",)),
        name="gla",
    )(zg3, zg3, zg3, zg3, zg3, wa, ba, gn)
    return out.reshape(T, GLA_PAD)


HALO = max(POOL_WINDOWS)


def _pool_kernel(u_ref, w_ref, s_ref, o_ref, xx):
    tp = u_ref.shape[0]
    i = pl.program_id(1)

    @pl.when(i == 0)
    def _():
        xx[0:HALO, :] = jnp.zeros((HALO, POOL_WIDTH), F32)

    @pl.when(i > 0)
    def _():
        xx[0:HALO, :] = xx[tp:tp + HALO, :]

    u = u_ref[...]
    xx[HALO:HALO + tp, :] = u
    lane = lax.broadcasted_iota(jnp.int32, (1, POOL_WIDTH), 1)
    grp = lane >> (POOL_GROUP.bit_length() - 1)
    pos = lax.broadcasted_iota(jnp.int32, (tp, 1), 0) + i * tp + 1
    acc = u
    mean = jnp.zeros_like(u)
    for j in range(1, HALO):
        acc = acc + xx[HALO - j:HALO - j + tp, :]
        w = j + 1
        if w in POOL_WINDOWS:
            g = POOL_WINDOWS.index(w)
            inv_cnt = 1.0 / jnp.minimum(pos, w).astype(F32)
            mean = jnp.where(grp == g, acc * inv_cnt, mean)
    pooled = mean - u
    mixed = _dot(pooled.astype(BF16), w_ref[...])
    o_ref[...] = (mixed * s_ref[...]).astype(o_ref.dtype)


def _layer_norm(r, g, b):
    mu = jnp.mean(r, axis=-1, keepdims=True)
    d = r - mu
    var = jnp.mean(d * d, axis=-1, keepdims=True)
    return d * lax.rsqrt(var + LN_EPS) * g + b


def _outproj_kernel(of_ref, og_ref, op_ref, x_ref, wf_ref, wg_ref, wp_ref, gate_ref, lg_ref, lb_ref,
                    sc_ref, sh_ref, wr_ref, br_ref,
                    x1_ref, h2_ref, e_ref, gt_ref, cnt_ref, *, alpha):
    y = _dot(of_ref[...], wf_ref[...]) + _dot(og_ref[...], wg_ref[...]) + _dot(op_ref[...], wp_ref[...])
    r = alpha * x_ref[...] + (1.0 + gate_ref[0]) * y
    x1 = _layer_norm(r, lg_ref[...], lb_ref[...])
    x1_ref[...] = x1
    h2 = x1 * (1.0 + sc_ref[0]) + sh_ref[0]
    h2_ref[...] = h2
    h_hi = h2.astype(BF16)
    h_lo = (h2 - h_hi.astype(F32)).astype(BF16)
    logits = (_dot(h_hi, wr_ref[0]) + (_dot(h_hi, wr_ref[1]) + _dot(h_lo, wr_ref[0]))
              + br_ref[...])
    tm = logits.shape[0]
    lane_i = lax.broadcasted_iota(jnp.int32, (tm, LANES), 1)
    lane = lane_i.astype(F32)
    work = logits
    tops, idxs = [], []
    onehot = jnp.zeros((tm, LANES), F32)
    for _ in range(TOP_K):
        m = jnp.max(work, axis=-1, keepdims=True)
        idx = jnp.min(jnp.where(work == m, lane, float(LANES)), axis=-1, keepdims=True)
        sel = lane == idx
        onehot = onehot + sel.astype(F32)
        work = jnp.where(sel, -jnp.inf, work)
        tops.append(m)
        idxs.append(idx)
    ex = [jnp.exp(t - tops[0]) for t in tops]
    den = ex[0] + ex[1] + ex[2] + ex[3]
    e_out = jnp.zeros((tm, LANES), jnp.int32)
    g_out = jnp.zeros((tm, LANES), F32)
    for k in range(TOP_K):
        e_out = jnp.where(lane_i == k, idxs[k].astype(jnp.int32), e_out)
        g_out = jnp.where(lane_i == k, ex[k] / den, g_out)
    e_ref[...] = e_out
    gt_ref[...] = g_out
    for u in range(tm // ROUTE_TILE):
        rows = slice(u * ROUTE_TILE, (u + 1) * ROUTE_TILE)
        cnt_ref[u] = jnp.sum(onehot[rows], axis=0, keepdims=True).astype(jnp.int32)


def _outproj(o_fox, o_gla, o_pool, x2d, wf, wg, wp, gate1, ln_g, ln_b, scale2, shift2, wr, br,
             B, S, tm, alpha):
    T, D = x2d.shape
    nt = S // tm
    full = lambda a: pl.BlockSpec(a.shape, lambda i: (0,) * a.ndim)
    row = lambda w: pl.BlockSpec((tm, w), lambda i: (i, 0))
    mod = pl.BlockSpec((1, 1, D), lambda i: (i // nt, 0, 0))
    return pl.pallas_call(
        functools.partial(_outproj_kernel, alpha=alpha),
        grid=(T // tm,),
        in_specs=[row(FOX_PAD), row(GLA_PAD), row(POOL_WIDTH), row(D), full(wf), full(wg), full(wp),
                  mod, full(ln_g), full(ln_b), mod, mod, full(wr), full(br)],
        out_specs=[row(D), row(D), row(LANES), row(LANES),
                   pl.BlockSpec((tm // ROUTE_TILE, 1, LANES), lambda i: (i, 0, 0))],
        out_shape=[jax.ShapeDtypeStruct((T, D), F32), jax.ShapeDtypeStruct((T, D), F32),
                   jax.ShapeDtypeStruct((T, LANES), jnp.int32), jax.ShapeDtypeStruct((T, LANES), F32),
                   jax.ShapeDtypeStruct((T // ROUTE_TILE, 1, LANES), jnp.int32)],
        compiler_params=_cparams(("arbitrary",)),
        name="outproj_router",
    )(o_fox, o_gla, o_pool, x2d, wf, wg, wp, gate1, ln_g, ln_b, scale2, shift2, wr, br)


def _tile_slots(e_i32, off_row):
    td = e_i32.shape[0]
    lane = lax.broadcasted_iota(jnp.int32, (td, LANES), 1).astype(F32)
    ef = e_i32.astype(F32)
    sel = [lane == ef[:, k:k + 1] for k in range(TOP_K)]
    onehot = sel[0].astype(F32)
    for k in range(1, TOP_K):
        onehot = onehot + sel[k].astype(F32)
    rr = lax.broadcasted_iota(jnp.int32, (td, td), 0)
    cc = lax.broadcasted_iota(jnp.int32, (td, td), 1)
    stril = (cc < rr).astype(BF16)
    tab = _dot(stril, onehot.astype(BF16)) + off_row
    return [jnp.sum(jnp.where(sel[k], tab, 0.0), axis=-1, keepdims=True) for k in range(TOP_K)]


def _group_bits(max_groups):
    return [1 << s for s in range(max_groups.bit_length() - 1, -1, -1)]


PIECE_CLASSES = (ROUTE_TILE // ROW_GROUP).bit_length()
PIECE_LIST = 1024
SLOT_BITS = 8
assert ROUTE_SLOTS // ROW_GROUP < (1 << SLOT_BITS) and PIECE_CLASSES * N_EXPERTS <= PIECE_LIST


def _piece_lists(m, off, dst):
    nt = m.shape[0]
    j = jnp.arange(N_EXPERTS, dtype=jnp.int32)
    b = jnp.arange(PIECE_CLASSES, dtype=jnp.int32)[None, :, None]
    mm, oo, dd = m[:, None, :], off[:, None, :], dst[:, None, :]
    has = (mm >> b) & 1
    before = (mm & ~((2 << b) - 1)) * ROW_GROUP
    packed = (((dd + before) // ROW_GROUP) << SLOT_BITS) | ((oo + before) // ROW_GROUP)
    rank = _excl_prefix(has, 2)
    hit = (has[..., None] == 1) & (rank[..., None] == j)
    lst = jnp.sum(jnp.where(hit, packed[..., None], 0), axis=2)
    lst = _pad_last(lst.reshape(nt, PIECE_CLASSES * N_EXPERTS).astype(jnp.int32), PIECE_LIST)
    return lst.reshape(-1), jnp.sum(has, axis=2).astype(jnp.int32).reshape(-1)


def _for_each_piece(tile, cnt_ref, list_ref, fn):
    for b in range(PIECE_CLASSES):
        def body(i, carry, b=b):
            v = list_ref[b * N_EXPERTS + i]
            fn(pl.multiple_of((v & ((1 << SLOT_BITS) - 1)) * ROW_GROUP, ROW_GROUP),
               pl.multiple_of(lax.shift_right_logical(v, SLOT_BITS) * ROW_GROUP, ROW_GROUP),
               (1 << b) * ROW_GROUP)
            return carry

        lax.fori_loop(0, cnt_ref[tile * PIECE_CLASSES + b], body, 0)


def _for_each_total_piece(total_groups, fn):
    for bit in _group_bits(ROUTE_SLOTS // ROW_GROUP):
        @pl.when((total_groups & bit) != 0)
        def _(bit=bit):
            fn(bit * ROW_GROUP)


def _dispatch_kernel(cnt_ref, tot_ref, zblk_ref, nu_ref, list_ref, e_ref, offrow_ref, h_ref,
                     xr_ref, slot_ref, sort_sc, zero_sc, sems, zsem, *, n_blocks, nt):
    step = pl.program_id(0)

    @pl.when(step == 0)
    def _():
        zero_sc[...] = jnp.zeros_like(zero_sc)

        def zcopy(blk):
            return pltpu.make_async_copy(
                zero_sc, xr_ref.at[pl.ds(pl.multiple_of(blk * MOE_BLOCK, MOE_BLOCK), MOE_BLOCK), :], zsem)

        def zstart(e, c):
            @pl.when(zblk_ref[e] >= 0)
            def _():
                zcopy(zblk_ref[e]).start()
            return c

        def zwait(e, c):
            @pl.when(zblk_ref[e] >= 0)
            def _():
                zcopy(0).wait()
            return c

        def tstart(blk, c):
            zcopy(blk).start()
            return c

        def twait(blk, c):
            zcopy(0).wait()
            return c

        lax.fori_loop(0, N_EXPERTS, zstart, 0)
        lax.fori_loop(nu_ref[0], n_blocks, tstart, 0)
        lax.fori_loop(0, N_EXPERTS, zwait, 0)
        lax.fori_loop(nu_ref[0], n_blocks, twait, 0)

    td = h_ref.shape[0]
    slots = _tile_slots(e_ref[...], offrow_ref[0])
    lane = lax.broadcasted_iota(jnp.int32, (td, LANES), 1)
    cols = jnp.full((td, LANES), -1.0, F32)
    for k in range(TOP_K):
        cols = jnp.where(lane == k, slots[k], cols)
    slot_ref[...] = cols
    rows_t = cols.T
    sub = lax.broadcasted_iota(jnp.int32, (ROUTE_SLOTS, td), 0).astype(F32)
    pick = sub == rows_t[0:1, :]
    for k in range(1, TOP_K):
        pick = pick | (sub == rows_t[k:k + 1, :])
    buf = step & 1
    sort_sc[buf] = _dot(pick.astype(BF16), h_ref[...].astype(BF16))

    def start_chunk(slot0, row0, n):
        pltpu.make_async_copy(sort_sc.at[buf, pl.ds(slot0, n), :], xr_ref.at[pl.ds(row0, n), :],
                              sems.at[buf]).start()

    _for_each_piece(step, cnt_ref, list_ref, start_chunk)

    def wait_tile(tile, b):
        def wait_piece(n):
            pltpu.make_async_copy(sort_sc.at[b, pl.ds(0, n), :], xr_ref.at[pl.ds(0, n), :],
                                  sems.at[b]).wait()
        _for_each_total_piece(tot_ref[tile], wait_piece)

    @pl.when(step > 0)
    def _():
        wait_tile(step - 1, 1 - buf)

    @pl.when(step == nt - 1)
    def _():
        wait_tile(step, buf)


def _dispatch(h2, e_pad, tables, rows):
    T, D = h2.shape
    td = ROUTE_TILE
    nt = T // td
    piece_list, piece_cnt, tot_tab, zblk, n_used, off_rows = tables
    grid_spec = pltpu.PrefetchScalarGridSpec(
        num_scalar_prefetch=4,
        grid=(nt,),
        in_specs=[pl.BlockSpec((PIECE_LIST,), lambda i, *_: (i,), memory_space=pltpu.SMEM),
                  pl.BlockSpec((td, LANES), lambda i, *_: (i, 0)),
                  pl.BlockSpec((1, 1, LANES), lambda i, *_: (i, 0, 0)),
                  pl.BlockSpec((td, D), lambda i, *_: (i, 0))],
        out_specs=[pl.BlockSpec(memory_space=pl.ANY),
                   pl.BlockSpec((td, LANES), lambda i, *_: (i, 0))],
        scratch_shapes=[pltpu.VMEM((2, ROUTE_SLOTS, D), F32), pltpu.VMEM((MOE_BLOCK, D), F32),
                        pltpu.SemaphoreType.DMA((2,)), pltpu.SemaphoreType.DMA],
    )
    return pl.pallas_call(
        functools.partial(_dispatch_kernel, n_blocks=rows // MOE_BLOCK, nt=nt),
        grid_spec=grid_spec,
        out_shape=[jax.ShapeDtypeStruct((rows, D), F32), jax.ShapeDtypeStruct((T, LANES), F32)],
        compiler_params=_cparams(("arbitrary",)),
        name="moe_dispatch",
    )(piece_cnt, tot_tab, zblk, n_used, piece_list, e_pad, off_rows, h2)


def _expert_kernel(be_ref, nu_ref, vb_ref, par_ref, nxt_ref, x_ref, wgu_hbm, bgu_ref, wd_hbm, bd_ref, y_ref,
                   wgu_sc, wd_sc, gu_buf, d_buf, sems, *, first_expert):
    i = pl.program_id(0)
    used = i < nu_ref[0]

    def fetch(e, slot):
        return (pltpu.make_async_copy(wgu_hbm.at[first_expert + e], gu_buf.at[slot], sems.at[slot, 0]),
                pltpu.make_async_copy(wd_hbm.at[first_expert + e], d_buf.at[slot], sems.at[slot, 1]))

    @pl.when(used)
    def _():
        prev = be_ref[jnp.maximum(i - 1, 0)]

        @pl.when((i == 0) | (be_ref[i] != prev))
        def _():
            slot = par_ref[i]

            @pl.when(i == 0)
            def _():
                for cp in fetch(be_ref[i], slot):
                    cp.start()

            for cp in fetch(be_ref[i], slot):
                cp.wait()
            wgu_sc[...] = gu_buf[slot].astype(BF16)
            wd_sc[...] = d_buf[slot].astype(BF16)

            @pl.when(nxt_ref[i] >= 0)
            def _():
                for cp in fetch(nxt_ref[i], 1 - slot):
                    cp.start()

        def mlp(rows):
            x = x_ref[rows, :].astype(BF16)
            gu = _dot(x, wgu_sc[...]) + bgu_ref[0]
            glu = jnp.minimum(gu[:, :D_EXPERT], SWIGLU_LIMIT)
            lin = jnp.clip(gu[:, D_EXPERT:], -SWIGLU_LIMIT, SWIGLU_LIMIT)
            act = glu * jax.nn.sigmoid(SWIGLU_ALPHA * glu) * (lin + 1.0)
            y_ref[rows, :] = _dot(act.astype(BF16), wd_sc[...]) + bd_ref[0]

        half = MOE_BLOCK // 2
        full_block = vb_ref[i] > half

        @pl.when(full_block)
        def _():
            mlp(slice(0, MOE_BLOCK))

        @pl.when(jnp.logical_not(full_block))
        def _():
            mlp(slice(0, half))
            y_ref[half:, :] = jnp.zeros((MOE_BLOCK - half, y_ref.shape[1]), y_ref.dtype)

    @pl.when(jnp.logical_not(used))
    def _():
        y_ref[...] = jnp.zeros_like(y_ref)


def _experts(x_rows, block_expert, n_used, valid_rows, slot_parity, next_expert,
             w_gate_up, b_gate_up, w_down, b_down, layer):
    rows, D = x_rows.shape
    nb = rows // MOE_BLOCK
    E = w_gate_up.shape[1]
    grid_spec = pltpu.PrefetchScalarGridSpec(
        num_scalar_prefetch=5,
        grid=(nb,),
        in_specs=[pl.BlockSpec((MOE_BLOCK, D), lambda i, be, nu, *_: (jnp.minimum(i, nu[0] - 1), 0)),
                  pl.BlockSpec(memory_space=pl.ANY),
                  pl.BlockSpec((1, 1, 2 * D_EXPERT), lambda i, be, *_: (layer * E + be[i], 0, 0)),
                  pl.BlockSpec(memory_space=pl.ANY),
                  pl.BlockSpec((1, 1, D), lambda i, be, *_: (layer * E + be[i], 0, 0))],
        out_specs=pl.BlockSpec((MOE_BLOCK, D), lambda i, *_: (i, 0)),
        scratch_shapes=[pltpu.VMEM((D, 2 * D_EXPERT), BF16), pltpu.VMEM((D_EXPERT, D), BF16),
                        pltpu.VMEM((2, D, 2 * D_EXPERT), F32), pltpu.VMEM((2, D_EXPERT, D), F32),
                        pltpu.SemaphoreType.DMA((2, 2))],
    )
    L = w_gate_up.shape[0]
    return pl.pallas_call(
        functools.partial(_expert_kernel, first_expert=layer * E),
        grid_spec=grid_spec,
        out_shape=jax.ShapeDtypeStruct((rows, D), F32),
        compiler_params=_cparams(("arbitrary",)),
        name="moe_experts",
    )(block_expert, n_used, valid_rows, slot_parity, next_expert, x_rows,
      w_gate_up.reshape(L * E, D, 2 * D_EXPERT), b_gate_up.reshape(L * E, 1, 2 * D_EXPERT),
      w_down.reshape(L * E, D_EXPERT, D), b_down.reshape(L * E, 1, D))


def _combine_kernel(cnt_ref, tot_ref, list_ref, next_list_ref, slot_ref, gt_ref, y_ref, x_ref, gate_ref,
                    lg_ref, lb_ref, o_ref, ybuf, sems, *, nt, alpha):
    step = pl.program_id(0)
    buf = step & 1

    def fetch(tile, b, lst):
        def start_chunk(slot0, row0, n):
            pltpu.make_async_copy(y_ref.at[pl.ds(row0, n), :], ybuf.at[b, pl.ds(slot0, n), :],
                                  sems.at[b]).start()
        _for_each_piece(tile, cnt_ref, lst, start_chunk)

    @pl.when(step == 0)
    def _():
        ybuf[...] = jnp.zeros_like(ybuf)
        fetch(step, buf, list_ref)

    @pl.when(step + 1 < nt)
    def _():
        fetch(step + 1, 1 - buf, next_list_ref)

    def wait_piece(n):
        pltpu.make_async_copy(y_ref.at[pl.ds(0, n), :], ybuf.at[buf, pl.ds(0, n), :], sems.at[buf]).wait()

    _for_each_total_piece(tot_ref[step], wait_piece)

    td = x_ref.shape[0]
    slots = slot_ref[...]
    gt = gt_ref[...]
    lane = lax.broadcasted_iota(jnp.int32, (td, ROUTE_SLOTS), 1).astype(F32)
    w = jnp.where(lane == slots[:, 0:1], gt[:, 0:1], 0.0)
    for k in range(1, TOP_K):
        w = w + jnp.where(lane == slots[:, k:k + 1], gt[:, k:k + 1], 0.0)
    y = _dot(w.astype(BF16), ybuf[buf].astype(BF16))
    r = alpha * x_ref[...] + (1.0 + gate_ref[0]) * y
    o_ref[...] = _layer_norm(r, lg_ref[...], lb_ref[...])


def _combine(y_rows, slots, tables, gates, x1, gate2, ln_g, ln_b, B, S, alpha):
    T, D = x1.shape
    td = ROUTE_TILE
    nt = T // td
    per_batch = S // td
    piece_list, piece_cnt, tot_tab, _, _, _ = tables
    grid_spec = pltpu.PrefetchScalarGridSpec(
        num_scalar_prefetch=2,
        grid=(nt,),
        in_specs=[pl.BlockSpec((PIECE_LIST,), lambda i, *_: (i,), memory_space=pltpu.SMEM),
                  pl.BlockSpec((PIECE_LIST,), lambda i, *_: (jnp.minimum(i + 1, nt - 1),),
                               memory_space=pltpu.SMEM),
                  pl.BlockSpec((td, LANES), lambda i, *_: (i, 0)),
                  pl.BlockSpec((td, LANES), lambda i, *_: (i, 0)),
                  pl.BlockSpec(memory_space=pl.ANY),
                  pl.BlockSpec((td, D), lambda i, *_: (i, 0)),
                  pl.BlockSpec((1, 1, D), lambda i, *_: (i // per_batch, 0, 0)),
                  pl.BlockSpec((1, D), lambda i, *_: (0, 0)),
                  pl.BlockSpec((1, D), lambda i, *_: (0, 0))],
        out_specs=pl.BlockSpec((td, D), lambda i, *_: (i, 0)),
        scratch_shapes=[pltpu.VMEM((2, ROUTE_SLOTS, D), F32), pltpu.SemaphoreType.DMA((2,))],
    )
    return pl.pallas_call(
        functools.partial(_combine_kernel, nt=nt, alpha=alpha),
        grid_spec=grid_spec,
        out_shape=jax.ShapeDtypeStruct((T, D), F32),
        compiler_params=_cparams(("arbitrary",)),
        name="moe_combine",
    )(piece_cnt, tot_tab, piece_list, piece_list, slots, gates, y_rows, x1, gate2, ln_g, ln_b)


def _tile(n, pref):
    t = min(n, pref)
    assert n % t == 0, (n, t)
    return t


def kernel(x, c, w_ada, b_ada, w_in, b_in, gla_w_a2, gla_b_a, gla_norm_g, pool_w, pool_scale, w_out,
           ln1_g, ln1_b, w_router, b_router, w_gate_up, b_gate_up, w_down, b_down, ln2_g, ln2_b):
    B, S, D = x.shape
    L = w_ada.shape[0]
    T = B * S
    assert D == D_MODEL and S % GLA_CHUNK == 0
    alpha = float((2 * L) ** 0.25)
    assert T % ROUTE_TILE == 0
    n_tiles = T // ROUTE_TILE
    max_rows = T * TOP_K + n_tiles * N_EXPERTS * (ROW_GROUP - 1) + N_EXPERTS * (MOE_BLOCK - 1)
    n_blocks = -(-max_rows // MOE_BLOCK)
    rows = n_blocks * MOE_BLOCK

    mod = _ada_mod(c, w_ada, b_ada)
    prep_all = _prep_inproj(w_in, b_in)
    wa_all = jnp.pad(_pad_heads(gla_w_a2, GLA_HEADS, GLA_DK),
                     ((0, 0), (0, LANES - GLA_GATE_RANK), (0, 0))).astype(BF16)
    ba_all = _pad_heads(gla_b_a[:, None, :], GLA_HEADS, GLA_DK)
    gn_all = _pad_heads(gla_norm_g[:, None, :], GLA_HEADS, GLA_DV)
    w_bd_all = jnp.zeros((L, POOL_WIDTH, POOL_WIDTH), F32)
    for g in range(len(POOL_WINDOWS)):
        sl = slice(g * POOL_GROUP, (g + 1) * POOL_GROUP)
        w_bd_all = w_bd_all.at[:, sl, sl].set(pool_w[:, g])
    w_bd_all = w_bd_all.astype(BF16)
    pad_rows = lambda w, heads, dim: jnp.swapaxes(_pad_heads(jnp.swapaxes(w, 1, 2), heads, dim), 1, 2)
    wf_all = pad_rows(w_out[:, :FOX_WIDTH], FOX_HEADS, FOX_HEAD_DIM).astype(BF16)
    wgl_all = pad_rows(w_out[:, FOX_WIDTH:FOX_WIDTH + GLA_VWIDTH], GLA_HEADS, GLA_DV).astype(BF16)
    wp_all = w_out[:, FOX_WIDTH + GLA_VWIDTH:].astype(BF16)
    wr_f32 = _pad_last(w_router, LANES)
    wr_hi = wr_f32.astype(BF16)
    wr_all = jnp.stack([wr_hi, (wr_f32 - wr_hi.astype(F32)).astype(BF16)], axis=1)
    br_all = jnp.pad(b_router[:, None, :], ((0, 0), (0, 0), (0, LANES - N_EXPERTS)), constant_values=NEG_BIG)

    x2d = x.reshape(T, D)
    for l in range(L):
        shift1, scale1, gate1, shift2, scale2, gate2 = [mod[l, :, m] for m in range(N_MOD)]
        prep = [(w[l], b[l]) for w, b in prep_all]
        zq, zk, zv, zg = _inproj(x2d, scale1, shift1, prep, B, S, _tile(S, 512))
        q2, k2, vt, stats, o_pool = _fgate(zq, zk, zv, zg, w_bd_all[l], pool_scale[l][None, :],
                                           B, S, _tile(S, ATTN_BLOCK))
        o_fox = _attention(q2, k2, vt, stats, B, S, _tile(S, ATTN_BLOCK))
        o_gla = _gla(zg, wa_all[l], ba_all[l], gn_all[l], B, S, _tile(S, 256))
        x1, h2, e_pad, g_pad, cnt = _outproj(
            o_fox, o_gla, o_pool, x2d, wf_all[l], wgl_all[l], wp_all[l], gate1,
            ln1_g[l][None, :], ln1_b[l][None, :], scale2, shift2, wr_all[l], br_all[l],
            B, S, _tile(S, 512), alpha)
        cte = cnt[:, 0, :N_EXPERTS]
        c8 = (cte + ROW_GROUP - 1) // ROW_GROUP * ROW_GROUP
        off = _excl_prefix(c8, 1)
        per_expert = jnp.sum(c8, axis=0)
        padded = (per_expert + MOE_BLOCK - 1) // MOE_BLOCK * MOE_BLOCK
        pstart = _excl_prefix(padded, 0)
        pend = pstart + padded
        dst = pstart[None, :] + _excl_prefix(c8, 0)
        n_used = (pend[-1] // MOE_BLOCK).astype(jnp.int32)
        blk_start = jnp.arange(n_blocks, dtype=jnp.int32) * MOE_BLOCK
        be = jnp.minimum(jnp.sum(blk_start[:, None] >= pend[None, :], axis=1), N_EXPERTS - 1).astype(jnp.int32)
        last_used = jnp.minimum(jnp.sum(MOE_BLOCK * (n_used - 1) >= pend), N_EXPERTS - 1).astype(jnp.int32)
        be = jnp.where(jnp.arange(n_blocks) < n_used, be, last_used)
        zblk = jnp.where(padded > 0, pend // MOE_BLOCK - 1, -1).astype(jnp.int32)
        off_rows = _pad_last(off.astype(F32), LANES)[:, None, :]
        n_used = n_used.reshape(1)
        piece_list, piece_cnt = _piece_lists(c8 // ROW_GROUP, off, dst)
        tables = (piece_list, piece_cnt, (jnp.sum(c8, axis=1) // ROW_GROUP).astype(jnp.int32),
                  zblk, n_used, off_rows)
        x_rows, slots = _dispatch(h2, e_pad, tables, rows)
        owns = padded > 0
        ids = jnp.arange(N_EXPERTS, dtype=jnp.int32)
        later = jnp.where(owns[None, :] & (ids[None, :] > ids[:, None]), ids[None, :], N_EXPERTS)
        nxt_e = jnp.min(later, axis=1)
        nxt_e = jnp.where(nxt_e < N_EXPERTS, nxt_e, -1).astype(jnp.int32)
        par_e = (_excl_prefix(owns.astype(jnp.int32), 0) & 1).astype(jnp.int32)
        pick = be[:, None] == ids[None, :]
        lookup = lambda tab: jnp.sum(jnp.where(pick, tab[None, :], 0), axis=1).astype(jnp.int32)
        valid_rows = jnp.clip(lookup(per_expert) - (blk_start - lookup(pstart)), 0, MOE_BLOCK).astype(jnp.int32)
        y_rows = _experts(x_rows, be, n_used, valid_rows, lookup(par_e), lookup(nxt_e),
                          w_gate_up, b_gate_up, w_down, b_down, l)
        x2d = _combine(y_rows, slots, tables, g_pad, x1, gate2,
                       ln2_g[l][None, :], ln2_b[l][None, :], B, S, alpha)
    return x2d.reshape(B, S, D)
```

```python
import functools

import numpy as np
import jax
import jax.numpy as jnp
from jax import lax
from jax.experimental import pallas as pl
from jax.experimental.pallas import tpu as pltpu

F32 = jnp.float32
BF16 = jnp.bfloat16

D_MODEL = 1024
FOX_HEADS = 6
FOX_HEAD_DIM = 64
FOX_WIDTH = FOX_HEADS * FOX_HEAD_DIM
GLA_HEADS = 4
GLA_DV = 96
GLA_DK = 48
GLA_KWIDTH = GLA_HEADS * GLA_DK
GLA_VWIDTH = GLA_HEADS * GLA_DV
GLA_GATE_RANK = 16
GLA_GATE_TAU = 16.0
GLA_CHUNK = 64
POOL_WINDOWS = (2, 4, 8, 16)
POOL_GROUP = 64
POOL_WIDTH = len(POOL_WINDOWS) * POOL_GROUP
N_EXPERTS = 32
TOP_K = 4
D_EXPERT = 1024
SWIGLU_ALPHA = 1.702
SWIGLU_LIMIT = 7.0
N_MOD = 6
LN_EPS = 1e-5
RMS_EPS = 1e-6

LANES = 128
VMEM_LIMIT_BYTES = 56 * 1024 * 1024

FOX_PAD = FOX_HEADS * LANES
GLA_PAD = GLA_HEADS * LANES
BIAS_LANE = FOX_HEAD_DIM
GLA_KSLOT = 64
GLA_KPAD = GLA_HEADS * GLA_KSLOT
G_OFF_Q, G_OFF_K = 0, GLA_KPAD
G_OFF_V = 2 * GLA_KPAD
G_OFF_R = G_OFF_V + GLA_PAD
G_OFF_U = G_OFF_R + GLA_PAD
G_OFF_A = G_OFF_U + POOL_WIDTH
G_OFF_F = G_OFF_A + LANES
G_WIDTH = G_OFF_F + LANES

STAT_ROWS = 8
PRUNE_MARGIN = 105.0
NORM_SLACK = 1.01
ATTN_BLOCK = 512
MOE_BLOCK = 512
ROUTE_TILE = 256
ROW_GROUP = 8
ROUTE_SLOTS = ROUTE_TILE * TOP_K + N_EXPERTS * ROW_GROUP
NEG_BIG = -1e30


def _cparams(sem, vmem=None):
    return pltpu.CompilerParams(dimension_semantics=sem, vmem_limit_bytes=vmem or VMEM_LIMIT_BYTES)


def _log_sigmoid(x):
    return jnp.minimum(x, 0.0) - jnp.log(1.0 + jnp.exp(-jnp.abs(x)))


def _split3(x):
    hi = x.astype(BF16)
    r = x - hi.astype(F32)
    mid = r.astype(BF16)
    lo = (r - mid.astype(F32)).astype(BF16)
    return hi, mid, lo


def _dot(a, b):
    return jnp.dot(a, b, preferred_element_type=F32)


def _dot_nt(a, b):
    return lax.dot_general(a, b, (((1,), (1,)), ((), ())), preferred_element_type=F32)


def _dot_tn(a, b):
    return lax.dot_general(a, b, (((0,), (0,)), ((), ())), preferred_element_type=F32)


def _ada_kernel(c_ref, w_ref, b_ref, o_ref):
    c = c_ref[...]
    cond = c * jax.nn.sigmoid(c)
    o_ref[0] = jnp.dot(cond, w_ref[0], preferred_element_type=F32,
                       precision=lax.Precision.HIGHEST) + b_ref[0]


def _ada_mod(c, w_ada, b_ada):
    L, D, N = w_ada.shape
    B = c.shape[0]
    rows = 8
    c_pad = jnp.zeros((rows, D), F32).at[:B].set(c)
    tn = 1536
    out = pl.pallas_call(
        _ada_kernel,
        grid=(L, N // tn),
        in_specs=[pl.BlockSpec((rows, D), lambda l, j: (0, 0)),
                  pl.BlockSpec((1, D, tn), lambda l, j: (l, 0, j)),
                  pl.BlockSpec((1, 1, tn), lambda l, j: (l, 0, j))],
        out_specs=pl.BlockSpec((1, rows, tn), lambda l, j: (l, 0, j)),
        out_shape=jax.ShapeDtypeStruct((L, rows, N), F32),
        compiler_params=_cparams(("arbitrary", "arbitrary")),
        name="ada_mod",
    )(c_pad, w_ada, b_ada.reshape(L, 1, N))
    return out[:, :B].reshape(L, B, N_MOD, 1, D)


def _inproj_kernel(x_ref, sc_ref, sh_ref, wq_ref, wk_ref, wv_ref, wg_ref,
                   bq_ref, bk_ref, bv_ref, bg_ref, q_ref, k_ref, v_ref, g_ref):
    h = (x_ref[...] * (1.0 + sc_ref[0]) + sh_ref[0]).astype(BF16)
    q_ref[...] = (_dot(h, wq_ref[...]) + bq_ref[...]).astype(BF16)
    k_ref[...] = (_dot(h, wk_ref[...]) + bk_ref[...]).astype(BF16)
    v_ref[...] = (_dot(h, wv_ref[...]) + bv_ref[...]).astype(BF16)
    g_ref[...] = _dot(h, wg_ref[...]) + bg_ref[...]


def _pad_heads(w, heads, dim, slot=LANES):
    lead = w.shape[:-1]
    w = w.reshape(lead + (heads, dim))
    w = jnp.pad(w, [(0, 0)] * len(lead) + [(0, 0), (0, slot - dim)])
    return w.reshape(lead + (heads * slot,))


def _pad_last(a, width):
    return jnp.pad(a, [(0, 0)] * (a.ndim - 1) + [(0, width - a.shape[-1])])


def _excl_prefix(a, axis):
    n = a.shape[axis]
    idx = jnp.arange(n)
    earlier = idx[None, :] < idx[:, None]
    am = jnp.moveaxis(a, axis, -1)
    out = jnp.sum(jnp.where(earlier, am[..., None, :], 0), axis=-1)
    return jnp.moveaxis(out, -1, axis)


def _prep_inproj(w_in, b_in):
    def relayout(W):
        o = 0
        fq = W[..., o:o + FOX_WIDTH]; o += FOX_WIDTH
        fk = W[..., o:o + FOX_WIDTH]; o += FOX_WIDTH
        fv = W[..., o:o + FOX_WIDTH]; o += FOX_WIDTH
        ff = W[..., o:o + FOX_HEADS]; o += FOX_HEADS
        gq = W[..., o:o + GLA_KWIDTH]; o += GLA_KWIDTH
        gk = W[..., o:o + GLA_KWIDTH]; o += GLA_KWIDTH
        gv = W[..., o:o + GLA_VWIDTH]; o += GLA_VWIDTH
        gr = W[..., o:o + GLA_VWIDTH]; o += GLA_VWIDTH
        ga = W[..., o:o + GLA_GATE_RANK]; o += GLA_GATE_RANK
        pu = W[..., o:o + POOL_WIDTH]
        wq = fq * (FOX_HEAD_DIM ** -0.5)
        wg = jnp.concatenate([
            _pad_heads(gq, GLA_HEADS, GLA_DK, GLA_KSLOT), _pad_heads(gk, GLA_HEADS, GLA_DK, GLA_KSLOT),
            _pad_heads(gv, GLA_HEADS, GLA_DV), _pad_heads(gr, GLA_HEADS, GLA_DV),
            pu, _pad_last(ga, LANES), _pad_last(ff, LANES)], axis=-1)
        return wq, fk, fv, wg

    ws = relayout(w_in.astype(BF16))
    bs = relayout(b_in[:, None, :])
    return list(zip(ws, bs))


def _inproj(x2d, scale, shift, prep, B, S, tm):
    T, D = x2d.shape
    (wq, bq), (wk, bk), (wv, bv), (wg, bg) = prep
    nt = S // tm
    full = lambda a: pl.BlockSpec(a.shape, lambda i: (0,) * a.ndim)
    row = lambda w: pl.BlockSpec((tm, w), lambda i: (i, 0))
    mod = pl.BlockSpec((1, 1, D), lambda i: (i // nt, 0, 0))
    return pl.pallas_call(
        _inproj_kernel,
        grid=(T // tm,),
        in_specs=[row(D), mod, mod, full(wq), full(wk), full(wv), full(wg),
                  full(bq), full(bk), full(bv), full(bg)],
        out_specs=[row(FOX_WIDTH), row(FOX_WIDTH), row(FOX_WIDTH), row(G_WIDTH)],
        out_shape=[jax.ShapeDtypeStruct((T, FOX_WIDTH), BF16)] * 3
                  + [jax.ShapeDtypeStruct((T, G_WIDTH), F32)],
        compiler_params=_cparams(("arbitrary",)),
        name="inproj",
    )(x2d, scale, shift, wq, wk, wv, wg, bq, bk, bv, bg)


def _aug_constants():
    pq = np.zeros((3 * LANES, FOX_PAD), np.float32)
    pk = np.zeros((3 * LANES, FOX_PAD), np.float32)
    cq = np.zeros((1, FOX_PAD), np.float32)
    ck = np.zeros((1, FOX_PAD), np.float32)
    cv = np.zeros((1, FOX_PAD), np.float32)
    spread = np.zeros((FOX_WIDTH, FOX_PAD), np.float32)
    for h in range(FOX_HEADS):
        base = h * LANES + BIAS_LANE
        cv[0, base] = 1.0
        for d in range(FOX_HEAD_DIM):
            spread[h * FOX_HEAD_DIM + d, h * LANES + d] = 1.0
        for p in range(3):
            pq[p * LANES + h, base + p] = 1.0
            pk[p * LANES + h, base + 3 + p] = -1.0
            cq[0, base + 3 + p] = 1.0
            ck[0, base + p] = 1.0
    return pq, pk, cq, ck, cv, spread


def _fgate_kernel(q_ref, k_ref, v_ref, f_ref, pq_ref, pk_ref, cq_ref, ck_ref, cv_ref, sp_ref,
                  u_ref, wpool_ref, spool_ref,
                  q2_ref, k2_ref, vt_ref, st_ref, opool_ref, carry, xx):
    @pl.when(pl.program_id(1) == 0)
    def _():
        carry[...] = jnp.zeros_like(carry)

    _pool_kernel(u_ref, wpool_ref, spool_ref, opool_ref, xx)

    vf = _dot(v_ref[...], sp_ref[...]) + cv_ref[...]
    for h in range(FOX_HEADS):
        sl = slice(h * LANES, (h + 1) * LANES)
        vt_ref[0, 0, sl, :] = vf[:, sl].T.astype(BF16)

    tf = f_ref.shape[0]
    ls = _log_sigmoid(f_ref[...])
    r = lax.broadcasted_iota(jnp.int32, (tf, tf), 0)
    c = lax.broadcasted_iota(jnp.int32, (tf, tf), 1)
    tri = (c <= r).astype(BF16)
    hi, mid, lo = _split3(ls)
    cs = _dot(tri, hi) + _dot(tri, mid) + _dot(tri, lo)
    F = cs + carry[...]
    carry[...] = F[tf - 1:tf, :]
    fh, fm, fl = _split3(F)
    f3 = jnp.concatenate([fh, fm, fl], axis=1)
    qf = _dot(q_ref[...], sp_ref[...])
    kf = _dot(k_ref[...], sp_ref[...])
    q2_ref[...] = (qf + _dot(f3, pq_ref[...]) + cq_ref[...]).astype(BF16)
    k2_ref[...] = (kf + _dot(f3, pk_ref[...]) + ck_ref[...]).astype(BF16)
    lane = lax.broadcasted_iota(jnp.int32, (1, LANES), 1)
    qstat = jnp.zeros((1, LANES), F32)
    kstat = jnp.zeros((1, LANES), F32)
    for h in range(FOX_HEADS):
        sl = slice(h * LANES, (h + 1) * LANES)
        qm = jnp.max(jnp.sum(qf[:, sl] * qf[:, sl], axis=-1, keepdims=True), axis=0, keepdims=True)
        km = jnp.max(jnp.sum(kf[:, sl] * kf[:, sl], axis=-1, keepdims=True), axis=0, keepdims=True)
        qstat = jnp.where(lane == h, qm, qstat)
        kstat = jnp.where(lane == h, km, kstat)
    row = lax.broadcasted_iota(jnp.int32, (STAT_ROWS, LANES), 0)
    st_ref[...] = jnp.where(row == 0, F[0:1, :],
                            jnp.where(row == 1, F[tf - 1:tf, :],
                                      jnp.where(row == 2, qstat, jnp.where(row == 3, kstat, 0.0))))


def _fgate(zq, zk, zv, zg, w_pool, s_pool, B, S, tf):
    T = zq.shape[0]
    nt = S // tf
    pq, pk, cq, ck, cv, sp = _aug_constants()
    pq, pk, sp = jnp.asarray(pq, BF16), jnp.asarray(pk, BF16), jnp.asarray(sp, BF16)
    cq, ck, cv = jnp.asarray(cq), jnp.asarray(ck), jnp.asarray(cv)
    full = lambda a: pl.BlockSpec(a.shape, lambda b, i: (0,) * a.ndim)
    row = pl.BlockSpec((tf, FOX_PAD), lambda b, i: (b * nt + i, 0))
    packed = pl.BlockSpec((tf, FOX_WIDTH), lambda b, i: (b * nt + i, 0))
    return pl.pallas_call(
        _fgate_kernel,
        grid=(B, nt),
        in_specs=[packed, packed, packed,
                  pl.BlockSpec((tf, LANES), lambda b, i: (b * nt + i, G_OFF_F // LANES)),
                  full(pq), full(pk), full(cq), full(ck), full(cv), full(sp),
                  pl.BlockSpec((tf, POOL_WIDTH), lambda b, i: (b * nt + i, G_OFF_U // POOL_WIDTH)),
                  full(w_pool), full(s_pool)],
        out_specs=[row, row, pl.BlockSpec((1, 1, FOX_PAD, tf), lambda b, i: (b, i, 0, 0)),
                   pl.BlockSpec((STAT_ROWS, LANES), lambda b, i: (b * nt + i, 0)),
                   pl.BlockSpec((tf, POOL_WIDTH), lambda b, i: (b * nt + i, 0))],
        out_shape=[jax.ShapeDtypeStruct((T, FOX_PAD), BF16)] * 2
                  + [jax.ShapeDtypeStruct((B, nt, FOX_PAD, tf), BF16),
                     jax.ShapeDtypeStruct((B * nt * STAT_ROWS, LANES), F32),
                     jax.ShapeDtypeStruct((T, POOL_WIDTH), BF16)],
        scratch_shapes=[pltpu.VMEM((1, LANES), F32), pltpu.VMEM((tf + HALO, POOL_WIDTH), F32)],
        compiler_params=_cparams(("arbitrary", "arbitrary")),
        name="fgate_pool",
    )(zq, zk, zv, zg, pq, pk, cq, ck, cv, sp, zg, w_pool, s_pool)


def _attn_kernel(ff_ref, fl_ref, qn_ref, kn_ref, q_ref, k_ref, vt_ref, o_ref, m_sc, acc_sc, s_sc,
                 *, blk, nq, nb):
    h = pl.program_id(0)
    i = pl.program_id(1)

    def scores(slot, j):
        off = pl.multiple_of(j * blk, blk)
        for b in range(nb):
            s_sc[slot, b] = _dot_nt(k_ref[b, pl.ds(off, blk), :], q_ref[b])

    def softmax_pv(slot, j, diag):
        for b in range(nb):
            s = s_sc[slot, b]
            if diag:
                r = lax.broadcasted_iota(jnp.int32, (blk, blk), 0)
                c = lax.broadcasted_iota(jnp.int32, (blk, blk), 1)
                s = jnp.where(r <= c, s, NEG_BIG)
            m_prev = m_sc[b]
            m_new = jnp.maximum(m_prev, jnp.max(s, axis=0, keepdims=True))
            alpha = jnp.exp(m_prev - m_new)
            p = jnp.exp(s - m_new)
            acc_sc[b] = alpha * acc_sc[b] + _dot(vt_ref[b, j], p.astype(BF16))
            m_sc[b] = m_new

    m_sc[...] = jnp.full_like(m_sc, NEG_BIG)
    acc_sc[...] = jnp.zeros_like(acc_sc)
    scores(0, i)
    scores(1, jnp.maximum(i - 1, 0))
    softmax_pv(0, i, True)

    n = jnp.int32(0)
    for b in range(nb):
        base = (b * FOX_HEADS + h) * nq
        slack = (qn_ref[base + i] * kn_ref[b * FOX_HEADS + h] + ff_ref[base + i]
                 - jnp.min(m_sc[b]) + PRUNE_MARGIN)

        def cond(t, base=base, slack=slack):
            return jnp.logical_and(t < i, slack - fl_ref[base + jnp.maximum(i - 1 - t, 0)] >= 0.0)

        n = jnp.maximum(n, lax.while_loop(cond, lambda t: t + 1, jnp.int32(0)))

    def pair(u, carry):
        t = 1 + 2 * u
        ja = i - t
        scores(0, jnp.maximum(ja - 1, 0))
        softmax_pv(1, ja, False)

        @pl.when(t + 1 <= n)
        def _():
            scores(1, jnp.maximum(ja - 2, 0))
            softmax_pv(0, ja - 1, False)

        return carry

    lax.fori_loop(0, lax.shift_right_logical(n + 1, 1), pair, 0)
    for b in range(nb):
        acc = acc_sc[b]
        o_ref[b] = (acc / acc[BIAS_LANE:BIAS_LANE + 1, :]).T.astype(o_ref.dtype)


def _attention(q2, k2, vt, stats, B, S, blk):
    T = q2.shape[0]
    nq = S // blk
    H = FOX_HEADS
    st = stats.reshape(B, nq, STAT_ROWS, LANES)[:, :, :, :H]
    tab = lambda r: jnp.transpose(st[:, :, r, :], (0, 2, 1)).reshape(-1)
    ffirst, flast = tab(0), tab(1)
    qn = jnp.sqrt(tab(2)) * NORM_SLACK
    kn = jnp.sqrt(jnp.max(st[:, :, 3, :], axis=1)).reshape(-1) * NORM_SLACK
    r3 = lambda a: a.reshape(B, S, FOX_PAD)
    qspec = pl.BlockSpec((B, blk, LANES), lambda h, i, *_: (0, i, h))
    kspec = pl.BlockSpec((B, S, LANES), lambda h, i, *_: (0, 0, h))
    vtspec = pl.BlockSpec((B, nq, LANES, blk), lambda h, i, *_: (0, 0, h, 0))
    grid_spec = pltpu.PrefetchScalarGridSpec(
        num_scalar_prefetch=4,
        grid=(H, nq),
        in_specs=[qspec, kspec, vtspec],
        out_specs=qspec,
        scratch_shapes=[pltpu.VMEM((B, 1, blk), F32), pltpu.VMEM((B, LANES, blk), F32),
                        pltpu.VMEM((2, B, blk, blk), F32)],
    )
    out = pl.pallas_call(
        functools.partial(_attn_kernel, blk=blk, nq=nq, nb=B),
        grid_spec=grid_spec,
        out_shape=jax.ShapeDtypeStruct((B, S, FOX_PAD), BF16),
        compiler_params=_cparams(("arbitrary", "arbitrary")),
        name="fox_attention",
    )(ffirst, flast, qn, kn, r3(q2), r3(k2), vt)
    return out.reshape(T, FOX_PAD)


def _gla_kernel(q_ref, k_ref, v_ref, r_ref, a_ref, wa_ref, ba_ref, g_ref, o_ref, st_sc, *, tg, nb):
    @pl.when(pl.program_id(0) == 0)
    def _():
        st_sc[...] = jnp.zeros_like(st_sc)

    C = GLA_CHUNK
    nchunk = tg // C
    r = lax.broadcasted_iota(jnp.int32, (tg, tg), 0)
    c = lax.broadcasted_iota(jnp.int32, (tg, tg), 1)
    shift = C.bit_length() - 1
    tri = ((c <= r) & ((c >> shift) == (r >> shift))).astype(BF16)
    rc = lax.broadcasted_iota(jnp.int32, (C, C), 0)
    cc = lax.broadcasted_iota(jnp.int32, (C, C), 1)
    causal = cc <= rc
    lane = lax.broadcasted_iota(jnp.int32, (1, LANES), 1)
    vmask = (lane < GLA_DV).astype(F32)
    for bi in range(nb):
        la = (_log_sigmoid(_dot(a_ref[bi].astype(BF16), wa_ref[...]) + ba_ref[...])
              * (1.0 / GLA_GATE_TAU))
        hi, mid, lo = _split3(la)
        b = _dot(tri, hi) + _dot(tri, mid) + _dot(tri, lo)
        eb = jnp.exp(b)
        q_in = q_ref[bi] * (GLA_DK ** -0.5) * eb
        k_in = k_ref[bi] * jnp.exp(-b)
        v = v_ref[bi]
        outs = []
        for ci in range(nchunk):
            rows = slice(ci * C, (ci + 1) * C)
            b_last = b[ci * C + C - 1:ci * C + C, :]
            k_out = k_ref[bi, rows, :] * jnp.exp(b_last - b[rows, :])
            dec = jnp.exp(b_last)
            heads = []
            for h in range(GLA_HEADS):
                kl = slice(h * GLA_KSLOT, (h + 1) * GLA_KSLOT)
                ln = slice(h * LANES, (h + 1) * LANES)
                qh = q_in[rows, kl].astype(BF16)
                kh = k_in[rows, kl].astype(BF16)
                vh = v[rows, ln].astype(BF16)
                attn = jnp.where(causal, _dot_nt(qh, kh), 0.0)
                st = st_sc[bi * GLA_HEADS + h]
                o = _dot(attn.astype(BF16), vh) + _dot_nt(qh, st.astype(BF16))
                kv_t = _dot_tn(vh, k_out[:, kl].astype(BF16))
                st_sc[bi * GLA_HEADS + h] = st * dec[:, kl] + kv_t
                ms = jnp.sum(o * o, axis=-1, keepdims=True) * (1.0 / GLA_DV)
                heads.append(o * lax.rsqrt(ms + RMS_EPS) * vmask)
            outs.append(jnp.concatenate(heads, axis=1))
        o_all = jnp.concatenate(outs, axis=0)
        gr = r_ref[bi]
        o_ref[bi] = ((o_all * g_ref[...]) * (gr * jax.nn.sigmoid(gr))).astype(o_ref.dtype)


def _gla(zg, wa, ba, gn, B, S, tg):
    T = zg.shape[0]
    nt = S // tg
    zg3 = zg.reshape(B, S, G_WIDTH)
    col = lambda off, w: pl.BlockSpec((B, tg, w), lambda i: (0, i, off // w))
    full = lambda a: pl.BlockSpec(a.shape, lambda i: (0,) * a.ndim)
    out = pl.pallas_call(
        functools.partial(_gla_kernel, tg=tg, nb=B),
        grid=(nt,),
        in_specs=[col(G_OFF_Q, GLA_KPAD), col(G_OFF_K, GLA_KPAD), col(G_OFF_V, GLA_PAD),
                  col(G_OFF_R, GLA_PAD), col(G_OFF_A, LANES), full(wa), full(ba), full(gn)],
        out_specs=pl.BlockSpec((B, tg, GLA_PAD), lambda i: (0, i, 0)),
        out_shape=jax.ShapeDtypeStruct((B, S, GLA_PAD), BF16),
        scratch_shapes=[pltpu.VMEM((B * GLA_HEADS, LANES, GLA_KSLOT), F32)],
        compiler_params=_cparams(("arbitrary",)),
        name="gla",
    )(zg3, zg3, zg3, zg3, zg3, wa, ba, gn)
    return out.reshape(T, GLA_PAD)


HALO = max(POOL_WINDOWS)


def _pool_kernel(u_ref, w_ref, s_ref, o_ref, xx):
    tp = u_ref.shape[0]
    i = pl.program_id(1)

    @pl.when(i == 0)
    def _():
        xx[0:HALO, :] = jnp.zeros((HALO, POOL_WIDTH), F32)

    @pl.when(i > 0)
    def _():
        xx[0:HALO, :] = xx[tp:tp + HALO, :]

    u = u_ref[...]
    xx[HALO:HALO + tp, :] = u
    lane = lax.broadcasted_iota(jnp.int32, (1, POOL_WIDTH), 1)
    grp = lane >> (POOL_GROUP.bit_length() - 1)
    pos = lax.broadcasted_iota(jnp.int32, (tp, 1), 0) + i * tp + 1
    acc = u
    mean = jnp.zeros_like(u)
    for j in range(1, HALO):
        acc = acc + xx[HALO - j:HALO - j + tp, :]
        w = j + 1
        if w in POOL_WINDOWS:
            g = POOL_WINDOWS.index(w)
            inv_cnt = 1.0 / jnp.minimum(pos, w).astype(F32)
            mean = jnp.where(grp == g, acc * inv_cnt, mean)
    pooled = mean - u
    mixed = _dot(pooled.astype(BF16), w_ref[...])
    o_ref[...] = (mixed * s_ref[...]).astype(o_ref.dtype)


def _layer_norm(r, g, b):
    mu = jnp.mean(r, axis=-1, keepdims=True)
    d = r - mu
    var = jnp.mean(d * d, axis=-1, keepdims=True)
    return d * lax.rsqrt(var + LN_EPS) * g + b


def _outproj_kernel(of_ref, og_ref, op_ref, x_ref, wf_ref, wg_ref, wp_ref, gate_ref, lg_ref, lb_ref,
                    sc_ref, sh_ref, wr_ref, br_ref,
                    x1_ref, h2_ref, e_ref, gt_ref, cnt_ref, *, alpha):
    y = _dot(of_ref[...], wf_ref[...]) + _dot(og_ref[...], wg_ref[...]) + _dot(op_ref[...], wp_ref[...])
    r = alpha * x_ref[...] + (1.0 + gate_ref[0]) * y
    x1 = _layer_norm(r, lg_ref[...], lb_ref[...])
    x1_ref[...] = x1
    h2 = x1 * (1.0 + sc_ref[0]) + sh_ref[0]
    h2_ref[...] = h2
    h_hi = h2.astype(BF16)
    h_lo = (h2 - h_hi.astype(F32)).astype(BF16)
    logits = (_dot(h_hi, wr_ref[0]) + (_dot(h_hi, wr_ref[1]) + _dot(h_lo, wr_ref[0]))
              + br_ref[...])
    tm = logits.shape[0]
    lane_i = lax.broadcasted_iota(jnp.int32, (tm, LANES), 1)
    lane = lane_i.astype(F32)
    work = logits
    tops, idxs = [], []
    onehot = jnp.zeros((tm, LANES), F32)
    for _ in range(TOP_K):
        m = jnp.max(work, axis=-1, keepdims=True)
        idx = jnp.min(jnp.where(work == m, lane, float(LANES)), axis=-1, keepdims=True)
        sel = lane == idx
        onehot = onehot + sel.astype(F32)
        work = jnp.where(sel, -jnp.inf, work)
        tops.append(m)
        idxs.append(idx)
    ex = [jnp.exp(t - tops[0]) for t in tops]
    den = ex[0] + ex[1] + ex[2] + ex[3]
    e_out = jnp.zeros((tm, LANES), jnp.int32)
    g_out = jnp.zeros((tm, LANES), F32)
    for k in range(TOP_K):
        e_out = jnp.where(lane_i == k, idxs[k].astype(jnp.int32), e_out)
        g_out = jnp.where(lane_i == k, ex[k] / den, g_out)
    e_ref[...] = e_out
    gt_ref[...] = g_out
    for u in range(tm // ROUTE_TILE):
        rows = slice(u * ROUTE_TILE, (u + 1) * ROUTE_TILE)
        cnt_ref[u] = jnp.sum(onehot[rows], axis=0, keepdims=True).astype(jnp.int32)


def _outproj(o_fox, o_gla, o_pool, x2d, wf, wg, wp, gate1, ln_g, ln_b, scale2, shift2, wr, br,
             B, S, tm, alpha):
    T, D = x2d.shape
    nt = S // tm
    full = lambda a: pl.BlockSpec(a.shape, lambda i: (0,) * a.ndim)
    row = lambda w: pl.BlockSpec((tm, w), lambda i: (i, 0))
    mod = pl.BlockSpec((1, 1, D), lambda i: (i // nt, 0, 0))
    return pl.pallas_call(
        functools.partial(_outproj_kernel, alpha=alpha),
        grid=(T // tm,),
        in_specs=[row(FOX_PAD), row(GLA_PAD), row(POOL_WIDTH), row(D), full(wf), full(wg), full(wp),
                  mod, full(ln_g), full(ln_b), mod, mod, full(wr), full(br)],
        out_specs=[row(D), row(D), row(LANES), row(LANES),
                   pl.BlockSpec((tm // ROUTE_TILE, 1, LANES), lambda i: (i, 0, 0))],
        out_shape=[jax.ShapeDtypeStruct((T, D), F32), jax.ShapeDtypeStruct((T, D), F32),
                   jax.ShapeDtypeStruct((T, LANES), jnp.int32), jax.ShapeDtypeStruct((T, LANES), F32),
                   jax.ShapeDtypeStruct((T // ROUTE_TILE, 1, LANES), jnp.int32)],
        compiler_params=_cparams(("arbitrary",)),
        name="outproj_router",
    )(o_fox, o_gla, o_pool, x2d, wf, wg, wp, gate1, ln_g, ln_b, scale2, shift2, wr, br)


def _tile_slots(e_i32, off_row):
    td = e_i32.shape[0]
    lane = lax.broadcasted_iota(jnp.int32, (td, LANES), 1).astype(F32)
    ef = e_i32.astype(F32)
    sel = [lane == ef[:, k:k + 1] for k in range(TOP_K)]
    onehot = sel[0].astype(F32)
    for k in range(1, TOP_K):
        onehot = onehot + sel[k].astype(F32)
    rr = lax.broadcasted_iota(jnp.int32, (td, td), 0)
    cc = lax.broadcasted_iota(jnp.int32, (td, td), 1)
    stril = (cc < rr).astype(BF16)
    tab = _dot(stril, onehot.astype(BF16)) + off_row
    return [jnp.sum(jnp.where(sel[k], tab, 0.0), axis=-1, keepdims=True) for k in range(TOP_K)]


def _group_bits(max_groups):
    return [1 << s for s in range(max_groups.bit_length() - 1, -1, -1)]


PIECE_CLASSES = (ROUTE_TILE // ROW_GROUP).bit_length()
PIECE_LIST = 1024
SLOT_BITS = 8
assert ROUTE_SLOTS // ROW_GROUP < (1 << SLOT_BITS) and PIECE_CLASSES * N_EXPERTS <= PIECE_LIST


def _piece_lists(m, off, dst):
    nt = m.shape[0]
    j = jnp.arange(N_EXPERTS, dtype=jnp.int32)
    b = jnp.arange(PIECE_CLASSES, dtype=jnp.int32)[None, :, None]
    mm, oo, dd = m[:, None, :], off[:, None, :], dst[:, None, :]
    has = (mm >> b) & 1
    before = (mm & ~((2 << b) - 1)) * ROW_GROUP
    packed = (((dd + before) // ROW_GROUP) << SLOT_BITS) | ((oo + before) // ROW_GROUP)
    rank = _excl_prefix(has, 2)
    hit = (has[..., None] == 1) & (rank[..., None] == j)
    lst = jnp.sum(jnp.where(hit, packed[..., None], 0), axis=2)
    lst = _pad_last(lst.reshape(nt, PIECE_CLASSES * N_EXPERTS).astype(jnp.int32), PIECE_LIST)
    return lst.reshape(-1), jnp.sum(has, axis=2).astype(jnp.int32).reshape(-1)


def _for_each_piece(tile, cnt_ref, list_ref, fn):
    for b in range(PIECE_CLASSES):
        def body(i, carry, b=b):
            v = list_ref[b * N_EXPERTS + i]
            fn(pl.multiple_of((v & ((1 << SLOT_BITS) - 1)) * ROW_GROUP, ROW_GROUP),
               pl.multiple_of(lax.shift_right_logical(v, SLOT_BITS) * ROW_GROUP, ROW_GROUP),
               (1 << b) * ROW_GROUP)
            return carry

        lax.fori_loop(0, cnt_ref[tile * PIECE_CLASSES + b], body, 0)


def _for_each_total_piece(total_groups, fn):
    for bit in _group_bits(ROUTE_SLOTS // ROW_GROUP):
        @pl.when((total_groups & bit) != 0)
        def _(bit=bit):
            fn(bit * ROW_GROUP)


def _dispatch_kernel(cnt_ref, tot_ref, zblk_ref, nu_ref, list_ref, e_ref, offrow_ref, h_ref,
                     xr_ref, slot_ref, sort_sc, zero_sc, sems, zsem, *, n_blocks, nt):
    step = pl.program_id(0)

    @pl.when(step == 0)
    def _():
        zero_sc[...] = jnp.zeros_like(zero_sc)

        def zcopy(blk):
            return pltpu.make_async_copy(
                zero_sc, xr_ref.at[pl.ds(pl.multiple_of(blk * MOE_BLOCK, MOE_BLOCK), MOE_BLOCK), :], zsem)

        def zstart(e, c):
            @pl.when(zblk_ref[e] >= 0)
            def _():
                zcopy(zblk_ref[e]).start()
            return c

        def zwait(e, c):
            @pl.when(zblk_ref[e] >= 0)
            def _():
                zcopy(0).wait()
            return c

        def tstart(blk, c):
            zcopy(blk).start()
            return c

        def twait(blk, c):
            zcopy(0).wait()
            return c

        lax.fori_loop(0, N_EXPERTS, zstart, 0)
        lax.fori_loop(nu_ref[0], n_blocks, tstart, 0)
        lax.fori_loop(0, N_EXPERTS, zwait, 0)
        lax.fori_loop(nu_ref[0], n_blocks, twait, 0)

    td = h_ref.shape[0]
    slots = _tile_slots(e_ref[...], offrow_ref[0])
    lane = lax.broadcasted_iota(jnp.int32, (td, LANES), 1)
    cols = jnp.full((td, LANES), -1.0, F32)
    for k in range(TOP_K):
        cols = jnp.where(lane == k, slots[k], cols)
    slot_ref[...] = cols
    rows_t = cols.T
    sub = lax.broadcasted_iota(jnp.int32, (ROUTE_SLOTS, td), 0).astype(F32)
    pick = sub == rows_t[0:1, :]
    for k in range(1, TOP_K):
        pick = pick | (sub == rows_t[k:k + 1, :])
    buf = step & 1
    sort_sc[buf] = _dot(pick.astype(BF16), h_ref[...].astype(BF16))

    def start_chunk(slot0, row0, n):
        pltpu.make_async_copy(sort_sc.at[buf, pl.ds(slot0, n), :], xr_ref.at[pl.ds(row0, n), :],
                              sems.at[buf]).start()

    _for_each_piece(step, cnt_ref, list_ref, start_chunk)

    def wait_tile(tile, b):
        def wait_piece(n):
            pltpu.make_async_copy(sort_sc.at[b, pl.ds(0, n), :], xr_ref.at[pl.ds(0, n), :],
                                  sems.at[b]).wait()
        _for_each_total_piece(tot_ref[tile], wait_piece)

    @pl.when(step > 0)
    def _():
        wait_tile(step - 1, 1 - buf)

    @pl.when(step == nt - 1)
    def _():
        wait_tile(step, buf)


def _dispatch(h2, e_pad, tables, rows):
    T, D = h2.shape
    td = ROUTE_TILE
    nt = T // td
    piece_list, piece_cnt, tot_tab, zblk, n_used, off_rows = tables
    grid_spec = pltpu.PrefetchScalarGridSpec(
        num_scalar_prefetch=4,
        grid=(nt,),
        in_specs=[pl.BlockSpec((PIECE_LIST,), lambda i, *_: (i,), memory_space=pltpu.SMEM),
                  pl.BlockSpec((td, LANES), lambda i, *_: (i, 0)),
                  pl.BlockSpec((1, 1, LANES), lambda i, *_: (i, 0, 0)),
                  pl.BlockSpec((td, D), lambda i, *_: (i, 0))],
        out_specs=[pl.BlockSpec(memory_space=pl.ANY),
                   pl.BlockSpec((td, LANES), lambda i, *_: (i, 0))],
        scratch_shapes=[pltpu.VMEM((2, ROUTE_SLOTS, D), F32), pltpu.VMEM((MOE_BLOCK, D), F32),
                        pltpu.SemaphoreType.DMA((2,)), pltpu.SemaphoreType.DMA],
    )
    return pl.pallas_call(
        functools.partial(_dispatch_kernel, n_blocks=rows // MOE_BLOCK, nt=nt),
        grid_spec=grid_spec,
        out_shape=[jax.ShapeDtypeStruct((rows, D), F32), jax.ShapeDtypeStruct((T, LANES), F32)],
        compiler_params=_cparams(("arbitrary",)),
        name="moe_dispatch",
    )(piece_cnt, tot_tab, zblk, n_used, piece_list, e_pad, off_rows, h2)


def _expert_kernel(be_ref, nu_ref, vb_ref, par_ref, nxt_ref, x_ref, wgu_hbm, bgu_ref, wd_hbm, bd_ref, y_ref,
                   wgu_sc, wd_sc, gu_buf, d_buf, sems, *, first_expert):
    i = pl.program_id(0)
    used = i < nu_ref[0]

    def fetch(e, slot):
        return (pltpu.make_async_copy(wgu_hbm.at[first_expert + e], gu_buf.at[slot], sems.at[slot, 0]),
                pltpu.make_async_copy(wd_hbm.at[first_expert + e], d_buf.at[slot], sems.at[slot, 1]))

    @pl.when(used)
    def _():
        prev = be_ref[jnp.maximum(i - 1, 0)]

        @pl.when((i == 0) | (be_ref[i] != prev))
        def _():
            slot = par_ref[i]

            @pl.when(i == 0)
            def _():
                for cp in fetch(be_ref[i], slot):
                    cp.start()

            for cp in fetch(be_ref[i], slot):
                cp.wait()
            wgu_sc[...] = gu_buf[slot].astype(BF16)
            wd_sc[...] = d_buf[slot].astype(BF16)

            @pl.when(nxt_ref[i] >= 0)
            def _():
                for cp in fetch(nxt_ref[i], 1 - slot):
                    cp.start()

        def mlp(rows):
            x = x_ref[rows, :].astype(BF16)
            gu = _dot(x, wgu_sc[...]) + bgu_ref[0]
            glu = jnp.minimum(gu[:, :D_EXPERT], SWIGLU_LIMIT)
            lin = jnp.clip(gu[:, D_EXPERT:], -SWIGLU_LIMIT, SWIGLU_LIMIT)
            act = glu * jax.nn.sigmoid(SWIGLU_ALPHA * glu) * (lin + 1.0)
            y_ref[rows, :] = _dot(act.astype(BF16), wd_sc[...]) + bd_ref[0]

        half = MOE_BLOCK // 2
        full_block = vb_ref[i] > half

        @pl.when(full_block)
        def _():
            mlp(slice(0, MOE_BLOCK))

        @pl.when(jnp.logical_not(full_block))
        def _():
            mlp(slice(0, half))
            y_ref[half:, :] = jnp.zeros((MOE_BLOCK - half, y_ref.shape[1]), y_ref.dtype)

    @pl.when(jnp.logical_not(used))
    def _():
        y_ref[...] = jnp.zeros_like(y_ref)


def _experts(x_rows, block_expert, n_used, valid_rows, slot_parity, next_expert,
             w_gate_up, b_gate_up, w_down, b_down, layer):
    rows, D = x_rows.shape
    nb = rows // MOE_BLOCK
    E = w_gate_up.shape[1]
    grid_spec = pltpu.PrefetchScalarGridSpec(
        num_scalar_prefetch=5,
        grid=(nb,),
        in_specs=[pl.BlockSpec((MOE_BLOCK, D), lambda i, be, nu, *_: (jnp.minimum(i, nu[0] - 1), 0)),
                  pl.BlockSpec(memory_space=pl.ANY),
                  pl.BlockSpec((1, 1, 2 * D_EXPERT), lambda i, be, *_: (layer * E + be[i], 0, 0)),
                  pl.BlockSpec(memory_space=pl.ANY),
                  pl.BlockSpec((1, 1, D), lambda i, be, *_: (layer * E + be[i], 0, 0))],
        out_specs=pl.BlockSpec((MOE_BLOCK, D), lambda i, *_: (i, 0)),
        scratch_shapes=[pltpu.VMEM((D, 2 * D_EXPERT), BF16), pltpu.VMEM((D_EXPERT, D), BF16),
                        pltpu.VMEM((2, D, 2 * D_EXPERT), F32), pltpu.VMEM((2, D_EXPERT, D), F32),
                        pltpu.SemaphoreType.DMA((2, 2))],
    )
    L = w_gate_up.shape[0]
    return pl.pallas_call(
        functools.partial(_expert_kernel, first_expert=layer * E),
        grid_spec=grid_spec,
        out_shape=jax.ShapeDtypeStruct((rows, D), F32),
        compiler_params=_cparams(("arbitrary",)),
        name="moe_experts",
    )(block_expert, n_used, valid_rows, slot_parity, next_expert, x_rows,
      w_gate_up.reshape(L * E, D, 2 * D_EXPERT), b_gate_up.reshape(L * E, 1, 2 * D_EXPERT),
      w_down.reshape(L * E, D_EXPERT, D), b_down.reshape(L * E, 1, D))


def _combine_kernel(cnt_ref, tot_ref, list_ref, next_list_ref, slot_ref, gt_ref, y_ref, x_ref, gate_ref,
                    lg_ref, lb_ref, o_ref, ybuf, sems, *, nt, alpha):
    step = pl.program_id(0)
    buf = step & 1

    def fetch(tile, b, lst):
        def start_chunk(slot0, row0, n):
            pltpu.make_async_copy(y_ref.at[pl.ds(row0, n), :], ybuf.at[b, pl.ds(slot0, n), :],
                                  sems.at[b]).start()
        _for_each_piece(tile, cnt_ref, lst, start_chunk)

    @pl.when(step == 0)
    def _():
        ybuf[...] = jnp.zeros_like(ybuf)
        fetch(step, buf, list_ref)

    @pl.when(step + 1 < nt)
    def _():
        fetch(step + 1, 1 - buf, next_list_ref)

    def wait_piece(n):
        pltpu.make_async_copy(y_ref.at[pl.ds(0, n), :], ybuf.at[buf, pl.ds(0, n), :], sems.at[buf]).wait()

    _for_each_total_piece(tot_ref[step], wait_piece)

    td = x_ref.shape[0]
    slots = slot_ref[...]
    gt = gt_ref[...]
    lane = lax.broadcasted_iota(jnp.int32, (td, ROUTE_SLOTS), 1).astype(F32)
    w = jnp.where(lane == slots[:, 0:1], gt[:, 0:1], 0.0)
    for k in range(1, TOP_K):
        w = w + jnp.where(lane == slots[:, k:k + 1], gt[:, k:k + 1], 0.0)
    y = _dot(w.astype(BF16), ybuf[buf].astype(BF16))
    r = alpha * x_ref[...] + (1.0 + gate_ref[0]) * y
    o_ref[...] = _layer_norm(r, lg_ref[...], lb_ref[...])


def _combine(y_rows, slots, tables, gates, x1, gate2, ln_g, ln_b, B, S, alpha):
    T, D = x1.shape
    td = ROUTE_TILE
    nt = T // td
    per_batch = S // td
    piece_list, piece_cnt, tot_tab, _, _, _ = tables
    grid_spec = pltpu.PrefetchScalarGridSpec(
        num_scalar_prefetch=2,
        grid=(nt,),
        in_specs=[pl.BlockSpec((PIECE_LIST,), lambda i, *_: (i,), memory_space=pltpu.SMEM),
                  pl.BlockSpec((PIECE_LIST,), lambda i, *_: (jnp.minimum(i + 1, nt - 1),),
                               memory_space=pltpu.SMEM),
                  pl.BlockSpec((td, LANES), lambda i, *_: (i, 0)),
                  pl.BlockSpec((td, LANES), lambda i, *_: (i, 0)),
                  pl.BlockSpec(memory_space=pl.ANY),
                  pl.BlockSpec((td, D), lambda i, *_: (i, 0)),
                  pl.BlockSpec((1, 1, D), lambda i, *_: (i // per_batch, 0, 0)),
                  pl.BlockSpec((1, D), lambda i, *_: (0, 0)),
                  pl.BlockSpec((1, D), lambda i, *_: (0, 0))],
        out_specs=pl.BlockSpec((td, D), lambda i, *_: (i, 0)),
        scratch_shapes=[pltpu.VMEM((2, ROUTE_SLOTS, D), F32), pltpu.SemaphoreType.DMA((2,))],
    )
    return pl.pallas_call(
        functools.partial(_combine_kernel, nt=nt, alpha=alpha),
        grid_spec=grid_spec,
        out_shape=jax.ShapeDtypeStruct((T, D), F32),
        compiler_params=_cparams(("arbitrary",)),
        name="moe_combine",
    )(piece_cnt, tot_tab, piece_list, piece_list, slots, gates, y_rows, x1, gate2, ln_g, ln_b)


def _tile(n, pref):
    t = min(n, pref)
    assert n % t == 0, (n, t)
    return t


def kernel(x, c, w_ada, b_ada, w_in, b_in, gla_w_a2, gla_b_a, gla_norm_g, pool_w, pool_scale, w_out,
           ln1_g, ln1_b, w_router, b_router, w_gate_up, b_gate_up, w_down, b_down, ln2_g, ln2_b):
    B, S, D = x.shape
    L = w_ada.shape[0]
    T = B * S
    assert D == D_MODEL and S % GLA_CHUNK == 0
    alpha = float((2 * L) ** 0.25)
    assert T % ROUTE_TILE == 0
    n_tiles = T // ROUTE_TILE
    max_rows = T * TOP_K + n_tiles * N_EXPERTS * (ROW_GROUP - 1) + N_EXPERTS * (MOE_BLOCK - 1)
    n_blocks = -(-max_rows // MOE_BLOCK)
    rows = n_blocks * MOE_BLOCK

    mod = _ada_mod(c, w_ada, b_ada)
    prep_all = _prep_inproj(w_in, b_in)
    wa_all = jnp.pad(_pad_heads(gla_w_a2, GLA_HEADS, GLA_DK, GLA_KSLOT),
                     ((0, 0), (0, LANES - GLA_GATE_RANK), (0, 0))).astype(BF16)
    ba_all = _pad_heads(gla_b_a[:, None, :], GLA_HEADS, GLA_DK, GLA_KSLOT)
    gn_all = _pad_heads(gla_norm_g[:, None, :], GLA_HEADS, GLA_DV)
    w_bd_all = jnp.zeros((L, POOL_WIDTH, POOL_WIDTH), F32)
    for g in range(len(POOL_WINDOWS)):
        sl = slice(g * POOL_GROUP, (g + 1) * POOL_GROUP)
        w_bd_all = w_bd_all.at[:, sl, sl].set(pool_w[:, g])
    w_bd_all = w_bd_all.astype(BF16)
    pad_rows = lambda w, heads, dim: jnp.swapaxes(_pad_heads(jnp.swapaxes(w, 1, 2), heads, dim), 1, 2)
    wf_all = pad_rows(w_out[:, :FOX_WIDTH], FOX_HEADS, FOX_HEAD_DIM).astype(BF16)
    wgl_all = pad_rows(w_out[:, FOX_WIDTH:FOX_WIDTH + GLA_VWIDTH], GLA_HEADS, GLA_DV).astype(BF16)
    wp_all = w_out[:, FOX_WIDTH + GLA_VWIDTH:].astype(BF16)
    wr_f32 = _pad_last(w_router, LANES)
    wr_hi = wr_f32.astype(BF16)
    wr_all = jnp.stack([wr_hi, (wr_f32 - wr_hi.astype(F32)).astype(BF16)], axis=1)
    br_all = jnp.pad(b_router[:, None, :], ((0, 0), (0, 0), (0, LANES - N_EXPERTS)), constant_values=NEG_BIG)

    x2d = x.reshape(T, D)
    for l in range(L):
        shift1, scale1, gate1, shift2, scale2, gate2 = [mod[l, :, m] for m in range(N_MOD)]
        prep = [(w[l], b[l]) for w, b in prep_all]
        zq, zk, zv, zg = _inproj(x2d, scale1, shift1, prep, B, S, _tile(S, 512))
        q2, k2, vt, stats, o_pool = _fgate(zq, zk, zv, zg, w_bd_all[l], pool_scale[l][None, :],
                                           B, S, _tile(S, ATTN_BLOCK))
        o_fox = _attention(q2, k2, vt, stats, B, S, _tile(S, ATTN_BLOCK))
        o_gla = _gla(zg, wa_all[l], ba_all[l], gn_all[l], B, S, _tile(S, 256))
        x1, h2, e_pad, g_pad, cnt = _outproj(
            o_fox, o_gla, o_pool, x2d, wf_all[l], wgl_all[l], wp_all[l], gate1,
            ln1_g[l][None, :], ln1_b[l][None, :], scale2, shift2, wr_all[l], br_all[l],
            B, S, _tile(S, 512), alpha)
        cte = cnt[:, 0, :N_EXPERTS]
        c8 = (cte + ROW_GROUP - 1) // ROW_GROUP * ROW_GROUP
        off = _excl_prefix(c8, 1)
        per_expert = jnp.sum(c8, axis=0)
        padded = (per_expert + MOE_BLOCK - 1) // MOE_BLOCK * MOE_BLOCK
        pstart = _excl_prefix(padded, 0)
        pend = pstart + padded
        dst = pstart[None, :] + _excl_prefix(c8, 0)
        n_used = (pend[-1] // MOE_BLOCK).astype(jnp.int32)
        blk_start = jnp.arange(n_blocks, dtype=jnp.int32) * MOE_BLOCK
        be = jnp.minimum(jnp.sum(blk_start[:, None] >= pend[None, :], axis=1), N_EXPERTS - 1).astype(jnp.int32)
        last_used = jnp.minimum(jnp.sum(MOE_BLOCK * (n_used - 1) >= pend), N_EXPERTS - 1).astype(jnp.int32)
        be = jnp.where(jnp.arange(n_blocks) < n_used, be, last_used)
        zblk = jnp.where(padded > 0, pend // MOE_BLOCK - 1, -1).astype(jnp.int32)
        off_rows = _pad_last(off.astype(F32), LANES)[:, None, :]
        n_used = n_used.reshape(1)
        piece_list, piece_cnt = _piece_lists(c8 // ROW_GROUP, off, dst)
        tables = (piece_list, piece_cnt, (jnp.sum(c8, axis=1) // ROW_GROUP).astype(jnp.int32),
                  zblk, n_used, off_rows)
        x_rows, slots = _dispatch(h2, e_pad, tables, rows)
        owns = padded > 0
        ids = jnp.arange(N_EXPERTS, dtype=jnp.int32)
        later = jnp.where(owns[None, :] & (ids[None, :] > ids[:, None]), ids[None, :], N_EXPERTS)
        nxt_e = jnp.min(later, axis=1)
        nxt_e = jnp.where(nxt_e < N_EXPERTS, nxt_e, -1).astype(jnp.int32)
        par_e = (_excl_prefix(owns.astype(jnp.int32), 0) & 1).astype(jnp.int32)
        pick = be[:, None] == ids[None, :]
        lookup = lambda tab: jnp.sum(jnp.where(pick, tab[None, :], 0), axis=1).astype(jnp.int32)
        valid_rows = jnp.clip(lookup(per_expert) - (blk_start - lookup(pstart)), 0, MOE_BLOCK).astype(jnp.int32)
        y_rows = _experts(x_rows, be, n_used, valid_rows, lookup(par_e), lookup(nxt_e),
                          w_gate_up, b_gate_up, w_down, b_down, l)
        x2d = _combine(y_rows, slots, tables, g_pad, x1, gate2,
                       ln2_g[l][None, :], ln2_b[l][None, :], B, S, alpha)
    return x2d.reshape(B, S, D)
```

```python
import functools

import numpy as np
import jax
import jax.numpy as jnp
from jax import lax
from jax.experimental import pallas as pl
from jax.experimental.pallas import tpu as pltpu

F32 = jnp.float32
BF16 = jnp.bfloat16

D_MODEL = 1024
FOX_HEADS = 6
FOX_HEAD_DIM = 64
FOX_WIDTH = FOX_HEADS * FOX_HEAD_DIM
GLA_HEADS = 4
GLA_DV = 96
GLA_DK = 48
GLA_KWIDTH = GLA_HEADS * GLA_DK
GLA_VWIDTH = GLA_HEADS * GLA_DV
GLA_GATE_RANK = 16
GLA_GATE_TAU = 16.0
GLA_CHUNK = 64
POOL_WINDOWS = (2, 4, 8, 16)
POOL_GROUP = 64
POOL_WIDTH = len(POOL_WINDOWS) * POOL_GROUP
N_EXPERTS = 32
TOP_K = 4
D_EXPERT = 1024
SWIGLU_ALPHA = 1.702
SWIGLU_LIMIT = 7.0
N_MOD = 6
LN_EPS = 1e-5
RMS_EPS = 1e-6

LANES = 128
VMEM_LIMIT_BYTES = 56 * 1024 * 1024

FOX_PAD = FOX_HEADS * LANES
GLA_PAD = GLA_HEADS * LANES
BIAS_LANE = FOX_HEAD_DIM
GLA_KSLOT = 64
GLA_KPAD = GLA_HEADS * GLA_KSLOT
G_OFF_Q, G_OFF_K = 0, GLA_KPAD
G_OFF_V = 2 * GLA_KPAD
G_OFF_R = G_OFF_V + GLA_PAD
G_OFF_U = G_OFF_R + GLA_PAD
G_OFF_A = G_OFF_U + POOL_WIDTH
G_OFF_F = G_OFF_A + LANES
G_WIDTH = G_OFF_F + LANES

STAT_ROWS = 8
PRUNE_MARGIN = 105.0
NORM_SLACK = 1.01
ATTN_BLOCK = 512
MOE_BLOCK = 512
ROUTE_TILE = 256
ROW_GROUP = 8
ROUTE_SLOTS = ROUTE_TILE * TOP_K + N_EXPERTS * ROW_GROUP
TILES_PER_STEP = 2
NEG_BIG = -1e30


def _cparams(sem, vmem=None):
    return pltpu.CompilerParams(dimension_semantics=sem, vmem_limit_bytes=vmem or VMEM_LIMIT_BYTES)


def _log_sigmoid(x):
    return jnp.minimum(x, 0.0) - jnp.log(1.0 + jnp.exp(-jnp.abs(x)))


def _split3(x):
    hi = x.astype(BF16)
    r = x - hi.astype(F32)
    mid = r.astype(BF16)
    lo = (r - mid.astype(F32)).astype(BF16)
    return hi, mid, lo


def _dot(a, b):
    return jnp.dot(a, b, preferred_element_type=F32)


def _dot_nt(a, b):
    return lax.dot_general(a, b, (((1,), (1,)), ((), ())), preferred_element_type=F32)


def _dot_tn(a, b):
    return lax.dot_general(a, b, (((0,), (0,)), ((), ())), preferred_element_type=F32)


def _ada_kernel(c_ref, w_ref, b_ref, o_ref):
    c = c_ref[...]
    cond = c * jax.nn.sigmoid(c)
    o_ref[0] = jnp.dot(cond, w_ref[0], preferred_element_type=F32,
                       precision=lax.Precision.HIGHEST) + b_ref[0]


def _ada_mod(c, w_ada, b_ada):
    L, D, N = w_ada.shape
    B = c.shape[0]
    rows = 8
    c_pad = jnp.zeros((rows, D), F32).at[:B].set(c)
    tn = 1536
    out = pl.pallas_call(
        _ada_kernel,
        grid=(L, N // tn),
        in_specs=[pl.BlockSpec((rows, D), lambda l, j: (0, 0)),
                  pl.BlockSpec((1, D, tn), lambda l, j: (l, 0, j)),
                  pl.BlockSpec((1, 1, tn), lambda l, j: (l, 0, j))],
        out_specs=pl.BlockSpec((1, rows, tn), lambda l, j: (l, 0, j)),
        out_shape=jax.ShapeDtypeStruct((L, rows, N), F32),
        compiler_params=_cparams(("arbitrary", "arbitrary")),
        name="ada_mod",
    )(c_pad, w_ada, b_ada.reshape(L, 1, N))
    return out[:, :B].reshape(L, B, N_MOD, 1, D)


def _inproj_kernel(x_ref, sc_ref, sh_ref, wq_ref, wk_ref, wv_ref, wg_ref,
                   bq_ref, bk_ref, bv_ref, bg_ref, q_ref, k_ref, v_ref, g_ref):
    h = (x_ref[...] * (1.0 + sc_ref[0]) + sh_ref[0]).astype(BF16)
    q_ref[...] = (_dot(h, wq_ref[...]) + bq_ref[...]).astype(BF16)
    k_ref[...] = (_dot(h, wk_ref[...]) + bk_ref[...]).astype(BF16)
    v_ref[...] = (_dot(h, wv_ref[...]) + bv_ref[...]).astype(BF16)
    g_ref[...] = _dot(h, wg_ref[...]) + bg_ref[...]


def _pad_heads(w, heads, dim, slot=LANES):
    lead = w.shape[:-1]
    w = w.reshape(lead + (heads, dim))
    w = jnp.pad(w, [(0, 0)] * len(lead) + [(0, 0), (0, slot - dim)])
    return w.reshape(lead + (heads * slot,))


def _pad_last(a, width):
    return jnp.pad(a, [(0, 0)] * (a.ndim - 1) + [(0, width - a.shape[-1])])


def _excl_prefix(a, axis):
    n = a.shape[axis]
    idx = jnp.arange(n)
    earlier = idx[None, :] < idx[:, None]
    am = jnp.moveaxis(a, axis, -1)
    out = jnp.sum(jnp.where(earlier, am[..., None, :], 0), axis=-1)
    return jnp.moveaxis(out, -1, axis)


def _prep_inproj(w_in, b_in):
    def relayout(W):
        o = 0
        fq = W[..., o:o + FOX_WIDTH]; o += FOX_WIDTH
        fk = W[..., o:o + FOX_WIDTH]; o += FOX_WIDTH
        fv = W[..., o:o + FOX_WIDTH]; o += FOX_WIDTH
        ff = W[..., o:o + FOX_HEADS]; o += FOX_HEADS
        gq = W[..., o:o + GLA_KWIDTH]; o += GLA_KWIDTH
        gk = W[..., o:o + GLA_KWIDTH]; o += GLA_KWIDTH
        gv = W[..., o:o + GLA_VWIDTH]; o += GLA_VWIDTH
        gr = W[..., o:o + GLA_VWIDTH]; o += GLA_VWIDTH
        ga = W[..., o:o + GLA_GATE_RANK]; o += GLA_GATE_RANK
        pu = W[..., o:o + POOL_WIDTH]
        wq = fq * (FOX_HEAD_DIM ** -0.5)
        wg = jnp.concatenate([
            _pad_heads(gq, GLA_HEADS, GLA_DK, GLA_KSLOT), _pad_heads(gk, GLA_HEADS, GLA_DK, GLA_KSLOT),
            _pad_heads(gv, GLA_HEADS, GLA_DV), _pad_heads(gr, GLA_HEADS, GLA_DV),
            pu, _pad_last(ga, LANES), _pad_last(ff, LANES)], axis=-1)
        return wq, fk, fv, wg

    ws = relayout(w_in.astype(BF16))
    bs = relayout(b_in[:, None, :])
    return list(zip(ws, bs))


def _inproj(x2d, scale, shift, prep, B, S, tm):
    T, D = x2d.shape
    (wq, bq), (wk, bk), (wv, bv), (wg, bg) = prep
    nt = S // tm
    full = lambda a: pl.BlockSpec(a.shape, lambda i: (0,) * a.ndim)
    row = lambda w: pl.BlockSpec((tm, w), lambda i: (i, 0))
    mod = pl.BlockSpec((1, 1, D), lambda i: (i // nt, 0, 0))
    return pl.pallas_call(
        _inproj_kernel,
        grid=(T // tm,),
        in_specs=[row(D), mod, mod, full(wq), full(wk), full(wv), full(wg),
                  full(bq), full(bk), full(bv), full(bg)],
        out_specs=[row(FOX_WIDTH), row(FOX_WIDTH), row(FOX_WIDTH), row(G_WIDTH)],
        out_shape=[jax.ShapeDtypeStruct((T, FOX_WIDTH), BF16)] * 3
                  + [jax.ShapeDtypeStruct((T, G_WIDTH), F32)],
        compiler_params=_cparams(("arbitrary",)),
        name="inproj",
    )(x2d, scale, shift, wq, wk, wv, wg, bq, bk, bv, bg)


def _aug_constants():
    pq = np.zeros((3 * LANES, FOX_PAD), np.float32)
    pk = np.zeros((3 * LANES, FOX_PAD), np.float32)
    cq = np.zeros((1, FOX_PAD), np.float32)
    ck = np.zeros((1, FOX_PAD), np.float32)
    cv = np.zeros((1, FOX_PAD), np.float32)
    spread = np.zeros((FOX_WIDTH, FOX_PAD), np.float32)
    for h in range(FOX_HEADS):
        base = h * LANES + BIAS_LANE
        cv[0, base] = 1.0
        for d in range(FOX_HEAD_DIM):
            spread[h * FOX_HEAD_DIM + d, h * LANES + d] = 1.0
        for p in range(3):
            pq[p * LANES + h, base + p] = 1.0
            pk[p * LANES + h, base + 3 + p] = -1.0
            cq[0, base + 3 + p] = 1.0
            ck[0, base + p] = 1.0
    return pq, pk, cq, ck, cv, spread


def _fgate_kernel(q_ref, k_ref, v_ref, f_ref, pq_ref, pk_ref, cq_ref, ck_ref, cv_ref, sp_ref,
                  u_ref, wpool_ref, spool_ref,
                  q2_ref, k2_ref, vt_ref, st_ref, opool_ref, carry, xx):
    @pl.when(pl.program_id(1) == 0)
    def _():
        carry[...] = jnp.zeros_like(carry)

    _pool_kernel(u_ref, wpool_ref, spool_ref, opool_ref, xx)

    vf = _dot(v_ref[...], sp_ref[...]) + cv_ref[...]
    for h in range(FOX_HEADS):
        sl = slice(h * LANES, (h + 1) * LANES)
        vt_ref[0, 0, sl, :] = vf[:, sl].T.astype(BF16)

    tf = f_ref.shape[0]
    ls = _log_sigmoid(f_ref[...])
    r = lax.broadcasted_iota(jnp.int32, (tf, tf), 0)
    c = lax.broadcasted_iota(jnp.int32, (tf, tf), 1)
    tri = (c <= r).astype(BF16)
    hi, mid, lo = _split3(ls)
    cs = _dot(tri, hi) + _dot(tri, mid) + _dot(tri, lo)
    F = cs + carry[...]
    carry[...] = F[tf - 1:tf, :]
    fh, fm, fl = _split3(F)
    f3 = jnp.concatenate([fh, fm, fl], axis=1)
    qf = _dot(q_ref[...], sp_ref[...])
    kf = _dot(k_ref[...], sp_ref[...])
    q2_ref[...] = (qf + _dot(f3, pq_ref[...]) + cq_ref[...]).astype(BF16)
    k2_ref[...] = (kf + _dot(f3, pk_ref[...]) + ck_ref[...]).astype(BF16)
    lane = lax.broadcasted_iota(jnp.int32, (1, LANES), 1)
    qstat = jnp.zeros((1, LANES), F32)
    kstat = jnp.zeros((1, LANES), F32)
    for h in range(FOX_HEADS):
        sl = slice(h * LANES, (h + 1) * LANES)
        qm = jnp.max(jnp.sum(qf[:, sl] * qf[:, sl], axis=-1, keepdims=True), axis=0, keepdims=True)
        km = jnp.max(jnp.sum(kf[:, sl] * kf[:, sl], axis=-1, keepdims=True), axis=0, keepdims=True)
        qstat = jnp.where(lane == h, qm, qstat)
        kstat = jnp.where(lane == h, km, kstat)
    row = lax.broadcasted_iota(jnp.int32, (STAT_ROWS, LANES), 0)
    st_ref[...] = jnp.where(row == 0, F[0:1, :],
                            jnp.where(row == 1, F[tf - 1:tf, :],
                                      jnp.where(row == 2, qstat, jnp.where(row == 3, kstat, 0.0))))


def _fgate(zq, zk, zv, zg, w_pool, s_pool, B, S, tf):
    T = zq.shape[0]
    nt = S // tf
    pq, pk, cq, ck, cv, sp = _aug_constants()
    pq, pk, sp = jnp.asarray(pq, BF16), jnp.asarray(pk, BF16), jnp.asarray(sp, BF16)
    cq, ck, cv = jnp.asarray(cq), jnp.asarray(ck), jnp.asarray(cv)
    full = lambda a: pl.BlockSpec(a.shape, lambda b, i: (0,) * a.ndim)
    row = pl.BlockSpec((tf, FOX_PAD), lambda b, i: (b * nt + i, 0))
    packed = pl.BlockSpec((tf, FOX_WIDTH), lambda b, i: (b * nt + i, 0))
    return pl.pallas_call(
        _fgate_kernel,
        grid=(B, nt),
        in_specs=[packed, packed, packed,
                  pl.BlockSpec((tf, LANES), lambda b, i: (b * nt + i, G_OFF_F // LANES)),
                  full(pq), full(pk), full(cq), full(ck), full(cv), full(sp),
                  pl.BlockSpec((tf, POOL_WIDTH), lambda b, i: (b * nt + i, G_OFF_U // POOL_WIDTH)),
                  full(w_pool), full(s_pool)],
        out_specs=[row, row, pl.BlockSpec((1, 1, FOX_PAD, tf), lambda b, i: (b, i, 0, 0)),
                   pl.BlockSpec((STAT_ROWS, LANES), lambda b, i: (b * nt + i, 0)),
                   pl.BlockSpec((tf, POOL_WIDTH), lambda b, i: (b * nt + i, 0))],
        out_shape=[jax.ShapeDtypeStruct((T, FOX_PAD), BF16)] * 2
                  + [jax.ShapeDtypeStruct((B, nt, FOX_PAD, tf), BF16),
                     jax.ShapeDtypeStruct((B * nt * STAT_ROWS, LANES), F32),
                     jax.ShapeDtypeStruct((T, POOL_WIDTH), BF16)],
        scratch_shapes=[pltpu.VMEM((1, LANES), F32), pltpu.VMEM((tf + HALO, POOL_WIDTH), F32)],
        compiler_params=_cparams(("arbitrary", "arbitrary")),
        name="fgate_pool",
    )(zq, zk, zv, zg, pq, pk, cq, ck, cv, sp, zg, w_pool, s_pool)


def _attn_kernel(ff_ref, fl_ref, qn_ref, kn_ref, q_ref, k_ref, vt_ref, o_ref, m_sc, acc_sc, s_sc,
                 *, blk, nq, nb):
    h = pl.program_id(0)
    i = pl.program_id(1)

    def scores(slot, j):
        off = pl.multiple_of(j * blk, blk)
        for b in range(nb):
            s_sc[slot, b] = _dot_nt(k_ref[b, pl.ds(off, blk), :], q_ref[b])

    def softmax_pv(slot, j, diag):
        for b in range(nb):
            s = s_sc[slot, b]
            if diag:
                r = lax.broadcasted_iota(jnp.int32, (blk, blk), 0)
                c = lax.broadcasted_iota(jnp.int32, (blk, blk), 1)
                s = jnp.where(r <= c, s, NEG_BIG)
            m_prev = m_sc[b]
            m_new = jnp.maximum(m_prev, jnp.max(s, axis=0, keepdims=True))
            alpha = jnp.exp(m_prev - m_new)
            p = jnp.exp(s - m_new)
            acc_sc[b] = alpha * acc_sc[b] + _dot(vt_ref[b, j], p.astype(BF16))
            m_sc[b] = m_new

    m_sc[...] = jnp.full_like(m_sc, NEG_BIG)
    acc_sc[...] = jnp.zeros_like(acc_sc)
    scores(0, i)
    scores(1, jnp.maximum(i - 1, 0))
    softmax_pv(0, i, True)

    n = jnp.int32(0)
    for b in range(nb):
        base = (b * FOX_HEADS + h) * nq
        slack = (qn_ref[base + i] * kn_ref[b * FOX_HEADS + h] + ff_ref[base + i]
                 - jnp.min(m_sc[b]) + PRUNE_MARGIN)

        def cond(t, base=base, slack=slack):
            return jnp.logical_and(t < i, slack - fl_ref[base + jnp.maximum(i - 1 - t, 0)] >= 0.0)

        n = jnp.maximum(n, lax.while_loop(cond, lambda t: t + 1, jnp.int32(0)))

    def pair(u, carry):
        t = 1 + 2 * u
        ja = i - t
        scores(0, jnp.maximum(ja - 1, 0))
        softmax_pv(1, ja, False)

        @pl.when(t + 1 <= n)
        def _():
            scores(1, jnp.maximum(ja - 2, 0))
            softmax_pv(0, ja - 1, False)

        return carry

    lax.fori_loop(0, lax.shift_right_logical(n + 1, 1), pair, 0)
    for b in range(nb):
        acc = acc_sc[b]
        o_ref[b] = (acc / acc[BIAS_LANE:BIAS_LANE + 1, :]).T.astype(o_ref.dtype)


def _attention(q2, k2, vt, stats, B, S, blk):
    T = q2.shape[0]
    nq = S // blk
    H = FOX_HEADS
    st = stats.reshape(B, nq, STAT_ROWS, LANES)[:, :, :, :H]
    tab = lambda r: jnp.transpose(st[:, :, r, :], (0, 2, 1)).reshape(-1)
    ffirst, flast = tab(0), tab(1)
    qn = jnp.sqrt(tab(2)) * NORM_SLACK
    kn = jnp.sqrt(jnp.max(st[:, :, 3, :], axis=1)).reshape(-1) * NORM_SLACK
    r3 = lambda a: a.reshape(B, S, FOX_PAD)
    qspec = pl.BlockSpec((B, blk, LANES), lambda h, i, *_: (0, i, h))
    kspec = pl.BlockSpec((B, S, LANES), lambda h, i, *_: (0, 0, h))
    vtspec = pl.BlockSpec((B, nq, LANES, blk), lambda h, i, *_: (0, 0, h, 0))
    grid_spec = pltpu.PrefetchScalarGridSpec(
        num_scalar_prefetch=4,
        grid=(H, nq),
        in_specs=[qspec, kspec, vtspec],
        out_specs=qspec,
        scratch_shapes=[pltpu.VMEM((B, 1, blk), F32), pltpu.VMEM((B, LANES, blk), F32),
                        pltpu.VMEM((2, B, blk, blk), F32)],
    )
    out = pl.pallas_call(
        functools.partial(_attn_kernel, blk=blk, nq=nq, nb=B),
        grid_spec=grid_spec,
        out_shape=jax.ShapeDtypeStruct((B, S, FOX_PAD), BF16),
        compiler_params=_cparams(("arbitrary", "arbitrary")),
        name="fox_attention",
    )(ffirst, flast, qn, kn, r3(q2), r3(k2), vt)
    return out.reshape(T, FOX_PAD)


def _gla_kernel(q_ref, k_ref, v_ref, r_ref, a_ref, wa_ref, ba_ref, g_ref, o_ref, st_sc, *, tg, nb):
    @pl.when(pl.program_id(0) == 0)
    def _():
        st_sc[...] = jnp.zeros_like(st_sc)

    C = GLA_CHUNK
    nchunk = tg // C
    r = lax.broadcasted_iota(jnp.int32, (tg, tg), 0)
    c = lax.broadcasted_iota(jnp.int32, (tg, tg), 1)
    shift = C.bit_length() - 1
    tri = ((c <= r) & ((c >> shift) == (r >> shift))).astype(BF16)
    rc = lax.broadcasted_iota(jnp.int32, (C, C), 0)
    cc = lax.broadcasted_iota(jnp.int32, (C, C), 1)
    causal = cc <= rc
    lane = lax.broadcasted_iota(jnp.int32, (1, LANES), 1)
    vmask = (lane < GLA_DV).astype(F32)
    for bi in range(nb):
        la = (_log_sigmoid(_dot(a_ref[bi].astype(BF16), wa_ref[...]) + ba_ref[...])
              * (1.0 / GLA_GATE_TAU))
        hi, mid, lo = _split3(la)
        b = _dot(tri, hi) + _dot(tri, mid) + _dot(tri, lo)
        eb = jnp.exp(b)
        q_in = q_ref[bi] * (GLA_DK ** -0.5) * eb
        k_in = k_ref[bi] * jnp.exp(-b)
        v = v_ref[bi]
        outs = []
        for ci in range(nchunk):
            rows = slice(ci * C, (ci + 1) * C)
            b_last = b[ci * C + C - 1:ci * C + C, :]
            k_out = k_ref[bi, rows, :] * jnp.exp(b_last - b[rows, :])
            dec = jnp.exp(b_last)
            heads = []
            for h in range(GLA_HEADS):
                kl = slice(h * GLA_KSLOT, (h + 1) * GLA_KSLOT)
                ln = slice(h * LANES, (h + 1) * LANES)
                qh = q_in[rows, kl].astype(BF16)
                kh = k_in[rows, kl].astype(BF16)
                vh = v[rows, ln].astype(BF16)
                attn = jnp.where(causal, _dot_nt(qh, kh), 0.0)
                st = st_sc[bi * GLA_HEADS + h]
                o = _dot(attn.astype(BF16), vh) + _dot_nt(qh, st.astype(BF16))
                kv_t = _dot_tn(vh, k_out[:, kl].astype(BF16))
                st_sc[bi * GLA_HEADS + h] = st * dec[:, kl] + kv_t
                ms = jnp.sum(o * o, axis=-1, keepdims=True) * (1.0 / GLA_DV)
                heads.append(o * lax.rsqrt(ms + RMS_EPS) * vmask)
            outs.append(jnp.concatenate(heads, axis=1))
        o_all = jnp.concatenate(outs, axis=0)
        gr = r_ref[bi]
        o_ref[bi] = ((o_all * g_ref[...]) * (gr * jax.nn.sigmoid(gr))).astype(o_ref.dtype)


def _gla(zg, wa, ba, gn, B, S, tg):
    T = zg.shape[0]
    nt = S // tg
    zg3 = zg.reshape(B, S, G_WIDTH)
    col = lambda off, w: pl.BlockSpec((B, tg, w), lambda i: (0, i, off // w))
    full = lambda a: pl.BlockSpec(a.shape, lambda i: (0,) * a.ndim)
    out = pl.pallas_call(
        functools.partial(_gla_kernel, tg=tg, nb=B),
        grid=(nt,),
        in_specs=[col(G_OFF_Q, GLA_KPAD), col(G_OFF_K, GLA_KPAD), col(G_OFF_V, GLA_PAD),
                  col(G_OFF_R, GLA_PAD), col(G_OFF_A, LANES), full(wa), full(ba), full(gn)],
        out_specs=pl.BlockSpec((B, tg, GLA_PAD), lambda i: (0, i, 0)),
        out_shape=jax.ShapeDtypeStruct((B, S, GLA_PAD), BF16),
        scratch_shapes=[pltpu.VMEM((B * GLA_HEADS, LANES, GLA_KSLOT), F32)],
        compiler_params=_cparams(("arbitrary",)),
        name="gla",
    )(zg3, zg3, zg3, zg3, zg3, wa, ba, gn)
    return out.reshape(T, GLA_PAD)


HALO = max(POOL_WINDOWS)


def _pool_kernel(u_ref, w_ref, s_ref, o_ref, xx):
    tp = u_ref.shape[0]
    i = pl.program_id(1)

    @pl.when(i == 0)
    def _():
        xx[0:HALO, :] = jnp.zeros((HALO, POOL_WIDTH), F32)

    @pl.when(i > 0)
    def _():
        xx[0:HALO, :] = xx[tp:tp + HALO, :]

    u = u_ref[...]
    xx[HALO:HALO + tp, :] = u
    lane = lax.broadcasted_iota(jnp.int32, (1, POOL_WIDTH), 1)
    grp = lane >> (POOL_GROUP.bit_length() - 1)
    pos = lax.broadcasted_iota(jnp.int32, (tp, 1), 0) + i * tp + 1
    acc = u
    mean = jnp.zeros_like(u)
    for j in range(1, HALO):
        acc = acc + xx[HALO - j:HALO - j + tp, :]
        w = j + 1
        if w in POOL_WINDOWS:
            g = POOL_WINDOWS.index(w)
            inv_cnt = 1.0 / jnp.minimum(pos, w).astype(F32)
            mean = jnp.where(grp == g, acc * inv_cnt, mean)
    pooled = mean - u
    mixed = _dot(pooled.astype(BF16), w_ref[...])
    o_ref[...] = (mixed * s_ref[...]).astype(o_ref.dtype)


def _layer_norm(r, g, b):
    mu = jnp.mean(r, axis=-1, keepdims=True)
    d = r - mu
    var = jnp.mean(d * d, axis=-1, keepdims=True)
    return d * lax.rsqrt(var + LN_EPS) * g + b


def _outproj_kernel(of_ref, og_ref, op_ref, x_ref, wf_ref, wg_ref, wp_ref, gate_ref, lg_ref, lb_ref,
                    sc_ref, sh_ref, wr_ref, br_ref,
                    x1_ref, h2_ref, e_ref, gt_ref, cnt_ref, *, alpha):
    y = _dot(of_ref[...], wf_ref[...]) + _dot(og_ref[...], wg_ref[...]) + _dot(op_ref[...], wp_ref[...])
    r = alpha * x_ref[...] + (1.0 + gate_ref[0]) * y
    x1 = _layer_norm(r, lg_ref[...], lb_ref[...])
    x1_ref[...] = x1
    h2 = x1 * (1.0 + sc_ref[0]) + sh_ref[0]
    h2_ref[...] = h2
    h_hi = h2.astype(BF16)
    h_lo = (h2 - h_hi.astype(F32)).astype(BF16)
    logits = (_dot(h_hi, wr_ref[0]) + (_dot(h_hi, wr_ref[1]) + _dot(h_lo, wr_ref[0]))
              + br_ref[...])
    tm = logits.shape[0]
    lane_i = lax.broadcasted_iota(jnp.int32, (tm, LANES), 1)
    lane = lane_i.astype(F32)
    work = logits
    tops, idxs = [], []
    onehot = jnp.zeros((tm, LANES), F32)
    for _ in range(TOP_K):
        m = jnp.max(work, axis=-1, keepdims=True)
        idx = jnp.min(jnp.where(work == m, lane, float(LANES)), axis=-1, keepdims=True)
        sel = lane == idx
        onehot = onehot + sel.astype(F32)
        work = jnp.where(sel, -jnp.inf, work)
        tops.append(m)
        idxs.append(idx)
    ex = [jnp.exp(t - tops[0]) for t in tops]
    den = ex[0] + ex[1] + ex[2] + ex[3]
    e_out = jnp.zeros((tm, LANES), jnp.int32)
    g_out = jnp.zeros((tm, LANES), F32)
    for k in range(TOP_K):
        e_out = jnp.where(lane_i == k, idxs[k].astype(jnp.int32), e_out)
        g_out = jnp.where(lane_i == k, ex[k] / den, g_out)
    e_ref[...] = e_out
    gt_ref[...] = g_out
    for u in range(tm // ROUTE_TILE):
        rows = slice(u * ROUTE_TILE, (u + 1) * ROUTE_TILE)
        cnt_ref[u] = jnp.sum(onehot[rows], axis=0, keepdims=True).astype(jnp.int32)


def _outproj(o_fox, o_gla, o_pool, x2d, wf, wg, wp, gate1, ln_g, ln_b, scale2, shift2, wr, br,
             B, S, tm, alpha):
    T, D = x2d.shape
    nt = S // tm
    full = lambda a: pl.BlockSpec(a.shape, lambda i: (0,) * a.ndim)
    row = lambda w: pl.BlockSpec((tm, w), lambda i: (i, 0))
    mod = pl.BlockSpec((1, 1, D), lambda i: (i // nt, 0, 0))
    return pl.pallas_call(
        functools.partial(_outproj_kernel, alpha=alpha),
        grid=(T // tm,),
        in_specs=[row(FOX_PAD), row(GLA_PAD), row(POOL_WIDTH), row(D), full(wf), full(wg), full(wp),
                  mod, full(ln_g), full(ln_b), mod, mod, full(wr), full(br)],
        out_specs=[row(D), row(D), row(LANES), row(LANES),
                   pl.BlockSpec((tm // ROUTE_TILE, 1, LANES), lambda i: (i, 0, 0))],
        out_shape=[jax.ShapeDtypeStruct((T, D), F32), jax.ShapeDtypeStruct((T, D), F32),
                   jax.ShapeDtypeStruct((T, LANES), jnp.int32), jax.ShapeDtypeStruct((T, LANES), F32),
                   jax.ShapeDtypeStruct((T // ROUTE_TILE, 1, LANES), jnp.int32)],
        compiler_params=_cparams(("arbitrary",)),
        name="outproj_router",
    )(o_fox, o_gla, o_pool, x2d, wf, wg, wp, gate1, ln_g, ln_b, scale2, shift2, wr, br)


def _tile_slots(e_i32, off_row):
    td = e_i32.shape[0]
    lane = lax.broadcasted_iota(jnp.int32, (td, LANES), 1).astype(F32)
    ef = e_i32.astype(F32)
    sel = [lane == ef[:, k:k + 1] for k in range(TOP_K)]
    onehot = sel[0].astype(F32)
    for k in range(1, TOP_K):
        onehot = onehot + sel[k].astype(F32)
    rr = lax.broadcasted_iota(jnp.int32, (td, td), 0)
    cc = lax.broadcasted_iota(jnp.int32, (td, td), 1)
    stril = (cc < rr).astype(BF16)
    tab = _dot(stril, onehot.astype(BF16)) + off_row
    return [jnp.sum(jnp.where(sel[k], tab, 0.0), axis=-1, keepdims=True) for k in range(TOP_K)]


def _group_bits(max_groups):
    return [1 << s for s in range(max_groups.bit_length() - 1, -1, -1)]


PIECE_CLASSES = (ROUTE_TILE // ROW_GROUP).bit_length()
PIECE_LIST = 1024
SLOT_BITS = 8
assert ROUTE_SLOTS // ROW_GROUP < (1 << SLOT_BITS) and PIECE_CLASSES * N_EXPERTS <= PIECE_LIST


def _piece_lists(m, off, dst):
    nt = m.shape[0]
    j = jnp.arange(N_EXPERTS, dtype=jnp.int32)
    b = jnp.arange(PIECE_CLASSES, dtype=jnp.int32)[None, :, None]
    mm, oo, dd = m[:, None, :], off[:, None, :], dst[:, None, :]
    has = (mm >> b) & 1
    before = (mm & ~((2 << b) - 1)) * ROW_GROUP
    packed = (((dd + before) // ROW_GROUP) << SLOT_BITS) | ((oo + before) // ROW_GROUP)
    rank = _excl_prefix(has, 2)
    hit = (has[..., None] == 1) & (rank[..., None] == j)
    lst = jnp.sum(jnp.where(hit, packed[..., None], 0), axis=2)
    lst = _pad_last(lst.reshape(nt, PIECE_CLASSES * N_EXPERTS).astype(jnp.int32), PIECE_LIST)
    return lst.reshape(-1), jnp.sum(has, axis=2).astype(jnp.int32).reshape(-1)


def _for_each_piece(tile, cnt_ref, list_ref, fn, base=0):
    for b in range(PIECE_CLASSES):
        def body(i, carry, b=b):
            v = list_ref[base + b * N_EXPERTS + i]
            fn(pl.multiple_of((v & ((1 << SLOT_BITS) - 1)) * ROW_GROUP, ROW_GROUP),
               pl.multiple_of(lax.shift_right_logical(v, SLOT_BITS) * ROW_GROUP, ROW_GROUP),
               (1 << b) * ROW_GROUP)
            return carry

        lax.fori_loop(0, cnt_ref[tile * PIECE_CLASSES + b], body, 0)


def _for_each_total_piece(total_groups, fn):
    for bit in _group_bits(ROUTE_SLOTS // ROW_GROUP):
        @pl.when((total_groups & bit) != 0)
        def _(bit=bit):
            fn(bit * ROW_GROUP)


def _dispatch_kernel(cnt_ref, tot_ref, zblk_ref, nu_ref, list_ref, e_ref, offrow_ref, h_ref,
                     xr_ref, slot_ref, sort_sc, zero_sc, sems, zsem, *, n_blocks, nt):
    step = pl.program_id(0)

    @pl.when(step == 0)
    def _():
        zero_sc[...] = jnp.zeros_like(zero_sc)

        def zcopy(blk):
            return pltpu.make_async_copy(
                zero_sc, xr_ref.at[pl.ds(pl.multiple_of(blk * MOE_BLOCK, MOE_BLOCK), MOE_BLOCK), :], zsem)

        def zstart(e, c):
            @pl.when(zblk_ref[e] >= 0)
            def _():
                zcopy(zblk_ref[e]).start()
            return c

        def zwait(e, c):
            @pl.when(zblk_ref[e] >= 0)
            def _():
                zcopy(0).wait()
            return c

        def tstart(blk, c):
            zcopy(blk).start()
            return c

        def twait(blk, c):
            zcopy(0).wait()
            return c

        lax.fori_loop(0, N_EXPERTS, zstart, 0)
        lax.fori_loop(nu_ref[0], n_blocks, tstart, 0)
        lax.fori_loop(0, N_EXPERTS, zwait, 0)
        lax.fori_loop(nu_ref[0], n_blocks, twait, 0)

    td = ROUTE_TILE
    par = step & 1
    for u in range(TILES_PER_STEP):
        rows = slice(u * td, (u + 1) * td)
        slots = _tile_slots(e_ref[rows, :], offrow_ref[u])
        lane = lax.broadcasted_iota(jnp.int32, (td, LANES), 1)
        cols = jnp.full((td, LANES), -1.0, F32)
        for k in range(TOP_K):
            cols = jnp.where(lane == k, slots[k], cols)
        slot_ref[rows, :] = cols
        rows_t = cols.T
        sub = lax.broadcasted_iota(jnp.int32, (ROUTE_SLOTS, td), 0).astype(F32)
        pick = sub == rows_t[0:1, :]
        for k in range(1, TOP_K):
            pick = pick | (sub == rows_t[k:k + 1, :])
        sort_sc[par * TILES_PER_STEP + u] = _dot(pick.astype(BF16), h_ref[rows, :].astype(BF16))

    for u in range(TILES_PER_STEP):
        buf = par * TILES_PER_STEP + u

        def start_chunk(slot0, row0, n, buf=buf, u=u):
            pltpu.make_async_copy(sort_sc.at[buf, pl.ds(slot0, n), :], xr_ref.at[pl.ds(row0, n), :],
                                  sems.at[par, u]).start()

        _for_each_piece(step * TILES_PER_STEP + u, cnt_ref, list_ref, start_chunk, base=u * PIECE_LIST)

    def wait_step(s, p):
        for u in range(TILES_PER_STEP):
            def wait_piece(n, u=u):
                pltpu.make_async_copy(sort_sc.at[p * TILES_PER_STEP + u, pl.ds(0, n), :],
                                      xr_ref.at[pl.ds(0, n), :], sems.at[p, u]).wait()
            _for_each_total_piece(tot_ref[s * TILES_PER_STEP + u], wait_piece)

    @pl.when(step > 0)
    def _():
        wait_step(step - 1, 1 - par)

    @pl.when(step == nt - 1)
    def _():
        wait_step(step, par)


def _dispatch(h2, e_pad, tables, rows):
    T, D = h2.shape
    tpn = TILES_PER_STEP
    td = ROUTE_TILE * tpn
    assert T % td == 0
    nt = T // td
    piece_list, piece_cnt, tot_tab, zblk, n_used, off_rows = tables
    grid_spec = pltpu.PrefetchScalarGridSpec(
        num_scalar_prefetch=4,
        grid=(nt,),
        in_specs=[pl.BlockSpec((tpn * PIECE_LIST,), lambda i, *_: (i,), memory_space=pltpu.SMEM),
                  pl.BlockSpec((td, LANES), lambda i, *_: (i, 0)),
                  pl.BlockSpec((tpn, 1, LANES), lambda i, *_: (i, 0, 0)),
                  pl.BlockSpec((td, D), lambda i, *_: (i, 0))],
        out_specs=[pl.BlockSpec(memory_space=pl.ANY),
                   pl.BlockSpec((td, LANES), lambda i, *_: (i, 0))],
        scratch_shapes=[pltpu.VMEM((2 * tpn, ROUTE_SLOTS, D), F32), pltpu.VMEM((MOE_BLOCK, D), F32),
                        pltpu.SemaphoreType.DMA((2, tpn)), pltpu.SemaphoreType.DMA],
    )
    return pl.pallas_call(
        functools.partial(_dispatch_kernel, n_blocks=rows // MOE_BLOCK, nt=nt),
        grid_spec=grid_spec,
        out_shape=[jax.ShapeDtypeStruct((rows, D), F32), jax.ShapeDtypeStruct((T, LANES), F32)],
        compiler_params=_cparams(("arbitrary",)),
        name="moe_dispatch",
    )(piece_cnt, tot_tab, zblk, n_used, piece_list, e_pad, off_rows, h2)


def _expert_kernel(be_ref, nu_ref, vb_ref, par_ref, nxt_ref, x_ref, wgu_hbm, bgu_ref, wd_hbm, bd_ref, y_ref,
                   wgu_sc, wd_sc, gu_buf, d_buf, sems, *, first_expert):
    i = pl.program_id(0)
    used = i < nu_ref[0]

    def fetch(e, slot):
        return (pltpu.make_async_copy(wgu_hbm.at[first_expert + e], gu_buf.at[slot], sems.at[slot, 0]),
                pltpu.make_async_copy(wd_hbm.at[first_expert + e], d_buf.at[slot], sems.at[slot, 1]))

    @pl.when(used)
    def _():
        prev = be_ref[jnp.maximum(i - 1, 0)]

        @pl.when((i == 0) | (be_ref[i] != prev))
        def _():
            slot = par_ref[i]

            @pl.when(i == 0)
            def _():
                for cp in fetch(be_ref[i], slot):
                    cp.start()

            for cp in fetch(be_ref[i], slot):
                cp.wait()
            wgu_sc[...] = gu_buf[slot].astype(BF16)
            wd_sc[...] = d_buf[slot].astype(BF16)

            @pl.when(nxt_ref[i] >= 0)
            def _():
                for cp in fetch(nxt_ref[i], 1 - slot):
                    cp.start()

        def mlp(rows):
            x = x_ref[rows, :].astype(BF16)
            gu = _dot(x, wgu_sc[...]) + bgu_ref[0]
            glu = jnp.minimum(gu[:, :D_EXPERT], SWIGLU_LIMIT)
            lin = jnp.clip(gu[:, D_EXPERT:], -SWIGLU_LIMIT, SWIGLU_LIMIT)
            act = glu * jax.nn.sigmoid(SWIGLU_ALPHA * glu) * (lin + 1.0)
            y_ref[rows, :] = _dot(act.astype(BF16), wd_sc[...]) + bd_ref[0]

        half = MOE_BLOCK // 2
        full_block = vb_ref[i] > half

        @pl.when(full_block)
        def _():
            mlp(slice(0, MOE_BLOCK))

        @pl.when(jnp.logical_not(full_block))
        def _():
            mlp(slice(0, half))
            y_ref[half:, :] = jnp.zeros((MOE_BLOCK - half, y_ref.shape[1]), y_ref.dtype)

    @pl.when(jnp.logical_not(used))
    def _():
        y_ref[...] = jnp.zeros_like(y_ref)


def _experts(x_rows, block_expert, n_used, valid_rows, slot_parity, next_expert,
             w_gate_up, b_gate_up, w_down, b_down, layer):
    rows, D = x_rows.shape
    nb = rows // MOE_BLOCK
    E = w_gate_up.shape[1]
    grid_spec = pltpu.PrefetchScalarGridSpec(
        num_scalar_prefetch=5,
        grid=(nb,),
        in_specs=[pl.BlockSpec((MOE_BLOCK, D), lambda i, be, nu, *_: (jnp.minimum(i, nu[0] - 1), 0)),
                  pl.BlockSpec(memory_space=pl.ANY),
                  pl.BlockSpec((1, 1, 2 * D_EXPERT), lambda i, be, *_: (layer * E + be[i], 0, 0)),
                  pl.BlockSpec(memory_space=pl.ANY),
                  pl.BlockSpec((1, 1, D), lambda i, be, *_: (layer * E + be[i], 0, 0))],
        out_specs=pl.BlockSpec((MOE_BLOCK, D), lambda i, *_: (i, 0)),
        scratch_shapes=[pltpu.VMEM((D, 2 * D_EXPERT), BF16), pltpu.VMEM((D_EXPERT, D), BF16),
                        pltpu.VMEM((2, D, 2 * D_EXPERT), F32), pltpu.VMEM((2, D_EXPERT, D), F32),
                        pltpu.SemaphoreType.DMA((2, 2))],
    )
    L = w_gate_up.shape[0]
    return pl.pallas_call(
        functools.partial(_expert_kernel, first_expert=layer * E),
        grid_spec=grid_spec,
        out_shape=jax.ShapeDtypeStruct((rows, D), F32),
        compiler_params=_cparams(("arbitrary",)),
        name="moe_experts",
    )(block_expert, n_used, valid_rows, slot_parity, next_expert, x_rows,
      w_gate_up.reshape(L * E, D, 2 * D_EXPERT), b_gate_up.reshape(L * E, 1, 2 * D_EXPERT),
      w_down.reshape(L * E, D_EXPERT, D), b_down.reshape(L * E, 1, D))


def _combine_kernel(cnt_ref, tot_ref, list_ref, next_list_ref, slot_ref, gt_ref, y_ref, x_ref, gate_ref,
                    lg_ref, lb_ref, o_ref, ybuf, sems, *, nt, alpha):
    step = pl.program_id(0)
    par = step & 1
    tpn = TILES_PER_STEP

    def fetch(s, p, lst):
        for u in range(tpn):
            def start_chunk(slot0, row0, n, u=u):
                pltpu.make_async_copy(y_ref.at[pl.ds(row0, n), :], ybuf.at[p * tpn + u, pl.ds(slot0, n), :],
                                      sems.at[p, u]).start()
            _for_each_piece(s * tpn + u, cnt_ref, lst, start_chunk, base=u * PIECE_LIST)

    @pl.when(step == 0)
    def _():
        ybuf[...] = jnp.zeros_like(ybuf)
        fetch(step, par, list_ref)

    @pl.when(step + 1 < nt)
    def _():
        fetch(step + 1, 1 - par, next_list_ref)

    for u in range(tpn):
        def wait_piece(n, u=u):
            pltpu.make_async_copy(y_ref.at[pl.ds(0, n), :], ybuf.at[par * tpn + u, pl.ds(0, n), :],
                                  sems.at[par, u]).wait()
        _for_each_total_piece(tot_ref[step * tpn + u], wait_piece)

    td = ROUTE_TILE
    for u in range(tpn):
        rows = slice(u * td, (u + 1) * td)
        slots = slot_ref[rows, :]
        gt = gt_ref[rows, :]
        lane = lax.broadcasted_iota(jnp.int32, (td, ROUTE_SLOTS), 1).astype(F32)
        w = jnp.where(lane == slots[:, 0:1], gt[:, 0:1], 0.0)
        for k in range(1, TOP_K):
            w = w + jnp.where(lane == slots[:, k:k + 1], gt[:, k:k + 1], 0.0)
        y = _dot(w.astype(BF16), ybuf[par * tpn + u].astype(BF16))
        r = alpha * x_ref[rows, :] + (1.0 + gate_ref[0]) * y
        o_ref[rows, :] = _layer_norm(r, lg_ref[...], lb_ref[...])


def _combine(y_rows, slots, tables, gates, x1, gate2, ln_g, ln_b, B, S, alpha):
    T, D = x1.shape
    tpn = TILES_PER_STEP
    td = ROUTE_TILE * tpn
    assert S % td == 0
    nt = T // td
    per_batch = S // td
    piece_list, piece_cnt, tot_tab, _, _, _ = tables
    grid_spec = pltpu.PrefetchScalarGridSpec(
        num_scalar_prefetch=2,
        grid=(nt,),
        in_specs=[pl.BlockSpec((tpn * PIECE_LIST,), lambda i, *_: (i,), memory_space=pltpu.SMEM),
                  pl.BlockSpec((tpn * PIECE_LIST,), lambda i, *_: (jnp.minimum(i + 1, nt - 1),),
                               memory_space=pltpu.SMEM),
                  pl.BlockSpec((td, LANES), lambda i, *_: (i, 0)),
                  pl.BlockSpec((td, LANES), lambda i, *_: (i, 0)),
                  pl.BlockSpec(memory_space=pl.ANY),
                  pl.BlockSpec((td, D), lambda i, *_: (i, 0)),
                  pl.BlockSpec((1, 1, D), lambda i, *_: (i // per_batch, 0, 0)),
                  pl.BlockSpec((1, D), lambda i, *_: (0, 0)),
                  pl.BlockSpec((1, D), lambda i, *_: (0, 0))],
        out_specs=pl.BlockSpec((td, D), lambda i, *_: (i, 0)),
        scratch_shapes=[pltpu.VMEM((2 * tpn, ROUTE_SLOTS, D), F32), pltpu.SemaphoreType.DMA((2, tpn))],
    )
    return pl.pallas_call(
        functools.partial(_combine_kernel, nt=nt, alpha=alpha),
        grid_spec=grid_spec,
        out_shape=jax.ShapeDtypeStruct((T, D), F32),
        compiler_params=_cparams(("arbitrary",)),
        name="moe_combine",
    )(piece_cnt, tot_tab, piece_list, piece_list, slots, gates, y_rows, x1, gate2, ln_g, ln_b)


def _tile(n, pref):
    t = min(n, pref)
    assert n % t == 0, (n, t)
    return t


def kernel(x, c, w_ada, b_ada, w_in, b_in, gla_w_a2, gla_b_a, gla_norm_g, pool_w, pool_scale, w_out,
           ln1_g, ln1_b, w_router, b_router, w_gate_up, b_gate_up, w_down, b_down, ln2_g, ln2_b):
    B, S, D = x.shape
    L = w_ada.shape[0]
    T = B * S
    assert D == D_MODEL and S % GLA_CHUNK == 0
    alpha = float((2 * L) ** 0.25)
    assert T % ROUTE_TILE == 0
    n_tiles = T // ROUTE_TILE
    max_rows = T * TOP_K + n_tiles * N_EXPERTS * (ROW_GROUP - 1) + N_EXPERTS * (MOE_BLOCK - 1)
    n_blocks = -(-max_rows // MOE_BLOCK)
    rows = n_blocks * MOE_BLOCK

    mod = _ada_mod(c, w_ada, b_ada)
    prep_all = _prep_inproj(w_in, b_in)
    wa_all = jnp.pad(_pad_heads(gla_w_a2, GLA_HEADS, GLA_DK, GLA_KSLOT),
                     ((0, 0), (0, LANES - GLA_GATE_RANK), (0, 0))).astype(BF16)
    ba_all = _pad_heads(gla_b_a[:, None, :], GLA_HEADS, GLA_DK, GLA_KSLOT)
    gn_all = _pad_heads(gla_norm_g[:, None, :], GLA_HEADS, GLA_DV)
    w_bd_all = jnp.zeros((L, POOL_WIDTH, POOL_WIDTH), F32)
    for g in range(len(POOL_WINDOWS)):
        sl = slice(g * POOL_GROUP, (g + 1) * POOL_GROUP)
        w_bd_all = w_bd_all.at[:, sl, sl].set(pool_w[:, g])
    w_bd_all = w_bd_all.astype(BF16)
    pad_rows = lambda w, heads, dim: jnp.swapaxes(_pad_heads(jnp.swapaxes(w, 1, 2), heads, dim), 1, 2)
    wf_all = pad_rows(w_out[:, :FOX_WIDTH], FOX_HEADS, FOX_HEAD_DIM).astype(BF16)
    wgl_all = pad_rows(w_out[:, FOX_WIDTH:FOX_WIDTH + GLA_VWIDTH], GLA_HEADS, GLA_DV).astype(BF16)
    wp_all = w_out[:, FOX_WIDTH + GLA_VWIDTH:].astype(BF16)
    wr_f32 = _pad_last(w_router, LANES)
    wr_hi = wr_f32.astype(BF16)
    wr_all = jnp.stack([wr_hi, (wr_f32 - wr_hi.astype(F32)).astype(BF16)], axis=1)
    br_all = jnp.pad(b_router[:, None, :], ((0, 0), (0, 0), (0, LANES - N_EXPERTS)), constant_values=NEG_BIG)

    x2d = x.reshape(T, D)
    for l in range(L):
        shift1, scale1, gate1, shift2, scale2, gate2 = [mod[l, :, m] for m in range(N_MOD)]
        prep = [(w[l], b[l]) for w, b in prep_all]
        zq, zk, zv, zg = _inproj(x2d, scale1, shift1, prep, B, S, _tile(S, 512))
        q2, k2, vt, stats, o_pool = _fgate(zq, zk, zv, zg, w_bd_all[l], pool_scale[l][None, :],
                                           B, S, _tile(S, ATTN_BLOCK))
        o_fox = _attention(q2, k2, vt, stats, B, S, _tile(S, ATTN_BLOCK))
        o_gla = _gla(zg, wa_all[l], ba_all[l], gn_all[l], B, S, _tile(S, 256))
        x1, h2, e_pad, g_pad, cnt = _outproj(
            o_fox, o_gla, o_pool, x2d, wf_all[l], wgl_all[l], wp_all[l], gate1,
            ln1_g[l][None, :], ln1_b[l][None, :], scale2, shift2, wr_all[l], br_all[l],
            B, S, _tile(S, 512), alpha)
        cte = cnt[:, 0, :N_EXPERTS]
        c8 = (cte + ROW_GROUP - 1) // ROW_GROUP * ROW_GROUP
        off = _excl_prefix(c8, 1)
        per_expert = jnp.sum(c8, axis=0)
        padded = (per_expert + MOE_BLOCK - 1) // MOE_BLOCK * MOE_BLOCK
        pstart = _excl_prefix(padded, 0)
        pend = pstart + padded
        dst = pstart[None, :] + _excl_prefix(c8, 0)
        n_used = (pend[-1] // MOE_BLOCK).astype(jnp.int32)
        blk_start = jnp.arange(n_blocks, dtype=jnp.int32) * MOE_BLOCK
        be = jnp.minimum(jnp.sum(blk_start[:, None] >= pend[None, :], axis=1), N_EXPERTS - 1).astype(jnp.int32)
        last_used = jnp.minimum(jnp.sum(MOE_BLOCK * (n_used - 1) >= pend), N_EXPERTS - 1).astype(jnp.int32)
        be = jnp.where(jnp.arange(n_blocks) < n_used, be, last_used)
        zblk = jnp.where(padded > 0, pend // MOE_BLOCK - 1, -1).astype(jnp.int32)
        off_rows = _pad_last(off.astype(F32), LANES)[:, None, :]
        n_used = n_used.reshape(1)
        piece_list, piece_cnt = _piece_lists(c8 // ROW_GROUP, off, dst)
        tables = (piece_list, piece_cnt, (jnp.sum(c8, axis=1) // ROW_GROUP).astype(jnp.int32),
                  zblk, n_used, off_rows)
        x_rows, slots = _dispatch(h2, e_pad, tables, rows)
        owns = padded > 0
        ids = jnp.arange(N_EXPERTS, dtype=jnp.int32)
        later = jnp.where(owns[None, :] & (ids[None, :] > ids[:, None]), ids[None, :], N_EXPERTS)
        nxt_e = jnp.min(later, axis=1)
        nxt_e = jnp.where(nxt_e < N_EXPERTS, nxt_e, -1).astype(jnp.int32)
        par_e = (_excl_prefix(owns.astype(jnp.int32), 0) & 1).astype(jnp.int32)
        pick = be[:, None] == ids[None, :]
        lookup = lambda tab: jnp.sum(jnp.where(pick, tab[None, :], 0), axis=1).astype(jnp.int32)
        valid_rows = jnp.clip(lookup(per_expert) - (blk_start - lookup(pstart)), 0, MOE_BLOCK).astype(jnp.int32)
        y_rows = _experts(x_rows, be, n_used, valid_rows, lookup(par_e), lookup(nxt_e),
                          w_gate_up, b_gate_up, w_down, b_down, l)
        x2d = _combine(y_rows, slots, tables, g_pad, x1, gate2,
                       ln2_g[l][None, :], ln2_b[l][None, :], B, S, alpha)
    return x2d.reshape(B, S, D)
```

```python
import functools

import numpy as np
import jax
import jax.numpy as jnp
from jax import lax
from jax.experimental import pallas as pl
from jax.experimental.pallas import tpu as pltpu

F32 = jnp.float32
BF16 = jnp.bfloat16

D_MODEL = 1024
FOX_HEADS = 6
FOX_HEAD_DIM = 64
FOX_WIDTH = FOX_HEADS * FOX_HEAD_DIM
GLA_HEADS = 4
GLA_DV = 96
GLA_DK = 48
GLA_KWIDTH = GLA_HEADS * GLA_DK
GLA_VWIDTH = GLA_HEADS * GLA_DV
GLA_GATE_RANK = 16
GLA_GATE_TAU = 16.0
GLA_CHUNK = 64
POOL_WINDOWS = (2, 4, 8, 16)
POOL_GROUP = 64
POOL_WIDTH = len(POOL_WINDOWS) * POOL_GROUP
N_EXPERTS = 32
TOP_K = 4
D_EXPERT = 1024
SWIGLU_ALPHA = 1.702
SWIGLU_LIMIT = 7.0
N_MOD = 6
LN_EPS = 1e-5
RMS_EPS = 1e-6

LANES = 128
VMEM_LIMIT_BYTES = 56 * 1024 * 1024

FOX_PAD = FOX_HEADS * LANES
GLA_PAD = GLA_HEADS * LANES
BIAS_LANE = FOX_HEAD_DIM
GLA_KSLOT = 64
GLA_KPAD = GLA_HEADS * GLA_KSLOT
G_OFF_Q, G_OFF_K = 0, GLA_KPAD
G_OFF_V = 2 * GLA_KPAD
G_OFF_R = G_OFF_V + GLA_PAD
G_OFF_U = G_OFF_R + GLA_PAD
G_OFF_A = G_OFF_U + POOL_WIDTH
G_OFF_F = G_OFF_A + LANES
G_WIDTH = G_OFF_F + LANES

STAT_ROWS = 8
PRUNE_MARGIN = 105.0
NORM_SLACK = 1.01
ATTN_BLOCK = 512
MOE_BLOCK = 512
ROUTE_TILE = 256
ROW_GROUP = 8
ROUTE_SLOTS = ROUTE_TILE * TOP_K + N_EXPERTS * ROW_GROUP
TILES_PER_STEP = 2
NEG_BIG = -1e30


def _cparams(sem, vmem=None):
    return pltpu.CompilerParams(dimension_semantics=sem, vmem_limit_bytes=vmem or VMEM_LIMIT_BYTES)


def _log_sigmoid(x):
    return jnp.minimum(x, 0.0) - jnp.log(1.0 + jnp.exp(-jnp.abs(x)))


def _split3(x):
    hi = x.astype(BF16)
    r = x - hi.astype(F32)
    mid = r.astype(BF16)
    lo = (r - mid.astype(F32)).astype(BF16)
    return hi, mid, lo


def _dot(a, b):
    return jnp.dot(a, b, preferred_element_type=F32)


def _dot_nt(a, b):
    return lax.dot_general(a, b, (((1,), (1,)), ((), ())), preferred_element_type=F32)


def _dot_tn(a, b):
    return lax.dot_general(a, b, (((0,), (0,)), ((), ())), preferred_element_type=F32)


def _ada_kernel(c_ref, w_ref, b_ref, o_ref):
    c = c_ref[...]
    cond = c * jax.nn.sigmoid(c)
    o_ref[0] = jnp.dot(cond, w_ref[0], preferred_element_type=F32,
                       precision=lax.Precision.HIGHEST) + b_ref[0]


def _ada_mod(c, w_ada, b_ada):
    L, D, N = w_ada.shape
    B = c.shape[0]
    rows = 8
    c_pad = jnp.zeros((rows, D), F32).at[:B].set(c)
    tn = 1536
    out = pl.pallas_call(
        _ada_kernel,
        grid=(L, N // tn),
        in_specs=[pl.BlockSpec((rows, D), lambda l, j: (0, 0)),
                  pl.BlockSpec((1, D, tn), lambda l, j: (l, 0, j)),
                  pl.BlockSpec((1, 1, tn), lambda l, j: (l, 0, j))],
        out_specs=pl.BlockSpec((1, rows, tn), lambda l, j: (l, 0, j)),
        out_shape=jax.ShapeDtypeStruct((L, rows, N), F32),
        compiler_params=_cparams(("arbitrary", "arbitrary")),
        name="ada_mod",
    )(c_pad, w_ada, b_ada.reshape(L, 1, N))
    return out[:, :B].reshape(L, B, N_MOD, 1, D)


def _inproj_kernel(x_ref, sc_ref, sh_ref, wq_ref, wk_ref, wv_ref, wg_ref,
                   bq_ref, bk_ref, bv_ref, bg_ref, q_ref, k_ref, v_ref, g_ref):
    h = (x_ref[...] * (1.0 + sc_ref[0]) + sh_ref[0]).astype(BF16)
    q_ref[...] = (_dot(h, wq_ref[...]) + bq_ref[...]).astype(BF16)
    k_ref[...] = (_dot(h, wk_ref[...]) + bk_ref[...]).astype(BF16)
    v_ref[...] = (_dot(h, wv_ref[...]) + bv_ref[...]).astype(BF16)
    g_ref[...] = _dot(h, wg_ref[...]) + bg_ref[...]


def _pad_heads(w, heads, dim, slot=LANES):
    lead = w.shape[:-1]
    w = w.reshape(lead + (heads, dim))
    w = jnp.pad(w, [(0, 0)] * len(lead) + [(0, 0), (0, slot - dim)])
    return w.reshape(lead + (heads * slot,))


def _pad_last(a, width):
    return jnp.pad(a, [(0, 0)] * (a.ndim - 1) + [(0, width - a.shape[-1])])


def _excl_prefix(a, axis):
    n = a.shape[axis]
    idx = jnp.arange(n)
    earlier = idx[None, :] < idx[:, None]
    am = jnp.moveaxis(a, axis, -1)
    out = jnp.sum(jnp.where(earlier, am[..., None, :], 0), axis=-1)
    return jnp.moveaxis(out, -1, axis)


def _prep_inproj(w_in, b_in):
    def relayout(W):
        o = 0
        fq = W[..., o:o + FOX_WIDTH]; o += FOX_WIDTH
        fk = W[..., o:o + FOX_WIDTH]; o += FOX_WIDTH
        fv = W[..., o:o + FOX_WIDTH]; o += FOX_WIDTH
        ff = W[..., o:o + FOX_HEADS]; o += FOX_HEADS
        gq = W[..., o:o + GLA_KWIDTH]; o += GLA_KWIDTH
        gk = W[..., o:o + GLA_KWIDTH]; o += GLA_KWIDTH
        gv = W[..., o:o + GLA_VWIDTH]; o += GLA_VWIDTH
        gr = W[..., o:o + GLA_VWIDTH]; o += GLA_VWIDTH
        ga = W[..., o:o + GLA_GATE_RANK]; o += GLA_GATE_RANK
        pu = W[..., o:o + POOL_WIDTH]
        wq = fq * (FOX_HEAD_DIM ** -0.5)
        wg = jnp.concatenate([
            _pad_heads(gq, GLA_HEADS, GLA_DK, GLA_KSLOT), _pad_heads(gk, GLA_HEADS, GLA_DK, GLA_KSLOT),
            _pad_heads(gv, GLA_HEADS, GLA_DV), _pad_heads(gr, GLA_HEADS, GLA_DV),
            pu, _pad_last(ga, LANES), _pad_last(ff, LANES)], axis=-1)
        return wq, fk, fv, wg

    ws = relayout(w_in.astype(BF16))
    bs = relayout(b_in[:, None, :])
    return list(zip(ws, bs))


def _inproj(x2d, scale, shift, prep, B, S, tm):
    T, D = x2d.shape
    (wq, bq), (wk, bk), (wv, bv), (wg, bg) = prep
    nt = S // tm
    full = lambda a: pl.BlockSpec(a.shape, lambda i: (0,) * a.ndim)
    row = lambda w: pl.BlockSpec((tm, w), lambda i: (i, 0))
    mod = pl.BlockSpec((1, 1, D), lambda i: (i // nt, 0, 0))
    return pl.pallas_call(
        _inproj_kernel,
        grid=(T // tm,),
        in_specs=[row(D), mod, mod, full(wq), full(wk), full(wv), full(wg),
                  full(bq), full(bk), full(bv), full(bg)],
        out_specs=[row(FOX_WIDTH), row(FOX_WIDTH), row(FOX_WIDTH), row(G_WIDTH)],
        out_shape=[jax.ShapeDtypeStruct((T, FOX_WIDTH), BF16)] * 3
                  + [jax.ShapeDtypeStruct((T, G_WIDTH), F32)],
        compiler_params=_cparams(("arbitrary",)),
        name="inproj",
    )(x2d, scale, shift, wq, wk, wv, wg, bq, bk, bv, bg)


def _aug_constants():
    pq = np.zeros((3 * LANES, FOX_PAD), np.float32)
    pk = np.zeros((3 * LANES, FOX_PAD), np.float32)
    cq = np.zeros((1, FOX_PAD), np.float32)
    ck = np.zeros((1, FOX_PAD), np.float32)
    cv = np.zeros((1, FOX_PAD), np.float32)
    spread = np.zeros((FOX_WIDTH, FOX_PAD), np.float32)
    for h in range(FOX_HEADS):
        base = h * LANES + BIAS_LANE
        cv[0, base] = 1.0
        for d in range(FOX_HEAD_DIM):
            spread[h * FOX_HEAD_DIM + d, h * LANES + d] = 1.0
        for p in range(3):
            pq[p * LANES + h, base + p] = 1.0
            pk[p * LANES + h, base + 3 + p] = -1.0
            cq[0, base + 3 + p] = 1.0
            ck[0, base + p] = 1.0
    return pq, pk, cq, ck, cv, spread


def _fgate_kernel(q_ref, k_ref, v_ref, f_ref, pq_ref, pk_ref, cq_ref, ck_ref, cv_ref, sp_ref,
                  u_ref, wpool_ref, spool_ref,
                  q2_ref, k2_ref, vt_ref, st_ref, opool_ref, carry, xx):
    @pl.when(pl.program_id(1) == 0)
    def _():
        carry[...] = jnp.zeros_like(carry)

    _pool_kernel(u_ref, wpool_ref, spool_ref, opool_ref, xx)

    vf = _dot(v_ref[...], sp_ref[...]) + cv_ref[...]
    for h in range(FOX_HEADS):
        sl = slice(h * LANES, (h + 1) * LANES)
        vt_ref[0, 0, sl, :] = vf[:, sl].T.astype(BF16)

    tf = f_ref.shape[0]
    ls = _log_sigmoid(f_ref[...])
    r = lax.broadcasted_iota(jnp.int32, (tf, tf), 0)
    c = lax.broadcasted_iota(jnp.int32, (tf, tf), 1)
    tri = (c <= r).astype(BF16)
    hi, mid, lo = _split3(ls)
    cs = _dot(tri, hi) + _dot(tri, mid) + _dot(tri, lo)
    F = cs + carry[...]
    carry[...] = F[tf - 1:tf, :]
    fh, fm, fl = _split3(F)
    f3 = jnp.concatenate([fh, fm, fl], axis=1)
    qf = _dot(q_ref[...], sp_ref[...])
    kf = _dot(k_ref[...], sp_ref[...])
    q2_ref[...] = (qf + _dot(f3, pq_ref[...]) + cq_ref[...]).astype(BF16)
    k2_ref[...] = (kf + _dot(f3, pk_ref[...]) + ck_ref[...]).astype(BF16)
    lane = lax.broadcasted_iota(jnp.int32, (1, LANES), 1)
    qstat = jnp.zeros((1, LANES), F32)
    kstat = jnp.zeros((1, LANES), F32)
    for h in range(FOX_HEADS):
        sl = slice(h * LANES, (h + 1) * LANES)
        qm = jnp.max(jnp.sum(qf[:, sl] * qf[:, sl], axis=-1, keepdims=True), axis=0, keepdims=True)
        km = jnp.max(jnp.sum(kf[:, sl] * kf[:, sl], axis=-1, keepdims=True), axis=0, keepdims=True)
        qstat = jnp.where(lane == h, qm, qstat)
        kstat = jnp.where(lane == h, km, kstat)
    row = lax.broadcasted_iota(jnp.int32, (STAT_ROWS, LANES), 0)
    st_ref[...] = jnp.where(row == 0, F[0:1, :],
                            jnp.where(row == 1, F[tf - 1:tf, :],
                                      jnp.where(row == 2, qstat, jnp.where(row == 3, kstat, 0.0))))


def _fgate(zq, zk, zv, zg, w_pool, s_pool, B, S, tf):
    T = zq.shape[0]
    nt = S // tf
    pq, pk, cq, ck, cv, sp = _aug_constants()
    pq, pk, sp = jnp.asarray(pq, BF16), jnp.asarray(pk, BF16), jnp.asarray(sp, BF16)
    cq, ck, cv = jnp.asarray(cq), jnp.asarray(ck), jnp.asarray(cv)
    full = lambda a: pl.BlockSpec(a.shape, lambda b, i: (0,) * a.ndim)
    row = pl.BlockSpec((tf, FOX_PAD), lambda b, i: (b * nt + i, 0))
    packed = pl.BlockSpec((tf, FOX_WIDTH), lambda b, i: (b * nt + i, 0))
    return pl.pallas_call(
        _fgate_kernel,
        grid=(B, nt),
        in_specs=[packed, packed, packed,
                  pl.BlockSpec((tf, LANES), lambda b, i: (b * nt + i, G_OFF_F // LANES)),
                  full(pq), full(pk), full(cq), full(ck), full(cv), full(sp),
                  pl.BlockSpec((tf, POOL_WIDTH), lambda b, i: (b * nt + i, G_OFF_U // POOL_WIDTH)),
                  full(w_pool), full(s_pool)],
        out_specs=[row, row, pl.BlockSpec((1, 1, FOX_PAD, tf), lambda b, i: (b, i, 0, 0)),
                   pl.BlockSpec((STAT_ROWS, LANES), lambda b, i: (b * nt + i, 0)),
                   pl.BlockSpec((tf, POOL_WIDTH), lambda b, i: (b * nt + i, 0))],
        out_shape=[jax.ShapeDtypeStruct((T, FOX_PAD), BF16)] * 2
                  + [jax.ShapeDtypeStruct((B, nt, FOX_PAD, tf), BF16),
                     jax.ShapeDtypeStruct((B * nt * STAT_ROWS, LANES), F32),
                     jax.ShapeDtypeStruct((T, POOL_WIDTH), BF16)],
        scratch_shapes=[pltpu.VMEM((1, LANES), F32), pltpu.VMEM((tf + HALO, POOL_WIDTH), F32)],
        compiler_params=_cparams(("arbitrary", "arbitrary")),
        name="fgate_pool",
    )(zq, zk, zv, zg, pq, pk, cq, ck, cv, sp, zg, w_pool, s_pool)


def _attn_kernel(ff_ref, fl_ref, qn_ref, kn_ref, q_ref, k_ref, vt_ref, o_ref, m_sc, acc_sc, s_sc,
                 *, blk, nq, nb):
    h = pl.program_id(0)
    i = pl.program_id(1)

    def scores(slot, j):
        off = pl.multiple_of(j * blk, blk)
        for b in range(nb):
            s_sc[slot, b] = _dot_nt(k_ref[b, pl.ds(off, blk), :], q_ref[b])

    def softmax_pv(slot, j, diag):
        for b in range(nb):
            s = s_sc[slot, b]
            if diag:
                r = lax.broadcasted_iota(jnp.int32, (blk, blk), 0)
                c = lax.broadcasted_iota(jnp.int32, (blk, blk), 1)
                s = jnp.where(r <= c, s, NEG_BIG)
            m_prev = m_sc[b]
            m_new = jnp.maximum(m_prev, jnp.max(s, axis=0, keepdims=True))
            alpha = jnp.exp(m_prev - m_new)
            p = jnp.exp(s - m_new)
            acc_sc[b] = alpha * acc_sc[b] + _dot(vt_ref[b, j], p.astype(BF16))
            m_sc[b] = m_new

    m_sc[...] = jnp.full_like(m_sc, NEG_BIG)
    acc_sc[...] = jnp.zeros_like(acc_sc)
    scores(0, i)
    scores(1, jnp.maximum(i - 1, 0))
    softmax_pv(0, i, True)

    n = jnp.int32(0)
    for b in range(nb):
        base = (b * FOX_HEADS + h) * nq
        slack = (qn_ref[base + i] * kn_ref[b * FOX_HEADS + h] + ff_ref[base + i]
                 - jnp.min(m_sc[b]) + PRUNE_MARGIN)

        def cond(t, base=base, slack=slack):
            return jnp.logical_and(t < i, slack - fl_ref[base + jnp.maximum(i - 1 - t, 0)] >= 0.0)

        n = jnp.maximum(n, lax.while_loop(cond, lambda t: t + 1, jnp.int32(0)))

    def pair(u, carry):
        t = 1 + 2 * u
        ja = i - t
        scores(0, jnp.maximum(ja - 1, 0))
        softmax_pv(1, ja, False)

        @pl.when(t + 1 <= n)
        def _():
            scores(1, jnp.maximum(ja - 2, 0))
            softmax_pv(0, ja - 1, False)

        return carry

    lax.fori_loop(0, lax.shift_right_logical(n + 1, 1), pair, 0)
    for b in range(nb):
        acc = acc_sc[b]
        o_ref[b] = (acc / acc[BIAS_LANE:BIAS_LANE + 1, :]).T.astype(o_ref.dtype)


def _attention(q2, k2, vt, stats, B, S, blk):
    T = q2.shape[0]
    nq = S // blk
    H = FOX_HEADS
    st = stats.reshape(B, nq, STAT_ROWS, LANES)[:, :, :, :H]
    tab = lambda r: jnp.transpose(st[:, :, r, :], (0, 2, 1)).reshape(-1)
    ffirst, flast = tab(0), tab(1)
    qn = jnp.sqrt(tab(2)) * NORM_SLACK
    kn = jnp.sqrt(jnp.max(st[:, :, 3, :], axis=1)).reshape(-1) * NORM_SLACK
    r3 = lambda a: a.reshape(B, S, FOX_PAD)
    qspec = pl.BlockSpec((B, blk, LANES), lambda h, i, *_: (0, i, h))
    kspec = pl.BlockSpec((B, S, LANES), lambda h, i, *_: (0, 0, h))
    vtspec = pl.BlockSpec((B, nq, LANES, blk), lambda h, i, *_: (0, 0, h, 0))
    grid_spec = pltpu.PrefetchScalarGridSpec(
        num_scalar_prefetch=4,
        grid=(H, nq),
        in_specs=[qspec, kspec, vtspec],
        out_specs=qspec,
        scratch_shapes=[pltpu.VMEM((B, 1, blk), F32), pltpu.VMEM((B, LANES, blk), F32),
                        pltpu.VMEM((2, B, blk, blk), F32)],
    )
    out = pl.pallas_call(
        functools.partial(_attn_kernel, blk=blk, nq=nq, nb=B),
        grid_spec=grid_spec,
        out_shape=jax.ShapeDtypeStruct((B, S, FOX_PAD), BF16),
        compiler_params=_cparams(("arbitrary", "arbitrary")),
        name="fox_attention",
    )(ffirst, flast, qn, kn, r3(q2), r3(k2), vt)
    return out.reshape(T, FOX_PAD)


def _gla_kernel(q_ref, k_ref, v_ref, r_ref, a_ref, wa_ref, ba_ref, g_ref, o_ref, st_sc, *, tg, nb):
    @pl.when(pl.program_id(0) == 0)
    def _():
        st_sc[...] = jnp.zeros_like(st_sc)

    C = GLA_CHUNK
    nchunk = tg // C
    r = lax.broadcasted_iota(jnp.int32, (tg, tg), 0)
    c = lax.broadcasted_iota(jnp.int32, (tg, tg), 1)
    shift = C.bit_length() - 1
    tri = ((c <= r) & ((c >> shift) == (r >> shift))).astype(BF16)
    rc = lax.broadcasted_iota(jnp.int32, (C, C), 0)
    cc = lax.broadcasted_iota(jnp.int32, (C, C), 1)
    causal = cc <= rc
    lane = lax.broadcasted_iota(jnp.int32, (1, LANES), 1)
    vmask = (lane < GLA_DV).astype(F32)
    for bi in range(nb):
        la = (_log_sigmoid(_dot(a_ref[bi].astype(BF16), wa_ref[...]) + ba_ref[...])
              * (1.0 / GLA_GATE_TAU))
        hi, mid, lo = _split3(la)
        b = _dot(tri, hi) + _dot(tri, mid) + _dot(tri, lo)
        eb = jnp.exp(b)
        q_in = q_ref[bi] * (GLA_DK ** -0.5) * eb
        k_in = k_ref[bi] * jnp.exp(-b)
        v = v_ref[bi]
        outs = []
        for ci in range(nchunk):
            rows = slice(ci * C, (ci + 1) * C)
            b_last = b[ci * C + C - 1:ci * C + C, :]
            k_out = k_ref[bi, rows, :] * jnp.exp(b_last - b[rows, :])
            dec = jnp.exp(b_last)
            heads = []
            for h in range(GLA_HEADS):
                kl = slice(h * GLA_KSLOT, (h + 1) * GLA_KSLOT)
                ln = slice(h * LANES, (h + 1) * LANES)
                qh = q_in[rows, kl].astype(BF16)
                kh = k_in[rows, kl].astype(BF16)
                vh = v[rows, ln].astype(BF16)
                attn = jnp.where(causal, _dot_nt(qh, kh), 0.0)
                st = st_sc[bi * GLA_HEADS + h]
                o = _dot(attn.astype(BF16), vh) + _dot_nt(qh, st.astype(BF16))
                kv_t = _dot_tn(vh, k_out[:, kl].astype(BF16))
                st_sc[bi * GLA_HEADS + h] = st * dec[:, kl] + kv_t
                ms = jnp.sum(o * o, axis=-1, keepdims=True) * (1.0 / GLA_DV)
                heads.append(o * lax.rsqrt(ms + RMS_EPS) * vmask)
            outs.append(jnp.concatenate(heads, axis=1))
        o_all = jnp.concatenate(outs, axis=0)
        gr = r_ref[bi]
        o_ref[bi] = ((o_all * g_ref[...]) * (gr * jax.nn.sigmoid(gr))).astype(o_ref.dtype)


def _gla(zg, wa, ba, gn, B, S, tg):
    T = zg.shape[0]
    nt = S // tg
    zg3 = zg.reshape(B, S, G_WIDTH)
    col = lambda off, w: pl.BlockSpec((B, tg, w), lambda i: (0, i, off // w))
    full = lambda a: pl.BlockSpec(a.shape, lambda i: (0,) * a.ndim)
    out = pl.pallas_call(
        functools.partial(_gla_kernel, tg=tg, nb=B),
        grid=(nt,),
        in_specs=[col(G_OFF_Q, GLA_KPAD), col(G_OFF_K, GLA_KPAD), col(G_OFF_V, GLA_PAD),
                  col(G_OFF_R, GLA_PAD), col(G_OFF_A, LANES), full(wa), full(ba), full(gn)],
        out_specs=pl.BlockSpec((B, tg, GLA_PAD), lambda i: (0, i, 0)),
        out_shape=jax.ShapeDtypeStruct((B, S, GLA_PAD), BF16),
        scratch_shapes=[pltpu.VMEM((B * GLA_HEADS, LANES, GLA_KSLOT), F32)],
        compiler_params=_cparams(("arbitrary",)),
        name="gla",
    )(zg3, zg3, zg3, zg3, zg3, wa, ba, gn)
    return out.reshape(T, GLA_PAD)


HALO = max(POOL_WINDOWS)


def _pool_kernel(u_ref, w_ref, s_ref, o_ref, xx):
    tp = u_ref.shape[0]
    i = pl.program_id(1)

    @pl.when(i == 0)
    def _():
        xx[0:HALO, :] = jnp.zeros((HALO, POOL_WIDTH), F32)

    @pl.when(i > 0)
    def _():
        xx[0:HALO, :] = xx[tp:tp + HALO, :]

    u = u_ref[...]
    xx[HALO:HALO + tp, :] = u
    lane = lax.broadcasted_iota(jnp.int32, (1, POOL_WIDTH), 1)
    grp = lane >> (POOL_GROUP.bit_length() - 1)
    pos = lax.broadcasted_iota(jnp.int32, (tp, 1), 0) + i * tp + 1
    acc = u
    mean = jnp.zeros_like(u)
    for j in range(1, HALO):
        acc = acc + xx[HALO - j:HALO - j + tp, :]
        w = j + 1
        if w in POOL_WINDOWS:
            g = POOL_WINDOWS.index(w)
            inv_cnt = 1.0 / jnp.minimum(pos, w).astype(F32)
            mean = jnp.where(grp == g, acc * inv_cnt, mean)
    pooled = mean - u
    mixed = _dot(pooled.astype(BF16), w_ref[...])
    o_ref[...] = (mixed * s_ref[...]).astype(o_ref.dtype)


def _layer_norm(r, g, b):
    mu = jnp.mean(r, axis=-1, keepdims=True)
    d = r - mu
    var = jnp.mean(d * d, axis=-1, keepdims=True)
    return d * lax.rsqrt(var + LN_EPS) * g + b


def _outproj_kernel(of_ref, og_ref, op_ref, x_ref, wf_ref, wg_ref, wp_ref, gate_ref, lg_ref, lb_ref,
                    sc_ref, sh_ref, wr_ref, br_ref,
                    x1_ref, h2_ref, e_ref, gt_ref, cnt_ref, *, alpha):
    y = _dot(of_ref[...], wf_ref[...]) + _dot(og_ref[...], wg_ref[...]) + _dot(op_ref[...], wp_ref[...])
    r = alpha * x_ref[...] + (1.0 + gate_ref[0]) * y
    x1 = _layer_norm(r, lg_ref[...], lb_ref[...])
    x1_ref[...] = x1
    h2 = x1 * (1.0 + sc_ref[0]) + sh_ref[0]
    h2_ref[...] = h2
    h_hi = h2.astype(BF16)
    h_lo = (h2 - h_hi.astype(F32)).astype(BF16)
    logits = (_dot(h_hi, wr_ref[0]) + (_dot(h_hi, wr_ref[1]) + _dot(h_lo, wr_ref[0]))
              + br_ref[...])
    tm = logits.shape[0]
    lane_i = lax.broadcasted_iota(jnp.int32, (tm, LANES), 1)
    lane = lane_i.astype(F32)
    work = logits
    tops, idxs = [], []
    onehot = jnp.zeros((tm, LANES), F32)
    for _ in range(TOP_K):
        m = jnp.max(work, axis=-1, keepdims=True)
        idx = jnp.min(jnp.where(work == m, lane, float(LANES)), axis=-1, keepdims=True)
        sel = lane == idx
        onehot = onehot + sel.astype(F32)
        work = jnp.where(sel, -jnp.inf, work)
        tops.append(m)
        idxs.append(idx)
    ex = [jnp.exp(t - tops[0]) for t in tops]
    den = ex[0] + ex[1] + ex[2] + ex[3]
    e_out = jnp.zeros((tm, LANES), jnp.int32)
    g_out = jnp.zeros((tm, LANES), F32)
    for k in range(TOP_K):
        e_out = jnp.where(lane_i == k, idxs[k].astype(jnp.int32), e_out)
        g_out = jnp.where(lane_i == k, ex[k] / den, g_out)
    e_ref[...] = e_out
    gt_ref[...] = g_out
    for u in range(tm // ROUTE_TILE):
        rows = slice(u * ROUTE_TILE, (u + 1) * ROUTE_TILE)
        cnt_ref[u] = jnp.sum(onehot[rows], axis=0, keepdims=True).astype(jnp.int32)


def _outproj(o_fox, o_gla, o_pool, x2d, wf, wg, wp, gate1, ln_g, ln_b, scale2, shift2, wr, br,
             B, S, tm, alpha):
    T, D = x2d.shape
    nt = S // tm
    full = lambda a: pl.BlockSpec(a.shape, lambda i: (0,) * a.ndim)
    row = lambda w: pl.BlockSpec((tm, w), lambda i: (i, 0))
    mod = pl.BlockSpec((1, 1, D), lambda i: (i // nt, 0, 0))
    return pl.pallas_call(
        functools.partial(_outproj_kernel, alpha=alpha),
        grid=(T // tm,),
        in_specs=[row(FOX_PAD), row(GLA_PAD), row(POOL_WIDTH), row(D), full(wf), full(wg), full(wp),
                  mod, full(ln_g), full(ln_b), mod, mod, full(wr), full(br)],
        out_specs=[row(D), row(D), row(LANES), row(LANES),
                   pl.BlockSpec((tm // ROUTE_TILE, 1, LANES), lambda i: (i, 0, 0))],
        out_shape=[jax.ShapeDtypeStruct((T, D), F32), jax.ShapeDtypeStruct((T, D), F32),
                   jax.ShapeDtypeStruct((T, LANES), jnp.int32), jax.ShapeDtypeStruct((T, LANES), F32),
                   jax.ShapeDtypeStruct((T // ROUTE_TILE, 1, LANES), jnp.int32)],
        compiler_params=_cparams(("arbitrary",)),
        name="outproj_router",
    )(o_fox, o_gla, o_pool, x2d, wf, wg, wp, gate1, ln_g, ln_b, scale2, shift2, wr, br)


def _tile_slots(e_i32, off_row):
    td = e_i32.shape[0]
    lane = lax.broadcasted_iota(jnp.int32, (td, LANES), 1).astype(F32)
    ef = e_i32.astype(F32)
    sel = [lane == ef[:, k:k + 1] for k in range(TOP_K)]
    onehot = sel[0].astype(F32)
    for k in range(1, TOP_K):
        onehot = onehot + sel[k].astype(F32)
    rr = lax.broadcasted_iota(jnp.int32, (td, td), 0)
    cc = lax.broadcasted_iota(jnp.int32, (td, td), 1)
    stril = (cc < rr).astype(BF16)
    tab = _dot(stril, onehot.astype(BF16)) + off_row
    return [jnp.sum(jnp.where(sel[k], tab, 0.0), axis=-1, keepdims=True) for k in range(TOP_K)]


def _group_bits(max_groups):
    return [1 << s for s in range(max_groups.bit_length() - 1, -1, -1)]


PIECE_CLASSES = (ROUTE_TILE // ROW_GROUP).bit_length()
PIECE_LIST = 1024
SLOT_BITS = 8
assert ROUTE_SLOTS // ROW_GROUP < (1 << SLOT_BITS) and PIECE_CLASSES * N_EXPERTS <= PIECE_LIST


def _piece_lists(m, off, dst):
    nt = m.shape[0]
    j = jnp.arange(N_EXPERTS, dtype=jnp.int32)
    b = jnp.arange(PIECE_CLASSES, dtype=jnp.int32)[None, :, None]
    mm, oo, dd = m[:, None, :], off[:, None, :], dst[:, None, :]
    has = (mm >> b) & 1
    before = (mm & ~((2 << b) - 1)) * ROW_GROUP
    packed = (((dd + before) // ROW_GROUP) << SLOT_BITS) | ((oo + before) // ROW_GROUP)
    rank = _excl_prefix(has, 2)
    hit = (has[..., None] == 1) & (rank[..., None] == j)
    lst = jnp.sum(jnp.where(hit, packed[..., None], 0), axis=2)
    lst = _pad_last(lst.reshape(nt, PIECE_CLASSES * N_EXPERTS).astype(jnp.int32), PIECE_LIST)
    return lst.reshape(-1), jnp.sum(has, axis=2).astype(jnp.int32).reshape(-1)


def _for_each_piece(tile, cnt_ref, list_ref, fn, base=0):
    for b in range(PIECE_CLASSES):
        def body(i, carry, b=b):
            v = list_ref[base + b * N_EXPERTS + i]
            fn(pl.multiple_of((v & ((1 << SLOT_BITS) - 1)) * ROW_GROUP, ROW_GROUP),
               pl.multiple_of(lax.shift_right_logical(v, SLOT_BITS) * ROW_GROUP, ROW_GROUP),
               (1 << b) * ROW_GROUP)
            return carry

        lax.fori_loop(0, cnt_ref[tile * PIECE_CLASSES + b], body, 0)


def _for_each_total_piece(total_groups, fn):
    for bit in _group_bits(ROUTE_SLOTS // ROW_GROUP):
        @pl.when((total_groups & bit) != 0)
        def _(bit=bit):
            fn(bit * ROW_GROUP)


def _dispatch_kernel(cnt_ref, tot_ref, zblk_ref, nu_ref, list_ref, e_ref, offrow_ref, h_ref,
                     xr_ref, slot_ref, sort_sc, zero_sc, sems, zsem, tsem, *, n_blocks, nt):
    step = pl.program_id(0)

    def zcopy(blk, sem):
        return pltpu.make_async_copy(
            zero_sc, xr_ref.at[pl.ds(pl.multiple_of(blk * MOE_BLOCK, MOE_BLOCK), MOE_BLOCK), :], sem)

    @pl.when(step == 0)
    def _():
        zero_sc[...] = jnp.zeros_like(zero_sc)

        def zstart(e, c):
            @pl.when(zblk_ref[e] >= 0)
            def _():
                zcopy(zblk_ref[e], zsem).start()
            return c

        def zwait(e, c):
            @pl.when(zblk_ref[e] >= 0)
            def _():
                zcopy(0, zsem).wait()
            return c

        def tstart(blk, c):
            zcopy(blk, tsem).start()
            return c

        lax.fori_loop(0, N_EXPERTS, zstart, 0)
        lax.fori_loop(nu_ref[0], n_blocks, tstart, 0)
        lax.fori_loop(0, N_EXPERTS, zwait, 0)

    td = ROUTE_TILE
    par = step & 1
    for u in range(TILES_PER_STEP):
        rows = slice(u * td, (u + 1) * td)
        slots = _tile_slots(e_ref[rows, :], offrow_ref[u])
        lane = lax.broadcasted_iota(jnp.int32, (td, LANES), 1)
        cols = jnp.full((td, LANES), -1.0, F32)
        for k in range(TOP_K):
            cols = jnp.where(lane == k, slots[k], cols)
        slot_ref[rows, :] = cols
        rows_t = cols.T
        sub = lax.broadcasted_iota(jnp.int32, (ROUTE_SLOTS, td), 0).astype(F32)
        pick = sub == rows_t[0:1, :]
        for k in range(1, TOP_K):
            pick = pick | (sub == rows_t[k:k + 1, :])
        sort_sc[par * TILES_PER_STEP + u] = _dot(pick.astype(BF16), h_ref[rows, :].astype(BF16))

    for u in range(TILES_PER_STEP):
        buf = par * TILES_PER_STEP + u

        def start_chunk(slot0, row0, n, buf=buf, u=u):
            pltpu.make_async_copy(sort_sc.at[buf, pl.ds(slot0, n), :], xr_ref.at[pl.ds(row0, n), :],
                                  sems.at[par, u]).start()

        _for_each_piece(step * TILES_PER_STEP + u, cnt_ref, list_ref, start_chunk, base=u * PIECE_LIST)

    def wait_step(s, p):
        for u in range(TILES_PER_STEP):
            def wait_piece(n, u=u):
                pltpu.make_async_copy(sort_sc.at[p * TILES_PER_STEP + u, pl.ds(0, n), :],
                                      xr_ref.at[pl.ds(0, n), :], sems.at[p, u]).wait()
            _for_each_total_piece(tot_ref[s * TILES_PER_STEP + u], wait_piece)

    @pl.when(step > 0)
    def _():
        wait_step(step - 1, 1 - par)

    @pl.when(step == nt - 1)
    def _():
        wait_step(step, par)

        def twait(blk, c):
            zcopy(0, tsem).wait()
            return c

        lax.fori_loop(nu_ref[0], n_blocks, twait, 0)


def _dispatch(h2, e_pad, tables, rows):
    T, D = h2.shape
    tpn = TILES_PER_STEP
    td = ROUTE_TILE * tpn
    assert T % td == 0
    nt = T // td
    piece_list, piece_cnt, tot_tab, zblk, n_used, off_rows = tables
    grid_spec = pltpu.PrefetchScalarGridSpec(
        num_scalar_prefetch=4,
        grid=(nt,),
        in_specs=[pl.BlockSpec((tpn * PIECE_LIST,), lambda i, *_: (i,), memory_space=pltpu.SMEM),
                  pl.BlockSpec((td, LANES), lambda i, *_: (i, 0)),
                  pl.BlockSpec((tpn, 1, LANES), lambda i, *_: (i, 0, 0)),
                  pl.BlockSpec((td, D), lambda i, *_: (i, 0))],
        out_specs=[pl.BlockSpec(memory_space=pl.ANY),
                   pl.BlockSpec((td, LANES), lambda i, *_: (i, 0))],
        scratch_shapes=[pltpu.VMEM((2 * tpn, ROUTE_SLOTS, D), F32), pltpu.VMEM((MOE_BLOCK, D), F32),
                        pltpu.SemaphoreType.DMA((2, tpn)), pltpu.SemaphoreType.DMA, pltpu.SemaphoreType.DMA],
    )
    return pl.pallas_call(
        functools.partial(_dispatch_kernel, n_blocks=rows // MOE_BLOCK, nt=nt),
        grid_spec=grid_spec,
        out_shape=[jax.ShapeDtypeStruct((rows, D), F32), jax.ShapeDtypeStruct((T, LANES), F32)],
        compiler_params=_cparams(("arbitrary",)),
        name="moe_dispatch",
    )(piece_cnt, tot_tab, zblk, n_used, piece_list, e_pad, off_rows, h2)


def _expert_kernel(be_ref, nu_ref, vb_ref, par_ref, nxt_ref, x_ref, wgu_hbm, bgu_ref, wd_hbm, bd_ref, y_ref,
                   wgu_sc, wd_sc, gu_buf, d_buf, sems, *, first_expert):
    i = pl.program_id(0)
    used = i < nu_ref[0]

    def fetch(e, slot):
        return (pltpu.make_async_copy(wgu_hbm.at[first_expert + e], gu_buf.at[slot], sems.at[slot, 0]),
                pltpu.make_async_copy(wd_hbm.at[first_expert + e], d_buf.at[slot], sems.at[slot, 1]))

    @pl.when(used)
    def _():
        prev = be_ref[jnp.maximum(i - 1, 0)]

        @pl.when((i == 0) | (be_ref[i] != prev))
        def _():
            slot = par_ref[i]

            @pl.when(i == 0)
            def _():
                for cp in fetch(be_ref[i], slot):
                    cp.start()

            for cp in fetch(be_ref[i], slot):
                cp.wait()
            wgu_sc[...] = gu_buf[slot].astype(BF16)
            wd_sc[...] = d_buf[slot].astype(BF16)

            @pl.when(nxt_ref[i] >= 0)
            def _():
                for cp in fetch(nxt_ref[i], 1 - slot):
                    cp.start()

        def mlp(rows):
            x = x_ref[rows, :].astype(BF16)
            gu = _dot(x, wgu_sc[...]) + bgu_ref[0]
            glu = jnp.minimum(gu[:, :D_EXPERT], SWIGLU_LIMIT)
            lin = jnp.clip(gu[:, D_EXPERT:], -SWIGLU_LIMIT, SWIGLU_LIMIT)
            act = glu * jax.nn.sigmoid(SWIGLU_ALPHA * glu) * (lin + 1.0)
            y_ref[rows, :] = _dot(act.astype(BF16), wd_sc[...]) + bd_ref[0]

        half = MOE_BLOCK // 2
        full_block = vb_ref[i] > half

        @pl.when(full_block)
        def _():
            mlp(slice(0, MOE_BLOCK))

        @pl.when(jnp.logical_not(full_block))
        def _():
            mlp(slice(0, half))
            y_ref[half:, :] = jnp.zeros((MOE_BLOCK - half, y_ref.shape[1]), y_ref.dtype)

    @pl.when(jnp.logical_not(used))
    def _():
        y_ref[...] = jnp.zeros_like(y_ref)


def _experts(x_rows, block_expert, n_used, valid_rows, slot_parity, next_expert,
             w_gate_up, b_gate_up, w_down, b_down, layer):
    rows, D = x_rows.shape
    nb = rows // MOE_BLOCK
    E = w_gate_up.shape[1]
    grid_spec = pltpu.PrefetchScalarGridSpec(
        num_scalar_prefetch=5,
        grid=(nb,),
        in_specs=[pl.BlockSpec((MOE_BLOCK, D), lambda i, be, nu, *_: (jnp.minimum(i, nu[0] - 1), 0)),
                  pl.BlockSpec(memory_space=pl.ANY),
                  pl.BlockSpec((1, 1, 2 * D_EXPERT), lambda i, be, *_: (layer * E + be[i], 0, 0)),
                  pl.BlockSpec(memory_space=pl.ANY),
                  pl.BlockSpec((1, 1, D), lambda i, be, *_: (layer * E + be[i], 0, 0))],
        out_specs=pl.BlockSpec((MOE_BLOCK, D), lambda i, *_: (i, 0)),
        scratch_shapes=[pltpu.VMEM((D, 2 * D_EXPERT), BF16), pltpu.VMEM((D_EXPERT, D), BF16),
                        pltpu.VMEM((2, D, 2 * D_EXPERT), F32), pltpu.VMEM((2, D_EXPERT, D), F32),
                        pltpu.SemaphoreType.DMA((2, 2))],
    )
    L = w_gate_up.shape[0]
    return pl.pallas_call(
        functools.partial(_expert_kernel, first_expert=layer * E),
        grid_spec=grid_spec,
        out_shape=jax.ShapeDtypeStruct((rows, D), F32),
        compiler_params=_cparams(("arbitrary",)),
        name="moe_experts",
    )(block_expert, n_used, valid_rows, slot_parity, next_expert, x_rows,
      w_gate_up.reshape(L * E, D, 2 * D_EXPERT), b_gate_up.reshape(L * E, 1, 2 * D_EXPERT),
      w_down.reshape(L * E, D_EXPERT, D), b_down.reshape(L * E, 1, D))


def _combine_kernel(cnt_ref, tot_ref, list_ref, next_list_ref, slot_ref, gt_ref, y_ref, x_ref, gate_ref,
                    lg_ref, lb_ref, o_ref, ybuf, sems, *, nt, alpha):
    step = pl.program_id(0)
    par = step & 1
    tpn = TILES_PER_STEP

    def fetch(s, p, lst):
        for u in range(tpn):
            def start_chunk(slot0, row0, n, u=u):
                pltpu.make_async_copy(y_ref.at[pl.ds(row0, n), :], ybuf.at[p * tpn + u, pl.ds(slot0, n), :],
                                      sems.at[p, u]).start()
            _for_each_piece(s * tpn + u, cnt_ref, lst, start_chunk, base=u * PIECE_LIST)

    @pl.when(step == 0)
    def _():
        ybuf[...] = jnp.zeros_like(ybuf)
        fetch(step, par, list_ref)

    @pl.when(step + 1 < nt)
    def _():
        fetch(step + 1, 1 - par, next_list_ref)

    for u in range(tpn):
        def wait_piece(n, u=u):
            pltpu.make_async_copy(y_ref.at[pl.ds(0, n), :], ybuf.at[par * tpn + u, pl.ds(0, n), :],
                                  sems.at[par, u]).wait()
        _for_each_total_piece(tot_ref[step * tpn + u], wait_piece)

    td = ROUTE_TILE
    for u in range(tpn):
        rows = slice(u * td, (u + 1) * td)
        slots = slot_ref[rows, :]
        gt = gt_ref[rows, :]
        lane = lax.broadcasted_iota(jnp.int32, (td, ROUTE_SLOTS), 1).astype(F32)
        w = jnp.where(lane == slots[:, 0:1], gt[:, 0:1], 0.0)
        for k in range(1, TOP_K):
            w = w + jnp.where(lane == slots[:, k:k + 1], gt[:, k:k + 1], 0.0)
        y = _dot(w.astype(BF16), ybuf[par * tpn + u].astype(BF16))
        r = alpha * x_ref[rows, :] + (1.0 + gate_ref[0]) * y
        o_ref[rows, :] = _layer_norm(r, lg_ref[...], lb_ref[...])


def _combine(y_rows, slots, tables, gates, x1, gate2, ln_g, ln_b, B, S, alpha):
    T, D = x1.shape
    tpn = TILES_PER_STEP
    td = ROUTE_TILE * tpn
    assert S % td == 0
    nt = T // td
    per_batch = S // td
    piece_list, piece_cnt, tot_tab, _, _, _ = tables
    grid_spec = pltpu.PrefetchScalarGridSpec(
        num_scalar_prefetch=2,
        grid=(nt,),
        in_specs=[pl.BlockSpec((tpn * PIECE_LIST,), lambda i, *_: (i,), memory_space=pltpu.SMEM),
                  pl.BlockSpec((tpn * PIECE_LIST,), lambda i, *_: (jnp.minimum(i + 1, nt - 1),),
                               memory_space=pltpu.SMEM),
                  pl.BlockSpec((td, LANES), lambda i, *_: (i, 0)),
                  pl.BlockSpec((td, LANES), lambda i, *_: (i, 0)),
                  pl.BlockSpec(memory_space=pl.ANY),
                  pl.BlockSpec((td, D), lambda i, *_: (i, 0)),
                  pl.BlockSpec((1, 1, D), lambda i, *_: (i // per_batch, 0, 0)),
                  pl.BlockSpec((1, D), lambda i, *_: (0, 0)),
                  pl.BlockSpec((1, D), lambda i, *_: (0, 0))],
        out_specs=pl.BlockSpec((td, D), lambda i, *_: (i, 0)),
        scratch_shapes=[pltpu.VMEM((2 * tpn, ROUTE_SLOTS, D), F32), pltpu.SemaphoreType.DMA((2, tpn))],
    )
    return pl.pallas_call(
        functools.partial(_combine_kernel, nt=nt, alpha=alpha),
        grid_spec=grid_spec,
        out_shape=jax.ShapeDtypeStruct((T, D), F32),
        compiler_params=_cparams(("arbitrary",)),
        name="moe_combine",
    )(piece_cnt, tot_tab, piece_list, piece_list, slots, gates, y_rows, x1, gate2, ln_g, ln_b)


def _tile(n, pref):
    t = min(n, pref)
    assert n % t == 0, (n, t)
    return t


def kernel(x, c, w_ada, b_ada, w_in, b_in, gla_w_a2, gla_b_a, gla_norm_g, pool_w, pool_scale, w_out,
           ln1_g, ln1_b, w_router, b_router, w_gate_up, b_gate_up, w_down, b_down, ln2_g, ln2_b):
    B, S, D = x.shape
    L = w_ada.shape[0]
    T = B * S
    assert D == D_MODEL and S % GLA_CHUNK == 0
    alpha = float((2 * L) ** 0.25)
    assert T % ROUTE_TILE == 0
    n_tiles = T // ROUTE_TILE
    max_rows = T * TOP_K + n_tiles * N_EXPERTS * (ROW_GROUP - 1) + N_EXPERTS * (MOE_BLOCK - 1)
    n_blocks = -(-max_rows // MOE_BLOCK)
    rows = n_blocks * MOE_BLOCK

    mod = _ada_mod(c, w_ada, b_ada)
    prep_all = _prep_inproj(w_in, b_in)
    wa_all = jnp.pad(_pad_heads(gla_w_a2, GLA_HEADS, GLA_DK, GLA_KSLOT),
                     ((0, 0), (0, LANES - GLA_GATE_RANK), (0, 0))).astype(BF16)
    ba_all = _pad_heads(gla_b_a[:, None, :], GLA_HEADS, GLA_DK, GLA_KSLOT)
    gn_all = _pad_heads(gla_norm_g[:, None, :], GLA_HEADS, GLA_DV)
    w_bd_all = jnp.zeros((L, POOL_WIDTH, POOL_WIDTH), F32)
    for g in range(len(POOL_WINDOWS)):
        sl = slice(g * POOL_GROUP, (g + 1) * POOL_GROUP)
        w_bd_all = w_bd_all.at[:, sl, sl].set(pool_w[:, g])
    w_bd_all = w_bd_all.astype(BF16)
    pad_rows = lambda w, heads, dim: jnp.swapaxes(_pad_heads(jnp.swapaxes(w, 1, 2), heads, dim), 1, 2)
    wf_all = pad_rows(w_out[:, :FOX_WIDTH], FOX_HEADS, FOX_HEAD_DIM).astype(BF16)
    wgl_all = pad_rows(w_out[:, FOX_WIDTH:FOX_WIDTH + GLA_VWIDTH], GLA_HEADS, GLA_DV).astype(BF16)
    wp_all = w_out[:, FOX_WIDTH + GLA_VWIDTH:].astype(BF16)
    wr_f32 = _pad_last(w_router, LANES)
    wr_hi = wr_f32.astype(BF16)
    wr_all = jnp.stack([wr_hi, (wr_f32 - wr_hi.astype(F32)).astype(BF16)], axis=1)
    br_all = jnp.pad(b_router[:, None, :], ((0, 0), (0, 0), (0, LANES - N_EXPERTS)), constant_values=NEG_BIG)

    x2d = x.reshape(T, D)
    for l in range(L):
        shift1, scale1, gate1, shift2, scale2, gate2 = [mod[l, :, m] for m in range(N_MOD)]
        prep = [(w[l], b[l]) for w, b in prep_all]
        zq, zk, zv, zg = _inproj(x2d, scale1, shift1, prep, B, S, _tile(S, 512))
        q2, k2, vt, stats, o_pool = _fgate(zq, zk, zv, zg, w_bd_all[l], pool_scale[l][None, :],
                                           B, S, _tile(S, ATTN_BLOCK))
        o_fox = _attention(q2, k2, vt, stats, B, S, _tile(S, ATTN_BLOCK))
        o_gla = _gla(zg, wa_all[l], ba_all[l], gn_all[l], B, S, _tile(S, 256))
        x1, h2, e_pad, g_pad, cnt = _outproj(
            o_fox, o_gla, o_pool, x2d, wf_all[l], wgl_all[l], wp_all[l], gate1,
            ln1_g[l][None, :], ln1_b[l][None, :], scale2, shift2, wr_all[l], br_all[l],
            B, S, _tile(S, 512), alpha)
        cte = cnt[:, 0, :N_EXPERTS]
        c8 = (cte + ROW_GROUP - 1) // ROW_GROUP * ROW_GROUP
        off = _excl_prefix(c8, 1)
        per_expert = jnp.sum(c8, axis=0)
        padded = (per_expert + MOE_BLOCK - 1) // MOE_BLOCK * MOE_BLOCK
        pstart = _excl_prefix(padded, 0)
        pend = pstart + padded
        dst = pstart[None, :] + _excl_prefix(c8, 0)
        n_used = (pend[-1] // MOE_BLOCK).astype(jnp.int32)
        blk_start = jnp.arange(n_blocks, dtype=jnp.int32) * MOE_BLOCK
        be = jnp.minimum(jnp.sum(blk_start[:, None] >= pend[None, :], axis=1), N_EXPERTS - 1).astype(jnp.int32)
        last_used = jnp.minimum(jnp.sum(MOE_BLOCK * (n_used - 1) >= pend), N_EXPERTS - 1).astype(jnp.int32)
        be = jnp.where(jnp.arange(n_blocks) < n_used, be, last_used)
        zblk = jnp.where(padded > 0, pend // MOE_BLOCK - 1, -1).astype(jnp.int32)
        off_rows = _pad_last(off.astype(F32), LANES)[:, None, :]
        n_used = n_used.reshape(1)
        piece_list, piece_cnt = _piece_lists(c8 // ROW_GROUP, off, dst)
        tables = (piece_list, piece_cnt, (jnp.sum(c8, axis=1) // ROW_GROUP).astype(jnp.int32),
                  zblk, n_used, off_rows)
        x_rows, slots = _dispatch(h2, e_pad, tables, rows)
        owns = padded > 0
        ids = jnp.arange(N_EXPERTS, dtype=jnp.int32)
        later = jnp.where(owns[None, :] & (ids[None, :] > ids[:, None]), ids[None, :], N_EXPERTS)
        nxt_e = jnp.min(later, axis=1)
        nxt_e = jnp.where(nxt_e < N_EXPERTS, nxt_e, -1).astype(jnp.int32)
        par_e = (_excl_prefix(owns.astype(jnp.int32), 0) & 1).astype(jnp.int32)
        pick = be[:, None] == ids[None, :]
        lookup = lambda tab: jnp.sum(jnp.where(pick, tab[None, :], 0), axis=1).astype(jnp.int32)
        valid_rows = jnp.clip(lookup(per_expert) - (blk_start - lookup(pstart)), 0, MOE_BLOCK).astype(jnp.int32)
        y_rows = _experts(x_rows, be, n_used, valid_rows, lookup(par_e), lookup(nxt_e),
                          w_gate_up, b_gate_up, w_down, b_down, l)
        x2d = _combine(y_rows, slots, tables, g_pad, x1, gate2,
                       ln2_g[l][None, :], ln2_b[l][None, :], B, S, alpha)
    return x2d.reshape(B, S, D)
```

```python
import functools

import numpy as np
import jax
import jax.numpy as jnp
from jax import lax
from jax.experimental import pallas as pl
from jax.experimental.pallas import tpu as pltpu

F32 = jnp.float32
BF16 = jnp.bfloat16

D_MODEL = 1024
FOX_HEADS = 6
FOX_HEAD_DIM = 64
FOX_WIDTH = FOX_HEADS * FOX_HEAD_DIM
GLA_HEADS = 4
GLA_DV = 96
GLA_DK = 48
GLA_KWIDTH = GLA_HEADS * GLA_DK
GLA_VWIDTH = GLA_HEADS * GLA_DV
GLA_GATE_RANK = 16
GLA_GATE_TAU = 16.0
GLA_CHUNK = 64
POOL_WINDOWS = (2, 4, 8, 16)
POOL_GROUP = 64
POOL_WIDTH = len(POOL_WINDOWS) * POOL_GROUP
N_EXPERTS = 32
TOP_K = 4
D_EXPERT = 1024
SWIGLU_ALPHA = 1.702
SWIGLU_LIMIT = 7.0
N_MOD = 6
LN_EPS = 1e-5
RMS_EPS = 1e-6

LANES = 128
VMEM_LIMIT_BYTES = 56 * 1024 * 1024

FOX_PAD = FOX_HEADS * LANES
GLA_PAD = GLA_HEADS * LANES
BIAS_LANE = FOX_HEAD_DIM
GLA_KSLOT = 64
GLA_KPAD = GLA_HEADS * GLA_KSLOT
G_OFF_Q, G_OFF_K = 0, GLA_KPAD
G_OFF_V = 2 * GLA_KPAD
G_OFF_R = G_OFF_V + GLA_PAD
G_OFF_U = G_OFF_R + GLA_PAD
G_OFF_A = G_OFF_U + POOL_WIDTH
G_OFF_F = G_OFF_A + LANES
G_WIDTH = G_OFF_F + LANES

STAT_ROWS = 8
PRUNE_MARGIN = 105.0
NORM_SLACK = 1.01
ATTN_BLOCK = 512
MOE_BLOCK = 512
ROUTE_TILE = 256
ROW_GROUP = 8
ROUTE_SLOTS = ROUTE_TILE * TOP_K + N_EXPERTS * ROW_GROUP
TILES_PER_STEP = 2
NEG_BIG = -1e30


def _cparams(sem, vmem=None):
    return pltpu.CompilerParams(dimension_semantics=sem, vmem_limit_bytes=vmem or VMEM_LIMIT_BYTES)


def _log_sigmoid(x):
    return jnp.minimum(x, 0.0) - jnp.log(1.0 + jnp.exp(-jnp.abs(x)))


def _split3(x):
    hi = x.astype(BF16)
    r = x - hi.astype(F32)
    mid = r.astype(BF16)
    lo = (r - mid.astype(F32)).astype(BF16)
    return hi, mid, lo


def _dot(a, b):
    return jnp.dot(a, b, preferred_element_type=F32)


def _dot_nt(a, b):
    return lax.dot_general(a, b, (((1,), (1,)), ((), ())), preferred_element_type=F32)


def _dot_tn(a, b):
    return lax.dot_general(a, b, (((0,), (0,)), ((), ())), preferred_element_type=F32)


def _ada_kernel(c_ref, w_ref, b_ref, o_ref):
    c = c_ref[...]
    cond = c * jax.nn.sigmoid(c)
    o_ref[0] = jnp.dot(cond, w_ref[0], preferred_element_type=F32,
                       precision=lax.Precision.HIGHEST) + b_ref[0]


def _ada_mod(c, w_ada, b_ada):
    L, D, N = w_ada.shape
    B = c.shape[0]
    rows = 8
    c_pad = jnp.zeros((rows, D), F32).at[:B].set(c)
    tn = 1536
    out = pl.pallas_call(
        _ada_kernel,
        grid=(L, N // tn),
        in_specs=[pl.BlockSpec((rows, D), lambda l, j: (0, 0)),
                  pl.BlockSpec((1, D, tn), lambda l, j: (l, 0, j)),
                  pl.BlockSpec((1, 1, tn), lambda l, j: (l, 0, j))],
        out_specs=pl.BlockSpec((1, rows, tn), lambda l, j: (l, 0, j)),
        out_shape=jax.ShapeDtypeStruct((L, rows, N), F32),
        compiler_params=_cparams(("arbitrary", "arbitrary")),
        name="ada_mod",
    )(c_pad, w_ada, b_ada.reshape(L, 1, N))
    return out[:, :B].reshape(L, B, N_MOD, 1, D)


def _inproj_kernel(x_ref, sc_ref, sh_ref, wq_ref, wk_ref, wv_ref, wg_ref,
                   bq_ref, bk_ref, bv_ref, bg_ref, q_ref, k_ref, v_ref, g_ref):
    h = (x_ref[...] * (1.0 + sc_ref[0]) + sh_ref[0]).astype(BF16)
    q_ref[...] = (_dot(h, wq_ref[...]) + bq_ref[...]).astype(BF16)
    k_ref[...] = (_dot(h, wk_ref[...]) + bk_ref[...]).astype(BF16)
    v_ref[...] = (_dot(h, wv_ref[...]) + bv_ref[...]).astype(BF16)
    g_ref[...] = _dot(h, wg_ref[...]) + bg_ref[...]


def _pad_heads(w, heads, dim, slot=LANES):
    lead = w.shape[:-1]
    w = w.reshape(lead + (heads, dim))
    w = jnp.pad(w, [(0, 0)] * len(lead) + [(0, 0), (0, slot - dim)])
    return w.reshape(lead + (heads * slot,))


def _pad_last(a, width):
    return jnp.pad(a, [(0, 0)] * (a.ndim - 1) + [(0, width - a.shape[-1])])


def _excl_prefix(a, axis):
    n = a.shape[axis]
    idx = jnp.arange(n)
    earlier = idx[None, :] < idx[:, None]
    am = jnp.moveaxis(a, axis, -1)
    out = jnp.sum(jnp.where(earlier, am[..., None, :], 0), axis=-1)
    return jnp.moveaxis(out, -1, axis)


def _prep_inproj(w_in, b_in):
    def relayout(W):
        o = 0
        fq = W[..., o:o + FOX_WIDTH]; o += FOX_WIDTH
        fk = W[..., o:o + FOX_WIDTH]; o += FOX_WIDTH
        fv = W[..., o:o + FOX_WIDTH]; o += FOX_WIDTH
        ff = W[..., o:o + FOX_HEADS]; o += FOX_HEADS
        gq = W[..., o:o + GLA_KWIDTH]; o += GLA_KWIDTH
        gk = W[..., o:o + GLA_KWIDTH]; o += GLA_KWIDTH
        gv = W[..., o:o + GLA_VWIDTH]; o += GLA_VWIDTH
        gr = W[..., o:o + GLA_VWIDTH]; o += GLA_VWIDTH
        ga = W[..., o:o + GLA_GATE_RANK]; o += GLA_GATE_RANK
        pu = W[..., o:o + POOL_WIDTH]
        wq = fq * (FOX_HEAD_DIM ** -0.5)
        wg = jnp.concatenate([
            _pad_heads(gq, GLA_HEADS, GLA_DK, GLA_KSLOT), _pad_heads(gk, GLA_HEADS, GLA_DK, GLA_KSLOT),
            _pad_heads(gv, GLA_HEADS, GLA_DV), _pad_heads(gr, GLA_HEADS, GLA_DV),
            pu, _pad_last(ga, LANES), _pad_last(ff, LANES)], axis=-1)
        return wq, fk, fv, wg

    ws = relayout(w_in.astype(BF16))
    bs = relayout(b_in[:, None, :])
    return list(zip(ws, bs))


def _inproj(x2d, scale, shift, prep, B, S, tm):
    T, D = x2d.shape
    (wq, bq), (wk, bk), (wv, bv), (wg, bg) = prep
    nt = S // tm
    full = lambda a: pl.BlockSpec(a.shape, lambda i: (0,) * a.ndim)
    row = lambda w: pl.BlockSpec((tm, w), lambda i: (i, 0))
    mod = pl.BlockSpec((1, 1, D), lambda i: (i // nt, 0, 0))
    return pl.pallas_call(
        _inproj_kernel,
        grid=(T // tm,),
        in_specs=[row(D), mod, mod, full(wq), full(wk), full(wv), full(wg),
                  full(bq), full(bk), full(bv), full(bg)],
        out_specs=[row(FOX_WIDTH), row(FOX_WIDTH), row(FOX_WIDTH), row(G_WIDTH)],
        out_shape=[jax.ShapeDtypeStruct((T, FOX_WIDTH), BF16)] * 3
                  + [jax.ShapeDtypeStruct((T, G_WIDTH), F32)],
        compiler_params=_cparams(("arbitrary",)),
        name="inproj",
    )(x2d, scale, shift, wq, wk, wv, wg, bq, bk, bv, bg)


def _aug_constants():
    pq = np.zeros((3 * LANES, FOX_PAD), np.float32)
    pk = np.zeros((3 * LANES, FOX_PAD), np.float32)
    cq = np.zeros((1, FOX_PAD), np.float32)
    ck = np.zeros((1, FOX_PAD), np.float32)
    cv = np.zeros((1, FOX_PAD), np.float32)
    spread = np.zeros((FOX_WIDTH, FOX_PAD), np.float32)
    for h in range(FOX_HEADS):
        base = h * LANES + BIAS_LANE
        cv[0, base] = 1.0
        for d in range(FOX_HEAD_DIM):
            spread[h * FOX_HEAD_DIM + d, h * LANES + d] = 1.0
        for p in range(3):
            pq[p * LANES + h, base + p] = 1.0
            pk[p * LANES + h, base + 3 + p] = -1.0
            cq[0, base + 3 + p] = 1.0
            ck[0, base + p] = 1.0
    return pq, pk, cq, ck, cv, spread


def _fgate_kernel(q_ref, k_ref, v_ref, f_ref, pq_ref, pk_ref, cq_ref, ck_ref, cv_ref, sp_ref,
                  u_ref, wpool_ref, spool_ref,
                  q2_ref, k2_ref, vt_ref, st_ref, opool_ref, carry, xx):
    @pl.when(pl.program_id(1) == 0)
    def _():
        carry[...] = jnp.zeros_like(carry)

    _pool_kernel(u_ref, wpool_ref, spool_ref, opool_ref, xx)

    vf = _dot(v_ref[...], sp_ref[...]) + cv_ref[...]
    for h in range(FOX_HEADS):
        sl = slice(h * LANES, (h + 1) * LANES)
        vt_ref[0, 0, sl, :] = vf[:, sl].T.astype(BF16)

    tf = f_ref.shape[0]
    ls = _log_sigmoid(f_ref[...])
    r = lax.broadcasted_iota(jnp.int32, (tf, tf), 0)
    c = lax.broadcasted_iota(jnp.int32, (tf, tf), 1)
    tri = (c <= r).astype(BF16)
    hi, mid, lo = _split3(ls)
    cs = _dot(tri, hi) + _dot(tri, mid) + _dot(tri, lo)
    F = cs + carry[...]
    carry[...] = F[tf - 1:tf, :]
    fh, fm, fl = _split3(F)
    f3 = jnp.concatenate([fh, fm, fl], axis=1)
    qf = _dot(q_ref[...], sp_ref[...])
    kf = _dot(k_ref[...], sp_ref[...])
    q2_ref[...] = (qf + _dot(f3, pq_ref[...]) + cq_ref[...]).astype(BF16)
    k2_ref[...] = (kf + _dot(f3, pk_ref[...]) + ck_ref[...]).astype(BF16)
    lane = lax.broadcasted_iota(jnp.int32, (1, LANES), 1)
    qstat = jnp.zeros((1, LANES), F32)
    kstat = jnp.zeros((1, LANES), F32)
    for h in range(FOX_HEADS):
        sl = slice(h * LANES, (h + 1) * LANES)
        qm = jnp.max(jnp.sum(qf[:, sl] * qf[:, sl], axis=-1, keepdims=True), axis=0, keepdims=True)
        km = jnp.max(jnp.sum(kf[:, sl] * kf[:, sl], axis=-1, keepdims=True), axis=0, keepdims=True)
        qstat = jnp.where(lane == h, qm, qstat)
        kstat = jnp.where(lane == h, km, kstat)
    row = lax.broadcasted_iota(jnp.int32, (STAT_ROWS, LANES), 0)
    st_ref[...] = jnp.where(row == 0, F[0:1, :],
                            jnp.where(row == 1, F[tf - 1:tf, :],
                                      jnp.where(row == 2, qstat, jnp.where(row == 3, kstat, 0.0))))


def _fgate(zq, zk, zv, zg, w_pool, s_pool, B, S, tf):
    T = zq.shape[0]
    nt = S // tf
    pq, pk, cq, ck, cv, sp = _aug_constants()
    pq, pk, sp = jnp.asarray(pq, BF16), jnp.asarray(pk, BF16), jnp.asarray(sp, BF16)
    cq, ck, cv = jnp.asarray(cq), jnp.asarray(ck), jnp.asarray(cv)
    full = lambda a: pl.BlockSpec(a.shape, lambda b, i: (0,) * a.ndim)
    row = pl.BlockSpec((tf, FOX_PAD), lambda b, i: (b * nt + i, 0))
    packed = pl.BlockSpec((tf, FOX_WIDTH), lambda b, i: (b * nt + i, 0))
    return pl.pallas_call(
        _fgate_kernel,
        grid=(B, nt),
        in_specs=[packed, packed, packed,
                  pl.BlockSpec((tf, LANES), lambda b, i: (b * nt + i, G_OFF_F // LANES)),
                  full(pq), full(pk), full(cq), full(ck), full(cv), full(sp),
                  pl.BlockSpec((tf, POOL_WIDTH), lambda b, i: (b * nt + i, G_OFF_U // POOL_WIDTH)),
                  full(w_pool), full(s_pool)],
        out_specs=[row, row, pl.BlockSpec((1, 1, FOX_PAD, tf), lambda b, i: (b, i, 0, 0)),
                   pl.BlockSpec((STAT_ROWS, LANES), lambda b, i: (b * nt + i, 0)),
                   pl.BlockSpec((tf, POOL_WIDTH), lambda b, i: (b * nt + i, 0))],
        out_shape=[jax.ShapeDtypeStruct((T, FOX_PAD), BF16)] * 2
                  + [jax.ShapeDtypeStruct((B, nt, FOX_PAD, tf), BF16),
                     jax.ShapeDtypeStruct((B * nt * STAT_ROWS, LANES), F32),
                     jax.ShapeDtypeStruct((T, POOL_WIDTH), BF16)],
        scratch_shapes=[pltpu.VMEM((1, LANES), F32), pltpu.VMEM((tf + HALO, POOL_WIDTH), F32)],
        compiler_params=_cparams(("arbitrary", "arbitrary")),
        name="fgate_pool",
    )(zq, zk, zv, zg, pq, pk, cq, ck, cv, sp, zg, w_pool, s_pool)


def _attn_kernel(ff_ref, fl_ref, qn_ref, kn_ref, q_ref, k_ref, vt_ref, o_ref, m_sc, acc_sc, s_sc,
                 *, blk, nq, nb):
    h = pl.program_id(0)
    i = pl.program_id(1)

    def scores(slot, j):
        off = pl.multiple_of(j * blk, blk)
        for b in range(nb):
            s_sc[slot, b] = _dot_nt(k_ref[b, pl.ds(off, blk), :], q_ref[b])

    def softmax_pv(slot, j, diag):
        for b in range(nb):
            s = s_sc[slot, b]
            if diag:
                r = lax.broadcasted_iota(jnp.int32, (blk, blk), 0)
                c = lax.broadcasted_iota(jnp.int32, (blk, blk), 1)
                s = jnp.where(r <= c, s, NEG_BIG)
            m_prev = m_sc[b]
            m_new = jnp.maximum(m_prev, jnp.max(s, axis=0, keepdims=True))
            alpha = jnp.exp(m_prev - m_new)
            p = jnp.exp(s - m_new)
            acc_sc[b] = alpha * acc_sc[b] + _dot(vt_ref[b, j], p.astype(BF16))
            m_sc[b] = m_new

    m_sc[...] = jnp.full_like(m_sc, NEG_BIG)
    acc_sc[...] = jnp.zeros_like(acc_sc)
    scores(0, i)
    scores(1, jnp.maximum(i - 1, 0))
    softmax_pv(0, i, True)

    n = jnp.int32(0)
    for b in range(nb):
        base = (b * FOX_HEADS + h) * nq
        slack = (qn_ref[base + i] * kn_ref[b * FOX_HEADS + h] + ff_ref[base + i]
                 - jnp.min(m_sc[b]) + PRUNE_MARGIN)

        def cond(t, base=base, slack=slack):
            return jnp.logical_and(t < i, slack - fl_ref[base + jnp.maximum(i - 1 - t, 0)] >= 0.0)

        n = jnp.maximum(n, lax.while_loop(cond, lambda t: t + 1, jnp.int32(0)))

    def pair(u, carry):
        t = 1 + 2 * u
        ja = i - t
        scores(0, jnp.maximum(ja - 1, 0))
        softmax_pv(1, ja, False)

        @pl.when(t + 1 <= n)
        def _():
            scores(1, jnp.maximum(ja - 2, 0))
            softmax_pv(0, ja - 1, False)

        return carry

    lax.fori_loop(0, lax.shift_right_logical(n + 1, 1), pair, 0)
    for b in range(nb):
        acc = acc_sc[b]
        o_ref[b] = (acc / acc[BIAS_LANE:BIAS_LANE + 1, :]).T.astype(o_ref.dtype)


def _attention(q2, k2, vt, stats, B, S, blk):
    T = q2.shape[0]
    nq = S // blk
    H = FOX_HEADS
    st = stats.reshape(B, nq, STAT_ROWS, LANES)[:, :, :, :H]
    tab = lambda r: jnp.transpose(st[:, :, r, :], (0, 2, 1)).reshape(-1)
    ffirst, flast = tab(0), tab(1)
    qn = jnp.sqrt(tab(2)) * NORM_SLACK
    kn = jnp.sqrt(jnp.max(st[:, :, 3, :], axis=1)).reshape(-1) * NORM_SLACK
    r3 = lambda a: a.reshape(B, S, FOX_PAD)
    qspec = pl.BlockSpec((B, blk, LANES), lambda h, i, *_: (0, i, h))
    kspec = pl.BlockSpec((B, S, LANES), lambda h, i, *_: (0, 0, h))
    vtspec = pl.BlockSpec((B, nq, LANES, blk), lambda h, i, *_: (0, 0, h, 0))
    grid_spec = pltpu.PrefetchScalarGridSpec(
        num_scalar_prefetch=4,
        grid=(H, nq),
        in_specs=[qspec, kspec, vtspec],
        out_specs=qspec,
        scratch_shapes=[pltpu.VMEM((B, 1, blk), F32), pltpu.VMEM((B, LANES, blk), F32),
                        pltpu.VMEM((2, B, blk, blk), F32)],
    )
    out = pl.pallas_call(
        functools.partial(_attn_kernel, blk=blk, nq=nq, nb=B),
        grid_spec=grid_spec,
        out_shape=jax.ShapeDtypeStruct((B, S, FOX_PAD), BF16),
        compiler_params=_cparams(("arbitrary", "arbitrary")),
        name="fox_attention",
    )(ffirst, flast, qn, kn, r3(q2), r3(k2), vt)
    return out.reshape(T, FOX_PAD)


def _gla_kernel(q_ref, k_ref, v_ref, r_ref, a_ref, wa_ref, ba_ref, g_ref, o_ref, st_sc, *, tg, nb):
    @pl.when(pl.program_id(0) == 0)
    def _():
        st_sc[...] = jnp.zeros_like(st_sc)

    C = GLA_CHUNK
    nchunk = tg // C
    r = lax.broadcasted_iota(jnp.int32, (tg, tg), 0)
    c = lax.broadcasted_iota(jnp.int32, (tg, tg), 1)
    shift = C.bit_length() - 1
    tri = ((c <= r) & ((c >> shift) == (r >> shift))).astype(BF16)
    rc = lax.broadcasted_iota(jnp.int32, (C, C), 0)
    cc = lax.broadcasted_iota(jnp.int32, (C, C), 1)
    causal = cc <= rc
    lane = lax.broadcasted_iota(jnp.int32, (1, LANES), 1)
    vmask = (lane < GLA_DV).astype(F32)
    for bi in range(nb):
        la = (_log_sigmoid(_dot(a_ref[bi].astype(BF16), wa_ref[...]) + ba_ref[...])
              * (1.0 / GLA_GATE_TAU))
        hi, mid, lo = _split3(la)
        b = _dot(tri, hi) + _dot(tri, mid) + _dot(tri, lo)
        eb = jnp.exp(b)
        q_in = q_ref[bi] * (GLA_DK ** -0.5) * eb
        k_in = k_ref[bi] * jnp.exp(-b)
        v = v_ref[bi]
        outs = []
        for ci in range(nchunk):
            rows = slice(ci * C, (ci + 1) * C)
            b_last = b[ci * C + C - 1:ci * C + C, :]
            k_out = k_ref[bi, rows, :] * jnp.exp(b_last - b[rows, :])
            dec = jnp.exp(b_last)
            heads = []
            for h in range(GLA_HEADS):
                kl = slice(h * GLA_KSLOT, (h + 1) * GLA_KSLOT)
                ln = slice(h * LANES, (h + 1) * LANES)
                qh = q_in[rows, kl].astype(BF16)
                kh = k_in[rows, kl].astype(BF16)
                vh = v[rows, ln].astype(BF16)
                attn = jnp.where(causal, _dot_nt(qh, kh), 0.0)
                st = st_sc[bi * GLA_HEADS + h]
                o = _dot(attn.astype(BF16), vh) + _dot_nt(qh, st.astype(BF16))
                kv_t = _dot_tn(vh, k_out[:, kl].astype(BF16))
                st_sc[bi * GLA_HEADS + h] = st * dec[:, kl] + kv_t
                ms = jnp.sum(o * o, axis=-1, keepdims=True) * (1.0 / GLA_DV)
                heads.append(o * lax.rsqrt(ms + RMS_EPS) * vmask)
            outs.append(jnp.concatenate(heads, axis=1))
        o_all = jnp.concatenate(outs, axis=0)
        gr = r_ref[bi]
        o_ref[bi] = ((o_all * g_ref[...]) * (gr * jax.nn.sigmoid(gr))).astype(o_ref.dtype)


def _gla(zg, wa, ba, gn, B, S, tg):
    T = zg.shape[0]
    nt = S // tg
    zg3 = zg.reshape(B, S, G_WIDTH)
    col = lambda off, w: pl.BlockSpec((B, tg, w), lambda i: (0, i, off // w))
    full = lambda a: pl.BlockSpec(a.shape, lambda i: (0,) * a.ndim)
    out = pl.pallas_call(
        functools.partial(_gla_kernel, tg=tg, nb=B),
        grid=(nt,),
        in_specs=[col(G_OFF_Q, GLA_KPAD), col(G_OFF_K, GLA_KPAD), col(G_OFF_V, GLA_PAD),
                  col(G_OFF_R, GLA_PAD), col(G_OFF_A, LANES), full(wa), full(ba), full(gn)],
        out_specs=pl.BlockSpec((B, tg, GLA_PAD), lambda i: (0, i, 0)),
        out_shape=jax.ShapeDtypeStruct((B, S, GLA_PAD), BF16),
        scratch_shapes=[pltpu.VMEM((B * GLA_HEADS, LANES, GLA_KSLOT), F32)],
        compiler_params=_cparams(("arbitrary",)),
        name="gla",
    )(zg3, zg3, zg3, zg3, zg3, wa, ba, gn)
    return out.reshape(T, GLA_PAD)


HALO = max(POOL_WINDOWS)


def _pool_kernel(u_ref, w_ref, s_ref, o_ref, xx):
    tp = u_ref.shape[0]
    i = pl.program_id(1)

    @pl.when(i == 0)
    def _():
        xx[0:HALO, :] = jnp.zeros((HALO, POOL_WIDTH), F32)

    @pl.when(i > 0)
    def _():
        xx[0:HALO, :] = xx[tp:tp + HALO, :]

    u = u_ref[...]
    xx[HALO:HALO + tp, :] = u
    lane = lax.broadcasted_iota(jnp.int32, (1, POOL_WIDTH), 1)
    grp = lane >> (POOL_GROUP.bit_length() - 1)
    pos = lax.broadcasted_iota(jnp.int32, (tp, 1), 0) + i * tp + 1
    acc = u
    mean = jnp.zeros_like(u)
    for j in range(1, HALO):
        acc = acc + xx[HALO - j:HALO - j + tp, :]
        w = j + 1
        if w in POOL_WINDOWS:
            g = POOL_WINDOWS.index(w)
            inv_cnt = 1.0 / jnp.minimum(pos, w).astype(F32)
            mean = jnp.where(grp == g, acc * inv_cnt, mean)
    pooled = mean - u
    mixed = _dot(pooled.astype(BF16), w_ref[...])
    o_ref[...] = (mixed * s_ref[...]).astype(o_ref.dtype)


def _layer_norm(r, g, b):
    mu = jnp.mean(r, axis=-1, keepdims=True)
    d = r - mu
    var = jnp.mean(d * d, axis=-1, keepdims=True)
    return d * lax.rsqrt(var + LN_EPS) * g + b


def _outproj_kernel(of_ref, og_ref, op_ref, x_ref, wf_ref, wg_ref, wp_ref, gate_ref, lg_ref, lb_ref,
                    sc_ref, sh_ref, wr_ref, br_ref,
                    x1_ref, h2_ref, e_ref, gt_ref, cnt_ref, *, alpha):
    y = _dot(of_ref[...], wf_ref[...]) + _dot(og_ref[...], wg_ref[...]) + _dot(op_ref[...], wp_ref[...])
    r = alpha * x_ref[...] + (1.0 + gate_ref[0]) * y
    x1 = _layer_norm(r, lg_ref[...], lb_ref[...])
    x1_ref[...] = x1
    h2 = x1 * (1.0 + sc_ref[0]) + sh_ref[0]
    h2_ref[...] = h2
    h_hi = h2.astype(BF16)
    h_lo = (h2 - h_hi.astype(F32)).astype(BF16)
    logits = (_dot(h_hi, wr_ref[0]) + (_dot(h_hi, wr_ref[1]) + _dot(h_lo, wr_ref[0]))
              + br_ref[...])
    tm = logits.shape[0]
    lane_i = lax.broadcasted_iota(jnp.int32, (tm, LANES), 1)
    lane = lane_i.astype(F32)
    work = logits
    tops, idxs = [], []
    onehot = jnp.zeros((tm, LANES), F32)
    for _ in range(TOP_K):
        m = jnp.max(work, axis=-1, keepdims=True)
        idx = jnp.min(jnp.where(work == m, lane, float(LANES)), axis=-1, keepdims=True)
        sel = lane == idx
        onehot = onehot + sel.astype(F32)
        work = jnp.where(sel, -jnp.inf, work)
        tops.append(m)
        idxs.append(idx)
    ex = [jnp.exp(t - tops[0]) for t in tops]
    den = ex[0] + ex[1] + ex[2] + ex[3]
    e_out = jnp.zeros((tm, LANES), jnp.int32)
    g_out = jnp.zeros((tm, LANES), F32)
    for k in range(TOP_K):
        e_out = jnp.where(lane_i == k, idxs[k].astype(jnp.int32), e_out)
        g_out = jnp.where(lane_i == k, ex[k] / den, g_out)
    e_ref[...] = e_out
    gt_ref[...] = g_out
    for u in range(tm // ROUTE_TILE):
        rows = slice(u * ROUTE_TILE, (u + 1) * ROUTE_TILE)
        cnt_ref[u] = jnp.sum(onehot[rows], axis=0, keepdims=True).astype(jnp.int32)


def _outproj(o_fox, o_gla, o_pool, x2d, wf, wg, wp, gate1, ln_g, ln_b, scale2, shift2, wr, br,
             B, S, tm, alpha):
    T, D = x2d.shape
    nt = S // tm
    full = lambda a: pl.BlockSpec(a.shape, lambda i: (0,) * a.ndim)
    row = lambda w: pl.BlockSpec((tm, w), lambda i: (i, 0))
    mod = pl.BlockSpec((1, 1, D), lambda i: (i // nt, 0, 0))
    return pl.pallas_call(
        functools.partial(_outproj_kernel, alpha=alpha),
        grid=(T // tm,),
        in_specs=[row(FOX_PAD), row(GLA_PAD), row(POOL_WIDTH), row(D), full(wf), full(wg), full(wp),
                  mod, full(ln_g), full(ln_b), mod, mod, full(wr), full(br)],
        out_specs=[row(D), row(D), row(LANES), row(LANES),
                   pl.BlockSpec((tm // ROUTE_TILE, 1, LANES), lambda i: (i, 0, 0))],
        out_shape=[jax.ShapeDtypeStruct((T, D), F32), jax.ShapeDtypeStruct((T, D), F32),
                   jax.ShapeDtypeStruct((T, LANES), jnp.int32), jax.ShapeDtypeStruct((T, LANES), F32),
                   jax.ShapeDtypeStruct((T // ROUTE_TILE, 1, LANES), jnp.int32)],
        compiler_params=_cparams(("arbitrary",)),
        name="outproj_router",
    )(o_fox, o_gla, o_pool, x2d, wf, wg, wp, gate1, ln_g, ln_b, scale2, shift2, wr, br)


def _tile_slots(e_i32, off_row):
    td = e_i32.shape[0]
    lane = lax.broadcasted_iota(jnp.int32, (td, LANES), 1).astype(F32)
    ef = e_i32.astype(F32)
    sel = [lane == ef[:, k:k + 1] for k in range(TOP_K)]
    onehot = sel[0].astype(F32)
    for k in range(1, TOP_K):
        onehot = onehot + sel[k].astype(F32)
    rr = lax.broadcasted_iota(jnp.int32, (td, td), 0)
    cc = lax.broadcasted_iota(jnp.int32, (td, td), 1)
    stril = (cc < rr).astype(BF16)
    tab = _dot(stril, onehot.astype(BF16)) + off_row
    return [jnp.sum(jnp.where(sel[k], tab, 0.0), axis=-1, keepdims=True) for k in range(TOP_K)]


def _group_bits(max_groups):
    return [1 << s for s in range(max_groups.bit_length() - 1, -1, -1)]


PIECE_CLASSES = (ROUTE_TILE // ROW_GROUP).bit_length()
PIECE_LIST = 1024
SLOT_BITS = 8
assert ROUTE_SLOTS // ROW_GROUP < (1 << SLOT_BITS) and PIECE_CLASSES * N_EXPERTS <= PIECE_LIST


def _piece_lists(m, off, dst):
    nt = m.shape[0]
    j = jnp.arange(N_EXPERTS, dtype=jnp.int32)
    b = jnp.arange(PIECE_CLASSES, dtype=jnp.int32)[None, :, None]
    mm, oo, dd = m[:, None, :], off[:, None, :], dst[:, None, :]
    has = (mm >> b) & 1
    before = (mm & ~((2 << b) - 1)) * ROW_GROUP
    packed = (((dd + before) // ROW_GROUP) << SLOT_BITS) | ((oo + before) // ROW_GROUP)
    rank = _excl_prefix(has, 2)
    hit = (has[..., None] == 1) & (rank[..., None] == j)
    lst = jnp.sum(jnp.where(hit, packed[..., None], 0), axis=2)
    lst = _pad_last(lst.reshape(nt, PIECE_CLASSES * N_EXPERTS).astype(jnp.int32), PIECE_LIST)
    return lst.reshape(-1), jnp.sum(has, axis=2).astype(jnp.int32).reshape(-1)


def _for_each_piece(tile, cnt_ref, list_ref, fn, base=0):
    for b in range(PIECE_CLASSES):
        def body(i, carry, b=b):
            v = list_ref[base + b * N_EXPERTS + i]
            fn(pl.multiple_of((v & ((1 << SLOT_BITS) - 1)) * ROW_GROUP, ROW_GROUP),
               pl.multiple_of(lax.shift_right_logical(v, SLOT_BITS) * ROW_GROUP, ROW_GROUP),
               (1 << b) * ROW_GROUP)
            return carry

        lax.fori_loop(0, cnt_ref[tile * PIECE_CLASSES + b], body, 0)


def _for_each_total_piece(total_groups, fn):
    for bit in _group_bits(ROUTE_SLOTS // ROW_GROUP):
        @pl.when((total_groups & bit) != 0)
        def _(bit=bit):
            fn(bit * ROW_GROUP)


def _dispatch_kernel(cnt_ref, tot_ref, zblk_ref, nu_ref, list_ref, e_ref, offrow_ref, h_ref,
                     xr_ref, slot_ref, sort_sc, zero_sc, sems, zsem, tsem, *, n_blocks, nt):
    step = pl.program_id(0)

    def zcopy(blk, sem):
        return pltpu.make_async_copy(
            zero_sc, xr_ref.at[pl.ds(pl.multiple_of(blk * MOE_BLOCK, MOE_BLOCK), MOE_BLOCK), :], sem)

    @pl.when(step == 0)
    def _():
        zero_sc[...] = jnp.zeros_like(zero_sc)

        def zstart(e, c):
            @pl.when(zblk_ref[e] >= 0)
            def _():
                zcopy(zblk_ref[e], zsem).start()
            return c

        def zwait(e, c):
            @pl.when(zblk_ref[e] >= 0)
            def _():
                zcopy(0, zsem).wait()
            return c

        def tstart(blk, c):
            zcopy(blk, tsem).start()
            return c

        lax.fori_loop(0, N_EXPERTS, zstart, 0)
        lax.fori_loop(nu_ref[0], n_blocks, tstart, 0)
        lax.fori_loop(0, N_EXPERTS, zwait, 0)

    td = ROUTE_TILE
    par = step & 1
    for u in range(TILES_PER_STEP):
        rows = slice(u * td, (u + 1) * td)
        slots = _tile_slots(e_ref[rows, :], offrow_ref[u])
        lane = lax.broadcasted_iota(jnp.int32, (td, LANES), 1)
        cols = jnp.full((td, LANES), -1.0, F32)
        for k in range(TOP_K):
            cols = jnp.where(lane == k, slots[k], cols)
        slot_ref[rows, :] = cols
        rows_t = cols.T
        sub = lax.broadcasted_iota(jnp.int32, (ROUTE_SLOTS, td), 0).astype(F32)
        pick = sub == rows_t[0:1, :]
        for k in range(1, TOP_K):
            pick = pick | (sub == rows_t[k:k + 1, :])
        sort_sc[par * TILES_PER_STEP + u] = _dot(pick.astype(BF16), h_ref[rows, :].astype(BF16))

    for u in range(TILES_PER_STEP):
        buf = par * TILES_PER_STEP + u

        def start_chunk(slot0, row0, n, buf=buf, u=u):
            pltpu.make_async_copy(sort_sc.at[buf, pl.ds(slot0, n), :], xr_ref.at[pl.ds(row0, n), :],
                                  sems.at[par, u]).start(priority=u % 2)

        _for_each_piece(step * TILES_PER_STEP + u, cnt_ref, list_ref, start_chunk, base=u * PIECE_LIST)

    def wait_step(s, p):
        for u in range(TILES_PER_STEP):
            def wait_piece(n, u=u):
                pltpu.make_async_copy(sort_sc.at[p * TILES_PER_STEP + u, pl.ds(0, n), :],
                                      xr_ref.at[pl.ds(0, n), :], sems.at[p, u]).wait()
            _for_each_total_piece(tot_ref[s * TILES_PER_STEP + u], wait_piece)

    @pl.when(step > 0)
    def _():
        wait_step(step - 1, 1 - par)

    @pl.when(step == nt - 1)
    def _():
        wait_step(step, par)

        def twait(blk, c):
            zcopy(0, tsem).wait()
            return c

        lax.fori_loop(nu_ref[0], n_blocks, twait, 0)


def _dispatch(h2, e_pad, tables, rows):
    T, D = h2.shape
    tpn = TILES_PER_STEP
    td = ROUTE_TILE * tpn
    assert T % td == 0
    nt = T // td
    piece_list, piece_cnt, tot_tab, zblk, n_used, off_rows = tables
    grid_spec = pltpu.PrefetchScalarGridSpec(
        num_scalar_prefetch=4,
        grid=(nt,),
        in_specs=[pl.BlockSpec((tpn * PIECE_LIST,), lambda i, *_: (i,), memory_space=pltpu.SMEM),
                  pl.BlockSpec((td, LANES), lambda i, *_: (i, 0)),
                  pl.BlockSpec((tpn, 1, LANES), lambda i, *_: (i, 0, 0)),
                  pl.BlockSpec((td, D), lambda i, *_: (i, 0))],
        out_specs=[pl.BlockSpec(memory_space=pl.ANY),
                   pl.BlockSpec((td, LANES), lambda i, *_: (i, 0))],
        scratch_shapes=[pltpu.VMEM((2 * tpn, ROUTE_SLOTS, D), F32), pltpu.VMEM((MOE_BLOCK, D), F32),
                        pltpu.SemaphoreType.DMA((2, tpn)), pltpu.SemaphoreType.DMA, pltpu.SemaphoreType.DMA],
    )
    return pl.pallas_call(
        functools.partial(_dispatch_kernel, n_blocks=rows // MOE_BLOCK, nt=nt),
        grid_spec=grid_spec,
        out_shape=[jax.ShapeDtypeStruct((rows, D), F32), jax.ShapeDtypeStruct((T, LANES), F32)],
        compiler_params=_cparams(("arbitrary",)),
        name="moe_dispatch",
    )(piece_cnt, tot_tab, zblk, n_used, piece_list, e_pad, off_rows, h2)


def _expert_kernel(be_ref, nu_ref, vb_ref, par_ref, nxt_ref, x_ref, wgu_hbm, bgu_ref, wd_hbm, bd_ref, y_ref,
                   wgu_sc, wd_sc, gu_buf, d_buf, sems, *, first_expert):
    i = pl.program_id(0)
    used = i < nu_ref[0]

    def fetch(e, slot):
        return (pltpu.make_async_copy(wgu_hbm.at[first_expert + e], gu_buf.at[slot], sems.at[slot, 0]),
                pltpu.make_async_copy(wd_hbm.at[first_expert + e], d_buf.at[slot], sems.at[slot, 1]))

    @pl.when(used)
    def _():
        prev = be_ref[jnp.maximum(i - 1, 0)]

        @pl.when((i == 0) | (be_ref[i] != prev))
        def _():
            slot = par_ref[i]

            @pl.when(i == 0)
            def _():
                for cp in fetch(be_ref[i], slot):
                    cp.start()

            for cp in fetch(be_ref[i], slot):
                cp.wait()
            wgu_sc[...] = gu_buf[slot].astype(BF16)
            wd_sc[...] = d_buf[slot].astype(BF16)

            @pl.when(nxt_ref[i] >= 0)
            def _():
                for cp in fetch(nxt_ref[i], 1 - slot):
                    cp.start()

        def mlp(rows):
            x = x_ref[rows, :].astype(BF16)
            gu = _dot(x, wgu_sc[...]) + bgu_ref[0]
            glu = jnp.minimum(gu[:, :D_EXPERT], SWIGLU_LIMIT)
            lin = jnp.clip(gu[:, D_EXPERT:], -SWIGLU_LIMIT, SWIGLU_LIMIT)
            act = glu * jax.nn.sigmoid(SWIGLU_ALPHA * glu) * (lin + 1.0)
            y_ref[rows, :] = _dot(act.astype(BF16), wd_sc[...]) + bd_ref[0]

        half = MOE_BLOCK // 2
        full_block = vb_ref[i] > half

        @pl.when(full_block)
        def _():
            mlp(slice(0, MOE_BLOCK))

        @pl.when(jnp.logical_not(full_block))
        def _():
            mlp(slice(0, half))
            y_ref[half:, :] = jnp.zeros((MOE_BLOCK - half, y_ref.shape[1]), y_ref.dtype)

    @pl.when(jnp.logical_not(used))
    def _():
        y_ref[...] = jnp.zeros_like(y_ref)


def _experts(x_rows, block_expert, n_used, valid_rows, slot_parity, next_expert,
             w_gate_up, b_gate_up, w_down, b_down, layer):
    rows, D = x_rows.shape
    nb = rows // MOE_BLOCK
    E = w_gate_up.shape[1]
    grid_spec = pltpu.PrefetchScalarGridSpec(
        num_scalar_prefetch=5,
        grid=(nb,),
        in_specs=[pl.BlockSpec((MOE_BLOCK, D), lambda i, be, nu, *_: (jnp.minimum(i, nu[0] - 1), 0)),
                  pl.BlockSpec(memory_space=pl.ANY),
                  pl.BlockSpec((1, 1, 2 * D_EXPERT), lambda i, be, *_: (layer * E + be[i], 0, 0)),
                  pl.BlockSpec(memory_space=pl.ANY),
                  pl.BlockSpec((1, 1, D), lambda i, be, *_: (layer * E + be[i], 0, 0))],
        out_specs=pl.BlockSpec((MOE_BLOCK, D), lambda i, *_: (i, 0)),
        scratch_shapes=[pltpu.VMEM((D, 2 * D_EXPERT), BF16), pltpu.VMEM((D_EXPERT, D), BF16),
                        pltpu.VMEM((2, D, 2 * D_EXPERT), F32), pltpu.VMEM((2, D_EXPERT, D), F32),
                        pltpu.SemaphoreType.DMA((2, 2))],
    )
    L = w_gate_up.shape[0]
    return pl.pallas_call(
        functools.partial(_expert_kernel, first_expert=layer * E),
        grid_spec=grid_spec,
        out_shape=jax.ShapeDtypeStruct((rows, D), F32),
        compiler_params=_cparams(("arbitrary",)),
        name="moe_experts",
    )(block_expert, n_used, valid_rows, slot_parity, next_expert, x_rows,
      w_gate_up.reshape(L * E, D, 2 * D_EXPERT), b_gate_up.reshape(L * E, 1, 2 * D_EXPERT),
      w_down.reshape(L * E, D_EXPERT, D), b_down.reshape(L * E, 1, D))


def _combine_kernel(cnt_ref, tot_ref, list_ref, next_list_ref, slot_ref, gt_ref, y_ref, x_ref, gate_ref,
                    lg_ref, lb_ref, o_ref, ybuf, sems, *, nt, alpha):
    step = pl.program_id(0)
    par = step & 1
    tpn = TILES_PER_STEP

    def fetch(s, p, lst):
        for u in range(tpn):
            def start_chunk(slot0, row0, n, u=u):
                pltpu.make_async_copy(y_ref.at[pl.ds(row0, n), :], ybuf.at[p * tpn + u, pl.ds(slot0, n), :],
                                      sems.at[p, u]).start(priority=u % 2)
            _for_each_piece(s * tpn + u, cnt_ref, lst, start_chunk, base=u * PIECE_LIST)

    @pl.when(step == 0)
    def _():
        ybuf[...] = jnp.zeros_like(ybuf)
        fetch(step, par, list_ref)

    @pl.when(step + 1 < nt)
    def _():
        fetch(step + 1, 1 - par, next_list_ref)

    for u in range(tpn):
        def wait_piece(n, u=u):
            pltpu.make_async_copy(y_ref.at[pl.ds(0, n), :], ybuf.at[par * tpn + u, pl.ds(0, n), :],
                                  sems.at[par, u]).wait()
        _for_each_total_piece(tot_ref[step * tpn + u], wait_piece)

    td = ROUTE_TILE
    for u in range(tpn):
        rows = slice(u * td, (u + 1) * td)
        slots = slot_ref[rows, :]
        gt = gt_ref[rows, :]
        lane = lax.broadcasted_iota(jnp.int32, (td, ROUTE_SLOTS), 1).astype(F32)
        w = jnp.where(lane == slots[:, 0:1], gt[:, 0:1], 0.0)
        for k in range(1, TOP_K):
            w = w + jnp.where(lane == slots[:, k:k + 1], gt[:, k:k + 1], 0.0)
        y = _dot(w.astype(BF16), ybuf[par * tpn + u].astype(BF16))
        r = alpha * x_ref[rows, :] + (1.0 + gate_ref[0]) * y
        o_ref[rows, :] = _layer_norm(r, lg_ref[...], lb_ref[...])


def _combine(y_rows, slots, tables, gates, x1, gate2, ln_g, ln_b, B, S, alpha):
    T, D = x1.shape
    tpn = TILES_PER_STEP
    td = ROUTE_TILE * tpn
    assert S % td == 0
    nt = T // td
    per_batch = S // td
    piece_list, piece_cnt, tot_tab, _, _, _ = tables
    grid_spec = pltpu.PrefetchScalarGridSpec(
        num_scalar_prefetch=2,
        grid=(nt,),
        in_specs=[pl.BlockSpec((tpn * PIECE_LIST,), lambda i, *_: (i,), memory_space=pltpu.SMEM),
                  pl.BlockSpec((tpn * PIECE_LIST,), lambda i, *_: (jnp.minimum(i + 1, nt - 1),),
                               memory_space=pltpu.SMEM),
                  pl.BlockSpec((td, LANES), lambda i, *_: (i, 0)),
                  pl.BlockSpec((td, LANES), lambda i, *_: (i, 0)),
                  pl.BlockSpec(memory_space=pl.ANY),
                  pl.BlockSpec((td, D), lambda i, *_: (i, 0)),
                  pl.BlockSpec((1, 1, D), lambda i, *_: (i // per_batch, 0, 0)),
                  pl.BlockSpec((1, D), lambda i, *_: (0, 0)),
                  pl.BlockSpec((1, D), lambda i, *_: (0, 0))],
        out_specs=pl.BlockSpec((td, D), lambda i, *_: (i, 0)),
        scratch_shapes=[pltpu.VMEM((2 * tpn, ROUTE_SLOTS, D), F32), pltpu.SemaphoreType.DMA((2, tpn))],
    )
    return pl.pallas_call(
        functools.partial(_combine_kernel, nt=nt, alpha=alpha),
        grid_spec=grid_spec,
        out_shape=jax.ShapeDtypeStruct((T, D), F32),
        compiler_params=_cparams(("arbitrary",)),
        name="moe_combine",
    )(piece_cnt, tot_tab, piece_list, piece_list, slots, gates, y_rows, x1, gate2, ln_g, ln_b)


def _tile(n, pref):
    t = min(n, pref)
    assert n % t == 0, (n, t)
    return t


def kernel(x, c, w_ada, b_ada, w_in, b_in, gla_w_a2, gla_b_a, gla_norm_g, pool_w, pool_scale, w_out,
           ln1_g, ln1_b, w_router, b_router, w_gate_up, b_gate_up, w_down, b_down, ln2_g, ln2_b):
    B, S, D = x.shape
    L = w_ada.shape[0]
    T = B * S
    assert D == D_MODEL and S % GLA_CHUNK == 0
    alpha = float((2 * L) ** 0.25)
    assert T % ROUTE_TILE == 0
    n_tiles = T // ROUTE_TILE
    max_rows = T * TOP_K + n_tiles * N_EXPERTS * (ROW_GROUP - 1) + N_EXPERTS * (MOE_BLOCK - 1)
    n_blocks = -(-max_rows // MOE_BLOCK)
    rows = n_blocks * MOE_BLOCK

    mod = _ada_mod(c, w_ada, b_ada)
    prep_all = _prep_inproj(w_in, b_in)
    wa_all = jnp.pad(_pad_heads(gla_w_a2, GLA_HEADS, GLA_DK, GLA_KSLOT),
                     ((0, 0), (0, LANES - GLA_GATE_RANK), (0, 0))).astype(BF16)
    ba_all = _pad_heads(gla_b_a[:, None, :], GLA_HEADS, GLA_DK, GLA_KSLOT)
    gn_all = _pad_heads(gla_norm_g[:, None, :], GLA_HEADS, GLA_DV)
    w_bd_all = jnp.zeros((L, POOL_WIDTH, POOL_WIDTH), F32)
    for g in range(len(POOL_WINDOWS)):
        sl = slice(g * POOL_GROUP, (g + 1) * POOL_GROUP)
        w_bd_all = w_bd_all.at[:, sl, sl].set(pool_w[:, g])
    w_bd_all = w_bd_all.astype(BF16)
    pad_rows = lambda w, heads, dim: jnp.swapaxes(_pad_heads(jnp.swapaxes(w, 1, 2), heads, dim), 1, 2)
    wf_all = pad_rows(w_out[:, :FOX_WIDTH], FOX_HEADS, FOX_HEAD_DIM).astype(BF16)
    wgl_all = pad_rows(w_out[:, FOX_WIDTH:FOX_WIDTH + GLA_VWIDTH], GLA_HEADS, GLA_DV).astype(BF16)
    wp_all = w_out[:, FOX_WIDTH + GLA_VWIDTH:].astype(BF16)
    wr_f32 = _pad_last(w_router, LANES)
    wr_hi = wr_f32.astype(BF16)
    wr_all = jnp.stack([wr_hi, (wr_f32 - wr_hi.astype(F32)).astype(BF16)], axis=1)
    br_all = jnp.pad(b_router[:, None, :], ((0, 0), (0, 0), (0, LANES - N_EXPERTS)), constant_values=NEG_BIG)

    x2d = x.reshape(T, D)
    for l in range(L):
        shift1, scale1, gate1, shift2, scale2, gate2 = [mod[l, :, m] for m in range(N_MOD)]
        prep = [(w[l], b[l]) for w, b in prep_all]
        zq, zk, zv, zg = _inproj(x2d, scale1, shift1, prep, B, S, _tile(S, 512))
        q2, k2, vt, stats, o_pool = _fgate(zq, zk, zv, zg, w_bd_all[l], pool_scale[l][None, :],
                                           B, S, _tile(S, ATTN_BLOCK))
        o_fox = _attention(q2, k2, vt, stats, B, S, _tile(S, ATTN_BLOCK))
        o_gla = _gla(zg, wa_all[l], ba_all[l], gn_all[l], B, S, _tile(S, 256))
        x1, h2, e_pad, g_pad, cnt = _outproj(
            o_fox, o_gla, o_pool, x2d, wf_all[l], wgl_all[l], wp_all[l], gate1,
            ln1_g[l][None, :], ln1_b[l][None, :], scale2, shift2, wr_all[l], br_all[l],
            B, S, _tile(S, 512), alpha)
        cte = cnt[:, 0, :N_EXPERTS]
        c8 = (cte + ROW_GROUP - 1) // ROW_GROUP * ROW_GROUP
        off = _excl_prefix(c8, 1)
        per_expert = jnp.sum(c8, axis=0)
        padded = (per_expert + MOE_BLOCK - 1) // MOE_BLOCK * MOE_BLOCK
        pstart = _excl_prefix(padded, 0)
        pend = pstart + padded
        dst = pstart[None, :] + _excl_prefix(c8, 0)
        n_used = (pend[-1] // MOE_BLOCK).astype(jnp.int32)
        blk_start = jnp.arange(n_blocks, dtype=jnp.int32) * MOE_BLOCK
        be = jnp.minimum(jnp.sum(blk_start[:, None] >= pend[None, :], axis=1), N_EXPERTS - 1).astype(jnp.int32)
        last_used = jnp.minimum(jnp.sum(MOE_BLOCK * (n_used - 1) >= pend), N_EXPERTS - 1).astype(jnp.int32)
        be = jnp.where(jnp.arange(n_blocks) < n_used, be, last_used)
        zblk = jnp.where(padded > 0, pend // MOE_BLOCK - 1, -1).astype(jnp.int32)
        off_rows = _pad_last(off.astype(F32), LANES)[:, None, :]
        n_used = n_used.reshape(1)
        piece_list, piece_cnt = _piece_lists(c8 // ROW_GROUP, off, dst)
        tables = (piece_list, piece_cnt, (jnp.sum(c8, axis=1) // ROW_GROUP).astype(jnp.int32),
                  zblk, n_used, off_rows)
        x_rows, slots = _dispatch(h2, e_pad, tables, rows)
        owns = padded > 0
        ids = jnp.arange(N_EXPERTS, dtype=jnp.int32)
        later = jnp.where(owns[None, :] & (ids[None, :] > ids[:, None]), ids[None, :], N_EXPERTS)
        nxt_e = jnp.min(later, axis=1)
        nxt_e = jnp.where(nxt_e < N_EXPERTS, nxt_e, -1).astype(jnp.int32)
        par_e = (_excl_prefix(owns.astype(jnp.int32), 0) & 1).astype(jnp.int32)
        pick = be[:, None] == ids[None, :]
        lookup = lambda tab: jnp.sum(jnp.where(pick, tab[None, :], 0), axis=1).astype(jnp.int32)
        valid_rows = jnp.clip(lookup(per_expert) - (blk_start - lookup(pstart)), 0, MOE_BLOCK).astype(jnp.int32)
        y_rows = _experts(x_rows, be, n_used, valid_rows, lookup(par_e), lookup(nxt_e),
                          w_gate_up, b_gate_up, w_down, b_down, l)
        x2d = _combine(y_rows, slots, tables, g_pad, x1, gate2,
                       ln2_g[l][None, :], ln2_b[l][None, :], B, S, alpha)
    return x2d.reshape(B, S, D)
```
